```python
import jax, jax.numpy as jnp
from jax import lax
import numpy as np

D_MODEL = 2048
BATCH = 4
SEQ = 2048
DEPTH = 1

CONV_DIM = 1024
CONV_GROUPS = 16
CONV_WIDTH = 3
N_HEADS = 8
HEAD_DIM = 128
ATTN_DIM = N_HEADS * HEAD_DIM
MIX_DIM = CONV_DIM + ATTN_DIM
KV_RANK = 512
IDX_HEADS = 16
IDX_DIM = 128
TOPK_MAX = 256
Q_BLOCK = 128
N_GROUPS = 4
EXPERTS_PER_GROUP = 8
N_EXPERTS = N_GROUPS * EXPERTS_PER_GROUP
TOPK_EXPERTS = 2
EXPERT_FF = 512
N_MOD = 6
EPS = 1e-6

OFF_B = 0
OFF_C = OFF_B + CONV_DIM
OFF_X = OFF_C + CONV_DIM
OFF_Q = OFF_X + CONV_DIM
OFF_KV = OFF_Q + ATTN_DIM
OFF_IQ = OFF_KV + KV_RANK
OFF_IK = OFF_IQ + IDX_HEADS * IDX_DIM
OFF_IW = OFF_IK + IDX_DIM
IN_COLS = OFF_IW + IDX_HEADS

kernel_name = "hymba_conv_dsa_hmoe_adaln"


def rmsnorm(x, g):
    xf = x.astype(jnp.float32)
    y = xf * lax.rsqrt(jnp.mean(xf * xf, axis=-1, keepdims=True) + EPS)
    return (y * g.astype(jnp.float32)).astype(x.dtype)


def causal_dwconv(u, w):
    return lax.conv_general_dilated(
        u, w[:, None, :], window_strides=(1,), padding=[(CONV_WIDTH - 1, 0)],
        dimension_numbers=('NWC', 'WIO', 'NWC'), feature_group_count=u.shape[-1])


def dsa_attention(q_lat, iq, iw, ik, ckv):
    B, S = q_lat.shape[0], q_lat.shape[1]
    k_sel = min(TOPK_MAX, S // 4)
    nb = S // Q_BLOCK
    scale = HEAD_DIM ** -0.5
    key_pos = jnp.arange(S, dtype=jnp.int32)
    q_pos = key_pos.reshape(nb, Q_BLOCK)

    def to_blocks(a):
        return jnp.moveaxis(a.reshape((B, nb, Q_BLOCK) + a.shape[2:]), 1, 0)

    def block(args):
        ql, qi, wi, qp = args
        s_idx = jnp.einsum('bqhd,bsd->bqhs', qi, ik, preferred_element_type=jnp.float32)
        idx_score = jnp.einsum('bqhs,bqh->bqs', jax.nn.relu(s_idx), wi.astype(jnp.float32))
        causal = key_pos[None, :] <= qp[:, None]
        idx_score = jnp.where(causal[None], idx_score, -jnp.inf)
        _, sel = lax.top_k(idx_score, k_sel)
        valid = sel <= qp[None, :, None]
        kv = jax.vmap(lambda cb, ib: cb[ib])(ckv, sel)
        logits = jnp.einsum('bqhr,bqkr->bqhk', ql, kv, preferred_element_type=jnp.float32) * scale
        logits = jnp.where(valid[:, :, None, :], logits, -jnp.inf)
        p = jax.nn.softmax(logits, axis=-1).astype(kv.dtype)
        return jnp.einsum('bqhk,bqkr->bqhr', p, kv)

    o = lax.map(block, (to_blocks(q_lat), to_blocks(iq), to_blocks(iw), q_pos))
    return jnp.moveaxis(o, 0, 1).reshape(B, S, N_HEADS, KV_RANK)


def hier_moe(h, w_rg, b_rg, w_re, b_re, w_gate, w_up, w_down):
    B, S, D = h.shape
    t = h.reshape(-1, D)
    g_prob = jax.nn.softmax((t @ w_rg).astype(jnp.float32) + b_rg.astype(jnp.float32), axis=-1)
    p_group, g_sel = lax.top_k(g_prob, 1)
    e_logits = ((t @ w_re).astype(jnp.float32) + b_re.astype(jnp.float32)).reshape(-1, N_GROUPS, EXPERTS_PER_GROUP)
    e_logits = jnp.take_along_axis(e_logits, g_sel[:, :, None], axis=1)[:, 0]
    top_p, top_i = lax.top_k(jax.nn.softmax(e_logits, axis=-1), TOPK_EXPERTS)
    top_p = top_p / jnp.sum(top_p, axis=-1, keepdims=True)
    within = jnp.einsum('tk,tke->te', top_p, jax.nn.one_hot(top_i, EXPERTS_PER_GROUP, dtype=jnp.float32))
    combine = (jax.nn.one_hot(g_sel[:, 0], N_GROUPS, dtype=jnp.float32)[:, :, None]
               * (p_group * within)[:, None, :]).reshape(-1, N_EXPERTS).astype(t.dtype)
    out = jnp.zeros_like(t)
    for g in range(N_GROUPS):
        sl = slice(g * EXPERTS_PER_GROUP, (g + 1) * EXPERTS_PER_GROUP)
        a = jnp.einsum('td,edf->tef', t, w_gate[sl])
        u = jnp.einsum('td,edf->tef', t, w_up[sl])
        hid = jax.nn.silu(a) * u * combine[:, sl, None]
        out = out + jnp.einsum('tef,efd->td', hid, w_down[sl])
    return out.reshape(B, S, D)


def setup_inputs(seed: int = 0) -> dict:
    key = jax.random.key(seed)
    ks = jax.random.split(key, 24)
    f = jnp.float32
    L, D = DEPTH, D_MODEL

    def nrm(k, shape, std):
        return jax.random.normal(k, shape, f) * std

    def gain(k, shape):
        return 1.0 + 0.02 * jax.random.normal(k, shape, f)

    return {
        "x": nrm(ks[0], (BATCH, SEQ, D), 1.0),
        "c": nrm(ks[1], (BATCH, D), 1.0),
        "w_ada": nrm(ks[2], (L, D, N_MOD * D), 0.5 * D ** -0.5),
        "b_ada": nrm(ks[3], (L, N_MOD * D), 0.02),
        "g_mix": gain(ks[4], (L, D)),
        "w_in": nrm(ks[5], (L, D, IN_COLS), D ** -0.5),
        "conv_w": nrm(ks[6], (L, CONV_WIDTH, CONV_DIM), CONV_WIDTH ** -0.5),
        "w_uk": nrm(ks[7], (L, N_HEADS, KV_RANK, HEAD_DIM), KV_RANK ** -0.5),
        "kv_norm_g": gain(ks[8], (L, KV_RANK)),
        "w_uv": nrm(ks[9], (L, N_HEADS, KV_RANK, HEAD_DIM), KV_RANK ** -0.5),
        "g_conv_out": gain(ks[10], (L, CONV_DIM)),
        "g_attn_out": gain(ks[11], (L, ATTN_DIM)),
        "w_out": nrm(ks[12], (L, MIX_DIM, D), MIX_DIM ** -0.5),
        "g_ffn": gain(ks[13], (L, D)),
        "w_rg": nrm(ks[14], (L, D, N_GROUPS), D ** -0.5),
        "b_rg": nrm(ks[15], (L, N_GROUPS), 0.01),
        "w_re": nrm(ks[16], (L, D, N_EXPERTS), D ** -0.5),
        "b_re": nrm(ks[17], (L, N_EXPERTS), 0.01),
        "w_gate": nrm(ks[18], (L, N_EXPERTS, D, EXPERT_FF), D ** -0.5),
        "w_up": nrm(ks[19], (L, N_EXPERTS, D, EXPERT_FF), D ** -0.5),
        "w_down": nrm(ks[20], (L, N_EXPERTS, EXPERT_FF, D), EXPERT_FF ** -0.5),
        "w_ada_f": nrm(ks[21], (D, 2 * D), 0.5 * D ** -0.5),
        "b_ada_f": nrm(ks[22], (2 * D,), 0.02),
        "g_final": gain(ks[23], (D,)),
    }


def reference(x, c, w_ada, b_ada, g_mix, w_in, conv_w, w_uk, kv_norm_g, w_uv, g_conv_out, g_attn_out,
              w_out, g_ffn, w_rg, b_rg, w_re, b_re, w_gate, w_up, w_down, w_ada_f, b_ada_f, g_final):
    B, S, D = x.shape
    c_act = jax.nn.silu(c)
    for l in range(DEPTH):
        mod = (c_act @ w_ada[l] + b_ada[l])[:, None, :]
        sh1, sc1, gt1, sh2, sc2, gt2 = jnp.split(mod, N_MOD, axis=-1)

        h = rmsnorm(x, g_mix[l]) * (1.0 + sc1) + sh1
        proj = h @ w_in[l]

        bg = proj[..., OFF_B:OFF_C]
        cg = proj[..., OFF_C:OFF_X]
        xv = proj[..., OFF_X:OFF_Q]
        y_conv = bg * causal_dwconv(cg * xv, conv_w[l])

        q = proj[..., OFF_Q:OFF_KV].reshape(B, S, N_HEADS, HEAD_DIM)
        ckv = rmsnorm(proj[..., OFF_KV:OFF_IQ], kv_norm_g[l])
        iq = proj[..., OFF_IQ:OFF_IK].reshape(B, S, IDX_HEADS, IDX_DIM)
        ik = proj[..., OFF_IK:OFF_IW]
        iw = proj[..., OFF_IW:IN_COLS] * (IDX_HEADS ** -0.5 * IDX_DIM ** -0.5)
        q_lat = jnp.einsum('bshd,hrd->bshr', q, w_uk[l])
        o_lat = dsa_attention(q_lat, iq, iw, ik, ckv)
        y_attn = jnp.einsum('bshr,hrd->bshd', o_lat, w_uv[l]).reshape(B, S, ATTN_DIM)

        mix = jnp.concatenate([rmsnorm(y_conv, g_conv_out[l]), rmsnorm(y_attn, g_attn_out[l])], axis=-1)
        x = x + gt1 * (mix @ w_out[l])

        h2 = rmsnorm(x, g_ffn[l]) * (1.0 + sc2) + sh2
        x = x + gt2 * hier_moe(h2, w_rg[l], b_rg[l], w_re[l], b_re[l], w_gate[l], w_up[l], w_down[l])

    modf = (c_act @ w_ada_f + b_ada_f)[:, None, :]
    shf, scf = jnp.split(modf, 2, axis=-1)
    return rmsnorm(x, g_final) * (1.0 + scf) + shf
```

```python
import functools

import jax
import jax.numpy as jnp
from jax import lax
from jax.experimental import pallas as pl
from jax.experimental.pallas import tpu as pltpu

D_MODEL = 2048
CONV_DIM = 1024
CONV_WIDTH = 3
N_HEADS = 8
HEAD_DIM = 128
ATTN_DIM = N_HEADS * HEAD_DIM
KV_RANK = 512
IDX_HEADS = 16
IDX_DIM = 128
TOPK_MAX = 256
N_GROUPS = 4
EXPERTS_PER_GROUP = 8
N_EXPERTS = N_GROUPS * EXPERTS_PER_GROUP
EXPERT_FF = 512
N_MOD = 6
EPS = 1e-6

OFF_Q = 3 * CONV_DIM
OFF_KV = OFF_Q + ATTN_DIM
OFF_IQ = OFF_KV + KV_RANK
OFF_IK = OFF_IQ + IDX_HEADS * IDX_DIM
OFF_IW = OFF_IK + IDX_DIM
IN_COLS = OFF_IW + IDX_HEADS

LANES = 128
SUBLANES = 8
VMEM_LIMIT = 56 * 1024 * 1024

PROJ_TN = 512
PROJ_COLS = 14 * PROJ_TN
Q_BLOCK = 128
ROUTE_LANES = 128
INT_MIN = -2 ** 31
KEY_NEG_INF = INT_MIN + 0x7FFFFF
ATTN_KEY_STEP = 256


def _cparams(sem):
    return pltpu.CompilerParams(dimension_semantics=sem, vmem_limit_bytes=VMEM_LIMIT)


def _rms(v, axis=-1):
    return v * lax.rsqrt(jnp.mean(v * v, axis=axis, keepdims=True) + EPS)


def _mod_kernel(c_ref, w_ref, b_ref, o_ref):
    c = c_ref[...]
    ca = (c * jax.nn.sigmoid(c)).astype(jnp.bfloat16)
    o_ref[...] = jnp.dot(ca, w_ref[...].astype(jnp.bfloat16),
                         preferred_element_type=jnp.float32) + b_ref[...]


def _mod(c_pad, w, b, tn=1024):
    d, n = w.shape
    return pl.pallas_call(
        _mod_kernel,
        grid=(n // tn,),
        in_specs=[pl.BlockSpec((c_pad.shape[0], d), lambda j: (0, 0)),
                  pl.BlockSpec((d, tn), lambda j: (0, j)),
                  pl.BlockSpec((1, tn), lambda j: (0, j))],
        out_specs=pl.BlockSpec((c_pad.shape[0], tn), lambda j: (0, j)),
        out_shape=jax.ShapeDtypeStruct((c_pad.shape[0], n), jnp.float32),
        compiler_params=_cparams(("arbitrary",)),
        name="mod",
    )(c_pad, w, b.reshape(1, n))


def _proj_kernel(x_ref, g_ref, sc_ref, sh_ref, w_ref,
                 conv_ref, q_ref, kv_ref, iq_ref, ikw_ref, h_scr):
    n = pl.program_id(1)

    @pl.when(n == 0)
    def _():
        h = _rms(x_ref[...]) * g_ref[...] * (1.0 + sc_ref[0]) + sh_ref[0]
        h_scr[...] = h.astype(jnp.bfloat16)

    acc = jnp.dot(h_scr[...], w_ref[...], preferred_element_type=jnp.float32)

    @pl.when(n < 6)
    def _():
        conv_ref[...] = acc

    @pl.when((n >= 6) & (n < 8))
    def _():
        q_ref[...] = acc.astype(jnp.bfloat16)

    @pl.when(n == 8)
    def _():
        kv_ref[...] = acc

    @pl.when((n >= 9) & (n < 13))
    def _():
        iq_ref[...] = acc.astype(jnp.bfloat16)

    @pl.when(n == 13)
    def _():
        ikw_ref[...] = acc


def _proj(x2, g_mix, sc1, sh1, w_in_bf, seq, tm=1024):
    t, d = x2.shape
    tn = PROJ_TN
    per_b = seq // tm
    clipn = lambda n, lo, hi: jnp.clip(n - lo, 0, hi - lo)
    return pl.pallas_call(
        _proj_kernel,
        grid=(t // tm, PROJ_COLS // tn),
        in_specs=[pl.BlockSpec((tm, d), lambda m, n: (m, 0)),
                  pl.BlockSpec((1, d), lambda m, n: (0, 0)),
                  pl.BlockSpec((1, 1, d), lambda m, n: (m // per_b, 0, 0)),
                  pl.BlockSpec((1, 1, d), lambda m, n: (m // per_b, 0, 0)),
                  pl.BlockSpec((d, tn), lambda m, n: (0, n))],
        out_specs=[pl.BlockSpec((tm, tn), lambda m, n: (m, clipn(n, 0, 5))),
                   pl.BlockSpec((tm, tn), lambda m, n: (m, clipn(n, 6, 7))),
                   pl.BlockSpec((tm, tn), lambda m, n: (m, 0)),
                   pl.BlockSpec((tm, tn), lambda m, n: (m, clipn(n, 9, 12))),
                   pl.BlockSpec((tm, tn), lambda m, n: (m, 0))],
        out_shape=[jax.ShapeDtypeStruct((t, 3 * CONV_DIM), jnp.float32),
                   jax.ShapeDtypeStruct((t, ATTN_DIM), jnp.bfloat16),
                   jax.ShapeDtypeStruct((t, KV_RANK), jnp.float32),
                   jax.ShapeDtypeStruct((t, IDX_HEADS * IDX_DIM), jnp.bfloat16),
                   jax.ShapeDtypeStruct((t, tn), jnp.float32)],
        scratch_shapes=[pltpu.VMEM((tm, d), jnp.bfloat16)],
        compiler_params=_cparams(("arbitrary", "arbitrary")),
        name="proj",
    )(x2, g_mix, sc1, sh1, w_in_bf)


def _conv_kernel(bg_ref, cg_ref, xv_ref, cgh_ref, xvh_ref, w_ref, g_ref, o_ref):
    i = pl.program_id(1)
    u = cg_ref[0] * xv_ref[0]
    halo = cgh_ref[0] * xvh_ref[0]
    halo = jnp.where(i == 0, 0.0, halo)
    row = lax.broadcasted_iota(jnp.int32, u.shape, 0)
    h1 = halo[SUBLANES - 1:SUBLANES, :]
    h2 = halo[SUBLANES - 2:SUBLANES - 1, :]
    u1 = jnp.where(row == 0, h1, pltpu.roll(u, 1, 0))
    u2 = jnp.where(row == 0, h2, jnp.where(row == 1, h1, pltpu.roll(u, 2, 0)))
    w = w_ref[...]
    y = bg_ref[0] * (w[0:1, :] * u2 + w[1:2, :] * u1 + w[2:3, :] * u)
    o_ref[0] = (_rms(y) * g_ref[...]).astype(o_ref.dtype)


def _conv(conv3, conv_w, g_conv_out, tm=512):
    b, s, _ = conv3.shape
    c = CONV_DIM
    hb = tm // SUBLANES
    halo_map = lambda col: (lambda bi, i: (bi, jnp.maximum(i * hb - 1, 0), col))
    return pl.pallas_call(
        _conv_kernel,
        grid=(b, s // tm),
        in_specs=[pl.BlockSpec((1, tm, c), lambda bi, i: (bi, i, 0)),
                  pl.BlockSpec((1, tm, c), lambda bi, i: (bi, i, 1)),
                  pl.BlockSpec((1, tm, c), lambda bi, i: (bi, i, 2)),
                  pl.BlockSpec((1, SUBLANES, c), halo_map(1)),
                  pl.BlockSpec((1, SUBLANES, c), halo_map(2)),
                  pl.BlockSpec((CONV_WIDTH, c), lambda bi, i: (0, 0)),
                  pl.BlockSpec((1, c), lambda bi, i: (0, 0))],
        out_specs=pl.BlockSpec((1, tm, c), lambda bi, i: (bi, i, 0)),
        out_shape=jax.ShapeDtypeStruct((b, s, c), jnp.bfloat16),
        compiler_params=_cparams(("arbitrary", "arbitrary")),
        name="conv",
    )(conv3, conv3, conv3, conv3, conv3, conv_w, g_conv_out)


def _ordered_bits(v):
    return v ^ ((v >> 31) & jnp.int32(0x7FFFFFFF))


def _attn_block(nk, j, iq_ref, ikwq_ref, q_ref, wuk_ref, wuv_ref, go_ref, o_ref,
                ckv_scr, ik_scr, sc_scr, y_scr, k_sel):
    nt = (((1,), (1,)), ((), ()))
    iw = ikwq_ref[0][:, IDX_DIM:IDX_DIM + IDX_HEADS] * (IDX_HEADS ** -0.5 * IDX_DIM ** -0.5)
    ik = ik_scr[0:nk, :]
    score = jnp.zeros((Q_BLOCK, nk), jnp.float32)
    for h in range(IDX_HEADS):
        s = lax.dot_general(iq_ref[0][:, h * IDX_DIM:(h + 1) * IDX_DIM], ik, nt,
                            preferred_element_type=jnp.float32)
        score = score + jnp.maximum(s, 0.0) * iw[:, h:h + 1]

    kpos = lax.broadcasted_iota(jnp.int32, (Q_BLOCK, nk), 1)
    qpos = j * Q_BLOCK + lax.broadcasted_iota(jnp.int32, (Q_BLOCK, nk), 0)
    causal = kpos <= qpos
    sc_scr[:, 0:nk] = jnp.where(causal, score, -jnp.inf)

    def bit_step(i, thr):
        cand = thr + lax.shift_left(jnp.int32(1), 31 - i)
        cand_f = lax.bitcast_convert_type(_ordered_bits(cand), jnp.float32)
        cnt = jnp.sum(jnp.where(sc_scr[:, 0:nk] >= cand_f, 1.0, 0.0), axis=1, keepdims=True)
        cnt = jnp.where(cand < KEY_NEG_INF, float(nk), cnt)
        return jnp.where(cnt >= k_sel, cand, thr)

    thr = lax.fori_loop(0, 32, bit_step, jnp.full((Q_BLOCK, 1), INT_MIN, jnp.int32))
    thr_f = lax.bitcast_convert_type(_ordered_bits(thr), jnp.float32)
    sel = (sc_scr[:, 0:nk] >= thr_f) & causal

    scale = HEAD_DIM ** -0.5
    ckv = ckv_scr[0:nk, :]
    for h in range(N_HEADS):
        qh = q_ref[0][:, h * HEAD_DIM:(h + 1) * HEAD_DIM]
        ql = lax.dot_general(qh, wuk_ref[h], nt, preferred_element_type=jnp.float32)
        lg = lax.dot_general(ql.astype(jnp.bfloat16), ckv, nt,
                             preferred_element_type=jnp.float32) * scale
        lg = jnp.where(sel, lg, -jnp.inf)
        m = jnp.max(lg, axis=1, keepdims=True)
        p = jnp.exp(lg - m)
        l = jnp.sum(p, axis=1, keepdims=True)
        o = jnp.dot(p.astype(jnp.bfloat16), ckv, preferred_element_type=jnp.float32) / l
        y_scr[:, h * HEAD_DIM:(h + 1) * HEAD_DIM] = jnp.dot(
            o.astype(jnp.bfloat16), wuv_ref[h], preferred_element_type=jnp.float32)

    o_ref[0] = (_rms(y_scr[...]) * go_ref[...]).astype(o_ref.dtype)


def _attn_kernel(iq_ref, ikwq_ref, ikwa_ref, kv_ref, q_ref, wuk_ref, wuv_ref, kvg_ref, go_ref,
                 o_ref, ckv_scr, ik_scr, sc_scr, y_scr, *, seq, k_sel):
    j = pl.program_id(1)

    @pl.when(j == 0)
    def _():
        ckv_scr[...] = (_rms(kv_ref[0]) * kvg_ref[...]).astype(jnp.bfloat16)
        ik_scr[...] = ikwa_ref[0][:, :IDX_DIM].astype(jnp.bfloat16)

    per = ATTN_KEY_STEP // Q_BLOCK
    for v in range(seq // ATTN_KEY_STEP):
        @pl.when(j // per == v)
        def _(v=v):
            _attn_block((v + 1) * ATTN_KEY_STEP, j, iq_ref, ikwq_ref, q_ref, wuk_ref, wuv_ref,
                        go_ref, o_ref, ckv_scr, ik_scr, sc_scr, y_scr, k_sel)


def _attn(iq, ikw, kv, q, w_uk_bf, w_uv_bf, kv_norm_g, g_attn_out):
    b, s, _ = iq.shape
    k_sel = min(TOPK_MAX, s // 4)
    kern = functools.partial(_attn_kernel, seq=s, k_sel=k_sel)
    return pl.pallas_call(
        kern,
        grid=(b, s // Q_BLOCK),
        in_specs=[pl.BlockSpec((1, Q_BLOCK, IDX_HEADS * IDX_DIM), lambda bi, j: (bi, j, 0)),
                  pl.BlockSpec((1, Q_BLOCK, PROJ_TN), lambda bi, j: (bi, j, 0)),
                  pl.BlockSpec((1, s, PROJ_TN), lambda bi, j: (bi, 0, 0)),
                  pl.BlockSpec((1, s, KV_RANK), lambda bi, j: (bi, 0, 0)),
                  pl.BlockSpec((1, Q_BLOCK, ATTN_DIM), lambda bi, j: (bi, j, 0)),
                  pl.BlockSpec((N_HEADS, KV_RANK, HEAD_DIM), lambda bi, j: (0, 0, 0)),
                  pl.BlockSpec((N_HEADS, KV_RANK, HEAD_DIM), lambda bi, j: (0, 0, 0)),
                  pl.BlockSpec((1, KV_RANK), lambda bi, j: (0, 0)),
                  pl.BlockSpec((1, ATTN_DIM), lambda bi, j: (0, 0))],
        out_specs=pl.BlockSpec((1, Q_BLOCK, ATTN_DIM), lambda bi, j: (bi, j, 0)),
        out_shape=jax.ShapeDtypeStruct((b, s, ATTN_DIM), jnp.bfloat16),
        scratch_shapes=[pltpu.VMEM((s, KV_RANK), jnp.bfloat16),
                        pltpu.VMEM((s, IDX_DIM), jnp.bfloat16),
                        pltpu.VMEM((Q_BLOCK, s), jnp.float32),
                        pltpu.VMEM((Q_BLOCK, ATTN_DIM), jnp.float32)],
        compiler_params=_cparams(("arbitrary", "arbitrary")),
        name="attn",
    )(iq, ikw, ikw, kv, q, w_uk_bf, w_uv_bf, kv_norm_g, g_attn_out)


def _out_kernel(mc_ref, ma_ref, wc_ref, wa_ref, x_ref, gt_ref, g_ref, sc_ref, sh_ref, wr_ref, br_ref,
                x1_ref, h2_ref, comb_ref):
    mix = (jnp.dot(mc_ref[...], wc_ref[...], preferred_element_type=jnp.float32)
           + jnp.dot(ma_ref[...], wa_ref[...], preferred_element_type=jnp.float32))
    x1 = x_ref[...] + gt_ref[0] * mix
    x1_ref[...] = x1
    h2 = (_rms(x1) * g_ref[...] * (1.0 + sc_ref[0]) + sh_ref[0]).astype(jnp.bfloat16)
    h2_ref[...] = h2

    logit = jnp.dot(h2, wr_ref[...], preferred_element_type=jnp.float32) + br_ref[...]
    lane = lax.broadcasted_iota(jnp.int32, logit.shape, 1).astype(jnp.float32)
    neg = -jnp.inf
    big = float(ROUTE_LANES)
    is_g = lane < N_GROUPS
    gl = jnp.where(is_g, logit, neg)
    gmax = jnp.max(gl, axis=1, keepdims=True)
    p_group = 1.0 / jnp.sum(jnp.exp(gl - gmax), axis=1, keepdims=True)
    g_sel = jnp.min(jnp.where(is_g & (gl == gmax), lane, big), axis=1, keepdims=True)
    lo = N_GROUPS + g_sel * EXPERTS_PER_GROUP
    in_grp = (lane >= lo) & (lane < lo + EXPERTS_PER_GROUP)
    el = jnp.where(in_grp, logit, neg)
    m1 = jnp.max(el, axis=1, keepdims=True)
    i1 = jnp.min(jnp.where(in_grp & (el == m1), lane, big), axis=1, keepdims=True)
    el2 = jnp.where(lane == i1, neg, el)
    m2 = jnp.max(el2, axis=1, keepdims=True)
    i2 = jnp.min(jnp.where(in_grp & (lane != i1) & (el2 == m2), lane, big), axis=1, keepdims=True)
    r = jnp.exp(m2 - m1)
    w1 = p_group / (1.0 + r)
    w2 = p_group * r / (1.0 + r)
    comb_ref[...] = jnp.where(lane == i1, w1, jnp.where(lane == i2, w2, 0.0))


def _out(mix_c, mix_a, w_out_c, w_out_a, x2, gt1, g_ffn, sc2, sh2, w_route, b_route, seq, tm=512):
    t, d = x2.shape
    per_b = seq // tm
    bmap = lambda m: (m // per_b, 0, 0)
    return pl.pallas_call(
        _out_kernel,
        grid=(t // tm,),
        in_specs=[pl.BlockSpec((tm, CONV_DIM), lambda m: (m, 0)),
                  pl.BlockSpec((tm, ATTN_DIM), lambda m: (m, 0)),
                  pl.BlockSpec((CONV_DIM, d), lambda m: (0, 0)),
                  pl.BlockSpec((ATTN_DIM, d), lambda m: (0, 0)),
                  pl.BlockSpec((tm, d), lambda m: (m, 0)),
                  pl.BlockSpec((1, 1, d), bmap),
                  pl.BlockSpec((1, d), lambda m: (0, 0)),
                  pl.BlockSpec((1, 1, d), bmap),
                  pl.BlockSpec((1, 1, d), bmap),
                  pl.BlockSpec((d, ROUTE_LANES), lambda m: (0, 0)),
                  pl.BlockSpec((1, ROUTE_LANES), lambda m: (0, 0))],
        out_specs=[pl.BlockSpec((tm, d), lambda m: (m, 0)),
                   pl.BlockSpec((tm, d), lambda m: (m, 0)),
                   pl.BlockSpec((tm, ROUTE_LANES), lambda m: (m, 0))],
        out_shape=[jax.ShapeDtypeStruct((t, d), jnp.float32),
                   jax.ShapeDtypeStruct((t, d), jnp.bfloat16),
                   jax.ShapeDtypeStruct((t, ROUTE_LANES), jnp.float32)],
        compiler_params=_cparams(("arbitrary",)),
        name="out",
    )(mix_c, mix_a, w_out_c, w_out_a, x2, gt1, g_ffn, sc2, sh2, w_route, b_route)


def _moe_kernel(h_ref, comb_ref, wg_ref, wu_ref, wd_ref, o_ref):
    e = pl.program_id(1)

    @pl.when(e == 0)
    def _():
        o_ref[...] = jnp.zeros_like(o_ref)

    h = h_ref[...]
    a = jnp.dot(h, wg_ref[0], preferred_element_type=jnp.float32)
    u = jnp.dot(h, wu_ref[0], preferred_element_type=jnp.float32)
    comb = comb_ref[...]
    lane = lax.broadcasted_iota(jnp.int32, comb.shape, 1)
    cw = jnp.sum(jnp.where(lane == e + N_GROUPS, comb, 0.0), axis=1, keepdims=True)
    hid = (a * jax.nn.sigmoid(a)) * u * cw
    o_ref[...] += jnp.dot(hid.astype(jnp.bfloat16), wd_ref[0], preferred_element_type=jnp.float32)


def _moe(h2, comb, wg, wu, wd, tm=1024):
    t, d = h2.shape
    return pl.pallas_call(
        _moe_kernel,
        grid=(t // tm, N_EXPERTS),
        in_specs=[pl.BlockSpec((tm, d), lambda m, e: (m, 0)),
                  pl.BlockSpec((tm, ROUTE_LANES), lambda m, e: (m, 0)),
                  pl.BlockSpec((1, d, EXPERT_FF), lambda m, e: (e, 0, 0)),
                  pl.BlockSpec((1, d, EXPERT_FF), lambda m, e: (e, 0, 0)),
                  pl.BlockSpec((1, EXPERT_FF, d), lambda m, e: (e, 0, 0))],
        out_specs=pl.BlockSpec((tm, d), lambda m, e: (m, 0)),
        out_shape=jax.ShapeDtypeStruct((t, d), jnp.float32),
        compiler_params=_cparams(("arbitrary", "arbitrary")),
        name="moe",
    )(h2, comb, wg, wu, wd)


def _final_kernel(x1_ref, y_ref, gt_ref, g_ref, sc_ref, sh_ref, o_ref):
    x = x1_ref[...] + gt_ref[0] * y_ref[...]
    o_ref[...] = _rms(x) * g_ref[...] * (1.0 + sc_ref[0]) + sh_ref[0]


def _final(x1, y, gt2, g_final, scf, shf, seq, tm=512):
    t, d = x1.shape
    per_b = seq // tm
    bmap = lambda m: (m // per_b, 0, 0)
    return pl.pallas_call(
        _final_kernel,
        grid=(t // tm,),
        in_specs=[pl.BlockSpec((tm, d), lambda m: (m, 0)),
                  pl.BlockSpec((tm, d), lambda m: (m, 0)),
                  pl.BlockSpec((1, 1, d), bmap),
                  pl.BlockSpec((1, d), lambda m: (0, 0)),
                  pl.BlockSpec((1, 1, d), bmap),
                  pl.BlockSpec((1, 1, d), bmap)],
        out_specs=pl.BlockSpec((tm, d), lambda m: (m, 0)),
        out_shape=jax.ShapeDtypeStruct((t, d), jnp.float32),
        compiler_params=_cparams(("arbitrary",)),
        name="final",
    )(x1, y, gt2, g_final, scf, shf)


def kernel(x, c, w_ada, b_ada, g_mix, w_in, conv_w, w_uk, kv_norm_g, w_uv, g_conv_out, g_attn_out,
           w_out, g_ffn, w_rg, b_rg, w_re, b_re, w_gate, w_up, w_down, w_ada_f, b_ada_f, g_final):
    b, s, d = x.shape
    assert w_ada.shape[0] == 1, "single layer"
    bf = jnp.bfloat16
    x2 = x.reshape(b * s, d)

    c_pad = jnp.zeros((SUBLANES, d), jnp.float32).at[:b].set(c)
    mod = _mod(c_pad, w_ada[0], b_ada[0])[:b]
    modf = _mod(c_pad, w_ada_f, b_ada_f)[:b]
    vec = lambda a, i: a[:, i * d:(i + 1) * d].reshape(b, 1, d)
    sh1, sc1, gt1, sh2, sc2, gt2 = (vec(mod, i) for i in range(N_MOD))
    shf, scf = vec(modf, 0), vec(modf, 1)
    row = lambda a: a.reshape(1, -1)

    w_in_bf = jnp.pad(w_in[0].astype(bf), ((0, 0), (0, PROJ_COLS - IN_COLS)))
    conv3, q, kv, iq, ikw = _proj(x2, row(g_mix[0]), sc1, sh1, w_in_bf, s)

    mix_c = _conv(conv3.reshape(b, s, 3 * CONV_DIM), conv_w[0], row(g_conv_out[0]))
    mix_a = _attn(iq.reshape(b, s, -1), ikw.reshape(b, s, -1), kv.reshape(b, s, -1),
                  q.reshape(b, s, -1), w_uk[0].astype(bf), w_uv[0].astype(bf),
                  row(kv_norm_g[0]), row(g_attn_out[0]))

    w_out_bf = w_out[0].astype(bf)
    w_route = jnp.zeros((d, ROUTE_LANES), bf).at[:, :N_GROUPS].set(w_rg[0].astype(bf))
    w_route = w_route.at[:, N_GROUPS:N_GROUPS + N_EXPERTS].set(w_re[0].astype(bf))
    b_route = jnp.zeros((1, ROUTE_LANES), jnp.float32).at[0, :N_GROUPS].set(b_rg[0])
    b_route = b_route.at[0, N_GROUPS:N_GROUPS + N_EXPERTS].set(b_re[0])
    x1, h2, comb = _out(mix_c.reshape(b * s, -1), mix_a.reshape(b * s, -1),
                        w_out_bf[:CONV_DIM], w_out_bf[CONV_DIM:], x2, gt1, row(g_ffn[0]),
                        sc2, sh2, w_route, b_route, s)

    y = _moe(h2, comb, w_gate[0].astype(bf), w_up[0].astype(bf), w_down[0].astype(bf))
    out = _final(x1, y, gt2, row(g_final), scf, shf, s)
    return out.reshape(b, s, d)
```

```python
import functools

import jax
import jax.numpy as jnp
from jax import lax
from jax.experimental import pallas as pl
from jax.experimental.pallas import tpu as pltpu

D_MODEL = 2048
CONV_DIM = 1024
CONV_WIDTH = 3
N_HEADS = 8
HEAD_DIM = 128
ATTN_DIM = N_HEADS * HEAD_DIM
KV_RANK = 512
IDX_HEADS = 16
IDX_DIM = 128
TOPK_MAX = 256
N_GROUPS = 4
EXPERTS_PER_GROUP = 8
N_EXPERTS = N_GROUPS * EXPERTS_PER_GROUP
EXPERT_FF = 512
N_MOD = 6
EPS = 1e-6

OFF_Q = 3 * CONV_DIM
OFF_KV = OFF_Q + ATTN_DIM
OFF_IQ = OFF_KV + KV_RANK
OFF_IK = OFF_IQ + IDX_HEADS * IDX_DIM
OFF_IW = OFF_IK + IDX_DIM
IN_COLS = OFF_IW + IDX_HEADS

LANES = 128
SUBLANES = 8
VMEM_LIMIT = 56 * 1024 * 1024

PROJ_TN = 512
PROJ_COLS = 14 * PROJ_TN
ATTN_Q = 256
ATTN_KC = 256
MASKED = -1e30
ROUTE_LANES = 128
MOE_TM = 256
INT_MIN = -2 ** 31
KEY_NEG_INF = INT_MIN + 0x7FFFFF


def _cparams(sem):
    return pltpu.CompilerParams(dimension_semantics=sem, vmem_limit_bytes=VMEM_LIMIT)


def _rms(v, axis=-1):
    return v * lax.rsqrt(jnp.mean(v * v, axis=axis, keepdims=True) + EPS)


def _mod_kernel(c_ref, w_ref, b_ref, o_ref):
    c = c_ref[...]
    ca = (c * jax.nn.sigmoid(c)).astype(jnp.bfloat16)
    o_ref[...] = jnp.dot(ca, w_ref[...].astype(jnp.bfloat16),
                         preferred_element_type=jnp.float32) + b_ref[...]


def _mod(c_pad, w, b, tn=1024):
    d, n = w.shape
    return pl.pallas_call(
        _mod_kernel,
        grid=(n // tn,),
        in_specs=[pl.BlockSpec((c_pad.shape[0], d), lambda j: (0, 0)),
                  pl.BlockSpec((d, tn), lambda j: (0, j)),
                  pl.BlockSpec((1, tn), lambda j: (0, j))],
        out_specs=pl.BlockSpec((c_pad.shape[0], tn), lambda j: (0, j)),
        out_shape=jax.ShapeDtypeStruct((c_pad.shape[0], n), jnp.float32),
        compiler_params=_cparams(("arbitrary",)),
        name="mod",
    )(c_pad, w, b.reshape(1, n))


def _proj_kernel(x_ref, g_ref, sc_ref, sh_ref, w_ref,
                 conv_ref, q_ref, kv_ref, iq_ref, ikw_ref, h_scr):
    n = pl.program_id(1)

    @pl.when(n == 0)
    def _():
        h = _rms(x_ref[...]) * g_ref[...] * (1.0 + sc_ref[0]) + sh_ref[0]
        h_scr[...] = h.astype(jnp.bfloat16)

    acc = jnp.dot(h_scr[...], w_ref[...], preferred_element_type=jnp.float32)

    @pl.when(n < 6)
    def _():
        conv_ref[...] = acc

    @pl.when((n >= 6) & (n < 8))
    def _():
        q_ref[...] = acc.astype(jnp.bfloat16)

    @pl.when(n == 8)
    def _():
        kv_ref[...] = acc

    @pl.when((n >= 9) & (n < 13))
    def _():
        iq_ref[...] = acc.astype(jnp.bfloat16)

    @pl.when(n == 13)
    def _():
        ikw_ref[...] = acc


def _proj(x2, g_mix, sc1, sh1, w_in_bf, seq, tm=1024):
    t, d = x2.shape
    tn = PROJ_TN
    per_b = seq // tm
    clipn = lambda n, lo, hi: jnp.clip(n - lo, 0, hi - lo)
    return pl.pallas_call(
        _proj_kernel,
        grid=(t // tm, PROJ_COLS // tn),
        in_specs=[pl.BlockSpec((tm, d), lambda m, n: (m, 0)),
                  pl.BlockSpec((1, d), lambda m, n: (0, 0)),
                  pl.BlockSpec((1, 1, d), lambda m, n: (m // per_b, 0, 0)),
                  pl.BlockSpec((1, 1, d), lambda m, n: (m // per_b, 0, 0)),
                  pl.BlockSpec((d, tn), lambda m, n: (0, n))],
        out_specs=[pl.BlockSpec((tm, tn), lambda m, n: (m, clipn(n, 0, 5))),
                   pl.BlockSpec((tm, tn), lambda m, n: (m, clipn(n, 6, 7))),
                   pl.BlockSpec((tm, tn), lambda m, n: (m, 0)),
                   pl.BlockSpec((tm, tn), lambda m, n: (m, clipn(n, 9, 12))),
                   pl.BlockSpec((tm, tn), lambda m, n: (m, 0))],
        out_shape=[jax.ShapeDtypeStruct((t, 3 * CONV_DIM), jnp.float32),
                   jax.ShapeDtypeStruct((t, ATTN_DIM), jnp.bfloat16),
                   jax.ShapeDtypeStruct((t, KV_RANK), jnp.float32),
                   jax.ShapeDtypeStruct((t, IDX_HEADS * IDX_DIM), jnp.bfloat16),
                   jax.ShapeDtypeStruct((t, tn), jnp.float32)],
        scratch_shapes=[pltpu.VMEM((tm, d), jnp.bfloat16)],
        compiler_params=_cparams(("arbitrary", "arbitrary")),
        name="proj",
    )(x2, g_mix, sc1, sh1, w_in_bf)


def _conv_kernel(bg_ref, cg_ref, xv_ref, cgh_ref, xvh_ref, w_ref, g_ref, o_ref):
    i = pl.program_id(1)
    u = cg_ref[0] * xv_ref[0]
    halo = cgh_ref[0] * xvh_ref[0]
    halo = jnp.where(i == 0, 0.0, halo)
    row = lax.broadcasted_iota(jnp.int32, u.shape, 0)
    h1 = halo[SUBLANES - 1:SUBLANES, :]
    h2 = halo[SUBLANES - 2:SUBLANES - 1, :]
    u1 = jnp.where(row == 0, h1, pltpu.roll(u, 1, 0))
    u2 = jnp.where(row == 0, h2, jnp.where(row == 1, h1, pltpu.roll(u, 2, 0)))
    w = w_ref[...]
    y = bg_ref[0] * (w[0:1, :] * u2 + w[1:2, :] * u1 + w[2:3, :] * u)
    o_ref[0] = (_rms(y) * g_ref[...]).astype(o_ref.dtype)


def _conv(conv3, conv_w, g_conv_out, tm=512):
    b, s, _ = conv3.shape
    c = CONV_DIM
    hb = tm // SUBLANES
    halo_map = lambda col: (lambda bi, i: (bi, jnp.maximum(i * hb - 1, 0), col))
    return pl.pallas_call(
        _conv_kernel,
        grid=(b, s // tm),
        in_specs=[pl.BlockSpec((1, tm, c), lambda bi, i: (bi, i, 0)),
                  pl.BlockSpec((1, tm, c), lambda bi, i: (bi, i, 1)),
                  pl.BlockSpec((1, tm, c), lambda bi, i: (bi, i, 2)),
                  pl.BlockSpec((1, SUBLANES, c), halo_map(1)),
                  pl.BlockSpec((1, SUBLANES, c), halo_map(2)),
                  pl.BlockSpec((CONV_WIDTH, c), lambda bi, i: (0, 0)),
                  pl.BlockSpec((1, c), lambda bi, i: (0, 0))],
        out_specs=pl.BlockSpec((1, tm, c), lambda bi, i: (bi, i, 0)),
        out_shape=jax.ShapeDtypeStruct((b, s, c), jnp.bfloat16),
        compiler_params=_cparams(("arbitrary", "arbitrary")),
        name="conv",
    )(conv3, conv3, conv3, conv3, conv3, conv_w, g_conv_out)


def _ordered_bits(v):
    return v ^ ((v >> 31) & jnp.int32(0x7FFFFFFF))


def _attn_search(nch, sc_scr, k_sel):
    n_keys = float(nch * ATTN_KC)

    def bit_step(i, thr):
        cand = thr + lax.shift_left(jnp.int32(1), 31 - i)
        cand_f = lax.bitcast_convert_type(_ordered_bits(cand), jnp.float32)
        hits = jnp.where(sc_scr[0] >= cand_f, 1.0, 0.0)
        for c in range(1, nch):
            hits = hits + jnp.where(sc_scr[c] >= cand_f, 1.0, 0.0)
        cnt = jnp.sum(hits, axis=1, keepdims=True)
        cnt = jnp.where(cand < KEY_NEG_INF, n_keys, cnt)
        return jnp.where(cnt >= k_sel, cand, thr)

    thr = lax.fori_loop(0, 32, bit_step, jnp.full((ATTN_Q, 1), INT_MIN, jnp.int32))
    return lax.bitcast_convert_type(_ordered_bits(thr), jnp.float32)


def _attn_kernel(iq_ref, ikwq_ref, ikwa_ref, kv_ref, q_ref, wuk_ref, wuv_ref, kvg_ref, go_ref,
                 o_ref, ckv_scr, ik_scr, sc_scr, thr_scr, ql_scr, m_scr, l_scr, acc_scr, y_scr,
                 *, seq, k_sel):
    j = pl.program_id(1)
    nch = j + 1
    nt = (((1,), (1,)), ((), ()))

    @pl.when(j == 0)
    def _():
        ckv_scr[...] = (_rms(kv_ref[0]) * kvg_ref[...]).astype(jnp.bfloat16)
        ik_scr[...] = ikwa_ref[0][:, :IDX_DIM].astype(jnp.bfloat16)

    iw = ikwq_ref[0][:, IDX_DIM:IDX_DIM + IDX_HEADS] * (IDX_HEADS ** -0.5 * IDX_DIM ** -0.5)
    qpos = j * ATTN_Q + lax.broadcasted_iota(jnp.int32, (ATTN_Q, ATTN_KC), 0)
    kloc = lax.broadcasted_iota(jnp.int32, (ATTN_Q, ATTN_KC), 1)

    def score_chunk(c, carry):
        k0 = pl.multiple_of(c * ATTN_KC, ATTN_KC)
        ik_c = ik_scr[pl.ds(k0, ATTN_KC), :]
        score = jnp.zeros((ATTN_Q, ATTN_KC), jnp.float32)
        for h in range(IDX_HEADS):
            s = lax.dot_general(iq_ref[0][:, h * IDX_DIM:(h + 1) * IDX_DIM], ik_c, nt,
                                preferred_element_type=jnp.float32)
            score = score + jnp.maximum(s, 0.0) * iw[:, h:h + 1]
        sc_scr[c] = jnp.where(k0 + kloc <= qpos, score, -jnp.inf)
        return carry

    lax.fori_loop(0, nch, score_chunk, 0)

    for v in range(seq // ATTN_KC):
        @pl.when(j == v)
        def _(v=v):
            thr_scr[...] = _attn_search(v + 1, sc_scr, k_sel)

    for h in range(N_HEADS):
        ql = lax.dot_general(q_ref[0][:, h * HEAD_DIM:(h + 1) * HEAD_DIM], wuk_ref[h], nt,
                             preferred_element_type=jnp.float32)
        ql_scr[h * ATTN_Q:(h + 1) * ATTN_Q, :] = ql.astype(jnp.bfloat16)
    m_scr[...] = jnp.full(m_scr.shape, MASKED, jnp.float32)
    l_scr[...] = jnp.zeros(l_scr.shape, jnp.float32)
    acc_scr[...] = jnp.zeros(acc_scr.shape, jnp.float32)
    thr_f = thr_scr[...]
    scale = HEAD_DIM ** -0.5

    def attn_chunk(c, carry):
        k0 = pl.multiple_of(c * ATTN_KC, ATTN_KC)
        ckv_c = ckv_scr[pl.ds(k0, ATTN_KC), :]
        sel = (sc_scr[c] >= thr_f) & (k0 + kloc <= qpos)
        lg = lax.dot_general(ql_scr[...], ckv_c, nt, preferred_element_type=jnp.float32) * scale
        lg = jnp.where(sel[None], lg.reshape(N_HEADS, ATTN_Q, ATTN_KC), MASKED)
        lg = lg.reshape(N_HEADS * ATTN_Q, ATTN_KC)
        m_old = m_scr[...]
        m_new = jnp.maximum(m_old, jnp.max(lg, axis=1, keepdims=True))
        alpha = jnp.exp(m_old - m_new)
        p = jnp.exp(lg - m_new)
        l_scr[...] = alpha * l_scr[...] + jnp.sum(p, axis=1, keepdims=True)
        acc_scr[...] = alpha * acc_scr[...] + jnp.dot(p.astype(jnp.bfloat16), ckv_c,
                                                      preferred_element_type=jnp.float32)
        m_scr[...] = m_new
        return carry

    lax.fori_loop(0, nch, attn_chunk, 0)

    o = (acc_scr[...] / l_scr[...]).astype(jnp.bfloat16)
    for h in range(N_HEADS):
        y_scr[:, h * HEAD_DIM:(h + 1) * HEAD_DIM] = jnp.dot(
            o[h * ATTN_Q:(h + 1) * ATTN_Q], wuv_ref[h], preferred_element_type=jnp.float32)
    o_ref[0] = (_rms(y_scr[...]) * go_ref[...]).astype(o_ref.dtype)


def _attn(iq, ikw, kv, q, w_uk_bf, w_uv_bf, kv_norm_g, g_attn_out):
    b, s, _ = iq.shape
    assert ATTN_Q == ATTN_KC and s % ATTN_Q == 0
    k_sel = min(TOPK_MAX, s // 4)
    rows = N_HEADS * ATTN_Q
    kern = functools.partial(_attn_kernel, seq=s, k_sel=k_sel)
    return pl.pallas_call(
        kern,
        grid=(b, s // ATTN_Q),
        in_specs=[pl.BlockSpec((1, ATTN_Q, IDX_HEADS * IDX_DIM), lambda bi, j: (bi, j, 0)),
                  pl.BlockSpec((1, ATTN_Q, PROJ_TN), lambda bi, j: (bi, j, 0)),
                  pl.BlockSpec((1, s, PROJ_TN), lambda bi, j: (bi, 0, 0)),
                  pl.BlockSpec((1, s, KV_RANK), lambda bi, j: (bi, 0, 0)),
                  pl.BlockSpec((1, ATTN_Q, ATTN_DIM), lambda bi, j: (bi, j, 0)),
                  pl.BlockSpec((N_HEADS, KV_RANK, HEAD_DIM), lambda bi, j: (0, 0, 0)),
                  pl.BlockSpec((N_HEADS, KV_RANK, HEAD_DIM), lambda bi, j: (0, 0, 0)),
                  pl.BlockSpec((1, KV_RANK), lambda bi, j: (0, 0)),
                  pl.BlockSpec((1, ATTN_DIM), lambda bi, j: (0, 0))],
        out_specs=pl.BlockSpec((1, ATTN_Q, ATTN_DIM), lambda bi, j: (bi, j, 0)),
        out_shape=jax.ShapeDtypeStruct((b, s, ATTN_DIM), jnp.bfloat16),
        scratch_shapes=[pltpu.VMEM((s, KV_RANK), jnp.bfloat16),
                        pltpu.VMEM((s, IDX_DIM), jnp.bfloat16),
                        pltpu.VMEM((s // ATTN_KC, ATTN_Q, ATTN_KC), jnp.float32),
                        pltpu.VMEM((ATTN_Q, 1), jnp.float32),
                        pltpu.VMEM((rows, KV_RANK), jnp.bfloat16),
                        pltpu.VMEM((rows, 1), jnp.float32),
                        pltpu.VMEM((rows, 1), jnp.float32),
                        pltpu.VMEM((rows, KV_RANK), jnp.float32),
                        pltpu.VMEM((ATTN_Q, ATTN_DIM), jnp.float32)],
        compiler_params=_cparams(("arbitrary", "arbitrary")),
        name="attn",
    )(iq, ikw, ikw, kv, q, w_uk_bf, w_uv_bf, kv_norm_g, g_attn_out)


R_E1, R_E2, R_W1, R_W2, R_RANK1, R_RANK2 = range(6)


def _out_kernel(mc_ref, ma_ref, wc_ref, wa_ref, x_ref, gt_ref, g_ref, sc_ref, sh_ref, wr_ref, br_ref,
                x1_ref, h2_ref, route_ref, cnt_ref, cnt_scr):
    @pl.when(pl.program_id(0) == 0)
    def _():
        cnt_scr[...] = jnp.zeros(cnt_scr.shape, jnp.float32)

    mix = (jnp.dot(mc_ref[...], wc_ref[...], preferred_element_type=jnp.float32)
           + jnp.dot(ma_ref[...], wa_ref[...], preferred_element_type=jnp.float32))
    x1 = x_ref[...] + gt_ref[0] * mix
    x1_ref[...] = x1
    h2 = _rms(x1) * g_ref[...] * (1.0 + sc_ref[0]) + sh_ref[0]
    h2_ref[:, 0, :] = h2

    logit = jnp.dot(h2.astype(jnp.bfloat16), wr_ref[...],
                    preferred_element_type=jnp.float32) + br_ref[...]
    lane = lax.broadcasted_iota(jnp.int32, logit.shape, 1).astype(jnp.float32)
    neg = -jnp.inf
    big = float(ROUTE_LANES)
    is_g = lane < N_GROUPS
    gl = jnp.where(is_g, logit, neg)
    gmax = jnp.max(gl, axis=1, keepdims=True)
    p_group = 1.0 / jnp.sum(jnp.exp(gl - gmax), axis=1, keepdims=True)
    g_sel = jnp.min(jnp.where(is_g & (gl == gmax), lane, big), axis=1, keepdims=True)
    lo = N_GROUPS + g_sel * EXPERTS_PER_GROUP
    in_grp = (lane >= lo) & (lane < lo + EXPERTS_PER_GROUP)
    el = jnp.where(in_grp, logit, neg)
    m1 = jnp.max(el, axis=1, keepdims=True)
    i1 = jnp.min(jnp.where(in_grp & (el == m1), lane, big), axis=1, keepdims=True)
    el2 = jnp.where(lane == i1, neg, el)
    m2 = jnp.max(el2, axis=1, keepdims=True)
    i2 = jnp.min(jnp.where(in_grp & (lane != i1) & (el2 == m2), lane, big), axis=1, keepdims=True)
    r = jnp.exp(m2 - m1)
    w1 = p_group / (1.0 + r)
    w2 = p_group * r / (1.0 + r)

    member = jnp.where(lane == i1, 1.0, jnp.where(lane == i2, 1.0, 0.0))
    tm = member.shape[0]
    earlier = (lax.broadcasted_iota(jnp.int32, (tm, tm), 1)
               < lax.broadcasted_iota(jnp.int32, (tm, tm), 0))
    tri = jnp.where(earlier, 1.0, 0.0).astype(jnp.bfloat16)
    before = jnp.dot(tri, member.astype(jnp.bfloat16),
                     preferred_element_type=jnp.float32) + cnt_scr[...]
    rank1 = jnp.sum(jnp.where(lane == i1, before, 0.0), axis=1, keepdims=True)
    rank2 = jnp.sum(jnp.where(lane == i2, before, 0.0), axis=1, keepdims=True)
    cnt_scr[...] = cnt_scr[...] + jnp.sum(member, axis=0, keepdims=True)
    cnt_ref[...] = cnt_scr[...]

    route = jnp.zeros_like(logit)
    for idx, val in ((R_E1, i1 - N_GROUPS), (R_E2, i2 - N_GROUPS), (R_W1, w1), (R_W2, w2),
                     (R_RANK1, rank1), (R_RANK2, rank2)):
        route = jnp.where(lane == idx, val, route)
    route_ref[...] = route


def _out(mix_c, mix_a, w_out_c, w_out_a, x2, gt1, g_ffn, sc2, sh2, w_route, b_route, seq, tm=512):
    t, d = x2.shape
    per_b = seq // tm
    bmap = lambda m: (m // per_b, 0, 0)
    return pl.pallas_call(
        _out_kernel,
        grid=(t // tm,),
        in_specs=[pl.BlockSpec((tm, CONV_DIM), lambda m: (m, 0)),
                  pl.BlockSpec((tm, ATTN_DIM), lambda m: (m, 0)),
                  pl.BlockSpec((CONV_DIM, d), lambda m: (0, 0)),
                  pl.BlockSpec((ATTN_DIM, d), lambda m: (0, 0)),
                  pl.BlockSpec((tm, d), lambda m: (m, 0)),
                  pl.BlockSpec((1, 1, d), bmap),
                  pl.BlockSpec((1, d), lambda m: (0, 0)),
                  pl.BlockSpec((1, 1, d), bmap),
                  pl.BlockSpec((1, 1, d), bmap),
                  pl.BlockSpec((d, ROUTE_LANES), lambda m: (0, 0)),
                  pl.BlockSpec((1, ROUTE_LANES), lambda m: (0, 0))],
        out_specs=[pl.BlockSpec((tm, d), lambda m: (m, 0)),
                   pl.BlockSpec((tm, 1, d), lambda m: (m, 0, 0)),
                   pl.BlockSpec((tm, ROUTE_LANES), lambda m: (m, 0)),
                   pl.BlockSpec((1, ROUTE_LANES), lambda m: (0, 0))],
        out_shape=[jax.ShapeDtypeStruct((t, d), jnp.float32),
                   jax.ShapeDtypeStruct((t, 1, d), jnp.float32),
                   jax.ShapeDtypeStruct((t, ROUTE_LANES), jnp.float32),
                   jax.ShapeDtypeStruct((1, ROUTE_LANES), jnp.float32)],
        scratch_shapes=[pltpu.VMEM((1, ROUTE_LANES), jnp.float32)],
        compiler_params=_cparams(("arbitrary",)),
        name="out",
    )(mix_c, mix_a, w_out_c, w_out_a, x2, gt1, g_ffn, sc2, sh2, w_route, b_route)


def _dispatch_kernel(pos1_ref, pos2_ref, zs_ref, nu_ref, h2_ref, xs_ref, zbuf, sem_z, sem_r,
                     *, t, chunk, n_tiles):
    zbuf[...] = jnp.zeros(zbuf.shape, zbuf.dtype)

    def zero_copy(start):
        return pltpu.make_async_copy(zbuf, xs_ref.at[pl.ds(start, MOE_TM)], sem_z)

    for e in range(N_EXPERTS):
        zero_copy(zs_ref[e]).start()
    for e in range(N_EXPERTS):
        zero_copy(zs_ref[e]).wait()

    def zero_tile(i, carry):
        zero_copy(i * MOE_TM).start()
        zero_copy(i * MOE_TM).wait()
        return carry

    lax.fori_loop(nu_ref[0], n_tiles, zero_tile, 0)

    def row_copy(tok, pos):
        return pltpu.make_async_copy(h2_ref.at[pl.ds(tok, 1)], xs_ref.at[pl.ds(pos, 1)], sem_r)

    def wait_chunk():
        pltpu.make_async_copy(h2_ref.at[pl.ds(0, 2 * chunk)], xs_ref.at[pl.ds(0, 2 * chunk)],
                              sem_r).wait()

    def issue_chunk(ci, carry):
        def issue_row(r, c2):
            tok = ci * chunk + r
            row_copy(tok, pos1_ref[tok]).start()
            row_copy(tok, pos2_ref[tok]).start()
            return c2

        lax.fori_loop(0, chunk, issue_row, 0)

        @pl.when(ci > 0)
        def _():
            wait_chunk()

        return carry

    lax.fori_loop(0, t // chunk, issue_chunk, 0)
    wait_chunk()


def _dispatch(pos1, pos2, zstart, n_used, h2, n_rows, chunk=128):
    t, _, d = h2.shape
    kern = functools.partial(_dispatch_kernel, t=t, chunk=chunk, n_tiles=n_rows // MOE_TM)
    return pl.pallas_call(
        kern,
        grid_spec=pltpu.PrefetchScalarGridSpec(
            num_scalar_prefetch=4,
            grid=(1,),
            in_specs=[pl.BlockSpec(memory_space=pl.ANY)],
            out_specs=pl.BlockSpec(memory_space=pl.ANY),
            scratch_shapes=[pltpu.VMEM((MOE_TM, 1, d), jnp.float32),
                            pltpu.SemaphoreType.DMA(()),
                            pltpu.SemaphoreType.DMA(())]),
        out_shape=jax.ShapeDtypeStruct((n_rows, 1, d), jnp.float32),
        compiler_params=_cparams(("arbitrary",)),
        name="dispatch",
    )(pos1, pos2, zstart, n_used, h2)


def _mlp_kernel(te_ref, nu_ref, xs_ref, wg_ref, wu_ref, wd_ref, ys_ref, wg_bf, wu_bf, wd_bf):
    i = pl.program_id(0)

    @pl.when(i < nu_ref[0])
    def _():
        @pl.when((i == 0) | (te_ref[i] != te_ref[jnp.maximum(i - 1, 0)]))
        def _():
            wg_bf[...] = wg_ref[0].astype(jnp.bfloat16)
            wu_bf[...] = wu_ref[0].astype(jnp.bfloat16)
            wd_bf[...] = wd_ref[0].astype(jnp.bfloat16)

        x = xs_ref[:, 0, :].astype(jnp.bfloat16)
        a = jnp.dot(x, wg_bf[...], preferred_element_type=jnp.float32)
        u = jnp.dot(x, wu_bf[...], preferred_element_type=jnp.float32)
        hid = (a * jax.nn.sigmoid(a)) * u
        ys_ref[:, 0, :] = jnp.dot(hid.astype(jnp.bfloat16), wd_bf[...],
                                  preferred_element_type=jnp.float32)

    @pl.when(i >= nu_ref[0])
    def _():
        ys_ref[...] = jnp.zeros(ys_ref.shape, ys_ref.dtype)


def _mlp(tile_expert, n_used, xs, w_gate, w_up, w_down):
    n_rows, _, d = xs.shape
    f = w_gate.shape[-1]
    used = lambda i, te, nu: (jnp.minimum(i, nu[0] - 1), 0, 0)
    wmap = lambda i, te, nu: (te[i], 0, 0)
    return pl.pallas_call(
        _mlp_kernel,
        grid_spec=pltpu.PrefetchScalarGridSpec(
            num_scalar_prefetch=2,
            grid=(n_rows // MOE_TM,),
            in_specs=[pl.BlockSpec((MOE_TM, 1, d), used),
                      pl.BlockSpec((1, d, f), wmap),
                      pl.BlockSpec((1, d, f), wmap),
                      pl.BlockSpec((1, f, d), wmap)],
            out_specs=pl.BlockSpec((MOE_TM, 1, d), lambda i, te, nu: (i, 0, 0)),
            scratch_shapes=[pltpu.VMEM((d, f), jnp.bfloat16),
                            pltpu.VMEM((d, f), jnp.bfloat16),
                            pltpu.VMEM((f, d), jnp.bfloat16)]),
        out_shape=jax.ShapeDtypeStruct((n_rows, 1, d), jnp.float32),
        compiler_params=_cparams(("arbitrary",)),
        name="mlp",
    )(tile_expert, n_used, xs, w_gate, w_up, w_down)


def _combine_kernel(pos1_ref, pos2_ref, ys_ref, x1_ref, route_ref, gt_ref, g_ref, sc_ref, sh_ref,
                    o_ref, a0, b0, a1, b1, sem, *, tm, n_tiles):
    i = pl.program_id(0)
    bufs = ((a0, b0), (a1, b1))

    def issue(tile, sl):
        def issue_row(r, c):
            tok = tile * tm + r
            pltpu.make_async_copy(ys_ref.at[pl.ds(pos1_ref[tok], 1)],
                                  bufs[sl][0].at[pl.ds(r, 1)], sem.at[sl]).start()
            pltpu.make_async_copy(ys_ref.at[pl.ds(pos2_ref[tok], 1)],
                                  bufs[sl][1].at[pl.ds(r, 1)], sem.at[sl]).start()
            return c

        lax.fori_loop(0, tm, issue_row, 0)

    @pl.when(i == 0)
    def _():
        issue(0, 0)

    for sl in range(2):
        @pl.when(i % 2 == sl)
        def _(sl=sl):
            @pl.when(i + 1 < n_tiles)
            def _():
                issue(i + 1, 1 - sl)

            for buf in bufs[sl]:
                pltpu.make_async_copy(ys_ref.at[pl.ds(0, tm)], buf, sem.at[sl]).wait()

            route = route_ref[...]
            y = (route[:, R_W1:R_W1 + 1] * bufs[sl][0][:, 0, :]
                 + route[:, R_W2:R_W2 + 1] * bufs[sl][1][:, 0, :])
            x = x1_ref[...] + gt_ref[0] * y
            o_ref[...] = _rms(x) * g_ref[...] * (1.0 + sc_ref[0]) + sh_ref[0]


def _combine(pos1, pos2, ys, x1, route, gt2, g_final, scf, shf, seq, tm=256):
    t, d = x1.shape
    per_b = seq // tm
    n_tiles = t // tm
    bmap = lambda m, p1, p2: (m // per_b, 0, 0)
    kern = functools.partial(_combine_kernel, tm=tm, n_tiles=n_tiles)
    return pl.pallas_call(
        kern,
        grid_spec=pltpu.PrefetchScalarGridSpec(
            num_scalar_prefetch=2,
            grid=(n_tiles,),
            in_specs=[pl.BlockSpec(memory_space=pl.ANY),
                      pl.BlockSpec((tm, d), lambda m, p1, p2: (m, 0)),
                      pl.BlockSpec((tm, ROUTE_LANES), lambda m, p1, p2: (m, 0)),
                      pl.BlockSpec((1, 1, d), bmap),
                      pl.BlockSpec((1, d), lambda m, p1, p2: (0, 0)),
                      pl.BlockSpec((1, 1, d), bmap),
                      pl.BlockSpec((1, 1, d), bmap)],
            out_specs=pl.BlockSpec((tm, d), lambda m, p1, p2: (m, 0)),
            scratch_shapes=[pltpu.VMEM((tm, 1, d), jnp.float32) for _ in range(4)]
            + [pltpu.SemaphoreType.DMA((2,))]),
        out_shape=jax.ShapeDtypeStruct((t, d), jnp.float32),
        compiler_params=_cparams(("arbitrary",)),
        name="combine",
    )(pos1, pos2, ys, x1, route, gt2, g_final, scf, shf)


def _moe_plan(route, counts, t):
    i32 = jnp.int32
    cnt = counts[0, N_GROUPS:N_GROUPS + N_EXPERTS].astype(i32)
    padded = (cnt + MOE_TM - 1) // MOE_TM * MOE_TM
    off_end = jnp.cumsum(padded)
    off = off_end - padded
    n_tiles = (2 * t + N_EXPERTS * MOE_TM) // MOE_TM
    n_used = off_end[-1] // MOE_TM
    tile_start = jnp.arange(n_tiles, dtype=i32) * MOE_TM
    te = jnp.sum((off_end[None, :] <= tile_start[:, None]).astype(i32), axis=1)
    te = jnp.minimum(te, N_EXPERTS - 1)
    te = jnp.where(jnp.arange(n_tiles) < n_used, te, te[n_used - 1])
    pos1 = off[route[:, R_E1].astype(i32)] + route[:, R_RANK1].astype(i32)
    pos2 = off[route[:, R_E2].astype(i32)] + route[:, R_RANK2].astype(i32)
    zstart = jnp.maximum(off_end - MOE_TM, 0)
    return pos1, pos2, zstart, te, n_used.reshape(1), n_tiles * MOE_TM


def kernel(x, c, w_ada, b_ada, g_mix, w_in, conv_w, w_uk, kv_norm_g, w_uv, g_conv_out, g_attn_out,
           w_out, g_ffn, w_rg, b_rg, w_re, b_re, w_gate, w_up, w_down, w_ada_f, b_ada_f, g_final):
    b, s, d = x.shape
    assert w_ada.shape[0] == 1, "single layer"
    bf = jnp.bfloat16
    x2 = x.reshape(b * s, d)

    c_pad = jnp.zeros((SUBLANES, d), jnp.float32).at[:b].set(c)
    mod = _mod(c_pad, w_ada[0], b_ada[0])[:b]
    modf = _mod(c_pad, w_ada_f, b_ada_f)[:b]
    vec = lambda a, i: a[:, i * d:(i + 1) * d].reshape(b, 1, d)
    sh1, sc1, gt1, sh2, sc2, gt2 = (vec(mod, i) for i in range(N_MOD))
    shf, scf = vec(modf, 0), vec(modf, 1)
    row = lambda a: a.reshape(1, -1)

    w_in_bf = jnp.pad(w_in[0].astype(bf), ((0, 0), (0, PROJ_COLS - IN_COLS)))
    conv3, q, kv, iq, ikw = _proj(x2, row(g_mix[0]), sc1, sh1, w_in_bf, s)

    mix_c = _conv(conv3.reshape(b, s, 3 * CONV_DIM), conv_w[0], row(g_conv_out[0]))
    mix_a = _attn(iq.reshape(b, s, -1), ikw.reshape(b, s, -1), kv.reshape(b, s, -1),
                  q.reshape(b, s, -1), w_uk[0].astype(bf), w_uv[0].astype(bf),
                  row(kv_norm_g[0]), row(g_attn_out[0]))

    w_out_bf = w_out[0].astype(bf)
    w_route = jnp.zeros((d, ROUTE_LANES), bf).at[:, :N_GROUPS].set(w_rg[0].astype(bf))
    w_route = w_route.at[:, N_GROUPS:N_GROUPS + N_EXPERTS].set(w_re[0].astype(bf))
    b_route = jnp.zeros((1, ROUTE_LANES), jnp.float32).at[0, :N_GROUPS].set(b_rg[0])
    b_route = b_route.at[0, N_GROUPS:N_GROUPS + N_EXPERTS].set(b_re[0])
    x1, h2, route, counts = _out(mix_c.reshape(b * s, -1), mix_a.reshape(b * s, -1),
                                 w_out_bf[:CONV_DIM], w_out_bf[CONV_DIM:], x2, gt1, row(g_ffn[0]),
                                 sc2, sh2, w_route, b_route, s)

    pos1, pos2, zstart, tile_expert, n_used, n_rows = _moe_plan(route, counts, b * s)
    xs = _dispatch(pos1, pos2, zstart, n_used, h2, n_rows)
    ys = _mlp(tile_expert, n_used, xs, w_gate[0], w_up[0], w_down[0])
    out = _combine(pos1, pos2, ys, x1, route, gt2, row(g_final), scf, shf, s)
    return out.reshape(b, s, d)
```

```python
import functools

import jax
import jax.numpy as jnp
from jax import lax
from jax.experimental import pallas as pl
from jax.experimental.pallas import tpu as pltpu

D_MODEL = 2048
CONV_DIM = 1024
CONV_WIDTH = 3
N_HEADS = 8
HEAD_DIM = 128
ATTN_DIM = N_HEADS * HEAD_DIM
KV_RANK = 512
IDX_HEADS = 16
IDX_DIM = 128
TOPK_MAX = 256
N_GROUPS = 4
EXPERTS_PER_GROUP = 8
N_EXPERTS = N_GROUPS * EXPERTS_PER_GROUP
EXPERT_FF = 512
N_MOD = 6
EPS = 1e-6

OFF_Q = 3 * CONV_DIM
OFF_KV = OFF_Q + ATTN_DIM
OFF_IQ = OFF_KV + KV_RANK
OFF_IK = OFF_IQ + IDX_HEADS * IDX_DIM
OFF_IW = OFF_IK + IDX_DIM
IN_COLS = OFF_IW + IDX_HEADS

LANES = 128
SUBLANES = 8
VMEM_LIMIT = 56 * 1024 * 1024

PROJ_TN = 512
PROJ_COLS = 14 * PROJ_TN
ATTN_Q = 256
ATTN_KC = 256
MASKED = -1e30
ROUTE_LANES = 128
MOE_TM = 256
ROW_SUB = D_MODEL // LANES
INT_MIN = -2 ** 31
KEY_NEG_INF = INT_MIN + 0x7FFFFF


def _cparams(sem):
    return pltpu.CompilerParams(dimension_semantics=sem, vmem_limit_bytes=VMEM_LIMIT)


def _rms(v, axis=-1):
    return v * lax.rsqrt(jnp.mean(v * v, axis=axis, keepdims=True) + EPS)


def _rows_to_2d(ref, n):
    return jnp.concatenate([ref[pl.ds(c, n, stride=ROW_SUB), :] for c in range(ROW_SUB)], axis=1)


def _rows_from_2d(ref, val):
    n = val.shape[0]
    for c in range(ROW_SUB):
        ref[pl.ds(c, n, stride=ROW_SUB), :] = val[:, c * LANES:(c + 1) * LANES]


def _row(ref, i):
    return ref.at[pl.ds(pl.multiple_of(i * ROW_SUB, ROW_SUB), ROW_SUB)]


def _mod_kernel(c_ref, w_ref, b_ref, o_ref):
    c = c_ref[...]
    ca = (c * jax.nn.sigmoid(c)).astype(jnp.bfloat16)
    o_ref[...] = jnp.dot(ca, w_ref[...].astype(jnp.bfloat16),
                         preferred_element_type=jnp.float32) + b_ref[...]


def _mod(c_pad, w, b, tn=1024):
    d, n = w.shape
    return pl.pallas_call(
        _mod_kernel,
        grid=(n // tn,),
        in_specs=[pl.BlockSpec((c_pad.shape[0], d), lambda j: (0, 0)),
                  pl.BlockSpec((d, tn), lambda j: (0, j)),
                  pl.BlockSpec((1, tn), lambda j: (0, j))],
        out_specs=pl.BlockSpec((c_pad.shape[0], tn), lambda j: (0, j)),
        out_shape=jax.ShapeDtypeStruct((c_pad.shape[0], n), jnp.float32),
        compiler_params=_cparams(("arbitrary",)),
        name="mod",
    )(c_pad, w, b.reshape(1, n))


def _proj_kernel(x_ref, g_ref, sc_ref, sh_ref, w_ref,
                 conv_ref, q_ref, kv_ref, iq_ref, ikw_ref, h_scr):
    n = pl.program_id(1)

    @pl.when(n == 0)
    def _():
        h = _rms(x_ref[...]) * g_ref[...] * (1.0 + sc_ref[0]) + sh_ref[0]
        h_scr[...] = h.astype(jnp.bfloat16)

    acc = jnp.dot(h_scr[...], w_ref[...], preferred_element_type=jnp.float32)

    @pl.when(n < 6)
    def _():
        conv_ref[...] = acc

    @pl.when((n >= 6) & (n < 8))
    def _():
        q_ref[...] = acc.astype(jnp.bfloat16)

    @pl.when(n == 8)
    def _():
        kv_ref[...] = acc

    @pl.when((n >= 9) & (n < 13))
    def _():
        iq_ref[...] = acc.astype(jnp.bfloat16)

    @pl.when(n == 13)
    def _():
        ikw_ref[...] = acc


def _proj(x2, g_mix, sc1, sh1, w_in_bf, seq, tm=1024):
    t, d = x2.shape
    tn = PROJ_TN
    per_b = seq // tm
    clipn = lambda n, lo, hi: jnp.clip(n - lo, 0, hi - lo)
    return pl.pallas_call(
        _proj_kernel,
        grid=(t // tm, PROJ_COLS // tn),
        in_specs=[pl.BlockSpec((tm, d), lambda m, n: (m, 0)),
                  pl.BlockSpec((1, d), lambda m, n: (0, 0)),
                  pl.BlockSpec((1, 1, d), lambda m, n: (m // per_b, 0, 0)),
                  pl.BlockSpec((1, 1, d), lambda m, n: (m // per_b, 0, 0)),
                  pl.BlockSpec((d, tn), lambda m, n: (0, n))],
        out_specs=[pl.BlockSpec((tm, tn), lambda m, n: (m, clipn(n, 0, 5))),
                   pl.BlockSpec((tm, tn), lambda m, n: (m, clipn(n, 6, 7))),
                   pl.BlockSpec((tm, tn), lambda m, n: (m, 0)),
                   pl.BlockSpec((tm, tn), lambda m, n: (m, clipn(n, 9, 12))),
                   pl.BlockSpec((tm, tn), lambda m, n: (m, 0))],
        out_shape=[jax.ShapeDtypeStruct((t, 3 * CONV_DIM), jnp.float32),
                   jax.ShapeDtypeStruct((t, ATTN_DIM), jnp.bfloat16),
                   jax.ShapeDtypeStruct((t, KV_RANK), jnp.float32),
                   jax.ShapeDtypeStruct((t, IDX_HEADS * IDX_DIM), jnp.bfloat16),
                   jax.ShapeDtypeStruct((t, tn), jnp.float32)],
        scratch_shapes=[pltpu.VMEM((tm, d), jnp.bfloat16)],
        compiler_params=_cparams(("arbitrary", "arbitrary")),
        name="proj",
    )(x2, g_mix, sc1, sh1, w_in_bf)


def _conv_kernel(bg_ref, cg_ref, xv_ref, cgh_ref, xvh_ref, w_ref, g_ref, o_ref):
    i = pl.program_id(1)
    u = cg_ref[0] * xv_ref[0]
    halo = cgh_ref[0] * xvh_ref[0]
    halo = jnp.where(i == 0, 0.0, halo)
    row = lax.broadcasted_iota(jnp.int32, u.shape, 0)
    h1 = halo[SUBLANES - 1:SUBLANES, :]
    h2 = halo[SUBLANES - 2:SUBLANES - 1, :]
    u1 = jnp.where(row == 0, h1, pltpu.roll(u, 1, 0))
    u2 = jnp.where(row == 0, h2, jnp.where(row == 1, h1, pltpu.roll(u, 2, 0)))
    w = w_ref[...]
    y = bg_ref[0] * (w[0:1, :] * u2 + w[1:2, :] * u1 + w[2:3, :] * u)
    o_ref[0] = (_rms(y) * g_ref[...]).astype(o_ref.dtype)


def _conv(conv3, conv_w, g_conv_out, tm=512):
    b, s, _ = conv3.shape
    c = CONV_DIM
    hb = tm // SUBLANES
    halo_map = lambda col: (lambda bi, i: (bi, jnp.maximum(i * hb - 1, 0), col))
    return pl.pallas_call(
        _conv_kernel,
        grid=(b, s // tm),
        in_specs=[pl.BlockSpec((1, tm, c), lambda bi, i: (bi, i, 0)),
                  pl.BlockSpec((1, tm, c), lambda bi, i: (bi, i, 1)),
                  pl.BlockSpec((1, tm, c), lambda bi, i: (bi, i, 2)),
                  pl.BlockSpec((1, SUBLANES, c), halo_map(1)),
                  pl.BlockSpec((1, SUBLANES, c), halo_map(2)),
                  pl.BlockSpec((CONV_WIDTH, c), lambda bi, i: (0, 0)),
                  pl.BlockSpec((1, c), lambda bi, i: (0, 0))],
        out_specs=pl.BlockSpec((1, tm, c), lambda bi, i: (bi, i, 0)),
        out_shape=jax.ShapeDtypeStruct((b, s, c), jnp.bfloat16),
        compiler_params=_cparams(("arbitrary", "arbitrary")),
        name="conv",
    )(conv3, conv3, conv3, conv3, conv3, conv_w, g_conv_out)


def _ordered_bits(v):
    return v ^ ((v >> 31) & jnp.int32(0x7FFFFFFF))


def _attn_search(nch, sc_scr, k_sel):
    n_keys = float(nch * ATTN_KC)

    def bit_step(i, thr):
        cand = thr + lax.shift_left(jnp.int32(1), 31 - i)
        cand_f = lax.bitcast_convert_type(_ordered_bits(cand), jnp.float32)
        hits = jnp.where(sc_scr[0] >= cand_f, 1.0, 0.0)
        for c in range(1, nch):
            hits = hits + jnp.where(sc_scr[c] >= cand_f, 1.0, 0.0)
        cnt = jnp.sum(hits, axis=1, keepdims=True)
        cnt = jnp.where(cand < KEY_NEG_INF, n_keys, cnt)
        return jnp.where(cnt >= k_sel, cand, thr)

    thr = lax.fori_loop(0, 32, bit_step, jnp.full((ATTN_Q, 1), INT_MIN, jnp.int32))
    return lax.bitcast_convert_type(_ordered_bits(thr), jnp.float32)


def _attn_kernel(iq_ref, ikwq_ref, ikwa_ref, kv_ref, q_ref, wuk_ref, wuv_ref, kvg_ref, go_ref,
                 o_ref, ckv_scr, ik_scr, sc_scr, thr_scr, ql_scr, m_scr, l_scr, acc_scr, y_scr,
                 *, seq, k_sel):
    j = pl.program_id(1)
    nch = j + 1
    nt = (((1,), (1,)), ((), ()))

    @pl.when(j == 0)
    def _():
        ckv_scr[...] = (_rms(kv_ref[0]) * kvg_ref[...]).astype(jnp.bfloat16)
        ik_scr[...] = ikwa_ref[0][:, :IDX_DIM].astype(jnp.bfloat16)

    iw = ikwq_ref[0][:, IDX_DIM:IDX_DIM + IDX_HEADS] * (IDX_HEADS ** -0.5 * IDX_DIM ** -0.5)
    qpos = j * ATTN_Q + lax.broadcasted_iota(jnp.int32, (ATTN_Q, ATTN_KC), 0)
    kloc = lax.broadcasted_iota(jnp.int32, (ATTN_Q, ATTN_KC), 1)

    def score_chunk(c, carry):
        k0 = pl.multiple_of(c * ATTN_KC, ATTN_KC)
        ik_c = ik_scr[pl.ds(k0, ATTN_KC), :]
        score = jnp.zeros((ATTN_Q, ATTN_KC), jnp.float32)
        for h in range(IDX_HEADS):
            s = lax.dot_general(iq_ref[0][:, h * IDX_DIM:(h + 1) * IDX_DIM], ik_c, nt,
                                preferred_element_type=jnp.float32)
            score = score + jnp.maximum(s, 0.0) * iw[:, h:h + 1]
        sc_scr[c] = jnp.where(k0 + kloc <= qpos, score, -jnp.inf)
        return carry

    lax.fori_loop(0, nch, score_chunk, 0)

    for v in range(seq // ATTN_KC):
        @pl.when(j == v)
        def _(v=v):
            thr_scr[...] = _attn_search(v + 1, sc_scr, k_sel)

    for h in range(N_HEADS):
        ql = lax.dot_general(q_ref[0][:, h * HEAD_DIM:(h + 1) * HEAD_DIM], wuk_ref[h], nt,
                             preferred_element_type=jnp.float32)
        ql_scr[h * ATTN_Q:(h + 1) * ATTN_Q, :] = ql.astype(jnp.bfloat16)
    m_scr[...] = jnp.full(m_scr.shape, MASKED, jnp.float32)
    l_scr[...] = jnp.zeros(l_scr.shape, jnp.float32)
    acc_scr[...] = jnp.zeros(acc_scr.shape, jnp.float32)
    thr_f = thr_scr[...]
    scale = HEAD_DIM ** -0.5

    def attn_chunk(c, carry):
        k0 = pl.multiple_of(c * ATTN_KC, ATTN_KC)
        ckv_c = ckv_scr[pl.ds(k0, ATTN_KC), :]
        sel = (sc_scr[c] >= thr_f) & (k0 + kloc <= qpos)
        lg = lax.dot_general(ql_scr[...], ckv_c, nt, preferred_element_type=jnp.float32) * scale
        lg = jnp.where(sel[None], lg.reshape(N_HEADS, ATTN_Q, ATTN_KC), MASKED)
        lg = lg.reshape(N_HEADS * ATTN_Q, ATTN_KC)
        m_old = m_scr[...]
        m_new = jnp.maximum(m_old, jnp.max(lg, axis=1, keepdims=True))
        alpha = jnp.exp(m_old - m_new)
        p = jnp.exp(lg - m_new)
        l_scr[...] = alpha * l_scr[...] + jnp.sum(p, axis=1, keepdims=True)
        acc_scr[...] = alpha * acc_scr[...] + jnp.dot(p.astype(jnp.bfloat16), ckv_c,
                                                      preferred_element_type=jnp.float32)
        m_scr[...] = m_new
        return carry

    lax.fori_loop(0, nch, attn_chunk, 0)

    o = (acc_scr[...] / l_scr[...]).astype(jnp.bfloat16)
    for h in range(N_HEADS):
        y_scr[:, h * HEAD_DIM:(h + 1) * HEAD_DIM] = jnp.dot(
            o[h * ATTN_Q:(h + 1) * ATTN_Q], wuv_ref[h], preferred_element_type=jnp.float32)
    o_ref[0] = (_rms(y_scr[...]) * go_ref[...]).astype(o_ref.dtype)


def _attn(iq, ikw, kv, q, w_uk_bf, w_uv_bf, kv_norm_g, g_attn_out):
    b, s, _ = iq.shape
    assert ATTN_Q == ATTN_KC and s % ATTN_Q == 0
    k_sel = min(TOPK_MAX, s // 4)
    rows = N_HEADS * ATTN_Q
    kern = functools.partial(_attn_kernel, seq=s, k_sel=k_sel)
    return pl.pallas_call(
        kern,
        grid=(b, s // ATTN_Q),
        in_specs=[pl.BlockSpec((1, ATTN_Q, IDX_HEADS * IDX_DIM), lambda bi, j: (bi, j, 0)),
                  pl.BlockSpec((1, ATTN_Q, PROJ_TN), lambda bi, j: (bi, j, 0)),
                  pl.BlockSpec((1, s, PROJ_TN), lambda bi, j: (bi, 0, 0)),
                  pl.BlockSpec((1, s, KV_RANK), lambda bi, j: (bi, 0, 0)),
                  pl.BlockSpec((1, ATTN_Q, ATTN_DIM), lambda bi, j: (bi, j, 0)),
                  pl.BlockSpec((N_HEADS, KV_RANK, HEAD_DIM), lambda bi, j: (0, 0, 0)),
                  pl.BlockSpec((N_HEADS, KV_RANK, HEAD_DIM), lambda bi, j: (0, 0, 0)),
                  pl.BlockSpec((1, KV_RANK), lambda bi, j: (0, 0)),
                  pl.BlockSpec((1, ATTN_DIM), lambda bi, j: (0, 0))],
        out_specs=pl.BlockSpec((1, ATTN_Q, ATTN_DIM), lambda bi, j: (bi, j, 0)),
        out_shape=jax.ShapeDtypeStruct((b, s, ATTN_DIM), jnp.bfloat16),
        scratch_shapes=[pltpu.VMEM((s, KV_RANK), jnp.bfloat16),
                        pltpu.VMEM((s, IDX_DIM), jnp.bfloat16),
                        pltpu.VMEM((s // ATTN_KC, ATTN_Q, ATTN_KC), jnp.float32),
                        pltpu.VMEM((ATTN_Q, 1), jnp.float32),
                        pltpu.VMEM((rows, KV_RANK), jnp.bfloat16),
                        pltpu.VMEM((rows, 1), jnp.float32),
                        pltpu.VMEM((rows, 1), jnp.float32),
                        pltpu.VMEM((rows, KV_RANK), jnp.float32),
                        pltpu.VMEM((ATTN_Q, ATTN_DIM), jnp.float32)],
        compiler_params=_cparams(("arbitrary", "arbitrary")),
        name="attn",
    )(iq, ikw, ikw, kv, q, w_uk_bf, w_uv_bf, kv_norm_g, g_attn_out)


R_E1, R_E2, R_W1, R_W2, R_RANK1, R_RANK2 = range(6)


def _out_kernel(mc_ref, ma_ref, wc_ref, wa_ref, x_ref, gt_ref, g_ref, sc_ref, sh_ref, wr_ref, br_ref,
                x1_ref, h2_ref, route_ref, cnt_ref, cnt_scr):
    @pl.when(pl.program_id(0) == 0)
    def _():
        cnt_scr[...] = jnp.zeros(cnt_scr.shape, jnp.float32)

    mix = (jnp.dot(mc_ref[...], wc_ref[...], preferred_element_type=jnp.float32)
           + jnp.dot(ma_ref[...], wa_ref[...], preferred_element_type=jnp.float32))
    x1 = x_ref[...] + gt_ref[0] * mix
    x1_ref[...] = x1
    h2 = _rms(x1) * g_ref[...] * (1.0 + sc_ref[0]) + sh_ref[0]
    _rows_from_2d(h2_ref, h2)

    logit = jnp.dot(h2.astype(jnp.bfloat16), wr_ref[...],
                    preferred_element_type=jnp.float32) + br_ref[...]
    lane = lax.broadcasted_iota(jnp.int32, logit.shape, 1).astype(jnp.float32)
    neg = -jnp.inf
    big = float(ROUTE_LANES)
    is_g = lane < N_GROUPS
    gl = jnp.where(is_g, logit, neg)
    gmax = jnp.max(gl, axis=1, keepdims=True)
    p_group = 1.0 / jnp.sum(jnp.exp(gl - gmax), axis=1, keepdims=True)
    g_sel = jnp.min(jnp.where(is_g & (gl == gmax), lane, big), axis=1, keepdims=True)
    lo = N_GROUPS + g_sel * EXPERTS_PER_GROUP
    in_grp = (lane >= lo) & (lane < lo + EXPERTS_PER_GROUP)
    el = jnp.where(in_grp, logit, neg)
    m1 = jnp.max(el, axis=1, keepdims=True)
    i1 = jnp.min(jnp.where(in_grp & (el == m1), lane, big), axis=1, keepdims=True)
    el2 = jnp.where(lane == i1, neg, el)
    m2 = jnp.max(el2, axis=1, keepdims=True)
    i2 = jnp.min(jnp.where(in_grp & (lane != i1) & (el2 == m2), lane, big), axis=1, keepdims=True)
    r = jnp.exp(m2 - m1)
    w1 = p_group / (1.0 + r)
    w2 = p_group * r / (1.0 + r)

    member = jnp.where(lane == i1, 1.0, jnp.where(lane == i2, 1.0, 0.0))
    tm = member.shape[0]
    earlier = (lax.broadcasted_iota(jnp.int32, (tm, tm), 1)
               < lax.broadcasted_iota(jnp.int32, (tm, tm), 0))
    tri = jnp.where(earlier, 1.0, 0.0).astype(jnp.bfloat16)
    before = jnp.dot(tri, member.astype(jnp.bfloat16),
                     preferred_element_type=jnp.float32) + cnt_scr[...]
    rank1 = jnp.sum(jnp.where(lane == i1, before, 0.0), axis=1, keepdims=True)
    rank2 = jnp.sum(jnp.where(lane == i2, before, 0.0), axis=1, keepdims=True)
    cnt_scr[...] = cnt_scr[...] + jnp.sum(member, axis=0, keepdims=True)
    cnt_ref[...] = cnt_scr[...]

    route = jnp.zeros_like(logit)
    for idx, val in ((R_E1, i1 - N_GROUPS), (R_E2, i2 - N_GROUPS), (R_W1, w1), (R_W2, w2),
                     (R_RANK1, rank1), (R_RANK2, rank2)):
        route = jnp.where(lane == idx, val, route)
    route_ref[...] = route


def _out(mix_c, mix_a, w_out_c, w_out_a, x2, gt1, g_ffn, sc2, sh2, w_route, b_route, seq, tm=512):
    t, d = x2.shape
    per_b = seq // tm
    bmap = lambda m: (m // per_b, 0, 0)
    return pl.pallas_call(
        _out_kernel,
        grid=(t // tm,),
        in_specs=[pl.BlockSpec((tm, CONV_DIM), lambda m: (m, 0)),
                  pl.BlockSpec((tm, ATTN_DIM), lambda m: (m, 0)),
                  pl.BlockSpec((CONV_DIM, d), lambda m: (0, 0)),
                  pl.BlockSpec((ATTN_DIM, d), lambda m: (0, 0)),
                  pl.BlockSpec((tm, d), lambda m: (m, 0)),
                  pl.BlockSpec((1, 1, d), bmap),
                  pl.BlockSpec((1, d), lambda m: (0, 0)),
                  pl.BlockSpec((1, 1, d), bmap),
                  pl.BlockSpec((1, 1, d), bmap),
                  pl.BlockSpec((d, ROUTE_LANES), lambda m: (0, 0)),
                  pl.BlockSpec((1, ROUTE_LANES), lambda m: (0, 0))],
        out_specs=[pl.BlockSpec((tm, d), lambda m: (m, 0)),
                   pl.BlockSpec((tm * ROW_SUB, LANES), lambda m: (m, 0)),
                   pl.BlockSpec((tm, ROUTE_LANES), lambda m: (m, 0)),
                   pl.BlockSpec((1, ROUTE_LANES), lambda m: (0, 0))],
        out_shape=[jax.ShapeDtypeStruct((t, d), jnp.float32),
                   jax.ShapeDtypeStruct((t * ROW_SUB, LANES), jnp.float32),
                   jax.ShapeDtypeStruct((t, ROUTE_LANES), jnp.float32),
                   jax.ShapeDtypeStruct((1, ROUTE_LANES), jnp.float32)],
        scratch_shapes=[pltpu.VMEM((1, ROUTE_LANES), jnp.float32)],
        compiler_params=_cparams(("arbitrary",)),
        name="out",
    )(mix_c, mix_a, w_out_c, w_out_a, x2, gt1, g_ffn, sc2, sh2, w_route, b_route)


def _slot(off_ref, e_ref, r_ref, tok):
    return off_ref[e_ref[tok]] + r_ref[tok]


def _dispatch_kernel(e1_ref, e2_ref, r1_ref, r2_ref, off_ref, zs_ref, nu_ref, h2_ref, xs_ref,
                     zbuf, sem_z, sem_r, *, tm, n_tiles):
    i = pl.program_id(0)

    @pl.when(i == 0)
    def _():
        zbuf[...] = jnp.zeros(zbuf.shape, zbuf.dtype)

        def zero_copy(start):
            rows = pl.ds(pl.multiple_of(start * ROW_SUB, MOE_TM * ROW_SUB), MOE_TM * ROW_SUB)
            return pltpu.make_async_copy(zbuf, xs_ref.at[rows], sem_z)

        for e in range(N_EXPERTS):
            zero_copy(zs_ref[e]).start()
        for e in range(N_EXPERTS):
            zero_copy(zs_ref[e]).wait()

        def zero_tile(k, carry):
            zero_copy(k * MOE_TM).start()
            zero_copy(k * MOE_TM).wait()
            return carry

        lax.fori_loop(nu_ref[0], n_tiles, zero_tile, 0)

    def issue_row(r, carry):
        tok = i * tm + r
        for e_ref, r_ref in ((e1_ref, r1_ref), (e2_ref, r2_ref)):
            pltpu.make_async_copy(_row(h2_ref, r), _row(xs_ref, _slot(off_ref, e_ref, r_ref, tok)),
                                  sem_r).start()
        return carry

    lax.fori_loop(0, tm, issue_row, 0)
    for _ in range(2):
        pltpu.make_async_copy(h2_ref, xs_ref.at[pl.ds(0, tm * ROW_SUB)], sem_r).wait()


def _dispatch(plan, h2, n_rows, tm=256):
    t = h2.shape[0] // ROW_SUB
    kern = functools.partial(_dispatch_kernel, tm=tm, n_tiles=n_rows // MOE_TM)
    return pl.pallas_call(
        kern,
        grid_spec=pltpu.PrefetchScalarGridSpec(
            num_scalar_prefetch=len(plan),
            grid=(t // tm,),
            in_specs=[pl.BlockSpec((tm * ROW_SUB, LANES), lambda i, *_: (i, 0))],
            out_specs=pl.BlockSpec(memory_space=pl.ANY),
            scratch_shapes=[pltpu.VMEM((MOE_TM * ROW_SUB, LANES), jnp.float32),
                            pltpu.SemaphoreType.DMA(()),
                            pltpu.SemaphoreType.DMA(())]),
        out_shape=jax.ShapeDtypeStruct((n_rows * ROW_SUB, LANES), jnp.float32),
        compiler_params=_cparams(("arbitrary",)),
        name="dispatch",
    )(*plan, h2)


def _mlp_kernel(te_ref, nu_ref, xs_ref, wg_ref, wu_ref, wd_ref, ys_ref, wg_bf, wu_bf, wd_bf):
    i = pl.program_id(0)

    @pl.when(i < nu_ref[0])
    def _():
        @pl.when((i == 0) | (te_ref[i] != te_ref[jnp.maximum(i - 1, 0)]))
        def _():
            wg_bf[...] = wg_ref[0].astype(jnp.bfloat16)
            wu_bf[...] = wu_ref[0].astype(jnp.bfloat16)
            wd_bf[...] = wd_ref[0].astype(jnp.bfloat16)

        x = _rows_to_2d(xs_ref, MOE_TM).astype(jnp.bfloat16)
        a = jnp.dot(x, wg_bf[...], preferred_element_type=jnp.float32)
        u = jnp.dot(x, wu_bf[...], preferred_element_type=jnp.float32)
        hid = (a * jax.nn.sigmoid(a)) * u
        _rows_from_2d(ys_ref, jnp.dot(hid.astype(jnp.bfloat16), wd_bf[...],
                                      preferred_element_type=jnp.float32))

    @pl.when(i >= nu_ref[0])
    def _():
        ys_ref[...] = jnp.zeros(ys_ref.shape, ys_ref.dtype)


def _mlp(tile_expert, n_used, xs, w_gate, w_up, w_down):
    n_rows = xs.shape[0] // ROW_SUB
    _, d, f = w_gate.shape
    used = lambda i, te, nu: (jnp.minimum(i, nu[0] - 1), 0)
    wmap = lambda i, te, nu: (te[i], 0, 0)
    return pl.pallas_call(
        _mlp_kernel,
        grid_spec=pltpu.PrefetchScalarGridSpec(
            num_scalar_prefetch=2,
            grid=(n_rows // MOE_TM,),
            in_specs=[pl.BlockSpec((MOE_TM * ROW_SUB, LANES), used),
                      pl.BlockSpec((1, d, f), wmap),
                      pl.BlockSpec((1, d, f), wmap),
                      pl.BlockSpec((1, f, d), wmap)],
            out_specs=pl.BlockSpec((MOE_TM * ROW_SUB, LANES), lambda i, te, nu: (i, 0)),
            scratch_shapes=[pltpu.VMEM((d, f), jnp.bfloat16),
                            pltpu.VMEM((d, f), jnp.bfloat16),
                            pltpu.VMEM((f, d), jnp.bfloat16)]),
        out_shape=jax.ShapeDtypeStruct((n_rows * ROW_SUB, LANES), jnp.float32),
        compiler_params=_cparams(("arbitrary",)),
        name="mlp",
    )(tile_expert, n_used, xs, w_gate, w_up, w_down)


def _combine_kernel(e1_ref, e2_ref, r1_ref, r2_ref, off_ref, ys_ref, x1_ref, route_ref, gt_ref, g_ref,
                    sc_ref, sh_ref, o_ref, a0, b0, a1, b1, sem, *, tm, n_tiles):
    i = pl.program_id(0)
    bufs = ((a0, b0), (a1, b1))

    def issue(tile, sl):
        def issue_row(r, carry):
            tok = tile * tm + r
            for buf, e_ref, r_ref in ((bufs[sl][0], e1_ref, r1_ref), (bufs[sl][1], e2_ref, r2_ref)):
                pltpu.make_async_copy(_row(ys_ref, _slot(off_ref, e_ref, r_ref, tok)),
                                      _row(buf, r), sem.at[sl]).start()
            return carry

        lax.fori_loop(0, tm, issue_row, 0)

    @pl.when(i == 0)
    def _():
        issue(0, 0)

    for sl in range(2):
        @pl.when(i % 2 == sl)
        def _(sl=sl):
            @pl.when(i + 1 < n_tiles)
            def _():
                issue(i + 1, 1 - sl)

            for buf in bufs[sl]:
                pltpu.make_async_copy(ys_ref.at[pl.ds(0, tm * ROW_SUB)], buf, sem.at[sl]).wait()

            route = route_ref[...]
            y = (route[:, R_W1:R_W1 + 1] * _rows_to_2d(bufs[sl][0], tm)
                 + route[:, R_W2:R_W2 + 1] * _rows_to_2d(bufs[sl][1], tm))
            x = x1_ref[...] + gt_ref[0] * y
            o_ref[...] = _rms(x) * g_ref[...] * (1.0 + sc_ref[0]) + sh_ref[0]


def _combine(plan, ys, x1, route, gt2, g_final, scf, shf, seq, tm=256):
    t, d = x1.shape
    per_b = seq // tm
    n_tiles = t // tm
    bmap = lambda m, *_: (m // per_b, 0, 0)
    kern = functools.partial(_combine_kernel, tm=tm, n_tiles=n_tiles)
    return pl.pallas_call(
        kern,
        grid_spec=pltpu.PrefetchScalarGridSpec(
            num_scalar_prefetch=len(plan),
            grid=(n_tiles,),
            in_specs=[pl.BlockSpec(memory_space=pl.ANY),
                      pl.BlockSpec((tm, d), lambda m, *_: (m, 0)),
                      pl.BlockSpec((tm, ROUTE_LANES), lambda m, *_: (m, 0)),
                      pl.BlockSpec((1, 1, d), bmap),
                      pl.BlockSpec((1, d), lambda m, *_: (0, 0)),
                      pl.BlockSpec((1, 1, d), bmap),
                      pl.BlockSpec((1, 1, d), bmap)],
            out_specs=pl.BlockSpec((tm, d), lambda m, *_: (m, 0)),
            scratch_shapes=[pltpu.VMEM((tm * ROW_SUB, LANES), jnp.float32) for _ in range(4)]
            + [pltpu.SemaphoreType.DMA((2,))]),
        out_shape=jax.ShapeDtypeStruct((t, d), jnp.float32),
        compiler_params=_cparams(("arbitrary",)),
        name="combine",
    )(*plan, ys, x1, route, gt2, g_final, scf, shf)


def _moe_plan(route, counts, t):
    i32 = jnp.int32
    cnt = counts[0, N_GROUPS:N_GROUPS + N_EXPERTS].astype(i32)
    padded = (cnt + MOE_TM - 1) // MOE_TM * MOE_TM
    off_end = jnp.cumsum(padded)
    off = off_end - padded
    n_tiles = (2 * t + N_EXPERTS * MOE_TM) // MOE_TM
    n_used = off_end[-1] // MOE_TM
    tile_start = jnp.arange(n_tiles, dtype=i32) * MOE_TM
    te = jnp.sum((off_end[None, :] <= tile_start[:, None]).astype(i32), axis=1)
    te = jnp.minimum(te, N_EXPERTS - 1)
    te = jnp.where(jnp.arange(n_tiles) < n_used, te, te[n_used - 1])
    assign = tuple(route[:, k].astype(i32) for k in (R_E1, R_E2, R_RANK1, R_RANK2))
    zstart = jnp.maximum(off_end - MOE_TM, 0)
    return assign, off, zstart, te, n_used.reshape(1), n_tiles * MOE_TM


def kernel(x, c, w_ada, b_ada, g_mix, w_in, conv_w, w_uk, kv_norm_g, w_uv, g_conv_out, g_attn_out,
           w_out, g_ffn, w_rg, b_rg, w_re, b_re, w_gate, w_up, w_down, w_ada_f, b_ada_f, g_final):
    b, s, d = x.shape
    assert w_ada.shape[0] == 1, "single layer"
    bf = jnp.bfloat16
    x2 = x.reshape(b * s, d)

    c_pad = jnp.zeros((SUBLANES, d), jnp.float32).at[:b].set(c)
    mod = _mod(c_pad, w_ada[0], b_ada[0])[:b]
    modf = _mod(c_pad, w_ada_f, b_ada_f)[:b]
    vec = lambda a, i: a[:, i * d:(i + 1) * d].reshape(b, 1, d)
    sh1, sc1, gt1, sh2, sc2, gt2 = (vec(mod, i) for i in range(N_MOD))
    shf, scf = vec(modf, 0), vec(modf, 1)
    row = lambda a: a.reshape(1, -1)

    w_in_bf = jnp.pad(w_in[0].astype(bf), ((0, 0), (0, PROJ_COLS - IN_COLS)))
    conv3, q, kv, iq, ikw = _proj(x2, row(g_mix[0]), sc1, sh1, w_in_bf, s)

    mix_c = _conv(conv3.reshape(b, s, 3 * CONV_DIM), conv_w[0], row(g_conv_out[0]))
    mix_a = _attn(iq.reshape(b, s, -1), ikw.reshape(b, s, -1), kv.reshape(b, s, -1),
                  q.reshape(b, s, -1), w_uk[0].astype(bf), w_uv[0].astype(bf),
                  row(kv_norm_g[0]), row(g_attn_out[0]))

    w_out_bf = w_out[0].astype(bf)
    w_route = jnp.zeros((d, ROUTE_LANES), bf).at[:, :N_GROUPS].set(w_rg[0].astype(bf))
    w_route = w_route.at[:, N_GROUPS:N_GROUPS + N_EXPERTS].set(w_re[0].astype(bf))
    b_route = jnp.zeros((1, ROUTE_LANES), jnp.float32).at[0, :N_GROUPS].set(b_rg[0])
    b_route = b_route.at[0, N_GROUPS:N_GROUPS + N_EXPERTS].set(b_re[0])
    x1, h2, route, counts = _out(mix_c.reshape(b * s, -1), mix_a.reshape(b * s, -1),
                                 w_out_bf[:CONV_DIM], w_out_bf[CONV_DIM:], x2, gt1, row(g_ffn[0]),
                                 sc2, sh2, w_route, b_route, s)

    assign, off, zstart, tile_expert, n_used, n_rows = _moe_plan(route, counts, b * s)
    xs = _dispatch((*assign, off, zstart, n_used), h2, n_rows)
    ys = _mlp(tile_expert, n_used, xs, w_gate[0], w_up[0], w_down[0])
    out = _combine((*assign, off), ys, x1, route, gt2, row(g_final), scf, shf, s)
    return out.reshape(b, s, d)
```

```python
import functools

import jax
import jax.numpy as jnp
from jax import lax
from jax.experimental import pallas as pl
from jax.experimental.pallas import tpu as pltpu

D_MODEL = 2048
CONV_DIM = 1024
CONV_WIDTH = 3
N_HEADS = 8
HEAD_DIM = 128
ATTN_DIM = N_HEADS * HEAD_DIM
KV_RANK = 512
IDX_HEADS = 16
IDX_DIM = 128
TOPK_MAX = 256
N_GROUPS = 4
EXPERTS_PER_GROUP = 8
N_EXPERTS = N_GROUPS * EXPERTS_PER_GROUP
EXPERT_FF = 512
N_MOD = 6
EPS = 1e-6

OFF_Q = 3 * CONV_DIM
OFF_KV = OFF_Q + ATTN_DIM
OFF_IQ = OFF_KV + KV_RANK
OFF_IK = OFF_IQ + IDX_HEADS * IDX_DIM
OFF_IW = OFF_IK + IDX_DIM
IN_COLS = OFF_IW + IDX_HEADS

LANES = 128
SUBLANES = 8
VMEM_LIMIT = 56 * 1024 * 1024

PROJ_TN = 512
PROJ_FULL = OFF_IK // PROJ_TN
IKW_COLS = 256
ATTN_Q = 256
ATTN_KC = 256
MASKED = -1e30
ROUTE_LANES = 128
MOE_TM = 256
ROW_SUB = D_MODEL // LANES
INT_MIN = -2 ** 31
KEY_NEG_INF = INT_MIN + 0x7FFFFF


def _cparams(sem):
    return pltpu.CompilerParams(dimension_semantics=sem, vmem_limit_bytes=VMEM_LIMIT)


def _rms(v, axis=-1):
    return v * lax.rsqrt(jnp.mean(v * v, axis=axis, keepdims=True) + EPS)


def _tile_lanes(v, n):
    return jnp.concatenate([v] * n, axis=1)


def _rows_to_2d(ref, n):
    return jnp.concatenate([ref[pl.ds(c, n, stride=ROW_SUB), :] for c in range(ROW_SUB)], axis=1)


def _rows_from_2d(ref, val):
    n = val.shape[0]
    for c in range(ROW_SUB):
        ref[pl.ds(c, n, stride=ROW_SUB), :] = val[:, c * LANES:(c + 1) * LANES]


def _row(ref, i):
    return ref.at[pl.ds(pl.multiple_of(i * ROW_SUB, ROW_SUB), ROW_SUB)]


def _mod_kernel(c_ref, w_ref, b_ref, o_ref):
    c = c_ref[...]
    ca = (c * jax.nn.sigmoid(c)).astype(jnp.bfloat16)
    o_ref[...] = jnp.dot(ca, w_ref[...].astype(jnp.bfloat16),
                         preferred_element_type=jnp.float32) + b_ref[...]


def _mod(c_pad, w, b, tn=1024):
    d, n = w.shape
    return pl.pallas_call(
        _mod_kernel,
        grid=(n // tn,),
        in_specs=[pl.BlockSpec((c_pad.shape[0], d), lambda j: (0, 0)),
                  pl.BlockSpec((d, tn), lambda j: (0, j)),
                  pl.BlockSpec((1, tn), lambda j: (0, j))],
        out_specs=pl.BlockSpec((c_pad.shape[0], tn), lambda j: (0, j)),
        out_shape=jax.ShapeDtypeStruct((c_pad.shape[0], n), jnp.float32),
        compiler_params=_cparams(("arbitrary",)),
        name="mod",
    )(c_pad, w, b.reshape(1, n))


def _proj_kernel(x_ref, g_ref, sc_ref, sh_ref, w_ref, wt_ref,
                 conv_ref, q_ref, kv_ref, iq_ref, ikw_ref, h_scr):
    n = pl.program_id(1)

    @pl.when(n == 0)
    def _():
        h = _rms(x_ref[...]) * g_ref[...] * (1.0 + sc_ref[0]) + sh_ref[0]
        h_scr[...] = h.astype(jnp.bfloat16)

    def project(o_ref, weights=w_ref):
        o_ref[...] = jnp.dot(h_scr[...], weights[...].astype(jnp.bfloat16),
                             preferred_element_type=jnp.float32).astype(o_ref.dtype)

    pl.when(n < 6)(lambda: project(conv_ref))
    pl.when((n >= 6) & (n < 8))(lambda: project(q_ref))
    pl.when(n == 8)(lambda: project(kv_ref))
    pl.when((n >= 9) & (n < 13))(lambda: project(iq_ref))
    pl.when(n == PROJ_FULL)(lambda: project(ikw_ref, wt_ref))


def _proj(x2, g_mix, sc1, sh1, w_in, w_tail, seq, tm=1024):
    t, d = x2.shape
    tn = PROJ_TN
    per_b = seq // tm
    clipn = lambda n, lo, hi: jnp.clip(n - lo, 0, hi - lo)
    return pl.pallas_call(
        _proj_kernel,
        grid=(t // tm, PROJ_FULL + 1),
        in_specs=[pl.BlockSpec((tm, d), lambda m, n: (m, 0)),
                  pl.BlockSpec((1, d), lambda m, n: (0, 0)),
                  pl.BlockSpec((1, 1, d), lambda m, n: (m // per_b, 0, 0)),
                  pl.BlockSpec((1, 1, d), lambda m, n: (m // per_b, 0, 0)),
                  pl.BlockSpec((d, tn), lambda m, n: (0, jnp.minimum(n, PROJ_FULL - 1))),
                  pl.BlockSpec((d, IKW_COLS), lambda m, n: (0, 0))],
        out_specs=[pl.BlockSpec((tm, tn), lambda m, n: (m, clipn(n, 0, 5))),
                   pl.BlockSpec((tm, tn), lambda m, n: (m, clipn(n, 6, 7))),
                   pl.BlockSpec((tm, tn), lambda m, n: (m, 0)),
                   pl.BlockSpec((tm, tn), lambda m, n: (m, clipn(n, 9, 12))),
                   pl.BlockSpec((tm, IKW_COLS), lambda m, n: (m, 0))],
        out_shape=[jax.ShapeDtypeStruct((t, 3 * CONV_DIM), jnp.float32),
                   jax.ShapeDtypeStruct((t, ATTN_DIM), jnp.bfloat16),
                   jax.ShapeDtypeStruct((t, KV_RANK), jnp.float32),
                   jax.ShapeDtypeStruct((t, IDX_HEADS * IDX_DIM), jnp.bfloat16),
                   jax.ShapeDtypeStruct((t, IKW_COLS), jnp.float32)],
        scratch_shapes=[pltpu.VMEM((tm, d), jnp.bfloat16)],
        compiler_params=_cparams(("arbitrary", "arbitrary")),
        name="proj",
    )(x2, g_mix, sc1, sh1, w_in, w_tail)


def _conv_kernel(bg_ref, cg_ref, xv_ref, cgh_ref, xvh_ref, w_ref, g_ref, o_ref):
    i = pl.program_id(1)
    u = cg_ref[0] * xv_ref[0]
    halo = cgh_ref[0] * xvh_ref[0]
    halo = jnp.where(i == 0, 0.0, halo)
    row = lax.broadcasted_iota(jnp.int32, u.shape, 0)
    h1 = halo[SUBLANES - 1:SUBLANES, :]
    h2 = halo[SUBLANES - 2:SUBLANES - 1, :]
    u1 = jnp.where(row == 0, h1, pltpu.roll(u, 1, 0))
    u2 = jnp.where(row == 0, h2, jnp.where(row == 1, h1, pltpu.roll(u, 2, 0)))
    w = w_ref[...]
    y = bg_ref[0] * (w[0:1, :] * u2 + w[1:2, :] * u1 + w[2:3, :] * u)
    o_ref[0] = (_rms(y) * g_ref[...]).astype(o_ref.dtype)


def _conv(conv3, conv_w, g_conv_out, tm=512):
    b, s, _ = conv3.shape
    c = CONV_DIM
    hb = tm // SUBLANES
    halo_map = lambda col: (lambda bi, i: (bi, jnp.maximum(i * hb - 1, 0), col))
    return pl.pallas_call(
        _conv_kernel,
        grid=(b, s // tm),
        in_specs=[pl.BlockSpec((1, tm, c), lambda bi, i: (bi, i, 0)),
                  pl.BlockSpec((1, tm, c), lambda bi, i: (bi, i, 1)),
                  pl.BlockSpec((1, tm, c), lambda bi, i: (bi, i, 2)),
                  pl.BlockSpec((1, SUBLANES, c), halo_map(1)),
                  pl.BlockSpec((1, SUBLANES, c), halo_map(2)),
                  pl.BlockSpec((CONV_WIDTH, c), lambda bi, i: (0, 0)),
                  pl.BlockSpec((1, c), lambda bi, i: (0, 0))],
        out_specs=pl.BlockSpec((1, tm, c), lambda bi, i: (bi, i, 0)),
        out_shape=jax.ShapeDtypeStruct((b, s, c), jnp.bfloat16),
        compiler_params=_cparams(("arbitrary", "arbitrary")),
        name="conv",
    )(conv3, conv3, conv3, conv3, conv3, conv_w, g_conv_out)


def _ordered_bits(v):
    return v ^ ((v >> 31) & jnp.int32(0x7FFFFFFF))


def _attn_search(nch, sc_scr, k_sel):
    n_keys = float(nch * ATTN_KC)
    half = ATTN_Q // 2

    def half_step(bit, thr, rows):
        cand = thr + bit
        cand_f = lax.bitcast_convert_type(_ordered_bits(cand), jnp.float32)
        hits = jnp.where(sc_scr[0, rows, :] >= cand_f, 1.0, 0.0)
        for c in range(1, nch):
            hits = hits + jnp.where(sc_scr[c, rows, :] >= cand_f, 1.0, 0.0)
        cnt = jnp.sum(hits, axis=1, keepdims=True)
        cnt = jnp.where(cand < KEY_NEG_INF, n_keys, cnt)
        return jnp.where(cnt >= k_sel, cand, thr)

    def bit_step(i, thrs):
        bit = lax.shift_left(jnp.int32(1), 31 - i)
        return tuple(half_step(bit, thr, pl.ds(k * half, half)) for k, thr in enumerate(thrs))

    init = jnp.full((half, 1), INT_MIN, jnp.int32)
    thrs = lax.fori_loop(0, 32, bit_step, (init, init), unroll=2)
    thr = jnp.concatenate(thrs, axis=0)
    return lax.bitcast_convert_type(_ordered_bits(thr), jnp.float32)


def _attn_kernel(iq_ref, ikwq_ref, ikwa_ref, kv_ref, q_ref, wuk_ref, wuv_ref, kvg_ref, go_ref,
                 o_ref, ckv_scr, ik_scr, sc_scr, thr_scr, ql_scr, m_scr, l_scr, acc_scr, y_scr,
                 *, seq, k_sel):
    j = pl.program_id(1)
    nch = j + 1
    nt = (((1,), (1,)), ((), ()))

    @pl.when(j == 0)
    def _():
        ckv_scr[...] = (_rms(kv_ref[0]) * kvg_ref[...]).astype(jnp.bfloat16)
        ik_scr[...] = ikwa_ref[0][:, :IDX_DIM].astype(jnp.bfloat16)

    iw = ikwq_ref[0][:, IDX_DIM:IDX_DIM + IDX_HEADS] * (IDX_HEADS ** -0.5 * IDX_DIM ** -0.5)
    qpos = j * ATTN_Q + lax.broadcasted_iota(jnp.int32, (ATTN_Q, ATTN_KC), 0)
    kloc = lax.broadcasted_iota(jnp.int32, (ATTN_Q, ATTN_KC), 1)

    def score_chunk(c, carry):
        k0 = pl.multiple_of(c * ATTN_KC, ATTN_KC)
        ik_c = ik_scr[pl.ds(k0, ATTN_KC), :]
        score = jnp.zeros((ATTN_Q, ATTN_KC), jnp.float32)
        for h in range(IDX_HEADS):
            s = lax.dot_general(iq_ref[0][:, h * IDX_DIM:(h + 1) * IDX_DIM], ik_c, nt,
                                preferred_element_type=jnp.float32)
            score = score + jnp.maximum(s, 0.0) * iw[:, h:h + 1]
        sc_scr[c] = jnp.where(k0 + kloc <= qpos, score, -jnp.inf)
        return carry

    lax.fori_loop(0, nch, score_chunk, 0)

    for v in range(seq // ATTN_KC):
        @pl.when(j == v)
        def _(v=v):
            thr_scr[...] = _attn_search(v + 1, sc_scr, k_sel)

    for h in range(N_HEADS):
        ql = lax.dot_general(q_ref[0][:, h * HEAD_DIM:(h + 1) * HEAD_DIM], wuk_ref[h], nt,
                             preferred_element_type=jnp.float32)
        ql_scr[h * ATTN_Q:(h + 1) * ATTN_Q, :] = ql.astype(jnp.bfloat16)
    m_scr[...] = jnp.full(m_scr.shape, MASKED, jnp.float32)
    l_scr[...] = jnp.zeros(l_scr.shape, jnp.float32)
    acc_scr[...] = jnp.zeros(acc_scr.shape, jnp.float32)
    thr_f = thr_scr[...]
    scale = HEAD_DIM ** -0.5

    def attn_chunk(c, carry):
        k0 = pl.multiple_of(c * ATTN_KC, ATTN_KC)
        ckv_c = ckv_scr[pl.ds(k0, ATTN_KC), :]
        sel = (sc_scr[c] >= thr_f) & (k0 + kloc <= qpos)
        for h in range(N_HEADS):
            rows = pl.ds(h * ATTN_Q, ATTN_Q)
            lg = lax.dot_general(ql_scr[rows, :], ckv_c, nt,
                                 preferred_element_type=jnp.float32) * scale
            lg = jnp.where(sel, lg, MASKED)
            m_old = m_scr[rows, :]
            m_new = jnp.maximum(m_old, jnp.max(lg, axis=1, keepdims=True))
            alpha = jnp.exp(m_old - m_new)
            p = jnp.exp(lg - _tile_lanes(m_new, ATTN_KC // LANES))
            l_scr[rows, :] = alpha * l_scr[rows, :] + jnp.sum(p, axis=1, keepdims=True)
            pv = jnp.dot(p.astype(jnp.bfloat16), ckv_c, preferred_element_type=jnp.float32)
            acc_scr[rows, :] = _tile_lanes(alpha, KV_RANK // LANES) * acc_scr[rows, :] + pv
            m_scr[rows, :] = m_new
        return carry

    lax.fori_loop(0, nch, attn_chunk, 0)

    o = (acc_scr[...] / _tile_lanes(l_scr[...], KV_RANK // LANES)).astype(jnp.bfloat16)
    for h in range(N_HEADS):
        y_scr[:, h * HEAD_DIM:(h + 1) * HEAD_DIM] = jnp.dot(
            o[h * ATTN_Q:(h + 1) * ATTN_Q], wuv_ref[h], preferred_element_type=jnp.float32)
    o_ref[0] = (_rms(y_scr[...]) * go_ref[...]).astype(o_ref.dtype)


def _attn(iq, ikw, kv, q, w_uk_bf, w_uv_bf, kv_norm_g, g_attn_out):
    b, s, _ = iq.shape
    assert ATTN_Q == ATTN_KC and s % ATTN_Q == 0
    k_sel = min(TOPK_MAX, s // 4)
    rows = N_HEADS * ATTN_Q
    kern = functools.partial(_attn_kernel, seq=s, k_sel=k_sel)
    return pl.pallas_call(
        kern,
        grid=(b, s // ATTN_Q),
        in_specs=[pl.BlockSpec((1, ATTN_Q, IDX_HEADS * IDX_DIM), lambda bi, j: (bi, j, 0)),
                  pl.BlockSpec((1, ATTN_Q, IKW_COLS), lambda bi, j: (bi, j, 0)),
                  pl.BlockSpec((1, s, IKW_COLS), lambda bi, j: (bi, 0, 0)),
                  pl.BlockSpec((1, s, KV_RANK), lambda bi, j: (bi, 0, 0)),
                  pl.BlockSpec((1, ATTN_Q, ATTN_DIM), lambda bi, j: (bi, j, 0)),
                  pl.BlockSpec((N_HEADS, KV_RANK, HEAD_DIM), lambda bi, j: (0, 0, 0)),
                  pl.BlockSpec((N_HEADS, KV_RANK, HEAD_DIM), lambda bi, j: (0, 0, 0)),
                  pl.BlockSpec((1, KV_RANK), lambda bi, j: (0, 0)),
                  pl.BlockSpec((1, ATTN_DIM), lambda bi, j: (0, 0))],
        out_specs=pl.BlockSpec((1, ATTN_Q, ATTN_DIM), lambda bi, j: (bi, j, 0)),
        out_shape=jax.ShapeDtypeStruct((b, s, ATTN_DIM), jnp.bfloat16),
        scratch_shapes=[pltpu.VMEM((s, KV_RANK), jnp.bfloat16),
                        pltpu.VMEM((s, IDX_DIM), jnp.bfloat16),
                        pltpu.VMEM((s // ATTN_KC, ATTN_Q, ATTN_KC), jnp.float32),
                        pltpu.VMEM((ATTN_Q, 1), jnp.float32),
                        pltpu.VMEM((rows, KV_RANK), jnp.bfloat16),
                        pltpu.VMEM((rows, LANES), jnp.float32),
                        pltpu.VMEM((rows, LANES), jnp.float32),
                        pltpu.VMEM((rows, KV_RANK), jnp.float32),
                        pltpu.VMEM((ATTN_Q, ATTN_DIM), jnp.float32)],
        compiler_params=_cparams(("arbitrary", "arbitrary")),
        name="attn",
    )(iq, ikw, ikw, kv, q, w_uk_bf, w_uv_bf, kv_norm_g, g_attn_out)


R_E1, R_E2, R_W1, R_W2, R_RANK1, R_RANK2 = range(6)


def _out_kernel(mc_ref, ma_ref, wc_ref, wa_ref, x_ref, gt_ref, g_ref, sc_ref, sh_ref, wr_ref, br_ref,
                x1_ref, h2_ref, route_ref, cnt_ref, cnt_scr):
    @pl.when(pl.program_id(0) == 0)
    def _():
        cnt_scr[...] = jnp.zeros(cnt_scr.shape, jnp.float32)

    mix = (jnp.dot(mc_ref[...], wc_ref[...], preferred_element_type=jnp.float32)
           + jnp.dot(ma_ref[...], wa_ref[...], preferred_element_type=jnp.float32))
    x1 = x_ref[...] + gt_ref[0] * mix
    x1_ref[...] = x1
    h2 = _rms(x1) * g_ref[...] * (1.0 + sc_ref[0]) + sh_ref[0]
    _rows_from_2d(h2_ref, h2)

    logit = jnp.dot(h2.astype(jnp.bfloat16), wr_ref[...],
                    preferred_element_type=jnp.float32) + br_ref[...]
    lane = lax.broadcasted_iota(jnp.int32, logit.shape, 1).astype(jnp.float32)
    neg = -jnp.inf
    big = float(ROUTE_LANES)
    is_g = lane < N_GROUPS
    gl = jnp.where(is_g, logit, neg)
    gmax = jnp.max(gl, axis=1, keepdims=True)
    p_group = 1.0 / jnp.sum(jnp.exp(gl - gmax), axis=1, keepdims=True)
    g_sel = jnp.min(jnp.where(is_g & (gl == gmax), lane, big), axis=1, keepdims=True)
    lo = N_GROUPS + g_sel * EXPERTS_PER_GROUP
    in_grp = (lane >= lo) & (lane < lo + EXPERTS_PER_GROUP)
    el = jnp.where(in_grp, logit, neg)
    m1 = jnp.max(el, axis=1, keepdims=True)
    i1 = jnp.min(jnp.where(in_grp & (el == m1), lane, big), axis=1, keepdims=True)
    el2 = jnp.where(lane == i1, neg, el)
    m2 = jnp.max(el2, axis=1, keepdims=True)
    i2 = jnp.min(jnp.where(in_grp & (lane != i1) & (el2 == m2), lane, big), axis=1, keepdims=True)
    r = jnp.exp(m2 - m1)
    w1 = p_group / (1.0 + r)
    w2 = p_group * r / (1.0 + r)

    member = jnp.where(lane == i1, 1.0, jnp.where(lane == i2, 1.0, 0.0))
    tm = member.shape[0]
    earlier = (lax.broadcasted_iota(jnp.int32, (tm, tm), 1)
               < lax.broadcasted_iota(jnp.int32, (tm, tm), 0))
    tri = jnp.where(earlier, 1.0, 0.0).astype(jnp.bfloat16)
    before = jnp.dot(tri, member.astype(jnp.bfloat16),
                     preferred_element_type=jnp.float32) + cnt_scr[...]
    rank1 = jnp.sum(jnp.where(lane == i1, before, 0.0), axis=1, keepdims=True)
    rank2 = jnp.sum(jnp.where(lane == i2, before, 0.0), axis=1, keepdims=True)
    cnt_scr[...] = cnt_scr[...] + jnp.sum(member, axis=0, keepdims=True)
    cnt_ref[...] = cnt_scr[...]

    route = jnp.zeros_like(logit)
    for idx, val in ((R_E1, i1 - N_GROUPS), (R_E2, i2 - N_GROUPS), (R_W1, w1), (R_W2, w2),
                     (R_RANK1, rank1), (R_RANK2, rank2)):
        route = jnp.where(lane == idx, val, route)
    route_ref[...] = route


def _out(mix_c, mix_a, w_out_c, w_out_a, x2, gt1, g_ffn, sc2, sh2, w_route, b_route, seq, tm=512):
    t, d = x2.shape
    per_b = seq // tm
    bmap = lambda m: (m // per_b, 0, 0)
    return pl.pallas_call(
        _out_kernel,
        grid=(t // tm,),
        in_specs=[pl.BlockSpec((tm, CONV_DIM), lambda m: (m, 0)),
                  pl.BlockSpec((tm, ATTN_DIM), lambda m: (m, 0)),
                  pl.BlockSpec((CONV_DIM, d), lambda m: (0, 0)),
                  pl.BlockSpec((ATTN_DIM, d), lambda m: (0, 0)),
                  pl.BlockSpec((tm, d), lambda m: (m, 0)),
                  pl.BlockSpec((1, 1, d), bmap),
                  pl.BlockSpec((1, d), lambda m: (0, 0)),
                  pl.BlockSpec((1, 1, d), bmap),
                  pl.BlockSpec((1, 1, d), bmap),
                  pl.BlockSpec((d, ROUTE_LANES), lambda m: (0, 0)),
                  pl.BlockSpec((1, ROUTE_LANES), lambda m: (0, 0))],
        out_specs=[pl.BlockSpec((tm, d), lambda m: (m, 0)),
                   pl.BlockSpec((tm * ROW_SUB, LANES), lambda m: (m, 0)),
                   pl.BlockSpec((tm, ROUTE_LANES), lambda m: (m, 0)),
                   pl.BlockSpec((1, ROUTE_LANES), lambda m: (0, 0))],
        out_shape=[jax.ShapeDtypeStruct((t, d), jnp.float32),
                   jax.ShapeDtypeStruct((t * ROW_SUB, LANES), jnp.float32),
                   jax.ShapeDtypeStruct((t, ROUTE_LANES), jnp.float32),
                   jax.ShapeDtypeStruct((1, ROUTE_LANES), jnp.float32)],
        scratch_shapes=[pltpu.VMEM((1, ROUTE_LANES), jnp.float32)],
        compiler_params=_cparams(("arbitrary",)),
        name="out",
    )(mix_c, mix_a, w_out_c, w_out_a, x2, gt1, g_ffn, sc2, sh2, w_route, b_route)


def _slot(off_ref, e_ref, r_ref, tok):
    return off_ref[e_ref[tok]] + r_ref[tok]


def _dispatch_kernel(e1_ref, e2_ref, r1_ref, r2_ref, off_ref, zs_ref, nu_ref, h2_ref, xs_ref,
                     zbuf, sem_z, sem_r, *, tm, n_tiles):
    i = pl.program_id(0)

    @pl.when(i == 0)
    def _():
        zbuf[...] = jnp.zeros(zbuf.shape, zbuf.dtype)

        def zero_copy(start):
            rows = pl.ds(pl.multiple_of(start * ROW_SUB, MOE_TM * ROW_SUB), MOE_TM * ROW_SUB)
            return pltpu.make_async_copy(zbuf, xs_ref.at[rows], sem_z)

        for e in range(N_EXPERTS):
            zero_copy(zs_ref[e]).start()
        for e in range(N_EXPERTS):
            zero_copy(zs_ref[e]).wait()

        def zero_tile(k, carry):
            zero_copy(k * MOE_TM).start()
            zero_copy(k * MOE_TM).wait()
            return carry

        lax.fori_loop(nu_ref[0], n_tiles, zero_tile, 0)

    def issue_row(r, carry):
        tok = i * tm + r
        for e_ref, r_ref in ((e1_ref, r1_ref), (e2_ref, r2_ref)):
            pltpu.make_async_copy(_row(h2_ref, r), _row(xs_ref, _slot(off_ref, e_ref, r_ref, tok)),
                                  sem_r).start()
        return carry

    lax.fori_loop(0, tm, issue_row, 0)
    for _ in range(2):
        pltpu.make_async_copy(h2_ref, xs_ref.at[pl.ds(0, tm * ROW_SUB)], sem_r).wait()


def _dispatch(plan, h2, n_rows, tm=256):
    t = h2.shape[0] // ROW_SUB
    kern = functools.partial(_dispatch_kernel, tm=tm, n_tiles=n_rows // MOE_TM)
    return pl.pallas_call(
        kern,
        grid_spec=pltpu.PrefetchScalarGridSpec(
            num_scalar_prefetch=len(plan),
            grid=(t // tm,),
            in_specs=[pl.BlockSpec((tm * ROW_SUB, LANES), lambda i, *_: (i, 0))],
            out_specs=pl.BlockSpec(memory_space=pl.ANY),
            scratch_shapes=[pltpu.VMEM((MOE_TM * ROW_SUB, LANES), jnp.float32),
                            pltpu.SemaphoreType.DMA(()),
                            pltpu.SemaphoreType.DMA(())]),
        out_shape=jax.ShapeDtypeStruct((n_rows * ROW_SUB, LANES), jnp.float32),
        compiler_params=_cparams(("arbitrary",)),
        name="dispatch",
    )(*plan, h2)


def _mlp_kernel(te_ref, nu_ref, first_ref, nxt_ref, slot_ref, xs_ref, wg_hbm, wu_hbm, wd_hbm, ys_ref,
                wg_f32, wu_f32, wd_f32, wg_bf, wu_bf, wd_bf, sem):
    i = pl.program_id(0)

    def copies(e, s):
        return [pltpu.make_async_copy(hbm.at[e], buf.at[s], sem.at[s])
                for hbm, buf in ((wg_hbm, wg_f32), (wu_hbm, wu_f32), (wd_hbm, wd_f32))]

    @pl.when(i == 0)
    def _():
        for cp in copies(te_ref[0], 0):
            cp.start()

    @pl.when((i < nu_ref[0]) & (first_ref[i] == 1))
    def _():
        s = slot_ref[i]
        for cp in copies(te_ref[i], s):
            cp.wait()
        wg_bf[...] = wg_f32[s].astype(jnp.bfloat16)
        wu_bf[...] = wu_f32[s].astype(jnp.bfloat16)
        wd_bf[...] = wd_f32[s].astype(jnp.bfloat16)

        @pl.when(nxt_ref[i] >= 0)
        def _():
            for cp in copies(nxt_ref[i], 1 - s):
                cp.start()

    @pl.when(i < nu_ref[0])
    def _():
        x = _rows_to_2d(xs_ref, MOE_TM).astype(jnp.bfloat16)
        a = jnp.dot(x, wg_bf[...], preferred_element_type=jnp.float32)
        u = jnp.dot(x, wu_bf[...], preferred_element_type=jnp.float32)
        hid = (a * jax.nn.sigmoid(a)) * u
        _rows_from_2d(ys_ref, jnp.dot(hid.astype(jnp.bfloat16), wd_bf[...],
                                      preferred_element_type=jnp.float32))

    @pl.when(i >= nu_ref[0])
    def _():
        ys_ref[...] = jnp.zeros(ys_ref.shape, ys_ref.dtype)


def _mlp(tiles, xs, w_gate, w_up, w_down):
    n_rows = xs.shape[0] // ROW_SUB
    _, d, f = w_gate.shape
    used = lambda i, te, nu, *_: (jnp.minimum(i, nu[0] - 1), 0)
    return pl.pallas_call(
        _mlp_kernel,
        grid_spec=pltpu.PrefetchScalarGridSpec(
            num_scalar_prefetch=len(tiles),
            grid=(n_rows // MOE_TM,),
            in_specs=[pl.BlockSpec((MOE_TM * ROW_SUB, LANES), used),
                      pl.BlockSpec(memory_space=pl.ANY),
                      pl.BlockSpec(memory_space=pl.ANY),
                      pl.BlockSpec(memory_space=pl.ANY)],
            out_specs=pl.BlockSpec((MOE_TM * ROW_SUB, LANES), lambda i, *_: (i, 0)),
            scratch_shapes=[pltpu.VMEM((2, d, f), jnp.float32),
                            pltpu.VMEM((2, d, f), jnp.float32),
                            pltpu.VMEM((2, f, d), jnp.float32),
                            pltpu.VMEM((d, f), jnp.bfloat16),
                            pltpu.VMEM((d, f), jnp.bfloat16),
                            pltpu.VMEM((f, d), jnp.bfloat16),
                            pltpu.SemaphoreType.DMA((2,))]),
        out_shape=jax.ShapeDtypeStruct((n_rows * ROW_SUB, LANES), jnp.float32),
        compiler_params=_cparams(("arbitrary",)),
        name="mlp",
    )(*tiles, xs, w_gate, w_up, w_down)


def _combine_kernel(e1_ref, e2_ref, r1_ref, r2_ref, off_ref, ys_ref, x1_ref, route_ref, gt_ref, g_ref,
                    sc_ref, sh_ref, o_ref, a0, b0, a1, b1, sem, *, tm, n_tiles):
    i = pl.program_id(0)
    bufs = ((a0, b0), (a1, b1))

    def issue(tile, sl):
        def issue_row(r, carry):
            tok = tile * tm + r
            for buf, e_ref, r_ref in ((bufs[sl][0], e1_ref, r1_ref), (bufs[sl][1], e2_ref, r2_ref)):
                pltpu.make_async_copy(_row(ys_ref, _slot(off_ref, e_ref, r_ref, tok)),
                                      _row(buf, r), sem.at[sl]).start()
            return carry

        lax.fori_loop(0, tm, issue_row, 0)

    @pl.when(i == 0)
    def _():
        issue(0, 0)

    for sl in range(2):
        @pl.when(i % 2 == sl)
        def _(sl=sl):
            @pl.when(i + 1 < n_tiles)
            def _():
                issue(i + 1, 1 - sl)

            for buf in bufs[sl]:
                pltpu.make_async_copy(ys_ref.at[pl.ds(0, tm * ROW_SUB)], buf, sem.at[sl]).wait()

            route = route_ref[...]
            y = (route[:, R_W1:R_W1 + 1] * _rows_to_2d(bufs[sl][0], tm)
                 + route[:, R_W2:R_W2 + 1] * _rows_to_2d(bufs[sl][1], tm))
            x = x1_ref[...] + gt_ref[0] * y
            o_ref[...] = _rms(x) * g_ref[...] * (1.0 + sc_ref[0]) + sh_ref[0]


def _combine(plan, ys, x1, route, gt2, g_final, scf, shf, seq, tm=256):
    t, d = x1.shape
    per_b = seq // tm
    n_tiles = t // tm
    bmap = lambda m, *_: (m // per_b, 0, 0)
    kern = functools.partial(_combine_kernel, tm=tm, n_tiles=n_tiles)
    return pl.pallas_call(
        kern,
        grid_spec=pltpu.PrefetchScalarGridSpec(
            num_scalar_prefetch=len(plan),
            grid=(n_tiles,),
            in_specs=[pl.BlockSpec(memory_space=pl.ANY),
                      pl.BlockSpec((tm, d), lambda m, *_: (m, 0)),
                      pl.BlockSpec((tm, ROUTE_LANES), lambda m, *_: (m, 0)),
                      pl.BlockSpec((1, 1, d), bmap),
                      pl.BlockSpec((1, d), lambda m, *_: (0, 0)),
                      pl.BlockSpec((1, 1, d), bmap),
                      pl.BlockSpec((1, 1, d), bmap)],
            out_specs=pl.BlockSpec((tm, d), lambda m, *_: (m, 0)),
            scratch_shapes=[pltpu.VMEM((tm * ROW_SUB, LANES), jnp.float32) for _ in range(4)]
            + [pltpu.SemaphoreType.DMA((2,))]),
        out_shape=jax.ShapeDtypeStruct((t, d), jnp.float32),
        compiler_params=_cparams(("arbitrary",)),
        name="combine",
    )(*plan, ys, x1, route, gt2, g_final, scf, shf)


def _moe_plan(route, counts, t):
    i32 = jnp.int32
    cnt = counts[0, N_GROUPS:N_GROUPS + N_EXPERTS].astype(i32)
    padded = (cnt + MOE_TM - 1) // MOE_TM * MOE_TM
    off_end = jnp.cumsum(padded)
    off = off_end - padded
    n_tiles = (2 * t + N_EXPERTS * MOE_TM) // MOE_TM
    n_used = off_end[-1] // MOE_TM
    tile_start = jnp.arange(n_tiles, dtype=i32) * MOE_TM
    te = jnp.sum((off_end[None, :] <= tile_start[:, None]).astype(i32), axis=1)
    te = jnp.minimum(te, N_EXPERTS - 1)
    tile = jnp.arange(n_tiles, dtype=i32)
    te = jnp.where(tile < n_used, te, te[n_used - 1])
    first = jnp.concatenate([jnp.ones((1,), i32), (te[1:] != te[:-1]).astype(i32)])
    nxt_tile = off_end[te] // MOE_TM
    nxt = jnp.where(nxt_tile < n_used, te[jnp.minimum(nxt_tile, n_tiles - 1)], -1)
    slot = (jnp.cumsum(first) - 1) % 2
    assign = tuple(route[:, k].astype(i32) for k in (R_E1, R_E2, R_RANK1, R_RANK2))
    zstart = jnp.maximum(off_end - MOE_TM, 0)
    n_used = n_used.reshape(1)
    return assign, off, zstart, (te, n_used, first, nxt, slot), n_tiles * MOE_TM


def kernel(x, c, w_ada, b_ada, g_mix, w_in, conv_w, w_uk, kv_norm_g, w_uv, g_conv_out, g_attn_out,
           w_out, g_ffn, w_rg, b_rg, w_re, b_re, w_gate, w_up, w_down, w_ada_f, b_ada_f, g_final):
    b, s, d = x.shape
    assert w_ada.shape[0] == 1, "single layer"
    bf = jnp.bfloat16
    x2 = x.reshape(b * s, d)

    c_pad = jnp.zeros((SUBLANES, d), jnp.float32).at[:b].set(c)
    mod = _mod(c_pad, w_ada[0], b_ada[0])[:b]
    modf = _mod(c_pad, w_ada_f, b_ada_f)[:b]
    vec = lambda a, i: a[:, i * d:(i + 1) * d].reshape(b, 1, d)
    sh1, sc1, gt1, sh2, sc2, gt2 = (vec(mod, i) for i in range(N_MOD))
    shf, scf = vec(modf, 0), vec(modf, 1)
    row = lambda a: a.reshape(1, -1)

    assert PROJ_FULL * PROJ_TN == OFF_IK
    w_tail = jnp.pad(w_in[0][:, OFF_IK:], ((0, 0), (0, IKW_COLS - (IN_COLS - OFF_IK))))
    conv3, q, kv, iq, ikw = _proj(x2, row(g_mix[0]), sc1, sh1, w_in[0], w_tail, s)

    mix_c = _conv(conv3.reshape(b, s, 3 * CONV_DIM), conv_w[0], row(g_conv_out[0]))
    mix_a = _attn(iq.reshape(b, s, -1), ikw.reshape(b, s, -1), kv.reshape(b, s, -1),
                  q.reshape(b, s, -1), w_uk[0].astype(bf), w_uv[0].astype(bf),
                  row(kv_norm_g[0]), row(g_attn_out[0]))

    w_out_bf = w_out[0].astype(bf)
    w_route = jnp.zeros((d, ROUTE_LANES), bf).at[:, :N_GROUPS].set(w_rg[0].astype(bf))
    w_route = w_route.at[:, N_GROUPS:N_GROUPS + N_EXPERTS].set(w_re[0].astype(bf))
    b_route = jnp.zeros((1, ROUTE_LANES), jnp.float32).at[0, :N_GROUPS].set(b_rg[0])
    b_route = b_route.at[0, N_GROUPS:N_GROUPS + N_EXPERTS].set(b_re[0])
    x1, h2, route, counts = _out(mix_c.reshape(b * s, -1), mix_a.reshape(b * s, -1),
                                 w_out_bf[:CONV_DIM], w_out_bf[CONV_DIM:], x2, gt1, row(g_ffn[0]),
                                 sc2, sh2, w_route, b_route, s)

    assign, off, zstart, tiles, n_rows = _moe_plan(route, counts, b * s)
    xs = _dispatch((*assign, off, zstart, tiles[1]), h2, n_rows)
    ys = _mlp(tiles, xs, w_gate[0], w_up[0], w_down[0])
    out = _combine((*assign, off), ys, x1, route, gt2, row(g_final), scf, shf, s)
    return out.reshape(b, s, d)
```

```python
import functools

import jax
import jax.numpy as jnp
from jax import lax
from jax.experimental import pallas as pl
from jax.experimental.pallas import tpu as pltpu

D_MODEL = 2048
CONV_DIM = 1024
CONV_WIDTH = 3
N_HEADS = 8
HEAD_DIM = 128
ATTN_DIM = N_HEADS * HEAD_DIM
KV_RANK = 512
IDX_HEADS = 16
IDX_DIM = 128
TOPK_MAX = 256
N_GROUPS = 4
EXPERTS_PER_GROUP = 8
N_EXPERTS = N_GROUPS * EXPERTS_PER_GROUP
EXPERT_FF = 512
N_MOD = 6
EPS = 1e-6

OFF_Q = 3 * CONV_DIM
OFF_KV = OFF_Q + ATTN_DIM
OFF_IQ = OFF_KV + KV_RANK
OFF_IK = OFF_IQ + IDX_HEADS * IDX_DIM
OFF_IW = OFF_IK + IDX_DIM
IN_COLS = OFF_IW + IDX_HEADS

LANES = 128
SUBLANES = 8
VMEM_LIMIT = 56 * 1024 * 1024

PROJ_TN = 512
PROJ_FULL = OFF_IK // PROJ_TN
IKW_COLS = 256
ATTN_Q = 256
ATTN_KC = 256
MASKED = -1e30
TIE_ALL = 2 ** 30
ROUTE_LANES = 128
OUT_STAGE_ROWS = 512
MOE_TM = 256
ROW_SUB = D_MODEL // LANES
INT_MIN = -2 ** 31
KEY_NEG_INF = INT_MIN + 0x7FFFFF


def _cparams(sem):
    return pltpu.CompilerParams(dimension_semantics=sem, vmem_limit_bytes=VMEM_LIMIT)


def _rms(v, axis=-1):
    return v * lax.rsqrt(jnp.mean(v * v, axis=axis, keepdims=True) + EPS)


def _tile_lanes(v, n):
    return jnp.concatenate([v] * n, axis=1)


def _rows_to_2d(ref, n):
    return jnp.concatenate([ref[pl.ds(c, n, stride=ROW_SUB), :] for c in range(ROW_SUB)], axis=1)


def _rows_from_2d(ref, val):
    n = val.shape[0]
    for c in range(ROW_SUB):
        ref[pl.ds(c, n, stride=ROW_SUB), :] = val[:, c * LANES:(c + 1) * LANES]


def _row(ref, i):
    return ref.at[pl.ds(pl.multiple_of(i * ROW_SUB, ROW_SUB), ROW_SUB)]


def _mod_kernel(c_ref, w_ref, b_ref, o_ref):
    c = c_ref[...]
    ca = (c * jax.nn.sigmoid(c)).astype(jnp.bfloat16)
    o_ref[...] = jnp.dot(ca, w_ref[...].astype(jnp.bfloat16),
                         preferred_element_type=jnp.float32) + b_ref[...]


def _mod(c_pad, w, b, tn=1024):
    d, n = w.shape
    return pl.pallas_call(
        _mod_kernel,
        grid=(n // tn,),
        in_specs=[pl.BlockSpec((c_pad.shape[0], d), lambda j: (0, 0)),
                  pl.BlockSpec((d, tn), lambda j: (0, j)),
                  pl.BlockSpec((1, tn), lambda j: (0, j))],
        out_specs=pl.BlockSpec((c_pad.shape[0], tn), lambda j: (0, j)),
        out_shape=jax.ShapeDtypeStruct((c_pad.shape[0], n), jnp.float32),
        compiler_params=_cparams(("arbitrary",)),
        name="mod",
    )(c_pad, w, b.reshape(1, n))


def _proj_kernel(x_ref, g_ref, sc_ref, sh_ref, w_ref, wt_ref,
                 conv_ref, q_ref, kv_ref, iq_ref, ikw_ref, h_scr):
    n = pl.program_id(1)

    @pl.when(n == 0)
    def _():
        h = _rms(x_ref[...]) * g_ref[...] * (1.0 + sc_ref[0]) + sh_ref[0]
        h_scr[...] = h.astype(jnp.bfloat16)

    def project(o_ref, weights=w_ref):
        o_ref[...] = jnp.dot(h_scr[...], weights[...],
                             preferred_element_type=jnp.float32).astype(o_ref.dtype)

    pl.when(n < 6)(lambda: project(conv_ref))
    pl.when((n >= 6) & (n < 8))(lambda: project(q_ref))
    pl.when(n == 8)(lambda: project(kv_ref))
    pl.when((n >= 9) & (n < 13))(lambda: project(iq_ref))
    pl.when(n == PROJ_FULL)(lambda: project(ikw_ref, wt_ref))


def _proj(x2, g_mix, sc1, sh1, w_in, w_tail, seq, tm=1024):
    t, d = x2.shape
    tn = PROJ_TN
    per_b = seq // tm
    clipn = lambda n, lo, hi: jnp.clip(n - lo, 0, hi - lo)
    return pl.pallas_call(
        _proj_kernel,
        grid=(t // tm, PROJ_FULL + 1),
        in_specs=[pl.BlockSpec((tm, d), lambda m, n: (m, 0)),
                  pl.BlockSpec((1, d), lambda m, n: (0, 0)),
                  pl.BlockSpec((1, 1, d), lambda m, n: (m // per_b, 0, 0)),
                  pl.BlockSpec((1, 1, d), lambda m, n: (m // per_b, 0, 0)),
                  pl.BlockSpec((d, tn), lambda m, n: (0, jnp.minimum(n, PROJ_FULL - 1))),
                  pl.BlockSpec((d, IKW_COLS), lambda m, n: (0, 0))],
        out_specs=[pl.BlockSpec((tm, tn), lambda m, n: (m, clipn(n, 0, 5))),
                   pl.BlockSpec((tm, tn), lambda m, n: (m, clipn(n, 6, 7))),
                   pl.BlockSpec((tm, tn), lambda m, n: (m, 0)),
                   pl.BlockSpec((tm, tn), lambda m, n: (m, clipn(n, 9, 12))),
                   pl.BlockSpec((tm, IKW_COLS), lambda m, n: (m, 0))],
        out_shape=[jax.ShapeDtypeStruct((t, 3 * CONV_DIM), jnp.float32),
                   jax.ShapeDtypeStruct((t, ATTN_DIM), jnp.bfloat16),
                   jax.ShapeDtypeStruct((t, KV_RANK), jnp.float32),
                   jax.ShapeDtypeStruct((t, IDX_HEADS * IDX_DIM), jnp.bfloat16),
                   jax.ShapeDtypeStruct((t, IKW_COLS), jnp.float32)],
        scratch_shapes=[pltpu.VMEM((tm, d), jnp.bfloat16)],
        compiler_params=_cparams(("arbitrary", "arbitrary")),
        name="proj",
    )(x2, g_mix, sc1, sh1, w_in, w_tail)


def _conv_kernel(bg_ref, cg_ref, xv_ref, cgh_ref, xvh_ref, w_ref, g_ref, o_ref):
    i = pl.program_id(1)
    u = cg_ref[0] * xv_ref[0]
    halo = cgh_ref[0] * xvh_ref[0]
    halo = jnp.where(i == 0, 0.0, halo)
    row = lax.broadcasted_iota(jnp.int32, u.shape, 0)
    h1 = halo[SUBLANES - 1:SUBLANES, :]
    h2 = halo[SUBLANES - 2:SUBLANES - 1, :]
    u1 = jnp.where(row == 0, h1, pltpu.roll(u, 1, 0))
    u2 = jnp.where(row == 0, h2, jnp.where(row == 1, h1, pltpu.roll(u, 2, 0)))
    w = w_ref[...]
    y = bg_ref[0] * (w[0:1, :] * u2 + w[1:2, :] * u1 + w[2:3, :] * u)
    o_ref[0] = (_rms(y) * g_ref[...]).astype(o_ref.dtype)


def _conv(conv3, conv_w, g_conv_out, tm=512):
    b, s, _ = conv3.shape
    c = CONV_DIM
    hb = tm // SUBLANES
    halo_map = lambda col: (lambda bi, i: (bi, jnp.maximum(i * hb - 1, 0), col))
    return pl.pallas_call(
        _conv_kernel,
        grid=(b, s // tm),
        in_specs=[pl.BlockSpec((1, tm, c), lambda bi, i: (bi, i, 0)),
                  pl.BlockSpec((1, tm, c), lambda bi, i: (bi, i, 1)),
                  pl.BlockSpec((1, tm, c), lambda bi, i: (bi, i, 2)),
                  pl.BlockSpec((1, SUBLANES, c), halo_map(1)),
                  pl.BlockSpec((1, SUBLANES, c), halo_map(2)),
                  pl.BlockSpec((CONV_WIDTH, c), lambda bi, i: (0, 0)),
                  pl.BlockSpec((1, c), lambda bi, i: (0, 0))],
        out_specs=pl.BlockSpec((1, tm, c), lambda bi, i: (bi, i, 0)),
        out_shape=jax.ShapeDtypeStruct((b, s, c), jnp.bfloat16),
        compiler_params=_cparams(("arbitrary", "arbitrary")),
        name="conv",
    )(conv3, conv3, conv3, conv3, conv3, conv_w, g_conv_out)


def _ordered_bits(v):
    return v ^ ((v >> 31) & jnp.int32(0x7FFFFFFF))


def _attn_search(nch, sc_scr, k_sel):
    n_keys = float(nch * ATTN_KC)
    half = ATTN_Q // 2

    def half_step(bit, thr, rows):
        cand = thr + bit
        cand_f = lax.bitcast_convert_type(_ordered_bits(cand), jnp.float32)
        hits = jnp.where(sc_scr[0, rows, :] >= cand_f, 1.0, 0.0)
        for c in range(1, nch):
            hits = hits + jnp.where(sc_scr[c, rows, :] >= cand_f, 1.0, 0.0)
        cnt = jnp.sum(hits, axis=1, keepdims=True)
        cnt = jnp.where(cand < KEY_NEG_INF, n_keys, cnt)
        return jnp.where(cnt >= k_sel, cand, thr)

    def bit_step(i, thrs):
        bit = lax.shift_left(jnp.int32(1), 31 - i)
        return tuple(half_step(bit, thr, pl.ds(k * half, half)) for k, thr in enumerate(thrs))

    init = jnp.full((half, 1), INT_MIN, jnp.int32)
    thrs = lax.fori_loop(0, 32, bit_step, (init, init), unroll=2)
    thr = jnp.concatenate(thrs, axis=0)
    return lax.bitcast_convert_type(_ordered_bits(thr), jnp.float32)


def _attn_tiebreak(nch, sc_scr, thr_f, k_sel, tie_scr):
    kloc = lax.broadcasted_iota(jnp.int32, (ATTN_Q, ATTN_KC), 1)

    def count(pred):
        hits = pred(sc_scr[0], 0)
        for c in range(1, nch):
            hits = hits + pred(sc_scr[c], c)
        return jnp.sum(hits, axis=1, keepdims=True)

    tie_scr[...] = jnp.full(tie_scr.shape, TIE_ALL, jnp.int32)
    n_ge = count(lambda sc, c: jnp.where(sc >= thr_f, 1.0, 0.0))
    excess = jnp.where(thr_f > -jnp.inf, n_ge - k_sel, 0.0)

    @pl.when(jnp.max(excess) > 0.0)
    def _():
        need = k_sel - count(lambda sc, c: jnp.where(sc > thr_f, 1.0, 0.0))
        n_bits = (nch * ATTN_KC - 1).bit_length()

        def step(i, lim):
            cand = lim + lax.shift_left(jnp.int32(1), n_bits - 1 - i)
            below = count(lambda sc, c: jnp.where(
                sc == thr_f, jnp.where(c * ATTN_KC + kloc < cand, 1.0, 0.0), 0.0))
            return jnp.where(below < need, cand, lim)

        lim = lax.fori_loop(0, n_bits, step, jnp.zeros((ATTN_Q, 1), jnp.int32))
        tie_scr[...] = jnp.where(excess > 0.0, lim, TIE_ALL)


def _attn_kernel(iq_ref, ikwq_ref, ikwa_ref, kv_ref, q_ref, wuk_ref, wuv_ref, kvg_ref, go_ref,
                 o_ref, ckv_scr, ik_scr, sc_scr, thr_scr, tie_scr, ql_scr, m_scr, l_scr, acc_scr, y_scr,
                 *, seq, k_sel):
    j = pl.program_id(1)
    nch = j + 1
    nt = (((1,), (1,)), ((), ()))

    @pl.when(j == 0)
    def _():
        ckv_scr[...] = (_rms(kv_ref[0]) * kvg_ref[...]).astype(jnp.bfloat16)
        ik_scr[...] = ikwa_ref[0][:, :IDX_DIM].astype(jnp.bfloat16)

    iw = ikwq_ref[0][:, IDX_DIM:IDX_DIM + IDX_HEADS] * (IDX_HEADS ** -0.5 * IDX_DIM ** -0.5)
    qpos = j * ATTN_Q + lax.broadcasted_iota(jnp.int32, (ATTN_Q, ATTN_KC), 0)
    kloc = lax.broadcasted_iota(jnp.int32, (ATTN_Q, ATTN_KC), 1)

    def score_chunk(c, carry):
        k0 = pl.multiple_of(c * ATTN_KC, ATTN_KC)
        ik_c = ik_scr[pl.ds(k0, ATTN_KC), :]
        score = jnp.zeros((ATTN_Q, ATTN_KC), jnp.float32)
        for h in range(IDX_HEADS):
            s = lax.dot_general(iq_ref[0][:, h * IDX_DIM:(h + 1) * IDX_DIM], ik_c, nt,
                                preferred_element_type=jnp.float32)
            score = score + jnp.maximum(s, 0.0) * iw[:, h:h + 1]
        sc_scr[c] = jnp.where(k0 + kloc <= qpos, score, -jnp.inf)
        return carry

    lax.fori_loop(0, nch, score_chunk, 0)

    for v in range(seq // ATTN_KC):
        @pl.when(j == v)
        def _(v=v):
            thr_scr[...] = _attn_search(v + 1, sc_scr, k_sel)
            _attn_tiebreak(v + 1, sc_scr, thr_scr[...], k_sel, tie_scr)

    for h in range(N_HEADS):
        ql = lax.dot_general(q_ref[0][:, h * HEAD_DIM:(h + 1) * HEAD_DIM], wuk_ref[h], nt,
                             preferred_element_type=jnp.float32)
        ql_scr[h * ATTN_Q:(h + 1) * ATTN_Q, :] = ql.astype(jnp.bfloat16)
    m_scr[...] = jnp.full(m_scr.shape, MASKED, jnp.float32)
    l_scr[...] = jnp.zeros(l_scr.shape, jnp.float32)
    acc_scr[...] = jnp.zeros(acc_scr.shape, jnp.float32)
    thr_f = thr_scr[...]
    tie = tie_scr[...]
    scale = HEAD_DIM ** -0.5

    def attn_chunk(c, carry):
        k0 = pl.multiple_of(c * ATTN_KC, ATTN_KC)
        ckv_c = ckv_scr[pl.ds(k0, ATTN_KC), :]
        sc = sc_scr[c]
        kpos = k0 + kloc
        tied = jnp.where(sc == thr_f, jnp.where(kpos <= tie, 0.0, MASKED), MASKED)
        bias = jnp.where(kpos <= qpos, jnp.where(sc > thr_f, 0.0, tied), MASKED)
        for h in range(N_HEADS):
            rows = pl.ds(h * ATTN_Q, ATTN_Q)
            lg = lax.dot_general(ql_scr[rows, :], ckv_c, nt,
                                 preferred_element_type=jnp.float32) * scale + bias
            m_old = m_scr[rows, :]
            m_new = jnp.maximum(m_old, jnp.max(lg, axis=1, keepdims=True))
            alpha = jnp.exp(m_old - m_new)
            p = jnp.exp(lg - _tile_lanes(m_new, ATTN_KC // LANES))
            l_scr[rows, :] = alpha * l_scr[rows, :] + jnp.sum(p, axis=1, keepdims=True)
            pv = jnp.dot(p.astype(jnp.bfloat16), ckv_c, preferred_element_type=jnp.float32)
            acc_scr[rows, :] = _tile_lanes(alpha, KV_RANK // LANES) * acc_scr[rows, :] + pv
            m_scr[rows, :] = m_new
        return carry

    lax.fori_loop(0, nch, attn_chunk, 0)

    o = (acc_scr[...] / _tile_lanes(l_scr[...], KV_RANK // LANES)).astype(jnp.bfloat16)
    for h in range(N_HEADS):
        y_scr[:, h * HEAD_DIM:(h + 1) * HEAD_DIM] = jnp.dot(
            o[h * ATTN_Q:(h + 1) * ATTN_Q], wuv_ref[h], preferred_element_type=jnp.float32)
    o_ref[0] = (_rms(y_scr[...]) * go_ref[...]).astype(o_ref.dtype)


def _attn(iq, ikw, kv, q, w_uk_bf, w_uv_bf, kv_norm_g, g_attn_out):
    b, s, _ = iq.shape
    assert ATTN_Q == ATTN_KC and s % ATTN_Q == 0
    k_sel = min(TOPK_MAX, s // 4)
    rows = N_HEADS * ATTN_Q
    kern = functools.partial(_attn_kernel, seq=s, k_sel=k_sel)
    return pl.pallas_call(
        kern,
        grid=(b, s // ATTN_Q),
        in_specs=[pl.BlockSpec((1, ATTN_Q, IDX_HEADS * IDX_DIM), lambda bi, j: (bi, j, 0)),
                  pl.BlockSpec((1, ATTN_Q, IKW_COLS), lambda bi, j: (bi, j, 0)),
                  pl.BlockSpec((1, s, IKW_COLS), lambda bi, j: (bi, 0, 0)),
                  pl.BlockSpec((1, s, KV_RANK), lambda bi, j: (bi, 0, 0)),
                  pl.BlockSpec((1, ATTN_Q, ATTN_DIM), lambda bi, j: (bi, j, 0)),
                  pl.BlockSpec((N_HEADS, KV_RANK, HEAD_DIM), lambda bi, j: (0, 0, 0)),
                  pl.BlockSpec((N_HEADS, KV_RANK, HEAD_DIM), lambda bi, j: (0, 0, 0)),
                  pl.BlockSpec((1, KV_RANK), lambda bi, j: (0, 0)),
                  pl.BlockSpec((1, ATTN_DIM), lambda bi, j: (0, 0))],
        out_specs=pl.BlockSpec((1, ATTN_Q, ATTN_DIM), lambda bi, j: (bi, j, 0)),
        out_shape=jax.ShapeDtypeStruct((b, s, ATTN_DIM), jnp.bfloat16),
        scratch_shapes=[pltpu.VMEM((s, KV_RANK), jnp.bfloat16),
                        pltpu.VMEM((s, IDX_DIM), jnp.bfloat16),
                        pltpu.VMEM((s // ATTN_KC, ATTN_Q, ATTN_KC), jnp.float32),
                        pltpu.VMEM((ATTN_Q, 1), jnp.float32),
                        pltpu.VMEM((ATTN_Q, 1), jnp.int32),
                        pltpu.VMEM((rows, KV_RANK), jnp.bfloat16),
                        pltpu.VMEM((rows, LANES), jnp.float32),
                        pltpu.VMEM((rows, LANES), jnp.float32),
                        pltpu.VMEM((rows, KV_RANK), jnp.float32),
                        pltpu.VMEM((ATTN_Q, ATTN_DIM), jnp.float32)],
        compiler_params=_cparams(("arbitrary", "arbitrary")),
        name="attn",
    )(iq, ikw, ikw, kv, q, w_uk_bf, w_uv_bf, kv_norm_g, g_attn_out)


R_E1, R_E2, R_W1, R_W2, R_RANK1, R_RANK2 = range(6)


def _out_kernel(mc_ref, ma_ref, w_hbm, x_ref, gt_ref, g_ref, sc_ref, sh_ref, wr_ref, br_ref,
                x1_ref, h2_ref, route_ref, cnt_ref, cnt_scr, w_bf, stage, sem):
    @pl.when(pl.program_id(0) == 0)
    def _():
        cnt_scr[...] = jnp.zeros(cnt_scr.shape, jnp.float32)
        rows = stage.shape[0]
        for k in range(w_bf.shape[0] // rows):
            cp = pltpu.make_async_copy(w_hbm.at[pl.ds(k * rows, rows)], stage, sem)
            cp.start()
            cp.wait()
            w_bf[pl.ds(k * rows, rows), :] = stage[...].astype(jnp.bfloat16)

    mix = (jnp.dot(mc_ref[...], w_bf[0:CONV_DIM, :], preferred_element_type=jnp.float32)
           + jnp.dot(ma_ref[...], w_bf[CONV_DIM:, :], preferred_element_type=jnp.float32))
    x1 = x_ref[...] + gt_ref[0] * mix
    x1_ref[...] = x1
    h2 = _rms(x1) * g_ref[...] * (1.0 + sc_ref[0]) + sh_ref[0]
    _rows_from_2d(h2_ref, h2)

    logit = jnp.dot(h2.astype(jnp.bfloat16), wr_ref[...],
                    preferred_element_type=jnp.float32) + br_ref[...]
    lane = lax.broadcasted_iota(jnp.int32, logit.shape, 1).astype(jnp.float32)
    neg = -jnp.inf
    big = float(ROUTE_LANES)
    is_g = lane < N_GROUPS
    gl = jnp.where(is_g, logit, neg)
    gmax = jnp.max(gl, axis=1, keepdims=True)
    p_group = 1.0 / jnp.sum(jnp.exp(gl - gmax), axis=1, keepdims=True)
    g_sel = jnp.min(jnp.where(is_g & (gl == gmax), lane, big), axis=1, keepdims=True)
    lo = N_GROUPS + g_sel * EXPERTS_PER_GROUP
    in_grp = (lane >= lo) & (lane < lo + EXPERTS_PER_GROUP)
    el = jnp.where(in_grp, logit, neg)
    m1 = jnp.max(el, axis=1, keepdims=True)
    i1 = jnp.min(jnp.where(in_grp & (el == m1), lane, big), axis=1, keepdims=True)
    el2 = jnp.where(lane == i1, neg, el)
    m2 = jnp.max(el2, axis=1, keepdims=True)
    i2 = jnp.min(jnp.where(in_grp & (lane != i1) & (el2 == m2), lane, big), axis=1, keepdims=True)
    r = jnp.exp(m2 - m1)
    w1 = p_group / (1.0 + r)
    w2 = p_group * r / (1.0 + r)

    member = jnp.where(lane == i1, 1.0, jnp.where(lane == i2, 1.0, 0.0))
    tm = member.shape[0]
    earlier = (lax.broadcasted_iota(jnp.int32, (tm, tm), 1)
               < lax.broadcasted_iota(jnp.int32, (tm, tm), 0))
    tri = jnp.where(earlier, 1.0, 0.0).astype(jnp.bfloat16)
    before = jnp.dot(tri, member.astype(jnp.bfloat16),
                     preferred_element_type=jnp.float32) + cnt_scr[...]
    rank1 = jnp.sum(jnp.where(lane == i1, before, 0.0), axis=1, keepdims=True)
    rank2 = jnp.sum(jnp.where(lane == i2, before, 0.0), axis=1, keepdims=True)
    cnt_scr[...] = cnt_scr[...] + jnp.sum(member, axis=0, keepdims=True)
    cnt_ref[...] = cnt_scr[...]

    route = jnp.zeros_like(logit)
    for idx, val in ((R_E1, i1 - N_GROUPS), (R_E2, i2 - N_GROUPS), (R_W1, w1), (R_W2, w2),
                     (R_RANK1, rank1), (R_RANK2, rank2)):
        route = jnp.where(lane == idx, val, route)
    route_ref[...] = route


def _out(mix_c, mix_a, w_out, x2, gt1, g_ffn, sc2, sh2, w_route, b_route, seq, tm=512):
    t, d = x2.shape
    per_b = seq // tm
    bmap = lambda m: (m // per_b, 0, 0)
    return pl.pallas_call(
        _out_kernel,
        grid=(t // tm,),
        in_specs=[pl.BlockSpec((tm, CONV_DIM), lambda m: (m, 0)),
                  pl.BlockSpec((tm, ATTN_DIM), lambda m: (m, 0)),
                  pl.BlockSpec(memory_space=pl.ANY),
                  pl.BlockSpec((tm, d), lambda m: (m, 0)),
                  pl.BlockSpec((1, 1, d), bmap),
                  pl.BlockSpec((1, d), lambda m: (0, 0)),
                  pl.BlockSpec((1, 1, d), bmap),
                  pl.BlockSpec((1, 1, d), bmap),
                  pl.BlockSpec((d, ROUTE_LANES), lambda m: (0, 0)),
                  pl.BlockSpec((1, ROUTE_LANES), lambda m: (0, 0))],
        out_specs=[pl.BlockSpec((tm, d), lambda m: (m, 0)),
                   pl.BlockSpec((tm * ROW_SUB, LANES), lambda m: (m, 0)),
                   pl.BlockSpec((tm, ROUTE_LANES), lambda m: (m, 0)),
                   pl.BlockSpec((1, ROUTE_LANES), lambda m: (0, 0))],
        out_shape=[jax.ShapeDtypeStruct((t, d), jnp.float32),
                   jax.ShapeDtypeStruct((t * ROW_SUB, LANES), jnp.float32),
                   jax.ShapeDtypeStruct((t, ROUTE_LANES), jnp.float32),
                   jax.ShapeDtypeStruct((1, ROUTE_LANES), jnp.float32)],
        scratch_shapes=[pltpu.VMEM((1, ROUTE_LANES), jnp.float32),
                        pltpu.VMEM(w_out.shape, jnp.bfloat16),
                        pltpu.VMEM((OUT_STAGE_ROWS, d), jnp.float32),
                        pltpu.SemaphoreType.DMA(())],
        compiler_params=_cparams(("arbitrary",)),
        name="out",
    )(mix_c, mix_a, w_out, x2, gt1, g_ffn, sc2, sh2, w_route, b_route)


def _slot(off_ref, e_ref, r_ref, tok):
    return off_ref[e_ref[tok]] + r_ref[tok]


def _dispatch_kernel(e1_ref, e2_ref, r1_ref, r2_ref, off_ref, zs_ref, nu_ref, h2_ref, xs_ref,
                     zbuf, sem_z, sem_r, *, tm, n_tiles):
    i = pl.program_id(0)

    @pl.when(i == 0)
    def _():
        zbuf[...] = jnp.zeros(zbuf.shape, zbuf.dtype)

        def zero_copy(start):
            rows = pl.ds(pl.multiple_of(start * ROW_SUB, MOE_TM * ROW_SUB), MOE_TM * ROW_SUB)
            return pltpu.make_async_copy(zbuf, xs_ref.at[rows], sem_z)

        for e in range(N_EXPERTS):
            zero_copy(zs_ref[e]).start()
        for e in range(N_EXPERTS):
            zero_copy(zs_ref[e]).wait()

        def zero_tile(k, carry):
            zero_copy(k * MOE_TM).start()
            zero_copy(k * MOE_TM).wait()
            return carry

        lax.fori_loop(nu_ref[0], n_tiles, zero_tile, 0)

    def issue_row(r, carry):
        tok = i * tm + r
        for e_ref, r_ref in ((e1_ref, r1_ref), (e2_ref, r2_ref)):
            pltpu.make_async_copy(_row(h2_ref, r), _row(xs_ref, _slot(off_ref, e_ref, r_ref, tok)),
                                  sem_r).start()
        return carry

    lax.fori_loop(0, tm, issue_row, 0)
    for _ in range(2):
        pltpu.make_async_copy(h2_ref, xs_ref.at[pl.ds(0, tm * ROW_SUB)], sem_r).wait()


def _dispatch(plan, h2, n_rows, tm=256):
    t = h2.shape[0] // ROW_SUB
    kern = functools.partial(_dispatch_kernel, tm=tm, n_tiles=n_rows // MOE_TM)
    return pl.pallas_call(
        kern,
        grid_spec=pltpu.PrefetchScalarGridSpec(
            num_scalar_prefetch=len(plan),
            grid=(t // tm,),
            in_specs=[pl.BlockSpec((tm * ROW_SUB, LANES), lambda i, *_: (i, 0))],
            out_specs=pl.BlockSpec(memory_space=pl.ANY),
            scratch_shapes=[pltpu.VMEM((MOE_TM * ROW_SUB, LANES), jnp.float32),
                            pltpu.SemaphoreType.DMA(()),
                            pltpu.SemaphoreType.DMA(())]),
        out_shape=jax.ShapeDtypeStruct((n_rows * ROW_SUB, LANES), jnp.float32),
        compiler_params=_cparams(("arbitrary",)),
        name="dispatch",
    )(*plan, h2)


def _mlp_kernel(te_ref, nu_ref, first_ref, nxt_ref, slot_ref, xs_ref, wg_hbm, wu_hbm, wd_hbm, ys_ref,
                wg_f32, wu_f32, wd_f32, wg_bf, wu_bf, wd_bf, sem):
    i = pl.program_id(0)

    def copies(e, s):
        return [pltpu.make_async_copy(hbm.at[e], buf.at[s], sem.at[s])
                for hbm, buf in ((wg_hbm, wg_f32), (wu_hbm, wu_f32), (wd_hbm, wd_f32))]

    @pl.when(i == 0)
    def _():
        for cp in copies(te_ref[0], 0):
            cp.start()

    @pl.when((i < nu_ref[0]) & (first_ref[i] == 1))
    def _():
        s = slot_ref[i]
        for cp in copies(te_ref[i], s):
            cp.wait()
        wg_bf[...] = wg_f32[s].astype(jnp.bfloat16)
        wu_bf[...] = wu_f32[s].astype(jnp.bfloat16)
        wd_bf[...] = wd_f32[s].astype(jnp.bfloat16)

        @pl.when(nxt_ref[i] >= 0)
        def _():
            for cp in copies(nxt_ref[i], 1 - s):
                cp.start()

    @pl.when(i < nu_ref[0])
    def _():
        x = _rows_to_2d(xs_ref, MOE_TM).astype(jnp.bfloat16)
        a = jnp.dot(x, wg_bf[...], preferred_element_type=jnp.float32)
        u = jnp.dot(x, wu_bf[...], preferred_element_type=jnp.float32)
        hid = (a * jax.nn.sigmoid(a)) * u
        _rows_from_2d(ys_ref, jnp.dot(hid.astype(jnp.bfloat16), wd_bf[...],
                                      preferred_element_type=jnp.float32))

    @pl.when(i >= nu_ref[0])
    def _():
        ys_ref[...] = jnp.zeros(ys_ref.shape, ys_ref.dtype)


def _mlp(tiles, xs, w_gate, w_up, w_down):
    n_rows = xs.shape[0] // ROW_SUB
    _, d, f = w_gate.shape
    used = lambda i, te, nu, *_: (jnp.minimum(i, nu[0] - 1), 0)
    return pl.pallas_call(
        _mlp_kernel,
        grid_spec=pltpu.PrefetchScalarGridSpec(
            num_scalar_prefetch=len(tiles),
            grid=(n_rows // MOE_TM,),
            in_specs=[pl.BlockSpec((MOE_TM * ROW_SUB, LANES), used),
                      pl.BlockSpec(memory_space=pl.ANY),
                      pl.BlockSpec(memory_space=pl.ANY),
                      pl.BlockSpec(memory_space=pl.ANY)],
            out_specs=pl.BlockSpec((MOE_TM * ROW_SUB, LANES), lambda i, *_: (i, 0)),
            scratch_shapes=[pltpu.VMEM((2, d, f), jnp.float32),
                            pltpu.VMEM((2, d, f), jnp.float32),
                            pltpu.VMEM((2, f, d), jnp.float32),
                            pltpu.VMEM((d, f), jnp.bfloat16),
                            pltpu.VMEM((d, f), jnp.bfloat16),
                            pltpu.VMEM((f, d), jnp.bfloat16),
                            pltpu.SemaphoreType.DMA((2,))]),
        out_shape=jax.ShapeDtypeStruct((n_rows * ROW_SUB, LANES), jnp.float32),
        compiler_params=_cparams(("arbitrary",)),
        name="mlp",
    )(*tiles, xs, w_gate, w_up, w_down)


def _combine_kernel(e1_ref, e2_ref, r1_ref, r2_ref, off_ref, ys_ref, x1_ref, route_ref, gt_ref, g_ref,
                    sc_ref, sh_ref, o_ref, a0, b0, a1, b1, sem, *, tm, n_tiles):
    i = pl.program_id(0)
    bufs = ((a0, b0), (a1, b1))

    def issue(tile, sl):
        def issue_row(r, carry):
            tok = tile * tm + r
            for buf, e_ref, r_ref in ((bufs[sl][0], e1_ref, r1_ref), (bufs[sl][1], e2_ref, r2_ref)):
                pltpu.make_async_copy(_row(ys_ref, _slot(off_ref, e_ref, r_ref, tok)),
                                      _row(buf, r), sem.at[sl]).start()
            return carry

        lax.fori_loop(0, tm, issue_row, 0)

    @pl.when(i == 0)
    def _():
        issue(0, 0)

    for sl in range(2):
        @pl.when(i % 2 == sl)
        def _(sl=sl):
            @pl.when(i + 1 < n_tiles)
            def _():
                issue(i + 1, 1 - sl)

            for buf in bufs[sl]:
                pltpu.make_async_copy(ys_ref.at[pl.ds(0, tm * ROW_SUB)], buf, sem.at[sl]).wait()

            route = route_ref[...]
            y = (route[:, R_W1:R_W1 + 1] * _rows_to_2d(bufs[sl][0], tm)
                 + route[:, R_W2:R_W2 + 1] * _rows_to_2d(bufs[sl][1], tm))
            x = x1_ref[...] + gt_ref[0] * y
            o_ref[...] = _rms(x) * g_ref[...] * (1.0 + sc_ref[0]) + sh_ref[0]


def _combine(plan, ys, x1, route, gt2, g_final, scf, shf, seq, tm=256):
    t, d = x1.shape
    per_b = seq // tm
    n_tiles = t // tm
    bmap = lambda m, *_: (m // per_b, 0, 0)
    kern = functools.partial(_combine_kernel, tm=tm, n_tiles=n_tiles)
    return pl.pallas_call(
        kern,
        grid_spec=pltpu.PrefetchScalarGridSpec(
            num_scalar_prefetch=len(plan),
            grid=(n_tiles,),
            in_specs=[pl.BlockSpec(memory_space=pl.ANY),
                      pl.BlockSpec((tm, d), lambda m, *_: (m, 0)),
                      pl.BlockSpec((tm, ROUTE_LANES), lambda m, *_: (m, 0)),
                      pl.BlockSpec((1, 1, d), bmap),
                      pl.BlockSpec((1, d), lambda m, *_: (0, 0)),
                      pl.BlockSpec((1, 1, d), bmap),
                      pl.BlockSpec((1, 1, d), bmap)],
            out_specs=pl.BlockSpec((tm, d), lambda m, *_: (m, 0)),
            scratch_shapes=[pltpu.VMEM((tm * ROW_SUB, LANES), jnp.float32) for _ in range(4)]
            + [pltpu.SemaphoreType.DMA((2,))]),
        out_shape=jax.ShapeDtypeStruct((t, d), jnp.float32),
        compiler_params=_cparams(("arbitrary",)),
        name="combine",
    )(*plan, ys, x1, route, gt2, g_final, scf, shf)


def _moe_plan(route, counts, t):
    i32 = jnp.int32
    cnt = counts[0, N_GROUPS:N_GROUPS + N_EXPERTS].astype(i32)
    padded = (cnt + MOE_TM - 1) // MOE_TM * MOE_TM
    off_end = jnp.cumsum(padded)
    off = off_end - padded
    n_tiles = (2 * t + N_EXPERTS * MOE_TM) // MOE_TM
    n_used = off_end[-1] // MOE_TM
    tile_start = jnp.arange(n_tiles, dtype=i32) * MOE_TM
    te = jnp.sum((off_end[None, :] <= tile_start[:, None]).astype(i32), axis=1)
    te = jnp.minimum(te, N_EXPERTS - 1)
    tile = jnp.arange(n_tiles, dtype=i32)
    te = jnp.where(tile < n_used, te, te[n_used - 1])
    first = jnp.concatenate([jnp.ones((1,), i32), (te[1:] != te[:-1]).astype(i32)])
    nxt_tile = off_end[te] // MOE_TM
    nxt = jnp.where(nxt_tile < n_used, te[jnp.minimum(nxt_tile, n_tiles - 1)], -1)
    slot = (jnp.cumsum(first) - 1) % 2
    assign = tuple(route[:, k].astype(i32) for k in (R_E1, R_E2, R_RANK1, R_RANK2))
    zstart = jnp.maximum(off_end - MOE_TM, 0)
    n_used = n_used.reshape(1)
    return assign, off, zstart, (te, n_used, first, nxt, slot), n_tiles * MOE_TM


def kernel(x, c, w_ada, b_ada, g_mix, w_in, conv_w, w_uk, kv_norm_g, w_uv, g_conv_out, g_attn_out,
           w_out, g_ffn, w_rg, b_rg, w_re, b_re, w_gate, w_up, w_down, w_ada_f, b_ada_f, g_final):
    b, s, d = x.shape
    assert w_ada.shape[0] == 1, "single layer"
    bf = jnp.bfloat16
    x2 = x.reshape(b * s, d)

    c_pad = jnp.zeros((SUBLANES, d), jnp.float32).at[:b].set(c)
    mod = _mod(c_pad, w_ada[0], b_ada[0])[:b]
    modf = _mod(c_pad, w_ada_f, b_ada_f)[:b]
    vec = lambda a, i: a[:, i * d:(i + 1) * d].reshape(b, 1, d)
    sh1, sc1, gt1, sh2, sc2, gt2 = (vec(mod, i) for i in range(N_MOD))
    shf, scf = vec(modf, 0), vec(modf, 1)
    row = lambda a: a.reshape(1, -1)

    assert PROJ_FULL * PROJ_TN == OFF_IK
    w_tail = jnp.pad(w_in[0][:, OFF_IK:].astype(bf), ((0, 0), (0, IKW_COLS - (IN_COLS - OFF_IK))))
    conv3, q, kv, iq, ikw = _proj(x2, row(g_mix[0]), sc1, sh1, w_in[0][:, :OFF_IK].astype(bf), w_tail, s)

    mix_c = _conv(conv3.reshape(b, s, 3 * CONV_DIM), conv_w[0], row(g_conv_out[0]))
    mix_a = _attn(iq.reshape(b, s, -1), ikw.reshape(b, s, -1), kv.reshape(b, s, -1),
                  q.reshape(b, s, -1), w_uk[0].astype(bf), w_uv[0].astype(bf),
                  row(kv_norm_g[0]), row(g_attn_out[0]))

    w_route = jnp.zeros((d, ROUTE_LANES), bf).at[:, :N_GROUPS].set(w_rg[0].astype(bf))
    w_route = w_route.at[:, N_GROUPS:N_GROUPS + N_EXPERTS].set(w_re[0].astype(bf))
    b_route = jnp.zeros((1, ROUTE_LANES), jnp.float32).at[0, :N_GROUPS].set(b_rg[0])
    b_route = b_route.at[0, N_GROUPS:N_GROUPS + N_EXPERTS].set(b_re[0])
    x1, h2, route, counts = _out(mix_c.reshape(b * s, -1), mix_a.reshape(b * s, -1),
                                 w_out[0], x2, gt1, row(g_ffn[0]),
                                 sc2, sh2, w_route, b_route, s)

    assign, off, zstart, tiles, n_rows = _moe_plan(route, counts, b * s)
    xs = _dispatch((*assign, off, zstart, tiles[1]), h2, n_rows)
    ys = _mlp(tiles, xs, w_gate[0], w_up[0], w_down[0])
    out = _combine((*assign, off), ys, x1, route, gt2, row(g_final), scf, shf, s)
    return out.reshape(b, s, d)
```

```python
import functools

import jax
import jax.numpy as jnp
from jax import lax
from jax.experimental import pallas as pl
from jax.experimental.pallas import tpu as pltpu

D_MODEL = 2048
CONV_DIM = 1024
CONV_WIDTH = 3
N_HEADS = 8
HEAD_DIM = 128
ATTN_DIM = N_HEADS * HEAD_DIM
KV_RANK = 512
IDX_HEADS = 16
IDX_DIM = 128
TOPK_MAX = 256
N_GROUPS = 4
EXPERTS_PER_GROUP = 8
N_EXPERTS = N_GROUPS * EXPERTS_PER_GROUP
EXPERT_FF = 512
N_MOD = 6
EPS = 1e-6

OFF_Q = 3 * CONV_DIM
OFF_KV = OFF_Q + ATTN_DIM
OFF_IQ = OFF_KV + KV_RANK
OFF_IK = OFF_IQ + IDX_HEADS * IDX_DIM
OFF_IW = OFF_IK + IDX_DIM
IN_COLS = OFF_IW + IDX_HEADS

LANES = 128
SUBLANES = 8
VMEM_LIMIT = 56 * 1024 * 1024

IKW_COLS = 256
ATTN_Q = 256
ATTN_KC = 256
MASKED = -1e30
TIE_ALL = 2 ** 30
ROUTE_LANES = 128
OUT_STAGE_ROWS = 512
MOE_TM = 256
ROW_SUB = D_MODEL // LANES
INT_MIN = -2 ** 31
KEY_NEG_INF = INT_MIN + 0x7FFFFF


def _cparams(sem):
    return pltpu.CompilerParams(dimension_semantics=sem, vmem_limit_bytes=VMEM_LIMIT)


def _rms(v, axis=-1):
    return v * lax.rsqrt(jnp.mean(v * v, axis=axis, keepdims=True) + EPS)


def _tile_lanes(v, n):
    return jnp.concatenate([v] * n, axis=1)


def _rows_to_2d(ref, n):
    return jnp.concatenate([ref[pl.ds(c, n, stride=ROW_SUB), :] for c in range(ROW_SUB)], axis=1)


def _rows_from_2d(ref, val):
    n = val.shape[0]
    for c in range(ROW_SUB):
        ref[pl.ds(c, n, stride=ROW_SUB), :] = val[:, c * LANES:(c + 1) * LANES]


def _row(ref, i):
    return ref.at[pl.ds(pl.multiple_of(i * ROW_SUB, ROW_SUB), ROW_SUB)]


def _mod_kernel(c_ref, w_ref, b_ref, o_ref):
    c = c_ref[...]
    ca = (c * jax.nn.sigmoid(c)).astype(jnp.bfloat16)
    o_ref[...] = jnp.dot(ca, w_ref[...].astype(jnp.bfloat16),
                         preferred_element_type=jnp.float32) + b_ref[...]


def _mod(c_pad, w, b, tn=1024):
    d, n = w.shape
    return pl.pallas_call(
        _mod_kernel,
        grid=(n // tn,),
        in_specs=[pl.BlockSpec((c_pad.shape[0], d), lambda j: (0, 0)),
                  pl.BlockSpec((d, tn), lambda j: (0, j)),
                  pl.BlockSpec((1, tn), lambda j: (0, j))],
        out_specs=pl.BlockSpec((c_pad.shape[0], tn), lambda j: (0, j)),
        out_shape=jax.ShapeDtypeStruct((c_pad.shape[0], n), jnp.float32),
        compiler_params=_cparams(("arbitrary",)),
        name="mod",
    )(c_pad, w, b.reshape(1, n))


def _hnorm_kernel(x_ref, g_ref, sc_ref, sh_ref, h_ref):
    h = _rms(x_ref[...]) * g_ref[...] * (1.0 + sc_ref[0]) + sh_ref[0]
    h_ref[...] = h.astype(h_ref.dtype)


def _hnorm(x2, g_mix, sc1, sh1, seq, tm=1024):
    t, d = x2.shape
    per_b = seq // tm
    bmap = lambda m: (m // per_b, 0, 0)
    return pl.pallas_call(
        _hnorm_kernel,
        grid=(t // tm,),
        in_specs=[pl.BlockSpec((tm, d), lambda m: (m, 0)),
                  pl.BlockSpec((1, d), lambda m: (0, 0)),
                  pl.BlockSpec((1, 1, d), bmap),
                  pl.BlockSpec((1, 1, d), bmap)],
        out_specs=pl.BlockSpec((tm, d), lambda m: (m, 0)),
        out_shape=jax.ShapeDtypeStruct((t, d), jnp.bfloat16),
        compiler_params=_cparams(("arbitrary",)),
        name="hnorm",
    )(x2, g_mix, sc1, sh1)


def _colmm_kernel(h_ref, w_ref, o_ref, w_bf):
    @pl.when(pl.program_id(1) == 0)
    def _():
        w_bf[...] = w_ref[...].astype(jnp.bfloat16)

    o_ref[...] = jnp.dot(h_ref[...], w_bf[...],
                         preferred_element_type=jnp.float32).astype(o_ref.dtype)


def _colmm(h, w, col0, ncols, tn, out_dtype, name, tm=1024):
    t, d = h.shape
    assert col0 % tn == 0 and ncols % tn == 0
    return pl.pallas_call(
        _colmm_kernel,
        grid=(ncols // tn, t // tm),
        in_specs=[pl.BlockSpec((tm, d), lambda n, m: (m, 0)),
                  pl.BlockSpec((d, tn), lambda n, m: (0, col0 // tn + n))],
        out_specs=pl.BlockSpec((tm, tn), lambda n, m: (m, n)),
        out_shape=jax.ShapeDtypeStruct((t, ncols), out_dtype),
        scratch_shapes=[pltpu.VMEM((d, tn), jnp.bfloat16)],
        compiler_params=_cparams(("arbitrary", "arbitrary")),
        name=name,
    )(h, w)


def _conv_kernel(bg_ref, cg_ref, xv_ref, cgh_ref, xvh_ref, w_ref, g_ref, o_ref):
    i = pl.program_id(1)
    u = cg_ref[0] * xv_ref[0]
    halo = cgh_ref[0] * xvh_ref[0]
    halo = jnp.where(i == 0, 0.0, halo)
    row = lax.broadcasted_iota(jnp.int32, u.shape, 0)
    h1 = halo[SUBLANES - 1:SUBLANES, :]
    h2 = halo[SUBLANES - 2:SUBLANES - 1, :]
    u1 = jnp.where(row == 0, h1, pltpu.roll(u, 1, 0))
    u2 = jnp.where(row == 0, h2, jnp.where(row == 1, h1, pltpu.roll(u, 2, 0)))
    w = w_ref[...]
    y = bg_ref[0] * (w[0:1, :] * u2 + w[1:2, :] * u1 + w[2:3, :] * u)
    o_ref[0] = (_rms(y) * g_ref[...]).astype(o_ref.dtype)


def _conv(conv3, conv_w, g_conv_out, tm=512):
    b, s, _ = conv3.shape
    c = CONV_DIM
    hb = tm // SUBLANES
    halo_map = lambda col: (lambda bi, i: (bi, jnp.maximum(i * hb - 1, 0), col))
    return pl.pallas_call(
        _conv_kernel,
        grid=(b, s // tm),
        in_specs=[pl.BlockSpec((1, tm, c), lambda bi, i: (bi, i, 0)),
                  pl.BlockSpec((1, tm, c), lambda bi, i: (bi, i, 1)),
                  pl.BlockSpec((1, tm, c), lambda bi, i: (bi, i, 2)),
                  pl.BlockSpec((1, SUBLANES, c), halo_map(1)),
                  pl.BlockSpec((1, SUBLANES, c), halo_map(2)),
                  pl.BlockSpec((CONV_WIDTH, c), lambda bi, i: (0, 0)),
                  pl.BlockSpec((1, c), lambda bi, i: (0, 0))],
        out_specs=pl.BlockSpec((1, tm, c), lambda bi, i: (bi, i, 0)),
        out_shape=jax.ShapeDtypeStruct((b, s, c), jnp.bfloat16),
        compiler_params=_cparams(("arbitrary", "arbitrary")),
        name="conv",
    )(conv3, conv3, conv3, conv3, conv3, conv_w, g_conv_out)


def _ordered_bits(v):
    return v ^ ((v >> 31) & jnp.int32(0x7FFFFFFF))


def _attn_search(nch, sc_scr, k_sel):
    n_keys = float(nch * ATTN_KC)
    half = ATTN_Q // 2

    def half_step(bit, thr, rows):
        cand = thr + bit
        cand_f = lax.bitcast_convert_type(_ordered_bits(cand), jnp.float32)
        hits = jnp.where(sc_scr[0, rows, :] >= cand_f, 1.0, 0.0)
        for c in range(1, nch):
            hits = hits + jnp.where(sc_scr[c, rows, :] >= cand_f, 1.0, 0.0)
        cnt = jnp.sum(hits, axis=1, keepdims=True)
        cnt = jnp.where(cand < KEY_NEG_INF, n_keys, cnt)
        return jnp.where(cnt >= k_sel, cand, thr)

    def bit_step(i, thrs):
        bit = lax.shift_left(jnp.int32(1), 31 - i)
        return tuple(half_step(bit, thr, pl.ds(k * half, half)) for k, thr in enumerate(thrs))

    init = jnp.full((half, 1), INT_MIN, jnp.int32)
    thrs = lax.fori_loop(0, 32, bit_step, (init, init), unroll=2)
    thr = jnp.concatenate(thrs, axis=0)
    return lax.bitcast_convert_type(_ordered_bits(thr), jnp.float32)


def _attn_tiebreak(nch, sc_scr, thr_f, k_sel, tie_scr):
    kloc = lax.broadcasted_iota(jnp.int32, (ATTN_Q, ATTN_KC), 1)

    def count(pred):
        hits = pred(sc_scr[0], 0)
        for c in range(1, nch):
            hits = hits + pred(sc_scr[c], c)
        return jnp.sum(hits, axis=1, keepdims=True)

    tie_scr[...] = jnp.full(tie_scr.shape, TIE_ALL, jnp.int32)
    n_ge = count(lambda sc, c: jnp.where(sc >= thr_f, 1.0, 0.0))
    excess = jnp.where(thr_f > -jnp.inf, n_ge - k_sel, 0.0)

    @pl.when(jnp.max(excess) > 0.0)
    def _():
        need = k_sel - count(lambda sc, c: jnp.where(sc > thr_f, 1.0, 0.0))
        n_bits = (nch * ATTN_KC - 1).bit_length()

        def step(i, lim):
            cand = lim + lax.shift_left(jnp.int32(1), n_bits - 1 - i)
            below = count(lambda sc, c: jnp.where(
                sc == thr_f, jnp.where(c * ATTN_KC + kloc < cand, 1.0, 0.0), 0.0))
            return jnp.where(below < need, cand, lim)

        lim = lax.fori_loop(0, n_bits, step, jnp.zeros((ATTN_Q, 1), jnp.int32))
        tie_scr[...] = jnp.where(excess > 0.0, lim, TIE_ALL)


def _attn_kernel(iq_ref, ikwq_ref, ikwa_ref, kv_ref, q_ref, wuk_ref, wuv_ref, kvg_ref, go_ref,
                 o_ref, ckv_scr, ik_scr, sc_scr, thr_scr, tie_scr, ql_scr, m_scr, l_scr, acc_scr, y_scr,
                 *, seq, k_sel):
    j = pl.program_id(1)
    nch = j + 1
    nt = (((1,), (1,)), ((), ()))

    @pl.when(j == 0)
    def _():
        ckv_scr[...] = (_rms(kv_ref[0]) * kvg_ref[...]).astype(jnp.bfloat16)
        ik_scr[...] = ikwa_ref[0][:, :IDX_DIM].astype(jnp.bfloat16)

    iw = ikwq_ref[0][:, IDX_DIM:IDX_DIM + IDX_HEADS] * (IDX_HEADS ** -0.5 * IDX_DIM ** -0.5)
    qpos = j * ATTN_Q + lax.broadcasted_iota(jnp.int32, (ATTN_Q, ATTN_KC), 0)
    kloc = lax.broadcasted_iota(jnp.int32, (ATTN_Q, ATTN_KC), 1)

    def score_chunk(c, carry):
        k0 = pl.multiple_of(c * ATTN_KC, ATTN_KC)
        ik_c = ik_scr[pl.ds(k0, ATTN_KC), :]
        score = jnp.zeros((ATTN_Q, ATTN_KC), jnp.float32)
        for h in range(IDX_HEADS):
            s = lax.dot_general(iq_ref[0][:, h * IDX_DIM:(h + 1) * IDX_DIM], ik_c, nt,
                                preferred_element_type=jnp.float32)
            score = score + jnp.maximum(s, 0.0) * iw[:, h:h + 1]
        sc_scr[c] = jnp.where(k0 + kloc <= qpos, score, -jnp.inf)
        return carry

    lax.fori_loop(0, nch, score_chunk, 0)

    for v in range(seq // ATTN_KC):
        @pl.when(j == v)
        def _(v=v):
            thr_scr[...] = _attn_search(v + 1, sc_scr, k_sel)
            _attn_tiebreak(v + 1, sc_scr, thr_scr[...], k_sel, tie_scr)

    for h in range(N_HEADS):
        ql = lax.dot_general(q_ref[0][:, h * HEAD_DIM:(h + 1) * HEAD_DIM], wuk_ref[h], nt,
                             preferred_element_type=jnp.float32)
        ql_scr[h * ATTN_Q:(h + 1) * ATTN_Q, :] = ql.astype(jnp.bfloat16)
    m_scr[...] = jnp.full(m_scr.shape, MASKED, jnp.float32)
    l_scr[...] = jnp.zeros(l_scr.shape, jnp.float32)
    acc_scr[...] = jnp.zeros(acc_scr.shape, jnp.float32)
    thr_f = thr_scr[...]
    tie = tie_scr[...]
    scale = HEAD_DIM ** -0.5

    def attn_chunk(c, carry):
        k0 = pl.multiple_of(c * ATTN_KC, ATTN_KC)
        ckv_c = ckv_scr[pl.ds(k0, ATTN_KC), :]
        sc = sc_scr[c]
        kpos = k0 + kloc
        tied = jnp.where(sc == thr_f, jnp.where(kpos <= tie, 0.0, MASKED), MASKED)
        bias = jnp.where(kpos <= qpos, jnp.where(sc > thr_f, 0.0, tied), MASKED)
        for h in range(N_HEADS):
            rows = pl.ds(h * ATTN_Q, ATTN_Q)
            lg = lax.dot_general(ql_scr[rows, :], ckv_c, nt,
                                 preferred_element_type=jnp.float32) * scale + bias
            m_old = m_scr[rows, :]
            m_new = jnp.maximum(m_old, jnp.max(lg, axis=1, keepdims=True))
            alpha = jnp.exp(m_old - m_new)
            p = jnp.exp(lg - _tile_lanes(m_new, ATTN_KC // LANES))
            l_scr[rows, :] = alpha * l_scr[rows, :] + jnp.sum(p, axis=1, keepdims=True)
            pv = jnp.dot(p.astype(jnp.bfloat16), ckv_c, preferred_element_type=jnp.float32)
            acc_scr[rows, :] = _tile_lanes(alpha, KV_RANK // LANES) * acc_scr[rows, :] + pv
            m_scr[rows, :] = m_new
        return carry

    lax.fori_loop(0, nch, attn_chunk, 0)

    o = (acc_scr[...] / _tile_lanes(l_scr[...], KV_RANK // LANES)).astype(jnp.bfloat16)
    for h in range(N_HEADS):
        y_scr[:, h * HEAD_DIM:(h + 1) * HEAD_DIM] = jnp.dot(
            o[h * ATTN_Q:(h + 1) * ATTN_Q], wuv_ref[h], preferred_element_type=jnp.float32)
    o_ref[0] = (_rms(y_scr[...]) * go_ref[...]).astype(o_ref.dtype)


def _attn(iq, ikw, kv, q, w_uk_bf, w_uv_bf, kv_norm_g, g_attn_out):
    b, s, _ = iq.shape
    assert ATTN_Q == ATTN_KC and s % ATTN_Q == 0
    k_sel = min(TOPK_MAX, s // 4)
    rows = N_HEADS * ATTN_Q
    kern = functools.partial(_attn_kernel, seq=s, k_sel=k_sel)
    return pl.pallas_call(
        kern,
        grid=(b, s // ATTN_Q),
        in_specs=[pl.BlockSpec((1, ATTN_Q, IDX_HEADS * IDX_DIM), lambda bi, j: (bi, j, 0)),
                  pl.BlockSpec((1, ATTN_Q, IKW_COLS), lambda bi, j: (bi, j, 0)),
                  pl.BlockSpec((1, s, IKW_COLS), lambda bi, j: (bi, 0, 0)),
                  pl.BlockSpec((1, s, KV_RANK), lambda bi, j: (bi, 0, 0)),
                  pl.BlockSpec((1, ATTN_Q, ATTN_DIM), lambda bi, j: (bi, j, 0)),
                  pl.BlockSpec((N_HEADS, KV_RANK, HEAD_DIM), lambda bi, j: (0, 0, 0)),
                  pl.BlockSpec((N_HEADS, KV_RANK, HEAD_DIM), lambda bi, j: (0, 0, 0)),
                  pl.BlockSpec((1, KV_RANK), lambda bi, j: (0, 0)),
                  pl.BlockSpec((1, ATTN_DIM), lambda bi, j: (0, 0))],
        out_specs=pl.BlockSpec((1, ATTN_Q, ATTN_DIM), lambda bi, j: (bi, j, 0)),
        out_shape=jax.ShapeDtypeStruct((b, s, ATTN_DIM), jnp.bfloat16),
        scratch_shapes=[pltpu.VMEM((s, KV_RANK), jnp.bfloat16),
                        pltpu.VMEM((s, IDX_DIM), jnp.bfloat16),
                        pltpu.VMEM((s // ATTN_KC, ATTN_Q, ATTN_KC), jnp.float32),
                        pltpu.VMEM((ATTN_Q, 1), jnp.float32),
                        pltpu.VMEM((ATTN_Q, 1), jnp.int32),
                        pltpu.VMEM((rows, KV_RANK), jnp.bfloat16),
                        pltpu.VMEM((rows, LANES), jnp.float32),
                        pltpu.VMEM((rows, LANES), jnp.float32),
                        pltpu.VMEM((rows, KV_RANK), jnp.float32),
                        pltpu.VMEM((ATTN_Q, ATTN_DIM), jnp.float32)],
        compiler_params=_cparams(("arbitrary", "arbitrary")),
        name="attn",
    )(iq, ikw, ikw, kv, q, w_uk_bf, w_uv_bf, kv_norm_g, g_attn_out)


R_E1, R_E2, R_W1, R_W2, R_RANK1, R_RANK2 = range(6)


def _out_kernel(mc_ref, ma_ref, w_hbm, x_ref, gt_ref, g_ref, sc_ref, sh_ref, wr_ref, br_ref,
                x1_ref, h2_ref, route_ref, cnt_ref, cnt_scr, w_bf, stage, sem):
    @pl.when(pl.program_id(0) == 0)
    def _():
        cnt_scr[...] = jnp.zeros(cnt_scr.shape, jnp.float32)
        rows = stage.shape[0]
        for k in range(w_bf.shape[0] // rows):
            cp = pltpu.make_async_copy(w_hbm.at[pl.ds(k * rows, rows)], stage, sem)
            cp.start()
            cp.wait()
            w_bf[pl.ds(k * rows, rows), :] = stage[...].astype(jnp.bfloat16)

    mix = (jnp.dot(mc_ref[...], w_bf[0:CONV_DIM, :], preferred_element_type=jnp.float32)
           + jnp.dot(ma_ref[...], w_bf[CONV_DIM:, :], preferred_element_type=jnp.float32))
    x1 = x_ref[...] + gt_ref[0] * mix
    x1_ref[...] = x1
    h2 = _rms(x1) * g_ref[...] * (1.0 + sc_ref[0]) + sh_ref[0]
    _rows_from_2d(h2_ref, h2)

    logit = jnp.dot(h2.astype(jnp.bfloat16), wr_ref[...],
                    preferred_element_type=jnp.float32) + br_ref[...]
    lane = lax.broadcasted_iota(jnp.int32, logit.shape, 1).astype(jnp.float32)
    neg = -jnp.inf
    big = float(ROUTE_LANES)
    is_g = lane < N_GROUPS
    gl = jnp.where(is_g, logit, neg)
    gmax = jnp.max(gl, axis=1, keepdims=True)
    p_group = 1.0 / jnp.sum(jnp.exp(gl - gmax), axis=1, keepdims=True)
    g_sel = jnp.min(jnp.where(is_g & (gl == gmax), lane, big), axis=1, keepdims=True)
    lo = N_GROUPS + g_sel * EXPERTS_PER_GROUP
    in_grp = (lane >= lo) & (lane < lo + EXPERTS_PER_GROUP)
    el = jnp.where(in_grp, logit, neg)
    m1 = jnp.max(el, axis=1, keepdims=True)
    i1 = jnp.min(jnp.where(in_grp & (el == m1), lane, big), axis=1, keepdims=True)
    el2 = jnp.where(lane == i1, neg, el)
    m2 = jnp.max(el2, axis=1, keepdims=True)
    i2 = jnp.min(jnp.where(in_grp & (lane != i1) & (el2 == m2), lane, big), axis=1, keepdims=True)
    r = jnp.exp(m2 - m1)
    w1 = p_group / (1.0 + r)
    w2 = p_group * r / (1.0 + r)

    member = jnp.where(lane == i1, 1.0, jnp.where(lane == i2, 1.0, 0.0))
    tm = member.shape[0]
    earlier = (lax.broadcasted_iota(jnp.int32, (tm, tm), 1)
               < lax.broadcasted_iota(jnp.int32, (tm, tm), 0))
    tri = jnp.where(earlier, 1.0, 0.0).astype(jnp.bfloat16)
    before = jnp.dot(tri, member.astype(jnp.bfloat16),
                     preferred_element_type=jnp.float32) + cnt_scr[...]
    rank1 = jnp.sum(jnp.where(lane == i1, before, 0.0), axis=1, keepdims=True)
    rank2 = jnp.sum(jnp.where(lane == i2, before, 0.0), axis=1, keepdims=True)
    cnt_scr[...] = cnt_scr[...] + jnp.sum(member, axis=0, keepdims=True)
    cnt_ref[...] = cnt_scr[...]

    route = jnp.zeros_like(logit)
    for idx, val in ((R_E1, i1 - N_GROUPS), (R_E2, i2 - N_GROUPS), (R_W1, w1), (R_W2, w2),
                     (R_RANK1, rank1), (R_RANK2, rank2)):
        route = jnp.where(lane == idx, val, route)
    route_ref[...] = route


def _out(mix_c, mix_a, w_out, x2, gt1, g_ffn, sc2, sh2, w_route, b_route, seq, tm=512):
    t, d = x2.shape
    per_b = seq // tm
    bmap = lambda m: (m // per_b, 0, 0)
    return pl.pallas_call(
        _out_kernel,
        grid=(t // tm,),
        in_specs=[pl.BlockSpec((tm, CONV_DIM), lambda m: (m, 0)),
                  pl.BlockSpec((tm, ATTN_DIM), lambda m: (m, 0)),
                  pl.BlockSpec(memory_space=pl.ANY),
                  pl.BlockSpec((tm, d), lambda m: (m, 0)),
                  pl.BlockSpec((1, 1, d), bmap),
                  pl.BlockSpec((1, d), lambda m: (0, 0)),
                  pl.BlockSpec((1, 1, d), bmap),
                  pl.BlockSpec((1, 1, d), bmap),
                  pl.BlockSpec((d, ROUTE_LANES), lambda m: (0, 0)),
                  pl.BlockSpec((1, ROUTE_LANES), lambda m: (0, 0))],
        out_specs=[pl.BlockSpec((tm, d), lambda m: (m, 0)),
                   pl.BlockSpec((tm * ROW_SUB, LANES), lambda m: (m, 0)),
                   pl.BlockSpec((tm, ROUTE_LANES), lambda m: (m, 0)),
                   pl.BlockSpec((1, ROUTE_LANES), lambda m: (0, 0))],
        out_shape=[jax.ShapeDtypeStruct((t, d), jnp.float32),
                   jax.ShapeDtypeStruct((t * ROW_SUB, LANES), jnp.float32),
                   jax.ShapeDtypeStruct((t, ROUTE_LANES), jnp.float32),
                   jax.ShapeDtypeStruct((1, ROUTE_LANES), jnp.float32)],
        scratch_shapes=[pltpu.VMEM((1, ROUTE_LANES), jnp.float32),
                        pltpu.VMEM(w_out.shape, jnp.bfloat16),
                        pltpu.VMEM((OUT_STAGE_ROWS, d), jnp.float32),
                        pltpu.SemaphoreType.DMA(())],
        compiler_params=_cparams(("arbitrary",)),
        name="out",
    )(mix_c, mix_a, w_out, x2, gt1, g_ffn, sc2, sh2, w_route, b_route)


def _slot(off_ref, e_ref, r_ref, tok):
    return off_ref[e_ref[tok]] + r_ref[tok]


def _dispatch_kernel(e1_ref, e2_ref, r1_ref, r2_ref, off_ref, zs_ref, nu_ref, h2_ref, xs_ref,
                     zbuf, sem_z, sem_r, *, tm, n_tiles):
    i = pl.program_id(0)

    @pl.when(i == 0)
    def _():
        zbuf[...] = jnp.zeros(zbuf.shape, zbuf.dtype)

        def zero_copy(start):
            rows = pl.ds(pl.multiple_of(start * ROW_SUB, MOE_TM * ROW_SUB), MOE_TM * ROW_SUB)
            return pltpu.make_async_copy(zbuf, xs_ref.at[rows], sem_z)

        for e in range(N_EXPERTS):
            zero_copy(zs_ref[e]).start()
        for e in range(N_EXPERTS):
            zero_copy(zs_ref[e]).wait()

        def zero_tile(k, carry):
            zero_copy(k * MOE_TM).start()
            zero_copy(k * MOE_TM).wait()
            return carry

        lax.fori_loop(nu_ref[0], n_tiles, zero_tile, 0)

    def issue_row(r, carry):
        tok = i * tm + r
        for e_ref, r_ref in ((e1_ref, r1_ref), (e2_ref, r2_ref)):
            pltpu.make_async_copy(_row(h2_ref, r), _row(xs_ref, _slot(off_ref, e_ref, r_ref, tok)),
                                  sem_r).start()
        return carry

    lax.fori_loop(0, tm, issue_row, 0)
    for _ in range(2):
        pltpu.make_async_copy(h2_ref, xs_ref.at[pl.ds(0, tm * ROW_SUB)], sem_r).wait()


def _dispatch(plan, h2, n_rows, tm=256):
    t = h2.shape[0] // ROW_SUB
    kern = functools.partial(_dispatch_kernel, tm=tm, n_tiles=n_rows // MOE_TM)
    return pl.pallas_call(
        kern,
        grid_spec=pltpu.PrefetchScalarGridSpec(
            num_scalar_prefetch=len(plan),
            grid=(t // tm,),
            in_specs=[pl.BlockSpec((tm * ROW_SUB, LANES), lambda i, *_: (i, 0))],
            out_specs=pl.BlockSpec(memory_space=pl.ANY),
            scratch_shapes=[pltpu.VMEM((MOE_TM * ROW_SUB, LANES), jnp.float32),
                            pltpu.SemaphoreType.DMA(()),
                            pltpu.SemaphoreType.DMA(())]),
        out_shape=jax.ShapeDtypeStruct((n_rows * ROW_SUB, LANES), jnp.float32),
        compiler_params=_cparams(("arbitrary",)),
        name="dispatch",
    )(*plan, h2)


def _mlp_kernel(te_ref, nu_ref, first_ref, nxt_ref, slot_ref, xs_ref, wg_hbm, wu_hbm, wd_hbm, ys_ref,
                wg_f32, wu_f32, wd_f32, wg_bf, wu_bf, wd_bf, sem):
    i = pl.program_id(0)

    def copies(e, s):
        return [pltpu.make_async_copy(hbm.at[e], buf.at[s], sem.at[s])
                for hbm, buf in ((wg_hbm, wg_f32), (wu_hbm, wu_f32), (wd_hbm, wd_f32))]

    @pl.when(i == 0)
    def _():
        for cp in copies(te_ref[0], 0):
            cp.start()

    @pl.when((i < nu_ref[0]) & (first_ref[i] == 1))
    def _():
        s = slot_ref[i]
        for cp in copies(te_ref[i], s):
            cp.wait()
        wg_bf[...] = wg_f32[s].astype(jnp.bfloat16)
        wu_bf[...] = wu_f32[s].astype(jnp.bfloat16)
        wd_bf[...] = wd_f32[s].astype(jnp.bfloat16)

        @pl.when(nxt_ref[i] >= 0)
        def _():
            for cp in copies(nxt_ref[i], 1 - s):
                cp.start()

    @pl.when(i < nu_ref[0])
    def _():
        x = _rows_to_2d(xs_ref, MOE_TM).astype(jnp.bfloat16)
        a = jnp.dot(x, wg_bf[...], preferred_element_type=jnp.float32)
        u = jnp.dot(x, wu_bf[...], preferred_element_type=jnp.float32)
        hid = (a * jax.nn.sigmoid(a)) * u
        _rows_from_2d(ys_ref, jnp.dot(hid.astype(jnp.bfloat16), wd_bf[...],
                                      preferred_element_type=jnp.float32))

    @pl.when(i >= nu_ref[0])
    def _():
        ys_ref[...] = jnp.zeros(ys_ref.shape, ys_ref.dtype)


def _mlp(tiles, xs, w_gate, w_up, w_down):
    n_rows = xs.shape[0] // ROW_SUB
    _, d, f = w_gate.shape
    used = lambda i, te, nu, *_: (jnp.minimum(i, nu[0] - 1), 0)
    return pl.pallas_call(
        _mlp_kernel,
        grid_spec=pltpu.PrefetchScalarGridSpec(
            num_scalar_prefetch=len(tiles),
            grid=(n_rows // MOE_TM,),
            in_specs=[pl.BlockSpec((MOE_TM * ROW_SUB, LANES), used),
                      pl.BlockSpec(memory_space=pl.ANY),
                      pl.BlockSpec(memory_space=pl.ANY),
                      pl.BlockSpec(memory_space=pl.ANY)],
            out_specs=pl.BlockSpec((MOE_TM * ROW_SUB, LANES), lambda i, *_: (i, 0)),
            scratch_shapes=[pltpu.VMEM((2, d, f), jnp.float32),
                            pltpu.VMEM((2, d, f), jnp.float32),
                            pltpu.VMEM((2, f, d), jnp.float32),
                            pltpu.VMEM((d, f), jnp.bfloat16),
                            pltpu.VMEM((d, f), jnp.bfloat16),
                            pltpu.VMEM((f, d), jnp.bfloat16),
                            pltpu.SemaphoreType.DMA((2,))]),
        out_shape=jax.ShapeDtypeStruct((n_rows * ROW_SUB, LANES), jnp.float32),
        compiler_params=_cparams(("arbitrary",)),
        name="mlp",
    )(*tiles, xs, w_gate, w_up, w_down)


def _combine_kernel(e1_ref, e2_ref, r1_ref, r2_ref, off_ref, ys_ref, x1_ref, route_ref, gt_ref, g_ref,
                    sc_ref, sh_ref, o_ref, a0, b0, a1, b1, sem, *, tm, n_tiles):
    i = pl.program_id(0)
    bufs = ((a0, b0), (a1, b1))

    def issue(tile, sl):
        def issue_row(r, carry):
            tok = tile * tm + r
            for buf, e_ref, r_ref in ((bufs[sl][0], e1_ref, r1_ref), (bufs[sl][1], e2_ref, r2_ref)):
                pltpu.make_async_copy(_row(ys_ref, _slot(off_ref, e_ref, r_ref, tok)),
                                      _row(buf, r), sem.at[sl]).start()
            return carry

        lax.fori_loop(0, tm, issue_row, 0)

    @pl.when(i == 0)
    def _():
        issue(0, 0)

    for sl in range(2):
        @pl.when(i % 2 == sl)
        def _(sl=sl):
            @pl.when(i + 1 < n_tiles)
            def _():
                issue(i + 1, 1 - sl)

            for buf in bufs[sl]:
                pltpu.make_async_copy(ys_ref.at[pl.ds(0, tm * ROW_SUB)], buf, sem.at[sl]).wait()

            route = route_ref[...]
            y = (route[:, R_W1:R_W1 + 1] * _rows_to_2d(bufs[sl][0], tm)
                 + route[:, R_W2:R_W2 + 1] * _rows_to_2d(bufs[sl][1], tm))
            x = x1_ref[...] + gt_ref[0] * y
            o_ref[...] = _rms(x) * g_ref[...] * (1.0 + sc_ref[0]) + sh_ref[0]


def _combine(plan, ys, x1, route, gt2, g_final, scf, shf, seq, tm=256):
    t, d = x1.shape
    per_b = seq // tm
    n_tiles = t // tm
    bmap = lambda m, *_: (m // per_b, 0, 0)
    kern = functools.partial(_combine_kernel, tm=tm, n_tiles=n_tiles)
    return pl.pallas_call(
        kern,
        grid_spec=pltpu.PrefetchScalarGridSpec(
            num_scalar_prefetch=len(plan),
            grid=(n_tiles,),
            in_specs=[pl.BlockSpec(memory_space=pl.ANY),
                      pl.BlockSpec((tm, d), lambda m, *_: (m, 0)),
                      pl.BlockSpec((tm, ROUTE_LANES), lambda m, *_: (m, 0)),
                      pl.BlockSpec((1, 1, d), bmap),
                      pl.BlockSpec((1, d), lambda m, *_: (0, 0)),
                      pl.BlockSpec((1, 1, d), bmap),
                      pl.BlockSpec((1, 1, d), bmap)],
            out_specs=pl.BlockSpec((tm, d), lambda m, *_: (m, 0)),
            scratch_shapes=[pltpu.VMEM((tm * ROW_SUB, LANES), jnp.float32) for _ in range(4)]
            + [pltpu.SemaphoreType.DMA((2,))]),
        out_shape=jax.ShapeDtypeStruct((t, d), jnp.float32),
        compiler_params=_cparams(("arbitrary",)),
        name="combine",
    )(*plan, ys, x1, route, gt2, g_final, scf, shf)


def _moe_plan(route, counts, t):
    i32 = jnp.int32
    cnt = counts[0, N_GROUPS:N_GROUPS + N_EXPERTS].astype(i32)
    padded = (cnt + MOE_TM - 1) // MOE_TM * MOE_TM
    off_end = jnp.cumsum(padded)
    off = off_end - padded
    n_tiles = (2 * t + N_EXPERTS * MOE_TM) // MOE_TM
    n_used = off_end[-1] // MOE_TM
    tile_start = jnp.arange(n_tiles, dtype=i32) * MOE_TM
    te = jnp.sum((off_end[None, :] <= tile_start[:, None]).astype(i32), axis=1)
    te = jnp.minimum(te, N_EXPERTS - 1)
    tile = jnp.arange(n_tiles, dtype=i32)
    te = jnp.where(tile < n_used, te, te[n_used - 1])
    first = jnp.concatenate([jnp.ones((1,), i32), (te[1:] != te[:-1]).astype(i32)])
    nxt_tile = off_end[te] // MOE_TM
    nxt = jnp.where(nxt_tile < n_used, te[jnp.minimum(nxt_tile, n_tiles - 1)], -1)
    slot = (jnp.cumsum(first) - 1) % 2
    assign = tuple(route[:, k].astype(i32) for k in (R_E1, R_E2, R_RANK1, R_RANK2))
    zstart = jnp.maximum(off_end - MOE_TM, 0)
    n_used = n_used.reshape(1)
    return assign, off, zstart, (te, n_used, first, nxt, slot), n_tiles * MOE_TM


def kernel(x, c, w_ada, b_ada, g_mix, w_in, conv_w, w_uk, kv_norm_g, w_uv, g_conv_out, g_attn_out,
           w_out, g_ffn, w_rg, b_rg, w_re, b_re, w_gate, w_up, w_down, w_ada_f, b_ada_f, g_final):
    b, s, d = x.shape
    assert w_ada.shape[0] == 1, "single layer"
    bf = jnp.bfloat16
    x2 = x.reshape(b * s, d)

    c_pad = jnp.zeros((SUBLANES, d), jnp.float32).at[:b].set(c)
    mod = _mod(c_pad, w_ada[0], b_ada[0])[:b]
    modf = _mod(c_pad, w_ada_f, b_ada_f)[:b]
    vec = lambda a, i: a[:, i * d:(i + 1) * d].reshape(b, 1, d)
    sh1, sc1, gt1, sh2, sc2, gt2 = (vec(mod, i) for i in range(N_MOD))
    shf, scf = vec(modf, 0), vec(modf, 1)
    row = lambda a: a.reshape(1, -1)

    h = _hnorm(x2, row(g_mix[0]), sc1, sh1, s)
    w_tail = jnp.pad(w_in[0][:, OFF_IK:], ((0, 0), (0, IKW_COLS - (IN_COLS - OFF_IK))))
    conv3 = _colmm(h, w_in[0], 0, 3 * CONV_DIM, 1024, jnp.float32, "proj_conv")
    q = _colmm(h, w_in[0], OFF_Q, ATTN_DIM, 1024, bf, "proj_q")
    kv = _colmm(h, w_in[0], OFF_KV, KV_RANK, 512, jnp.float32, "proj_kv")
    iq = _colmm(h, w_in[0], OFF_IQ, IDX_HEADS * IDX_DIM, 512, bf, "proj_iq")
    ikw = _colmm(h, w_tail, 0, IKW_COLS, IKW_COLS, jnp.float32, "proj_ikw")

    mix_c = _conv(conv3.reshape(b, s, 3 * CONV_DIM), conv_w[0], row(g_conv_out[0]))
    mix_a = _attn(iq.reshape(b, s, -1), ikw.reshape(b, s, -1), kv.reshape(b, s, -1),
                  q.reshape(b, s, -1), w_uk[0].astype(bf), w_uv[0].astype(bf),
                  row(kv_norm_g[0]), row(g_attn_out[0]))

    w_route = jnp.zeros((d, ROUTE_LANES), bf).at[:, :N_GROUPS].set(w_rg[0].astype(bf))
    w_route = w_route.at[:, N_GROUPS:N_GROUPS + N_EXPERTS].set(w_re[0].astype(bf))
    b_route = jnp.zeros((1, ROUTE_LANES), jnp.float32).at[0, :N_GROUPS].set(b_rg[0])
    b_route = b_route.at[0, N_GROUPS:N_GROUPS + N_EXPERTS].set(b_re[0])
    x1, h2, route, counts = _out(mix_c.reshape(b * s, -1), mix_a.reshape(b * s, -1),
                                 w_out[0], x2, gt1, row(g_ffn[0]),
                                 sc2, sh2, w_route, b_route, s)

    assign, off, zstart, tiles, n_rows = _moe_plan(route, counts, b * s)
    xs = _dispatch((*assign, off, zstart, tiles[1]), h2, n_rows)
    ys = _mlp(tiles, xs, w_gate[0], w_up[0], w_down[0])
    out = _combine((*assign, off), ys, x1, route, gt2, row(g_final), scf, shf, s)
    return out.reshape(b, s, d)
```

```python
import functools

import jax
import jax.numpy as jnp
from jax import lax
from jax.experimental import pallas as pl
from jax.experimental.pallas import tpu as pltpu

D_MODEL = 2048
CONV_DIM = 1024
CONV_WIDTH = 3
N_HEADS = 8
HEAD_DIM = 128
ATTN_DIM = N_HEADS * HEAD_DIM
KV_RANK = 512
IDX_HEADS = 16
IDX_DIM = 128
TOPK_MAX = 256
N_GROUPS = 4
EXPERTS_PER_GROUP = 8
N_EXPERTS = N_GROUPS * EXPERTS_PER_GROUP
EXPERT_FF = 512
N_MOD = 6
EPS = 1e-6

OFF_Q = 3 * CONV_DIM
OFF_KV = OFF_Q + ATTN_DIM
OFF_IQ = OFF_KV + KV_RANK
OFF_IK = OFF_IQ + IDX_HEADS * IDX_DIM
OFF_IW = OFF_IK + IDX_DIM
IN_COLS = OFF_IW + IDX_HEADS

LANES = 128
SUBLANES = 8
VMEM_LIMIT = 56 * 1024 * 1024

IKW_COLS = 256
ATTN_Q = 256
ATTN_KC = 256
ATTN_RB = 256
MASKED = -1e30
TIE_ALL = 2 ** 30
ROUTE_LANES = 128
OUT_STAGE_ROWS = 512
MOE_TM = 256
ROW_SUB = D_MODEL // LANES
INT_MIN = -2 ** 31
KEY_NEG_INF = INT_MIN + 0x7FFFFF


def _cparams(sem):
    return pltpu.CompilerParams(dimension_semantics=sem, vmem_limit_bytes=VMEM_LIMIT)


def _rms(v, axis=-1):
    return v * lax.rsqrt(jnp.mean(v * v, axis=axis, keepdims=True) + EPS)


def _tile_lanes(v, n):
    return jnp.concatenate([v] * n, axis=1)


def _rows_to_2d(ref, n):
    return jnp.concatenate([ref[pl.ds(c, n, stride=ROW_SUB), :] for c in range(ROW_SUB)], axis=1)


def _rows_from_2d(ref, val):
    n = val.shape[0]
    for c in range(ROW_SUB):
        ref[pl.ds(c, n, stride=ROW_SUB), :] = val[:, c * LANES:(c + 1) * LANES]


def _row(ref, i):
    return ref.at[pl.ds(pl.multiple_of(i * ROW_SUB, ROW_SUB), ROW_SUB)]


def _mod_kernel(c_ref, w_ref, b_ref, o_ref):
    c = c_ref[...]
    ca = (c * jax.nn.sigmoid(c)).astype(jnp.bfloat16)
    o_ref[...] = jnp.dot(ca, w_ref[...].astype(jnp.bfloat16),
                         preferred_element_type=jnp.float32) + b_ref[...]


def _mod(c_pad, w, b, tn=1024):
    d, n = w.shape
    return pl.pallas_call(
        _mod_kernel,
        grid=(n // tn,),
        in_specs=[pl.BlockSpec((c_pad.shape[0], d), lambda j: (0, 0)),
                  pl.BlockSpec((d, tn), lambda j: (0, j)),
                  pl.BlockSpec((1, tn), lambda j: (0, j))],
        out_specs=pl.BlockSpec((c_pad.shape[0], tn), lambda j: (0, j)),
        out_shape=jax.ShapeDtypeStruct((c_pad.shape[0], n), jnp.float32),
        compiler_params=_cparams(("arbitrary",)),
        name="mod",
    )(c_pad, w, b.reshape(1, n))


def _hnorm_kernel(x_ref, g_ref, sc_ref, sh_ref, h_ref):
    h = _rms(x_ref[...]) * g_ref[...] * (1.0 + sc_ref[0]) + sh_ref[0]
    h_ref[...] = h.astype(h_ref.dtype)


def _hnorm(x2, g_mix, sc1, sh1, seq, tm=1024):
    t, d = x2.shape
    per_b = seq // tm
    bmap = lambda m: (m // per_b, 0, 0)
    return pl.pallas_call(
        _hnorm_kernel,
        grid=(t // tm,),
        in_specs=[pl.BlockSpec((tm, d), lambda m: (m, 0)),
                  pl.BlockSpec((1, d), lambda m: (0, 0)),
                  pl.BlockSpec((1, 1, d), bmap),
                  pl.BlockSpec((1, 1, d), bmap)],
        out_specs=pl.BlockSpec((tm, d), lambda m: (m, 0)),
        out_shape=jax.ShapeDtypeStruct((t, d), jnp.bfloat16),
        compiler_params=_cparams(("arbitrary",)),
        name="hnorm",
    )(x2, g_mix, sc1, sh1)


def _colmm_kernel(h_ref, wt_ref, o_ref, w_bf):
    @pl.when(pl.program_id(1) == 0)
    def _():
        w_bf[...] = wt_ref[...].astype(jnp.bfloat16)

    o_ref[...] = lax.dot_general(h_ref[...], w_bf[...], (((1,), (1,)), ((), ())),
                                 preferred_element_type=jnp.float32).astype(o_ref.dtype)


def _colmm(h, wt, col0, ncols, tn, out_dtype, name, tm=1024):
    t, d = h.shape
    assert col0 % tn == 0 and ncols % tn == 0
    return pl.pallas_call(
        _colmm_kernel,
        grid=(ncols // tn, t // tm),
        in_specs=[pl.BlockSpec((tm, d), lambda n, m: (m, 0)),
                  pl.BlockSpec((tn, d), lambda n, m: (col0 // tn + n, 0))],
        out_specs=pl.BlockSpec((tm, tn), lambda n, m: (m, n)),
        out_shape=jax.ShapeDtypeStruct((t, ncols), out_dtype),
        scratch_shapes=[pltpu.VMEM((tn, d), jnp.bfloat16)],
        compiler_params=_cparams(("arbitrary", "arbitrary")),
        name=name,
    )(h, wt)


def _conv_kernel(bg_ref, cg_ref, xv_ref, cgh_ref, xvh_ref, w_ref, g_ref, o_ref):
    i = pl.program_id(1)
    u = cg_ref[0] * xv_ref[0]
    halo = cgh_ref[0] * xvh_ref[0]
    halo = jnp.where(i == 0, 0.0, halo)
    row = lax.broadcasted_iota(jnp.int32, u.shape, 0)
    h1 = halo[SUBLANES - 1:SUBLANES, :]
    h2 = halo[SUBLANES - 2:SUBLANES - 1, :]
    u1 = jnp.where(row == 0, h1, pltpu.roll(u, 1, 0))
    u2 = jnp.where(row == 0, h2, jnp.where(row == 1, h1, pltpu.roll(u, 2, 0)))
    w = w_ref[...]
    y = bg_ref[0] * (w[0:1, :] * u2 + w[1:2, :] * u1 + w[2:3, :] * u)
    o_ref[0] = (_rms(y) * g_ref[...]).astype(o_ref.dtype)


def _conv(conv3, conv_w, g_conv_out, tm=512):
    b, s, _ = conv3.shape
    c = CONV_DIM
    hb = tm // SUBLANES
    halo_map = lambda col: (lambda bi, i: (bi, jnp.maximum(i * hb - 1, 0), col))
    return pl.pallas_call(
        _conv_kernel,
        grid=(b, s // tm),
        in_specs=[pl.BlockSpec((1, tm, c), lambda bi, i: (bi, i, 0)),
                  pl.BlockSpec((1, tm, c), lambda bi, i: (bi, i, 1)),
                  pl.BlockSpec((1, tm, c), lambda bi, i: (bi, i, 2)),
                  pl.BlockSpec((1, SUBLANES, c), halo_map(1)),
                  pl.BlockSpec((1, SUBLANES, c), halo_map(2)),
                  pl.BlockSpec((CONV_WIDTH, c), lambda bi, i: (0, 0)),
                  pl.BlockSpec((1, c), lambda bi, i: (0, 0))],
        out_specs=pl.BlockSpec((1, tm, c), lambda bi, i: (bi, i, 0)),
        out_shape=jax.ShapeDtypeStruct((b, s, c), jnp.bfloat16),
        compiler_params=_cparams(("arbitrary", "arbitrary")),
        name="conv",
    )(conv3, conv3, conv3, conv3, conv3, conv_w, g_conv_out)


def _ordered_bits(v):
    return v ^ ((v >> 31) & jnp.int32(0x7FFFFFFF))


def _attn_search(nch, sc_scr, k_sel):
    n_keys = float(nch * ATTN_KC)
    half = ATTN_Q // 2

    def half_step(bit, thr, rows):
        cand = thr + bit
        cand_f = lax.bitcast_convert_type(_ordered_bits(cand), jnp.float32)
        hits = jnp.where(sc_scr[0, rows, :] >= cand_f, 1.0, 0.0)
        for c in range(1, nch):
            hits = hits + jnp.where(sc_scr[c, rows, :] >= cand_f, 1.0, 0.0)
        cnt = jnp.sum(hits, axis=1, keepdims=True)
        cnt = jnp.where(cand < KEY_NEG_INF, n_keys, cnt)
        return jnp.where(cnt >= k_sel, cand, thr)

    def bit_step(i, thrs):
        bit = lax.shift_left(jnp.int32(1), 31 - i)
        return tuple(half_step(bit, thr, pl.ds(k * half, half)) for k, thr in enumerate(thrs))

    init = jnp.full((half, 1), INT_MIN, jnp.int32)
    thrs = lax.fori_loop(0, 32, bit_step, (init, init), unroll=2)
    thr = jnp.concatenate(thrs, axis=0)
    return lax.bitcast_convert_type(_ordered_bits(thr), jnp.float32)


def _attn_tiebreak(nch, sc_scr, thr_f, k_sel, tie_scr):
    kloc = lax.broadcasted_iota(jnp.int32, (ATTN_Q, ATTN_KC), 1)

    def count(pred):
        hits = pred(sc_scr[0], 0)
        for c in range(1, nch):
            hits = hits + pred(sc_scr[c], c)
        return jnp.sum(hits, axis=1, keepdims=True)

    tie_scr[...] = jnp.full(tie_scr.shape, TIE_ALL, jnp.int32)
    n_ge = count(lambda sc, c: jnp.where(sc >= thr_f, 1.0, 0.0))
    excess = jnp.where(thr_f > -jnp.inf, n_ge - k_sel, 0.0)

    @pl.when(jnp.max(excess) > 0.0)
    def _():
        need = k_sel - count(lambda sc, c: jnp.where(sc > thr_f, 1.0, 0.0))
        n_bits = (nch * ATTN_KC - 1).bit_length()

        def step(i, lim):
            cand = lim + lax.shift_left(jnp.int32(1), n_bits - 1 - i)
            below = count(lambda sc, c: jnp.where(
                sc == thr_f, jnp.where(c * ATTN_KC + kloc < cand, 1.0, 0.0), 0.0))
            return jnp.where(below < need, cand, lim)

        lim = lax.fori_loop(0, n_bits, step, jnp.zeros((ATTN_Q, 1), jnp.int32))
        tie_scr[...] = jnp.where(excess > 0.0, lim, TIE_ALL)


def _attn_kernel(iq_ref, ikwq_ref, ikwa_ref, kv_ref, q_ref, wuk_ref, wuv_ref, kvg_ref, go_ref,
                 o_ref, ckv_scr, ik_scr, sc_scr, thr_scr, tie_scr, ql_scr, m_scr, l_scr, acc_scr, y_scr,
                 *, seq, k_sel):
    j = pl.program_id(1)
    nch = j + 1
    nt = (((1,), (1,)), ((), ()))

    @pl.when(j == 0)
    def _():
        ckv_scr[...] = (_rms(kv_ref[0]) * kvg_ref[...]).astype(jnp.bfloat16)
        ik_scr[...] = ikwa_ref[0][:, :IDX_DIM].astype(jnp.bfloat16)

    iw = ikwq_ref[0][:, IDX_DIM:IDX_DIM + IDX_HEADS] * (IDX_HEADS ** -0.5 * IDX_DIM ** -0.5)
    qpos = j * ATTN_Q + lax.broadcasted_iota(jnp.int32, (ATTN_Q, ATTN_KC), 0)
    kloc = lax.broadcasted_iota(jnp.int32, (ATTN_Q, ATTN_KC), 1)

    def score_chunk(c, carry):
        k0 = pl.multiple_of(c * ATTN_KC, ATTN_KC)
        ik_c = ik_scr[pl.ds(k0, ATTN_KC), :]
        score = jnp.zeros((ATTN_Q, ATTN_KC), jnp.float32)
        for h in range(IDX_HEADS):
            s = lax.dot_general(iq_ref[0][:, h * IDX_DIM:(h + 1) * IDX_DIM], ik_c, nt,
                                preferred_element_type=jnp.float32)
            score = score + jnp.maximum(s, 0.0) * iw[:, h:h + 1]
        sc_scr[c] = jnp.where(k0 + kloc <= qpos, score, -jnp.inf)
        return carry

    lax.fori_loop(0, nch, score_chunk, 0)

    for v in range(seq // ATTN_KC):
        @pl.when(j == v)
        def _(v=v):
            thr_scr[...] = _attn_search(v + 1, sc_scr, k_sel)
            _attn_tiebreak(v + 1, sc_scr, thr_scr[...], k_sel, tie_scr)

    for h in range(N_HEADS):
        ql = lax.dot_general(q_ref[0][:, h * HEAD_DIM:(h + 1) * HEAD_DIM], wuk_ref[h], nt,
                             preferred_element_type=jnp.float32)
        ql_scr[h * ATTN_Q:(h + 1) * ATTN_Q, :] = ql.astype(jnp.bfloat16)
    m_scr[...] = jnp.full(m_scr.shape, MASKED, jnp.float32)
    l_scr[...] = jnp.zeros(l_scr.shape, jnp.float32)
    acc_scr[...] = jnp.zeros(acc_scr.shape, jnp.float32)
    thr_f = thr_scr[...]
    tie = tie_scr[...]
    scale = HEAD_DIM ** -0.5

    def attn_chunk(c, carry):
        k0 = pl.multiple_of(c * ATTN_KC, ATTN_KC)
        ckv_c = ckv_scr[pl.ds(k0, ATTN_KC), :]
        sc = sc_scr[c]
        kpos = k0 + kloc
        tied = jnp.where(sc == thr_f, jnp.where(kpos <= tie, 0.0, MASKED), MASKED)
        bias = jnp.where(kpos <= qpos, jnp.where(sc > thr_f, 0.0, tied), MASKED)
        for u in range(N_HEADS * ATTN_Q // ATTN_RB):
            rows = pl.ds(u * ATTN_RB, ATTN_RB)
            qrow = (u * ATTN_RB) % ATTN_Q
            lg = lax.dot_general(ql_scr[rows, :], ckv_c, nt,
                                 preferred_element_type=jnp.float32) * scale + bias[qrow:qrow + ATTN_RB]
            m_old = m_scr[rows, :]
            m_new = jnp.maximum(m_old, jnp.max(lg, axis=1, keepdims=True))
            alpha = jnp.exp(m_old - m_new)
            p = jnp.exp(lg - _tile_lanes(m_new, ATTN_KC // LANES))
            l_scr[rows, :] = alpha * l_scr[rows, :] + jnp.sum(p, axis=1, keepdims=True)
            pv = jnp.dot(p.astype(jnp.bfloat16), ckv_c, preferred_element_type=jnp.float32)
            acc_scr[rows, :] = _tile_lanes(alpha, KV_RANK // LANES) * acc_scr[rows, :] + pv
            m_scr[rows, :] = m_new
        return carry

    lax.fori_loop(0, nch, attn_chunk, 0)

    o = (acc_scr[...] / _tile_lanes(l_scr[...], KV_RANK // LANES)).astype(jnp.bfloat16)
    for h in range(N_HEADS):
        y_scr[:, h * HEAD_DIM:(h + 1) * HEAD_DIM] = jnp.dot(
            o[h * ATTN_Q:(h + 1) * ATTN_Q], wuv_ref[h], preferred_element_type=jnp.float32)
    o_ref[0] = (_rms(y_scr[...]) * go_ref[...]).astype(o_ref.dtype)


def _attn(iq, ikw, kv, q, w_uk_bf, w_uv_bf, kv_norm_g, g_attn_out):
    b, s, _ = iq.shape
    assert ATTN_Q == ATTN_KC and s % ATTN_Q == 0
    k_sel = min(TOPK_MAX, s // 4)
    rows = N_HEADS * ATTN_Q
    kern = functools.partial(_attn_kernel, seq=s, k_sel=k_sel)
    return pl.pallas_call(
        kern,
        grid=(b, s // ATTN_Q),
        in_specs=[pl.BlockSpec((1, ATTN_Q, IDX_HEADS * IDX_DIM), lambda bi, j: (bi, j, 0)),
                  pl.BlockSpec((1, ATTN_Q, IKW_COLS), lambda bi, j: (bi, j, 0)),
                  pl.BlockSpec((1, s, IKW_COLS), lambda bi, j: (bi, 0, 0)),
                  pl.BlockSpec((1, s, KV_RANK), lambda bi, j: (bi, 0, 0)),
                  pl.BlockSpec((1, ATTN_Q, ATTN_DIM), lambda bi, j: (bi, j, 0)),
                  pl.BlockSpec((N_HEADS, KV_RANK, HEAD_DIM), lambda bi, j: (0, 0, 0)),
                  pl.BlockSpec((N_HEADS, KV_RANK, HEAD_DIM), lambda bi, j: (0, 0, 0)),
                  pl.BlockSpec((1, KV_RANK), lambda bi, j: (0, 0)),
                  pl.BlockSpec((1, ATTN_DIM), lambda bi, j: (0, 0))],
        out_specs=pl.BlockSpec((1, ATTN_Q, ATTN_DIM), lambda bi, j: (bi, j, 0)),
        out_shape=jax.ShapeDtypeStruct((b, s, ATTN_DIM), jnp.bfloat16),
        scratch_shapes=[pltpu.VMEM((s, KV_RANK), jnp.bfloat16),
                        pltpu.VMEM((s, IDX_DIM), jnp.bfloat16),
                        pltpu.VMEM((s // ATTN_KC, ATTN_Q, ATTN_KC), jnp.float32),
                        pltpu.VMEM((ATTN_Q, 1), jnp.float32),
                        pltpu.VMEM((ATTN_Q, 1), jnp.int32),
                        pltpu.VMEM((rows, KV_RANK), jnp.bfloat16),
                        pltpu.VMEM((rows, LANES), jnp.float32),
                        pltpu.VMEM((rows, LANES), jnp.float32),
                        pltpu.VMEM((rows, KV_RANK), jnp.float32),
                        pltpu.VMEM((ATTN_Q, ATTN_DIM), jnp.float32)],
        compiler_params=_cparams(("arbitrary", "arbitrary")),
        name="attn",
    )(iq, ikw, ikw, kv, q, w_uk_bf, w_uv_bf, kv_norm_g, g_attn_out)


R_E1, R_E2, R_W1, R_W2, R_RANK1, R_RANK2 = range(6)


def _out_kernel(mc_ref, ma_ref, w_hbm, x_ref, gt_ref, g_ref, sc_ref, sh_ref, wr_ref, br_ref,
                x1_ref, h2_ref, route_ref, cnt_ref, cnt_scr, w_bf, stage, sem):
    @pl.when(pl.program_id(0) == 0)
    def _():
        cnt_scr[...] = jnp.zeros(cnt_scr.shape, jnp.float32)
        rows = stage.shape[0]
        for k in range(w_bf.shape[0] // rows):
            cp = pltpu.make_async_copy(w_hbm.at[pl.ds(k * rows, rows)], stage, sem)
            cp.start()
            cp.wait()
            w_bf[pl.ds(k * rows, rows), :] = stage[...].astype(jnp.bfloat16)

    mix = (jnp.dot(mc_ref[...], w_bf[0:CONV_DIM, :], preferred_element_type=jnp.float32)
           + jnp.dot(ma_ref[...], w_bf[CONV_DIM:, :], preferred_element_type=jnp.float32))
    x1 = x_ref[...] + gt_ref[0] * mix
    x1_ref[...] = x1
    h2 = _rms(x1) * g_ref[...] * (1.0 + sc_ref[0]) + sh_ref[0]
    _rows_from_2d(h2_ref, h2)

    logit = jnp.dot(h2.astype(jnp.bfloat16), wr_ref[...],
                    preferred_element_type=jnp.float32) + br_ref[...]
    lane = lax.broadcasted_iota(jnp.int32, logit.shape, 1).astype(jnp.float32)
    neg = -jnp.inf
    big = float(ROUTE_LANES)
    is_g = lane < N_GROUPS
    gl = jnp.where(is_g, logit, neg)
    gmax = jnp.max(gl, axis=1, keepdims=True)
    p_group = 1.0 / jnp.sum(jnp.exp(gl - gmax), axis=1, keepdims=True)
    g_sel = jnp.min(jnp.where(is_g & (gl == gmax), lane, big), axis=1, keepdims=True)
    lo = N_GROUPS + g_sel * EXPERTS_PER_GROUP
    in_grp = (lane >= lo) & (lane < lo + EXPERTS_PER_GROUP)
    el = jnp.where(in_grp, logit, neg)
    m1 = jnp.max(el, axis=1, keepdims=True)
    i1 = jnp.min(jnp.where(in_grp & (el == m1), lane, big), axis=1, keepdims=True)
    el2 = jnp.where(lane == i1, neg, el)
    m2 = jnp.max(el2, axis=1, keepdims=True)
    i2 = jnp.min(jnp.where(in_grp & (lane != i1) & (el2 == m2), lane, big), axis=1, keepdims=True)
    r = jnp.exp(m2 - m1)
    w1 = p_group / (1.0 + r)
    w2 = p_group * r / (1.0 + r)

    member = jnp.where(lane == i1, 1.0, jnp.where(lane == i2, 1.0, 0.0))
    tm = member.shape[0]
    earlier = (lax.broadcasted_iota(jnp.int32, (tm, tm), 1)
               < lax.broadcasted_iota(jnp.int32, (tm, tm), 0))
    tri = jnp.where(earlier, 1.0, 0.0).astype(jnp.bfloat16)
    before = jnp.dot(tri, member.astype(jnp.bfloat16),
                     preferred_element_type=jnp.float32) + cnt_scr[...]
    rank1 = jnp.sum(jnp.where(lane == i1, before, 0.0), axis=1, keepdims=True)
    rank2 = jnp.sum(jnp.where(lane == i2, before, 0.0), axis=1, keepdims=True)
    cnt_scr[...] = cnt_scr[...] + jnp.sum(member, axis=0, keepdims=True)
    cnt_ref[...] = cnt_scr[...]

    route = jnp.zeros_like(logit)
    for idx, val in ((R_E1, i1 - N_GROUPS), (R_E2, i2 - N_GROUPS), (R_W1, w1), (R_W2, w2),
                     (R_RANK1, rank1), (R_RANK2, rank2)):
        route = jnp.where(lane == idx, val, route)
    route_ref[...] = route


def _out(mix_c, mix_a, w_out, x2, gt1, g_ffn, sc2, sh2, w_route, b_route, seq, tm=512):
    t, d = x2.shape
    per_b = seq // tm
    bmap = lambda m: (m // per_b, 0, 0)
    return pl.pallas_call(
        _out_kernel,
        grid=(t // tm,),
        in_specs=[pl.BlockSpec((tm, CONV_DIM), lambda m: (m, 0)),
                  pl.BlockSpec((tm, ATTN_DIM), lambda m: (m, 0)),
                  pl.BlockSpec(memory_space=pl.ANY),
                  pl.BlockSpec((tm, d), lambda m: (m, 0)),
                  pl.BlockSpec((1, 1, d), bmap),
                  pl.BlockSpec((1, d), lambda m: (0, 0)),
                  pl.BlockSpec((1, 1, d), bmap),
                  pl.BlockSpec((1, 1, d), bmap),
                  pl.BlockSpec((d, ROUTE_LANES), lambda m: (0, 0)),
                  pl.BlockSpec((1, ROUTE_LANES), lambda m: (0, 0))],
        out_specs=[pl.BlockSpec((tm, d), lambda m: (m, 0)),
                   pl.BlockSpec((tm * ROW_SUB, LANES), lambda m: (m, 0)),
                   pl.BlockSpec((tm, ROUTE_LANES), lambda m: (m, 0)),
                   pl.BlockSpec((1, ROUTE_LANES), lambda m: (0, 0))],
        out_shape=[jax.ShapeDtypeStruct((t, d), jnp.float32),
                   jax.ShapeDtypeStruct((t * ROW_SUB, LANES), jnp.float32),
                   jax.ShapeDtypeStruct((t, ROUTE_LANES), jnp.float32),
                   jax.ShapeDtypeStruct((1, ROUTE_LANES), jnp.float32)],
        scratch_shapes=[pltpu.VMEM((1, ROUTE_LANES), jnp.float32),
                        pltpu.VMEM(w_out.shape, jnp.bfloat16),
                        pltpu.VMEM((OUT_STAGE_ROWS, d), jnp.float32),
                        pltpu.SemaphoreType.DMA(())],
        compiler_params=_cparams(("arbitrary",)),
        name="out",
    )(mix_c, mix_a, w_out, x2, gt1, g_ffn, sc2, sh2, w_route, b_route)


def _slot(off_ref, e_ref, r_ref, tok):
    return off_ref[e_ref[tok]] + r_ref[tok]


def _dispatch_kernel(e1_ref, e2_ref, r1_ref, r2_ref, off_ref, zs_ref, nu_ref, h2_ref, xs_ref,
                     zbuf, sem_z, sem_r, *, tm, n_tiles):
    i = pl.program_id(0)

    @pl.when(i == 0)
    def _():
        zbuf[...] = jnp.zeros(zbuf.shape, zbuf.dtype)

        def zero_copy(start):
            rows = pl.ds(pl.multiple_of(start * ROW_SUB, MOE_TM * ROW_SUB), MOE_TM * ROW_SUB)
            return pltpu.make_async_copy(zbuf, xs_ref.at[rows], sem_z)

        for e in range(N_EXPERTS):
            zero_copy(zs_ref[e]).start()
        for e in range(N_EXPERTS):
            zero_copy(zs_ref[e]).wait()

        def zero_tile(k, carry):
            zero_copy(k * MOE_TM).start()
            zero_copy(k * MOE_TM).wait()
            return carry

        lax.fori_loop(nu_ref[0], n_tiles, zero_tile, 0)

    def issue_row(r, carry):
        tok = i * tm + r
        for e_ref, r_ref in ((e1_ref, r1_ref), (e2_ref, r2_ref)):
            pltpu.make_async_copy(_row(h2_ref, r), _row(xs_ref, _slot(off_ref, e_ref, r_ref, tok)),
                                  sem_r).start()
        return carry

    lax.fori_loop(0, tm, issue_row, 0)
    for _ in range(2):
        pltpu.make_async_copy(h2_ref, xs_ref.at[pl.ds(0, tm * ROW_SUB)], sem_r).wait()


def _dispatch(plan, h2, n_rows, tm=256):
    t = h2.shape[0] // ROW_SUB
    kern = functools.partial(_dispatch_kernel, tm=tm, n_tiles=n_rows // MOE_TM)
    return pl.pallas_call(
        kern,
        grid_spec=pltpu.PrefetchScalarGridSpec(
            num_scalar_prefetch=len(plan),
            grid=(t // tm,),
            in_specs=[pl.BlockSpec((tm * ROW_SUB, LANES), lambda i, *_: (i, 0))],
            out_specs=pl.BlockSpec(memory_space=pl.ANY),
            scratch_shapes=[pltpu.VMEM((MOE_TM * ROW_SUB, LANES), h2.dtype),
                            pltpu.SemaphoreType.DMA(()),
                            pltpu.SemaphoreType.DMA(())]),
        out_shape=jax.ShapeDtypeStruct((n_rows * ROW_SUB, LANES), h2.dtype),
        compiler_params=_cparams(("arbitrary",)),
        name="dispatch",
    )(*plan, h2)


def _mlp_kernel(te_ref, nu_ref, first_ref, nxt_ref, slot_ref, xs_ref, wg_hbm, wu_hbm, wd_hbm, ys_ref,
                wg_f32, wu_f32, wd_f32, wg_bf, wu_bf, wd_bf, sem):
    i = pl.program_id(0)

    def copies(e, s):
        return [pltpu.make_async_copy(hbm.at[e], buf.at[s], sem.at[s])
                for hbm, buf in ((wg_hbm, wg_f32), (wu_hbm, wu_f32), (wd_hbm, wd_f32))]

    @pl.when(i == 0)
    def _():
        for cp in copies(te_ref[0], 0):
            cp.start()

    @pl.when((i < nu_ref[0]) & (first_ref[i] == 1))
    def _():
        s = slot_ref[i]
        for cp in copies(te_ref[i], s):
            cp.wait()
        wg_bf[...] = wg_f32[s].astype(jnp.bfloat16)
        wu_bf[...] = wu_f32[s].astype(jnp.bfloat16)
        wd_bf[...] = wd_f32[s].astype(jnp.bfloat16)

        @pl.when(nxt_ref[i] >= 0)
        def _():
            for cp in copies(nxt_ref[i], 1 - s):
                cp.start()

    @pl.when(i < nu_ref[0])
    def _():
        x = _rows_to_2d(xs_ref, MOE_TM).astype(jnp.bfloat16)
        a = jnp.dot(x, wg_bf[...], preferred_element_type=jnp.float32)
        u = jnp.dot(x, wu_bf[...], preferred_element_type=jnp.float32)
        hid = (a * jax.nn.sigmoid(a)) * u
        _rows_from_2d(ys_ref, jnp.dot(hid.astype(jnp.bfloat16), wd_bf[...],
                                      preferred_element_type=jnp.float32))

    @pl.when(i >= nu_ref[0])
    def _():
        ys_ref[...] = jnp.zeros(ys_ref.shape, ys_ref.dtype)


def _mlp(tiles, xs, w_gate, w_up, w_down):
    n_rows = xs.shape[0] // ROW_SUB
    _, d, f = w_gate.shape
    used = lambda i, te, nu, *_: (jnp.minimum(i, nu[0] - 1), 0)
    return pl.pallas_call(
        _mlp_kernel,
        grid_spec=pltpu.PrefetchScalarGridSpec(
            num_scalar_prefetch=len(tiles),
            grid=(n_rows // MOE_TM,),
            in_specs=[pl.BlockSpec((MOE_TM * ROW_SUB, LANES), used),
                      pl.BlockSpec(memory_space=pl.ANY),
                      pl.BlockSpec(memory_space=pl.ANY),
                      pl.BlockSpec(memory_space=pl.ANY)],
            out_specs=pl.BlockSpec((MOE_TM * ROW_SUB, LANES), lambda i, *_: (i, 0)),
            scratch_shapes=[pltpu.VMEM((2, d, f), jnp.float32),
                            pltpu.VMEM((2, d, f), jnp.float32),
                            pltpu.VMEM((2, f, d), jnp.float32),
                            pltpu.VMEM((d, f), jnp.bfloat16),
                            pltpu.VMEM((d, f), jnp.bfloat16),
                            pltpu.VMEM((f, d), jnp.bfloat16),
                            pltpu.SemaphoreType.DMA((2,))]),
        out_shape=jax.ShapeDtypeStruct((n_rows * ROW_SUB, LANES), jnp.float32),
        compiler_params=_cparams(("arbitrary",)),
        name="mlp",
    )(*tiles, xs, w_gate, w_up, w_down)


def _combine_kernel(e1_ref, e2_ref, r1_ref, r2_ref, off_ref, ys_ref, x1_ref, route_ref, gt_ref, g_ref,
                    sc_ref, sh_ref, o_ref, a0, b0, a1, b1, sem, *, tm, n_tiles):
    i = pl.program_id(0)
    bufs = ((a0, b0), (a1, b1))

    def issue(tile, sl):
        def issue_row(r, carry):
            tok = tile * tm + r
            for buf, e_ref, r_ref in ((bufs[sl][0], e1_ref, r1_ref), (bufs[sl][1], e2_ref, r2_ref)):
                pltpu.make_async_copy(_row(ys_ref, _slot(off_ref, e_ref, r_ref, tok)),
                                      _row(buf, r), sem.at[sl]).start()
            return carry

        lax.fori_loop(0, tm, issue_row, 0)

    @pl.when(i == 0)
    def _():
        issue(0, 0)

    for sl in range(2):
        @pl.when(i % 2 == sl)
        def _(sl=sl):
            @pl.when(i + 1 < n_tiles)
            def _():
                issue(i + 1, 1 - sl)

            for buf in bufs[sl]:
                pltpu.make_async_copy(ys_ref.at[pl.ds(0, tm * ROW_SUB)], buf, sem.at[sl]).wait()

            route = route_ref[...]
            y = (route[:, R_W1:R_W1 + 1] * _rows_to_2d(bufs[sl][0], tm)
                 + route[:, R_W2:R_W2 + 1] * _rows_to_2d(bufs[sl][1], tm))
            x = x1_ref[...] + gt_ref[0] * y
            o_ref[...] = _rms(x) * g_ref[...] * (1.0 + sc_ref[0]) + sh_ref[0]


def _combine(plan, ys, x1, route, gt2, g_final, scf, shf, seq, tm=256):
    t, d = x1.shape
    per_b = seq // tm
    n_tiles = t // tm
    bmap = lambda m, *_: (m // per_b, 0, 0)
    kern = functools.partial(_combine_kernel, tm=tm, n_tiles=n_tiles)
    return pl.pallas_call(
        kern,
        grid_spec=pltpu.PrefetchScalarGridSpec(
            num_scalar_prefetch=len(plan),
            grid=(n_tiles,),
            in_specs=[pl.BlockSpec(memory_space=pl.ANY),
                      pl.BlockSpec((tm, d), lambda m, *_: (m, 0)),
                      pl.BlockSpec((tm, ROUTE_LANES), lambda m, *_: (m, 0)),
                      pl.BlockSpec((1, 1, d), bmap),
                      pl.BlockSpec((1, d), lambda m, *_: (0, 0)),
                      pl.BlockSpec((1, 1, d), bmap),
                      pl.BlockSpec((1, 1, d), bmap)],
            out_specs=pl.BlockSpec((tm, d), lambda m, *_: (m, 0)),
            scratch_shapes=[pltpu.VMEM((tm * ROW_SUB, LANES), jnp.float32) for _ in range(4)]
            + [pltpu.SemaphoreType.DMA((2,))]),
        out_shape=jax.ShapeDtypeStruct((t, d), jnp.float32),
        compiler_params=_cparams(("arbitrary",)),
        name="combine",
    )(*plan, ys, x1, route, gt2, g_final, scf, shf)


def _moe_plan(route, counts, t):
    i32 = jnp.int32
    cnt = counts[0, N_GROUPS:N_GROUPS + N_EXPERTS].astype(i32)
    padded = (cnt + MOE_TM - 1) // MOE_TM * MOE_TM
    off_end = jnp.cumsum(padded)
    off = off_end - padded
    n_tiles = (2 * t + N_EXPERTS * MOE_TM) // MOE_TM
    n_used = off_end[-1] // MOE_TM
    tile_start = jnp.arange(n_tiles, dtype=i32) * MOE_TM
    te = jnp.sum((off_end[None, :] <= tile_start[:, None]).astype(i32), axis=1)
    te = jnp.minimum(te, N_EXPERTS - 1)
    tile = jnp.arange(n_tiles, dtype=i32)
    te = jnp.where(tile < n_used, te, te[n_used - 1])
    first = jnp.concatenate([jnp.ones((1,), i32), (te[1:] != te[:-1]).astype(i32)])
    nxt_tile = off_end[te] // MOE_TM
    nxt = jnp.where(nxt_tile < n_used, te[jnp.minimum(nxt_tile, n_tiles - 1)], -1)
    slot = (jnp.cumsum(first) - 1) % 2
    assign = tuple(route[:, k].astype(i32) for k in (R_E1, R_E2, R_RANK1, R_RANK2))
    zstart = jnp.maximum(off_end - MOE_TM, 0)
    n_used = n_used.reshape(1)
    return assign, off, zstart, (te, n_used, first, nxt, slot), n_tiles * MOE_TM


def kernel(x, c, w_ada, b_ada, g_mix, w_in, conv_w, w_uk, kv_norm_g, w_uv, g_conv_out, g_attn_out,
           w_out, g_ffn, w_rg, b_rg, w_re, b_re, w_gate, w_up, w_down, w_ada_f, b_ada_f, g_final):
    b, s, d = x.shape
    assert w_ada.shape[0] == 1, "single layer"
    bf = jnp.bfloat16
    x2 = x.reshape(b * s, d)

    c_pad = jnp.zeros((SUBLANES, d), jnp.float32).at[:b].set(c)
    mod = _mod(c_pad, w_ada[0], b_ada[0])[:b]
    modf = _mod(c_pad, w_ada_f, b_ada_f)[:b]
    vec = lambda a, i: a[:, i * d:(i + 1) * d].reshape(b, 1, d)
    sh1, sc1, gt1, sh2, sc2, gt2 = (vec(mod, i) for i in range(N_MOD))
    shf, scf = vec(modf, 0), vec(modf, 1)
    row = lambda a: a.reshape(1, -1)

    h = _hnorm(x2, row(g_mix[0]), sc1, sh1, s)
    w_in_t = jnp.swapaxes(w_in[0], 0, 1)
    w_tail = jnp.pad(w_in_t[OFF_IK:], ((0, IKW_COLS - (IN_COLS - OFF_IK)), (0, 0)))
    conv3 = _colmm(h, w_in_t, 0, 3 * CONV_DIM, 1024, jnp.float32, "proj_conv")
    q = _colmm(h, w_in_t, OFF_Q, ATTN_DIM, 1024, bf, "proj_q")
    kv = _colmm(h, w_in_t, OFF_KV, KV_RANK, 512, jnp.float32, "proj_kv")
    iq = _colmm(h, w_in_t, OFF_IQ, IDX_HEADS * IDX_DIM, 512, bf, "proj_iq")
    ikw = _colmm(h, w_tail, 0, IKW_COLS, IKW_COLS, jnp.float32, "proj_ikw")

    mix_c = _conv(conv3.reshape(b, s, 3 * CONV_DIM), conv_w[0], row(g_conv_out[0]))
    mix_a = _attn(iq.reshape(b, s, -1), ikw.reshape(b, s, -1), kv.reshape(b, s, -1),
                  q.reshape(b, s, -1), w_uk[0].astype(bf), w_uv[0].astype(bf),
                  row(kv_norm_g[0]), row(g_attn_out[0]))

    w_route = jnp.zeros((d, ROUTE_LANES), bf).at[:, :N_GROUPS].set(w_rg[0].astype(bf))
    w_route = w_route.at[:, N_GROUPS:N_GROUPS + N_EXPERTS].set(w_re[0].astype(bf))
    b_route = jnp.zeros((1, ROUTE_LANES), jnp.float32).at[0, :N_GROUPS].set(b_rg[0])
    b_route = b_route.at[0, N_GROUPS:N_GROUPS + N_EXPERTS].set(b_re[0])
    x1, h2, route, counts = _out(mix_c.reshape(b * s, -1), mix_a.reshape(b * s, -1),
                                 w_out[0], x2, gt1, row(g_ffn[0]),
                                 sc2, sh2, w_route, b_route, s)

    assign, off, zstart, tiles, n_rows = _moe_plan(route, counts, b * s)
    xs = _dispatch((*assign, off, zstart, tiles[1]), h2, n_rows)
    ys = _mlp(tiles, xs, w_gate[0], w_up[0], w_down[0])
    out = _combine((*assign, off), ys, x1, route, gt2, row(g_final), scf, shf, s)
    return out.reshape(b, s, d)
```

```python
import functools

import jax
import jax.numpy as jnp
from jax import lax
from jax.experimental import pallas as pl
from jax.experimental.pallas import tpu as pltpu

D_MODEL = 2048
CONV_DIM = 1024
CONV_WIDTH = 3
N_HEADS = 8
HEAD_DIM = 128
ATTN_DIM = N_HEADS * HEAD_DIM
KV_RANK = 512
IDX_HEADS = 16
IDX_DIM = 128
TOPK_MAX = 256
N_GROUPS = 4
EXPERTS_PER_GROUP = 8
N_EXPERTS = N_GROUPS * EXPERTS_PER_GROUP
EXPERT_FF = 512
N_MOD = 6
EPS = 1e-6

OFF_Q = 3 * CONV_DIM
OFF_KV = OFF_Q + ATTN_DIM
OFF_IQ = OFF_KV + KV_RANK
OFF_IK = OFF_IQ + IDX_HEADS * IDX_DIM
OFF_IW = OFF_IK + IDX_DIM
IN_COLS = OFF_IW + IDX_HEADS

LANES = 128
SUBLANES = 8
VMEM_LIMIT = 56 * 1024 * 1024

IKW_COLS = 256
ATTN_Q = 256
ATTN_KC = 256
ATTN_RB = 256
MASKED = -1e30
TIE_ALL = 2 ** 30
ROUTE_LANES = 128
OUT_STAGE_ROWS = 512
MOE_TM = 128
ROW_SUB = D_MODEL // LANES
INT_MIN = -2 ** 31
KEY_NEG_INF = INT_MIN + 0x7FFFFF


def _cparams(sem):
    return pltpu.CompilerParams(dimension_semantics=sem, vmem_limit_bytes=VMEM_LIMIT)


def _rms(v, axis=-1):
    return v * lax.rsqrt(jnp.mean(v * v, axis=axis, keepdims=True) + EPS)


def _tile_lanes(v, n):
    return jnp.concatenate([v] * n, axis=1)


def _rows_to_2d(ref, n):
    return jnp.concatenate([ref[pl.ds(c, n, stride=ROW_SUB), :] for c in range(ROW_SUB)], axis=1)


def _rows_from_2d(ref, val):
    n = val.shape[0]
    for c in range(ROW_SUB):
        ref[pl.ds(c, n, stride=ROW_SUB), :] = val[:, c * LANES:(c + 1) * LANES]


def _row(ref, i):
    return ref.at[pl.ds(pl.multiple_of(i * ROW_SUB, ROW_SUB), ROW_SUB)]


def _mod_kernel(c_ref, w_ref, b_ref, o_ref):
    c = c_ref[...]
    ca = (c * jax.nn.sigmoid(c)).astype(jnp.bfloat16)
    o_ref[...] = jnp.dot(ca, w_ref[...].astype(jnp.bfloat16),
                         preferred_element_type=jnp.float32) + b_ref[...]


def _mod(c_pad, w, b, tn=1024):
    d, n = w.shape
    return pl.pallas_call(
        _mod_kernel,
        grid=(n // tn,),
        in_specs=[pl.BlockSpec((c_pad.shape[0], d), lambda j: (0, 0)),
                  pl.BlockSpec((d, tn), lambda j: (0, j)),
                  pl.BlockSpec((1, tn), lambda j: (0, j))],
        out_specs=pl.BlockSpec((c_pad.shape[0], tn), lambda j: (0, j)),
        out_shape=jax.ShapeDtypeStruct((c_pad.shape[0], n), jnp.float32),
        compiler_params=_cparams(("arbitrary",)),
        name="mod",
    )(c_pad, w, b.reshape(1, n))


def _hnorm_kernel(x_ref, g_ref, sc_ref, sh_ref, h_ref):
    h = _rms(x_ref[...]) * g_ref[...] * (1.0 + sc_ref[0]) + sh_ref[0]
    h_ref[...] = h.astype(h_ref.dtype)


def _hnorm(x2, g_mix, sc1, sh1, seq, tm=1024):
    t, d = x2.shape
    per_b = seq // tm
    bmap = lambda m: (m // per_b, 0, 0)
    return pl.pallas_call(
        _hnorm_kernel,
        grid=(t // tm,),
        in_specs=[pl.BlockSpec((tm, d), lambda m: (m, 0)),
                  pl.BlockSpec((1, d), lambda m: (0, 0)),
                  pl.BlockSpec((1, 1, d), bmap),
                  pl.BlockSpec((1, 1, d), bmap)],
        out_specs=pl.BlockSpec((tm, d), lambda m: (m, 0)),
        out_shape=jax.ShapeDtypeStruct((t, d), jnp.bfloat16),
        compiler_params=_cparams(("arbitrary",)),
        name="hnorm",
    )(x2, g_mix, sc1, sh1)


def _colmm_kernel(h_ref, wt_ref, o_ref, w_bf):
    @pl.when(pl.program_id(1) == 0)
    def _():
        w_bf[...] = wt_ref[...].astype(jnp.bfloat16)

    o_ref[...] = lax.dot_general(h_ref[...], w_bf[...], (((1,), (1,)), ((), ())),
                                 preferred_element_type=jnp.float32).astype(o_ref.dtype)


def _colmm(h, wt, col0, ncols, tn, out_dtype, name, tm=1024):
    t, d = h.shape
    assert col0 % tn == 0 and ncols % tn == 0
    return pl.pallas_call(
        _colmm_kernel,
        grid=(ncols // tn, t // tm),
        in_specs=[pl.BlockSpec((tm, d), lambda n, m: (m, 0)),
                  pl.BlockSpec((tn, d), lambda n, m: (col0 // tn + n, 0))],
        out_specs=pl.BlockSpec((tm, tn), lambda n, m: (m, n)),
        out_shape=jax.ShapeDtypeStruct((t, ncols), out_dtype),
        scratch_shapes=[pltpu.VMEM((tn, d), jnp.bfloat16)],
        compiler_params=_cparams(("arbitrary", "arbitrary")),
        name=name,
    )(h, wt)


def _conv_kernel(bg_ref, cg_ref, xv_ref, cgh_ref, xvh_ref, w_ref, g_ref, o_ref):
    i = pl.program_id(1)
    u = cg_ref[0] * xv_ref[0]
    halo = cgh_ref[0] * xvh_ref[0]
    halo = jnp.where(i == 0, 0.0, halo)
    row = lax.broadcasted_iota(jnp.int32, u.shape, 0)
    h1 = halo[SUBLANES - 1:SUBLANES, :]
    h2 = halo[SUBLANES - 2:SUBLANES - 1, :]
    u1 = jnp.where(row == 0, h1, pltpu.roll(u, 1, 0))
    u2 = jnp.where(row == 0, h2, jnp.where(row == 1, h1, pltpu.roll(u, 2, 0)))
    w = w_ref[...]
    y = bg_ref[0] * (w[0:1, :] * u2 + w[1:2, :] * u1 + w[2:3, :] * u)
    o_ref[0] = (_rms(y) * g_ref[...]).astype(o_ref.dtype)


def _conv(conv3, conv_w, g_conv_out, tm=512):
    b, s, _ = conv3.shape
    c = CONV_DIM
    hb = tm // SUBLANES
    halo_map = lambda col: (lambda bi, i: (bi, jnp.maximum(i * hb - 1, 0), col))
    return pl.pallas_call(
        _conv_kernel,
        grid=(b, s // tm),
        in_specs=[pl.BlockSpec((1, tm, c), lambda bi, i: (bi, i, 0)),
                  pl.BlockSpec((1, tm, c), lambda bi, i: (bi, i, 1)),
                  pl.BlockSpec((1, tm, c), lambda bi, i: (bi, i, 2)),
                  pl.BlockSpec((1, SUBLANES, c), halo_map(1)),
                  pl.BlockSpec((1, SUBLANES, c), halo_map(2)),
                  pl.BlockSpec((CONV_WIDTH, c), lambda bi, i: (0, 0)),
                  pl.BlockSpec((1, c), lambda bi, i: (0, 0))],
        out_specs=pl.BlockSpec((1, tm, c), lambda bi, i: (bi, i, 0)),
        out_shape=jax.ShapeDtypeStruct((b, s, c), jnp.bfloat16),
        compiler_params=_cparams(("arbitrary", "arbitrary")),
        name="conv",
    )(conv3, conv3, conv3, conv3, conv3, conv_w, g_conv_out)


def _ordered_bits(v):
    return v ^ ((v >> 31) & jnp.int32(0x7FFFFFFF))


def _attn_search(nch, sc_scr, k_sel):
    n_keys = float(nch * ATTN_KC)
    half = ATTN_Q // 2

    def half_step(bit, thr, rows):
        cand = thr + bit
        cand_f = lax.bitcast_convert_type(_ordered_bits(cand), jnp.float32)
        hits = jnp.where(sc_scr[0, rows, :] >= cand_f, 1.0, 0.0)
        for c in range(1, nch):
            hits = hits + jnp.where(sc_scr[c, rows, :] >= cand_f, 1.0, 0.0)
        cnt = jnp.sum(hits, axis=1, keepdims=True)
        cnt = jnp.where(cand < KEY_NEG_INF, n_keys, cnt)
        return jnp.where(cnt >= k_sel, cand, thr)

    def bit_step(i, thrs):
        bit = lax.shift_left(jnp.int32(1), 31 - i)
        return tuple(half_step(bit, thr, pl.ds(k * half, half)) for k, thr in enumerate(thrs))

    init = jnp.full((half, 1), INT_MIN, jnp.int32)
    thrs = lax.fori_loop(0, 32, bit_step, (init, init), unroll=2)
    thr = jnp.concatenate(thrs, axis=0)
    return lax.bitcast_convert_type(_ordered_bits(thr), jnp.float32)


def _attn_tiebreak(nch, sc_scr, thr_f, k_sel, tie_scr):
    kloc = lax.broadcasted_iota(jnp.int32, (ATTN_Q, ATTN_KC), 1)

    def count(pred):
        hits = pred(sc_scr[0], 0)
        for c in range(1, nch):
            hits = hits + pred(sc_scr[c], c)
        return jnp.sum(hits, axis=1, keepdims=True)

    tie_scr[...] = jnp.full(tie_scr.shape, TIE_ALL, jnp.int32)
    n_ge = count(lambda sc, c: jnp.where(sc >= thr_f, 1.0, 0.0))
    excess = jnp.where(thr_f > -jnp.inf, n_ge - k_sel, 0.0)

    @pl.when(jnp.max(excess) > 0.0)
    def _():
        need = k_sel - count(lambda sc, c: jnp.where(sc > thr_f, 1.0, 0.0))
        n_bits = (nch * ATTN_KC - 1).bit_length()

        def step(i, lim):
            cand = lim + lax.shift_left(jnp.int32(1), n_bits - 1 - i)
            below = count(lambda sc, c: jnp.where(
                sc == thr_f, jnp.where(c * ATTN_KC + kloc < cand, 1.0, 0.0), 0.0))
            return jnp.where(below < need, cand, lim)

        lim = lax.fori_loop(0, n_bits, step, jnp.zeros((ATTN_Q, 1), jnp.int32))
        tie_scr[...] = jnp.where(excess > 0.0, lim, TIE_ALL)


def _attn_kernel(iq_ref, ikwq_ref, ikwa_ref, kv_ref, q_ref, wuk_ref, wuv_ref, kvg_ref, go_ref,
                 o_ref, ckv_scr, ik_scr, sc_scr, thr_scr, tie_scr, ql_scr, m_scr, l_scr, acc_scr, y_scr,
                 *, seq, k_sel):
    j = pl.program_id(1)
    nch = j + 1
    nt = (((1,), (1,)), ((), ()))

    @pl.when(j == 0)
    def _():
        ckv_scr[...] = (_rms(kv_ref[0]) * kvg_ref[...]).astype(jnp.bfloat16)
        ik_scr[...] = ikwa_ref[0][:, :IDX_DIM].astype(jnp.bfloat16)

    iw = ikwq_ref[0][:, IDX_DIM:IDX_DIM + IDX_HEADS] * (IDX_HEADS ** -0.5 * IDX_DIM ** -0.5)
    qpos = j * ATTN_Q + lax.broadcasted_iota(jnp.int32, (ATTN_Q, ATTN_KC), 0)
    kloc = lax.broadcasted_iota(jnp.int32, (ATTN_Q, ATTN_KC), 1)

    def score_chunk(c, carry):
        k0 = pl.multiple_of(c * ATTN_KC, ATTN_KC)
        ik_c = ik_scr[pl.ds(k0, ATTN_KC), :]
        score = jnp.zeros((ATTN_Q, ATTN_KC), jnp.float32)
        for h in range(IDX_HEADS):
            s = lax.dot_general(iq_ref[0][:, h * IDX_DIM:(h + 1) * IDX_DIM], ik_c, nt,
                                preferred_element_type=jnp.float32)
            score = score + jnp.maximum(s, 0.0) * iw[:, h:h + 1]
        sc_scr[c] = jnp.where(k0 + kloc <= qpos, score, -jnp.inf)
        return carry

    lax.fori_loop(0, nch, score_chunk, 0)

    for v in range(seq // ATTN_KC):
        @pl.when(j == v)
        def _(v=v):
            thr_scr[...] = _attn_search(v + 1, sc_scr, k_sel)
            _attn_tiebreak(v + 1, sc_scr, thr_scr[...], k_sel, tie_scr)

    for h in range(N_HEADS):
        ql = lax.dot_general(q_ref[0][:, h * HEAD_DIM:(h + 1) * HEAD_DIM], wuk_ref[h], nt,
                             preferred_element_type=jnp.float32)
        ql_scr[h * ATTN_Q:(h + 1) * ATTN_Q, :] = ql.astype(jnp.bfloat16)
    m_scr[...] = jnp.full(m_scr.shape, MASKED, jnp.float32)
    l_scr[...] = jnp.zeros(l_scr.shape, jnp.float32)
    acc_scr[...] = jnp.zeros(acc_scr.shape, jnp.float32)
    thr_f = thr_scr[...]
    tie = tie_scr[...]
    scale = HEAD_DIM ** -0.5

    def attn_chunk(c, carry):
        k0 = pl.multiple_of(c * ATTN_KC, ATTN_KC)
        ckv_c = ckv_scr[pl.ds(k0, ATTN_KC), :]
        sc = sc_scr[c]
        kpos = k0 + kloc
        tied = jnp.where(sc == thr_f, jnp.where(kpos <= tie, 0.0, MASKED), MASKED)
        bias = jnp.where(kpos <= qpos, jnp.where(sc > thr_f, 0.0, tied), MASKED)
        for u in range(N_HEADS * ATTN_Q // ATTN_RB):
            rows = pl.ds(u * ATTN_RB, ATTN_RB)
            qrow = (u * ATTN_RB) % ATTN_Q
            lg = lax.dot_general(ql_scr[rows, :], ckv_c, nt,
                                 preferred_element_type=jnp.float32) * scale + bias[qrow:qrow + ATTN_RB]
            m_old = m_scr[rows, :]
            m_new = jnp.maximum(m_old, jnp.max(lg, axis=1, keepdims=True))
            alpha = jnp.exp(m_old - m_new)
            p = jnp.exp(lg - _tile_lanes(m_new, ATTN_KC // LANES))
            l_scr[rows, :] = alpha * l_scr[rows, :] + jnp.sum(p, axis=1, keepdims=True)
            pv = jnp.dot(p.astype(jnp.bfloat16), ckv_c, preferred_element_type=jnp.float32)
            acc_scr[rows, :] = _tile_lanes(alpha, KV_RANK // LANES) * acc_scr[rows, :] + pv
            m_scr[rows, :] = m_new
        return carry

    lax.fori_loop(0, nch, attn_chunk, 0)

    o = (acc_scr[...] * _tile_lanes(1.0 / l_scr[...], KV_RANK // LANES)).astype(jnp.bfloat16)
    for h in range(N_HEADS):
        y_scr[:, h * HEAD_DIM:(h + 1) * HEAD_DIM] = jnp.dot(
            o[h * ATTN_Q:(h + 1) * ATTN_Q], wuv_ref[h], preferred_element_type=jnp.float32)
    o_ref[0] = (_rms(y_scr[...]) * go_ref[...]).astype(o_ref.dtype)


def _attn(iq, ikw, kv, q, w_uk_bf, w_uv_bf, kv_norm_g, g_attn_out):
    b, s, _ = iq.shape
    assert ATTN_Q == ATTN_KC and s % ATTN_Q == 0
    k_sel = min(TOPK_MAX, s // 4)
    rows = N_HEADS * ATTN_Q
    kern = functools.partial(_attn_kernel, seq=s, k_sel=k_sel)
    return pl.pallas_call(
        kern,
        grid=(b, s // ATTN_Q),
        in_specs=[pl.BlockSpec((1, ATTN_Q, IDX_HEADS * IDX_DIM), lambda bi, j: (bi, j, 0)),
                  pl.BlockSpec((1, ATTN_Q, IKW_COLS), lambda bi, j: (bi, j, 0)),
                  pl.BlockSpec((1, s, IKW_COLS), lambda bi, j: (bi, 0, 0)),
                  pl.BlockSpec((1, s, KV_RANK), lambda bi, j: (bi, 0, 0)),
                  pl.BlockSpec((1, ATTN_Q, ATTN_DIM), lambda bi, j: (bi, j, 0)),
                  pl.BlockSpec((N_HEADS, KV_RANK, HEAD_DIM), lambda bi, j: (0, 0, 0)),
                  pl.BlockSpec((N_HEADS, KV_RANK, HEAD_DIM), lambda bi, j: (0, 0, 0)),
                  pl.BlockSpec((1, KV_RANK), lambda bi, j: (0, 0)),
                  pl.BlockSpec((1, ATTN_DIM), lambda bi, j: (0, 0))],
        out_specs=pl.BlockSpec((1, ATTN_Q, ATTN_DIM), lambda bi, j: (bi, j, 0)),
        out_shape=jax.ShapeDtypeStruct((b, s, ATTN_DIM), jnp.bfloat16),
        scratch_shapes=[pltpu.VMEM((s, KV_RANK), jnp.bfloat16),
                        pltpu.VMEM((s, IDX_DIM), jnp.bfloat16),
                        pltpu.VMEM((s // ATTN_KC, ATTN_Q, ATTN_KC), jnp.float32),
                        pltpu.VMEM((ATTN_Q, 1), jnp.float32),
                        pltpu.VMEM((ATTN_Q, 1), jnp.int32),
                        pltpu.VMEM((rows, KV_RANK), jnp.bfloat16),
                        pltpu.VMEM((rows, LANES), jnp.float32),
                        pltpu.VMEM((rows, LANES), jnp.float32),
                        pltpu.VMEM((rows, KV_RANK), jnp.float32),
                        pltpu.VMEM((ATTN_Q, ATTN_DIM), jnp.float32)],
        compiler_params=_cparams(("arbitrary", "arbitrary")),
        name="attn",
    )(iq, ikw, ikw, kv, q, w_uk_bf, w_uv_bf, kv_norm_g, g_attn_out)


R_E1, R_E2, R_W1, R_W2, R_RANK1, R_RANK2 = range(6)


def _out_kernel(mc_ref, ma_ref, w_hbm, x_ref, gt_ref, g_ref, sc_ref, sh_ref, wr_ref, br_ref,
                x1_ref, h2_ref, route_ref, cnt_ref, cnt_scr, tri_scr, w_bf, stage, sem):
    @pl.when(pl.program_id(0) == 0)
    def _():
        cnt_scr[...] = jnp.zeros(cnt_scr.shape, jnp.float32)
        earlier = (lax.broadcasted_iota(jnp.int32, tri_scr.shape, 1)
                   < lax.broadcasted_iota(jnp.int32, tri_scr.shape, 0))
        tri_scr[...] = jnp.where(earlier, 1.0, 0.0).astype(tri_scr.dtype)
        rows = stage.shape[0]
        for k in range(w_bf.shape[0] // rows):
            cp = pltpu.make_async_copy(w_hbm.at[pl.ds(k * rows, rows)], stage, sem)
            cp.start()
            cp.wait()
            w_bf[pl.ds(k * rows, rows), :] = stage[...].astype(jnp.bfloat16)

    mix = (jnp.dot(mc_ref[...], w_bf[0:CONV_DIM, :], preferred_element_type=jnp.float32)
           + jnp.dot(ma_ref[...], w_bf[CONV_DIM:, :], preferred_element_type=jnp.float32))
    x1 = x_ref[...] + gt_ref[0] * mix
    x1_ref[...] = x1
    h2 = _rms(x1) * g_ref[...] * (1.0 + sc_ref[0]) + sh_ref[0]
    _rows_from_2d(h2_ref, h2)

    logit = jnp.dot(h2.astype(jnp.bfloat16), wr_ref[...],
                    preferred_element_type=jnp.float32) + br_ref[...]
    lane = lax.broadcasted_iota(jnp.int32, logit.shape, 1).astype(jnp.float32)
    neg = -jnp.inf
    big = float(ROUTE_LANES)
    is_g = lane < N_GROUPS
    gl = jnp.where(is_g, logit, neg)
    gmax = jnp.max(gl, axis=1, keepdims=True)
    p_group = 1.0 / jnp.sum(jnp.exp(gl - gmax), axis=1, keepdims=True)
    g_sel = jnp.min(jnp.where(is_g & (gl == gmax), lane, big), axis=1, keepdims=True)
    lo = N_GROUPS + g_sel * EXPERTS_PER_GROUP
    in_grp = (lane >= lo) & (lane < lo + EXPERTS_PER_GROUP)
    el = jnp.where(in_grp, logit, neg)
    m1 = jnp.max(el, axis=1, keepdims=True)
    i1 = jnp.min(jnp.where(in_grp & (el == m1), lane, big), axis=1, keepdims=True)
    el2 = jnp.where(lane == i1, neg, el)
    m2 = jnp.max(el2, axis=1, keepdims=True)
    i2 = jnp.min(jnp.where(in_grp & (lane != i1) & (el2 == m2), lane, big), axis=1, keepdims=True)
    r = jnp.exp(m2 - m1)
    w1 = p_group / (1.0 + r)
    w2 = p_group * r / (1.0 + r)

    member = jnp.where(lane == i1, 1.0, jnp.where(lane == i2, 1.0, 0.0))
    before = jnp.dot(tri_scr[...], member.astype(jnp.bfloat16),
                     preferred_element_type=jnp.float32) + cnt_scr[...]
    rank1 = jnp.sum(jnp.where(lane == i1, before, 0.0), axis=1, keepdims=True)
    rank2 = jnp.sum(jnp.where(lane == i2, before, 0.0), axis=1, keepdims=True)
    cnt_scr[...] = cnt_scr[...] + jnp.sum(member, axis=0, keepdims=True)
    cnt_ref[...] = cnt_scr[...]

    route = jnp.zeros_like(logit)
    for idx, val in ((R_E1, i1 - N_GROUPS), (R_E2, i2 - N_GROUPS), (R_W1, w1), (R_W2, w2),
                     (R_RANK1, rank1), (R_RANK2, rank2)):
        route = jnp.where(lane == idx, val, route)
    route_ref[...] = route


def _out(mix_c, mix_a, w_out, x2, gt1, g_ffn, sc2, sh2, w_route, b_route, seq, tm=512):
    t, d = x2.shape
    per_b = seq // tm
    bmap = lambda m: (m // per_b, 0, 0)
    return pl.pallas_call(
        _out_kernel,
        grid=(t // tm,),
        in_specs=[pl.BlockSpec((tm, CONV_DIM), lambda m: (m, 0)),
                  pl.BlockSpec((tm, ATTN_DIM), lambda m: (m, 0)),
                  pl.BlockSpec(memory_space=pl.ANY),
                  pl.BlockSpec((tm, d), lambda m: (m, 0)),
                  pl.BlockSpec((1, 1, d), bmap),
                  pl.BlockSpec((1, d), lambda m: (0, 0)),
                  pl.BlockSpec((1, 1, d), bmap),
                  pl.BlockSpec((1, 1, d), bmap),
                  pl.BlockSpec((d, ROUTE_LANES), lambda m: (0, 0)),
                  pl.BlockSpec((1, ROUTE_LANES), lambda m: (0, 0))],
        out_specs=[pl.BlockSpec((tm, d), lambda m: (m, 0)),
                   pl.BlockSpec((tm * ROW_SUB, LANES), lambda m: (m, 0)),
                   pl.BlockSpec((tm, ROUTE_LANES), lambda m: (m, 0)),
                   pl.BlockSpec((1, ROUTE_LANES), lambda m: (0, 0))],
        out_shape=[jax.ShapeDtypeStruct((t, d), jnp.float32),
                   jax.ShapeDtypeStruct((t * ROW_SUB, LANES), jnp.float32),
                   jax.ShapeDtypeStruct((t, ROUTE_LANES), jnp.float32),
                   jax.ShapeDtypeStruct((1, ROUTE_LANES), jnp.float32)],
        scratch_shapes=[pltpu.VMEM((1, ROUTE_LANES), jnp.float32),
                        pltpu.VMEM((tm, tm), jnp.bfloat16),
                        pltpu.VMEM(w_out.shape, jnp.bfloat16),
                        pltpu.VMEM((OUT_STAGE_ROWS, d), jnp.float32),
                        pltpu.SemaphoreType.DMA(())],
        compiler_params=_cparams(("arbitrary",)),
        name="out",
    )(mix_c, mix_a, w_out, x2, gt1, g_ffn, sc2, sh2, w_route, b_route)


def _slot(off_ref, e_ref, r_ref, tok):
    return off_ref[e_ref[tok]] + r_ref[tok]


def _dispatch_kernel(e1_ref, e2_ref, r1_ref, r2_ref, off_ref, zs_ref, nu_ref, h2_ref, xs_ref,
                     zbuf, sem_z, sem_r, *, tm, n_tiles):
    i = pl.program_id(0)

    @pl.when(i == 0)
    def _():
        zbuf[...] = jnp.zeros(zbuf.shape, zbuf.dtype)

        def zero_copy(start):
            rows = pl.ds(pl.multiple_of(start * ROW_SUB, MOE_TM * ROW_SUB), MOE_TM * ROW_SUB)
            return pltpu.make_async_copy(zbuf, xs_ref.at[rows], sem_z)

        for e in range(N_EXPERTS):
            zero_copy(zs_ref[e]).start()
        for e in range(N_EXPERTS):
            zero_copy(zs_ref[e]).wait()

        def zero_tile(k, carry):
            zero_copy(k * MOE_TM).start()
            zero_copy(k * MOE_TM).wait()
            return carry

        lax.fori_loop(nu_ref[0], n_tiles, zero_tile, 0)

    def issue_row(r, carry):
        tok = i * tm + r
        for e_ref, r_ref in ((e1_ref, r1_ref), (e2_ref, r2_ref)):
            pltpu.make_async_copy(_row(h2_ref, r), _row(xs_ref, _slot(off_ref, e_ref, r_ref, tok)),
                                  sem_r).start()
        return carry

    lax.fori_loop(0, tm, issue_row, 0)
    for _ in range(2):
        pltpu.make_async_copy(h2_ref, xs_ref.at[pl.ds(0, tm * ROW_SUB)], sem_r).wait()


def _dispatch(plan, h2, n_rows, tm=256):
    t = h2.shape[0] // ROW_SUB
    kern = functools.partial(_dispatch_kernel, tm=tm, n_tiles=n_rows // MOE_TM)
    return pl.pallas_call(
        kern,
        grid_spec=pltpu.PrefetchScalarGridSpec(
            num_scalar_prefetch=len(plan),
            grid=(t // tm,),
            in_specs=[pl.BlockSpec((tm * ROW_SUB, LANES), lambda i, *_: (i, 0))],
            out_specs=pl.BlockSpec(memory_space=pl.ANY),
            scratch_shapes=[pltpu.VMEM((MOE_TM * ROW_SUB, LANES), h2.dtype),
                            pltpu.SemaphoreType.DMA(()),
                            pltpu.SemaphoreType.DMA(())]),
        out_shape=jax.ShapeDtypeStruct((n_rows * ROW_SUB, LANES), h2.dtype),
        compiler_params=_cparams(("arbitrary",)),
        name="dispatch",
    )(*plan, h2)


def _mlp_kernel(te_ref, nu_ref, first_ref, nxt_ref, slot_ref, xs_ref, wg_hbm, wu_hbm, wd_hbm, ys_ref,
                wg_f32, wu_f32, wd_f32, wg_bf, wu_bf, wd_bf, sem):
    i = pl.program_id(0)

    def copies(e, s):
        return [pltpu.make_async_copy(hbm.at[e], buf.at[s], sem.at[s])
                for hbm, buf in ((wg_hbm, wg_f32), (wu_hbm, wu_f32), (wd_hbm, wd_f32))]

    @pl.when(i == 0)
    def _():
        for cp in copies(te_ref[0], 0):
            cp.start()

    @pl.when((i < nu_ref[0]) & (first_ref[i] == 1))
    def _():
        s = slot_ref[i]
        for cp in copies(te_ref[i], s):
            cp.wait()
        wg_bf[...] = wg_f32[s].astype(jnp.bfloat16)
        wu_bf[...] = wu_f32[s].astype(jnp.bfloat16)
        wd_bf[...] = wd_f32[s].astype(jnp.bfloat16)

        @pl.when(nxt_ref[i] >= 0)
        def _():
            for cp in copies(nxt_ref[i], 1 - s):
                cp.start()

    @pl.when(i < nu_ref[0])
    def _():
        x = _rows_to_2d(xs_ref, MOE_TM).astype(jnp.bfloat16)
        a = jnp.dot(x, wg_bf[...], preferred_element_type=jnp.float32)
        u = jnp.dot(x, wu_bf[...], preferred_element_type=jnp.float32)
        hid = (a * jax.nn.sigmoid(a)) * u
        _rows_from_2d(ys_ref, jnp.dot(hid.astype(jnp.bfloat16), wd_bf[...],
                                      preferred_element_type=jnp.float32))

    @pl.when(i >= nu_ref[0])
    def _():
        ys_ref[...] = jnp.zeros(ys_ref.shape, ys_ref.dtype)


def _mlp(tiles, xs, w_gate, w_up, w_down):
    n_rows = xs.shape[0] // ROW_SUB
    _, d, f = w_gate.shape
    used = lambda i, te, nu, *_: (jnp.minimum(i, nu[0] - 1), 0)
    return pl.pallas_call(
        _mlp_kernel,
        grid_spec=pltpu.PrefetchScalarGridSpec(
            num_scalar_prefetch=len(tiles),
            grid=(n_rows // MOE_TM,),
            in_specs=[pl.BlockSpec((MOE_TM * ROW_SUB, LANES), used),
                      pl.BlockSpec(memory_space=pl.ANY),
                      pl.BlockSpec(memory_space=pl.ANY),
                      pl.BlockSpec(memory_space=pl.ANY)],
            out_specs=pl.BlockSpec((MOE_TM * ROW_SUB, LANES), lambda i, *_: (i, 0)),
            scratch_shapes=[pltpu.VMEM((2, d, f), jnp.float32),
                            pltpu.VMEM((2, d, f), jnp.float32),
                            pltpu.VMEM((2, f, d), jnp.float32),
                            pltpu.VMEM((d, f), jnp.bfloat16),
                            pltpu.VMEM((d, f), jnp.bfloat16),
                            pltpu.VMEM((f, d), jnp.bfloat16),
                            pltpu.SemaphoreType.DMA((2,))]),
        out_shape=jax.ShapeDtypeStruct((n_rows * ROW_SUB, LANES), jnp.float32),
        compiler_params=_cparams(("arbitrary",)),
        name="mlp",
    )(*tiles, xs, w_gate, w_up, w_down)


def _combine_kernel(e1_ref, e2_ref, r1_ref, r2_ref, off_ref, ys_ref, x1_ref, route_ref, gt_ref, g_ref,
                    sc_ref, sh_ref, o_ref, a0, b0, a1, b1, sem, *, tm, n_tiles):
    i = pl.program_id(0)
    bufs = ((a0, b0), (a1, b1))

    def issue(tile, sl):
        def issue_row(r, carry):
            tok = tile * tm + r
            for buf, e_ref, r_ref in ((bufs[sl][0], e1_ref, r1_ref), (bufs[sl][1], e2_ref, r2_ref)):
                pltpu.make_async_copy(_row(ys_ref, _slot(off_ref, e_ref, r_ref, tok)),
                                      _row(buf, r), sem.at[sl]).start()
            return carry

        lax.fori_loop(0, tm, issue_row, 0)

    @pl.when(i == 0)
    def _():
        issue(0, 0)

    for sl in range(2):
        @pl.when(i % 2 == sl)
        def _(sl=sl):
            @pl.when(i + 1 < n_tiles)
            def _():
                issue(i + 1, 1 - sl)

            for buf in bufs[sl]:
                pltpu.make_async_copy(ys_ref.at[pl.ds(0, tm * ROW_SUB)], buf, sem.at[sl]).wait()

            route = route_ref[...]
            y = (route[:, R_W1:R_W1 + 1] * _rows_to_2d(bufs[sl][0], tm)
                 + route[:, R_W2:R_W2 + 1] * _rows_to_2d(bufs[sl][1], tm))
            x = x1_ref[...] + gt_ref[0] * y
            o_ref[...] = _rms(x) * g_ref[...] * (1.0 + sc_ref[0]) + sh_ref[0]


def _combine(plan, ys, x1, route, gt2, g_final, scf, shf, seq, tm=256):
    t, d = x1.shape
    per_b = seq // tm
    n_tiles = t // tm
    bmap = lambda m, *_: (m // per_b, 0, 0)
    kern = functools.partial(_combine_kernel, tm=tm, n_tiles=n_tiles)
    return pl.pallas_call(
        kern,
        grid_spec=pltpu.PrefetchScalarGridSpec(
            num_scalar_prefetch=len(plan),
            grid=(n_tiles,),
            in_specs=[pl.BlockSpec(memory_space=pl.ANY),
                      pl.BlockSpec((tm, d), lambda m, *_: (m, 0)),
                      pl.BlockSpec((tm, ROUTE_LANES), lambda m, *_: (m, 0)),
                      pl.BlockSpec((1, 1, d), bmap),
                      pl.BlockSpec((1, d), lambda m, *_: (0, 0)),
                      pl.BlockSpec((1, 1, d), bmap),
                      pl.BlockSpec((1, 1, d), bmap)],
            out_specs=pl.BlockSpec((tm, d), lambda m, *_: (m, 0)),
            scratch_shapes=[pltpu.VMEM((tm * ROW_SUB, LANES), jnp.float32) for _ in range(4)]
            + [pltpu.SemaphoreType.DMA((2,))]),
        out_shape=jax.ShapeDtypeStruct((t, d), jnp.float32),
        compiler_params=_cparams(("arbitrary",)),
        name="combine",
    )(*plan, ys, x1, route, gt2, g_final, scf, shf)


def _moe_plan(route, counts, t):
    i32 = jnp.int32
    cnt = counts[0, N_GROUPS:N_GROUPS + N_EXPERTS].astype(i32)
    padded = (cnt + MOE_TM - 1) // MOE_TM * MOE_TM
    off_end = jnp.cumsum(padded)
    off = off_end - padded
    n_tiles = (2 * t + N_EXPERTS * MOE_TM) // MOE_TM
    n_used = off_end[-1] // MOE_TM
    tile_start = jnp.arange(n_tiles, dtype=i32) * MOE_TM
    te = jnp.sum((off_end[None, :] <= tile_start[:, None]).astype(i32), axis=1)
    te = jnp.minimum(te, N_EXPERTS - 1)
    tile = jnp.arange(n_tiles, dtype=i32)
    te = jnp.where(tile < n_used, te, te[n_used - 1])
    first = jnp.concatenate([jnp.ones((1,), i32), (te[1:] != te[:-1]).astype(i32)])
    nxt_tile = off_end[te] // MOE_TM
    nxt = jnp.where(nxt_tile < n_used, te[jnp.minimum(nxt_tile, n_tiles - 1)], -1)
    slot = (jnp.cumsum(first) - 1) % 2
    assign = tuple(route[:, k].astype(i32) for k in (R_E1, R_E2, R_RANK1, R_RANK2))
    zstart = jnp.maximum(off_end - MOE_TM, 0)
    n_used = n_used.reshape(1)
    return assign, off, zstart, (te, n_used, first, nxt, slot), n_tiles * MOE_TM


def kernel(x, c, w_ada, b_ada, g_mix, w_in, conv_w, w_uk, kv_norm_g, w_uv, g_conv_out, g_attn_out,
           w_out, g_ffn, w_rg, b_rg, w_re, b_re, w_gate, w_up, w_down, w_ada_f, b_ada_f, g_final):
    b, s, d = x.shape
    assert w_ada.shape[0] == 1, "single layer"
    bf = jnp.bfloat16
    x2 = x.reshape(b * s, d)

    c_pad = jnp.zeros((SUBLANES, d), jnp.float32).at[:b].set(c)
    mod = _mod(c_pad, w_ada[0], b_ada[0])[:b]
    modf = _mod(c_pad, w_ada_f, b_ada_f)[:b]
    vec = lambda a, i: a[:, i * d:(i + 1) * d].reshape(b, 1, d)
    sh1, sc1, gt1, sh2, sc2, gt2 = (vec(mod, i) for i in range(N_MOD))
    shf, scf = vec(modf, 0), vec(modf, 1)
    row = lambda a: a.reshape(1, -1)

    h = _hnorm(x2, row(g_mix[0]), sc1, sh1, s)
    w_in_t = jnp.swapaxes(w_in[0], 0, 1)
    w_tail = jnp.pad(w_in_t[OFF_IK:], ((0, IKW_COLS - (IN_COLS - OFF_IK)), (0, 0)))
    conv3 = _colmm(h, w_in_t, 0, 3 * CONV_DIM, 1024, jnp.float32, "proj_conv")
    q = _colmm(h, w_in_t, OFF_Q, ATTN_DIM, 1024, bf, "proj_q")
    kv = _colmm(h, w_in_t, OFF_KV, KV_RANK, 512, jnp.float32, "proj_kv")
    iq = _colmm(h, w_in_t, OFF_IQ, IDX_HEADS * IDX_DIM, 512, bf, "proj_iq")
    ikw = _colmm(h, w_tail, 0, IKW_COLS, IKW_COLS, jnp.float32, "proj_ikw")

    mix_c = _conv(conv3.reshape(b, s, 3 * CONV_DIM), conv_w[0], row(g_conv_out[0]))
    mix_a = _attn(iq.reshape(b, s, -1), ikw.reshape(b, s, -1), kv.reshape(b, s, -1),
                  q.reshape(b, s, -1), w_uk[0].astype(bf), w_uv[0].astype(bf),
                  row(kv_norm_g[0]), row(g_attn_out[0]))

    w_route = jnp.zeros((d, ROUTE_LANES), bf).at[:, :N_GROUPS].set(w_rg[0].astype(bf))
    w_route = w_route.at[:, N_GROUPS:N_GROUPS + N_EXPERTS].set(w_re[0].astype(bf))
    b_route = jnp.zeros((1, ROUTE_LANES), jnp.float32).at[0, :N_GROUPS].set(b_rg[0])
    b_route = b_route.at[0, N_GROUPS:N_GROUPS + N_EXPERTS].set(b_re[0])
    x1, h2, route, counts = _out(mix_c.reshape(b * s, -1), mix_a.reshape(b * s, -1),
                                 w_out[0], x2, gt1, row(g_ffn[0]),
                                 sc2, sh2, w_route, b_route, s)

    assign, off, zstart, tiles, n_rows = _moe_plan(route, counts, b * s)
    xs = _dispatch((*assign, off, zstart, tiles[1]), h2, n_rows)
    ys = _mlp(tiles, xs, w_gate[0], w_up[0], w_down[0])
    out = _combine((*assign, off), ys, x1, route, gt2, row(g_final), scf, shf, s)
    return out.reshape(b, s, d)
```

```python
import functools

import jax
import jax.numpy as jnp
from jax import lax
from jax.experimental import pallas as pl
from jax.experimental.pallas import tpu as pltpu

D_MODEL = 2048
CONV_DIM = 1024
CONV_WIDTH = 3
N_HEADS = 8
HEAD_DIM = 128
ATTN_DIM = N_HEADS * HEAD_DIM
KV_RANK = 512
IDX_HEADS = 16
IDX_DIM = 128
TOPK_MAX = 256
N_GROUPS = 4
EXPERTS_PER_GROUP = 8
N_EXPERTS = N_GROUPS * EXPERTS_PER_GROUP
EXPERT_FF = 512
N_MOD = 6
EPS = 1e-6

OFF_Q = 3 * CONV_DIM
OFF_KV = OFF_Q + ATTN_DIM
OFF_IQ = OFF_KV + KV_RANK
OFF_IK = OFF_IQ + IDX_HEADS * IDX_DIM
OFF_IW = OFF_IK + IDX_DIM
IN_COLS = OFF_IW + IDX_HEADS

LANES = 128
SUBLANES = 8
VMEM_LIMIT = 56 * 1024 * 1024

IKW_COLS = 256
ATTN_Q = 256
ATTN_KC = 256
ATTN_RB = 256
MASKED = -1e30
TIE_ALL = 2 ** 30
ROUTE_LANES = 128
STAGE_ROWS = 512
OUT_SUB = 2
MOE_TM = 256
ROW_SUB = D_MODEL // LANES
INT_MIN = -2 ** 31
KEY_NEG_INF = INT_MIN + 0x7FFFFF


def _cparams(sem):
    return pltpu.CompilerParams(dimension_semantics=sem, vmem_limit_bytes=VMEM_LIMIT)


def _rms(v, axis=-1):
    return v * lax.rsqrt(jnp.mean(v * v, axis=axis, keepdims=True) + EPS)


def _tile_lanes(v, n):
    return jnp.concatenate([v] * n, axis=1)


def _rows_to_2d(ref, n):
    return jnp.concatenate([ref[pl.ds(c, n, stride=ROW_SUB), :] for c in range(ROW_SUB)], axis=1)


def _rows_from_2d(ref, val):
    n = val.shape[0]
    for c in range(ROW_SUB):
        ref[pl.ds(c, n, stride=ROW_SUB), :] = val[:, c * LANES:(c + 1) * LANES]


def _row(ref, i):
    return ref.at[pl.ds(pl.multiple_of(i * ROW_SUB, ROW_SUB), ROW_SUB)]


def _mod_kernel(c_ref, w_ref, b_ref, o_ref):
    c = c_ref[...]
    ca = (c * jax.nn.sigmoid(c)).astype(jnp.bfloat16)
    o_ref[...] = jnp.dot(ca, w_ref[...].astype(jnp.bfloat16),
                         preferred_element_type=jnp.float32) + b_ref[...]


def _mod(c_pad, w, b, tn=1024):
    d, n = w.shape
    return pl.pallas_call(
        _mod_kernel,
        grid=(n // tn,),
        in_specs=[pl.BlockSpec((c_pad.shape[0], d), lambda j: (0, 0)),
                  pl.BlockSpec((d, tn), lambda j: (0, j)),
                  pl.BlockSpec((1, tn), lambda j: (0, j))],
        out_specs=pl.BlockSpec((c_pad.shape[0], tn), lambda j: (0, j)),
        out_shape=jax.ShapeDtypeStruct((c_pad.shape[0], n), jnp.float32),
        compiler_params=_cparams(("arbitrary",)),
        name="mod",
    )(c_pad, w, b.reshape(1, n))


def _hnorm_kernel(x_ref, g_ref, sc_ref, sh_ref, wt_ref, h_ref, o_ref, w_bf):
    @pl.when(pl.program_id(0) == 0)
    def _():
        w_bf[...] = wt_ref[...].astype(jnp.bfloat16)

    h = (_rms(x_ref[...]) * g_ref[...] * (1.0 + sc_ref[0]) + sh_ref[0]).astype(h_ref.dtype)
    h_ref[...] = h
    o_ref[...] = lax.dot_general(h, w_bf[...], (((1,), (1,)), ((), ())),
                                 preferred_element_type=jnp.float32).astype(o_ref.dtype)


def _hnorm(x2, g_mix, sc1, sh1, wt, col0, ncols, seq, out_dtype, tm=1024):
    t, d = x2.shape
    per_b = seq // tm
    bmap = lambda m: (m // per_b, 0, 0)
    assert col0 % ncols == 0
    return pl.pallas_call(
        _hnorm_kernel,
        grid=(t // tm,),
        in_specs=[pl.BlockSpec((tm, d), lambda m: (m, 0)),
                  pl.BlockSpec((1, d), lambda m: (0, 0)),
                  pl.BlockSpec((1, 1, d), bmap),
                  pl.BlockSpec((1, 1, d), bmap),
                  pl.BlockSpec((ncols, d), lambda m: (col0 // ncols, 0))],
        out_specs=[pl.BlockSpec((tm, d), lambda m: (m, 0)),
                   pl.BlockSpec((tm, ncols), lambda m: (m, 0))],
        out_shape=[jax.ShapeDtypeStruct((t, d), jnp.bfloat16),
                   jax.ShapeDtypeStruct((t, ncols), out_dtype)],
        scratch_shapes=[pltpu.VMEM((ncols, d), jnp.bfloat16)],
        compiler_params=_cparams(("arbitrary",)),
        name="hnorm",
    )(x2, g_mix, sc1, sh1, wt)


def _colmm_kernel(h_ref, wt_ref, o_ref, w_bf):
    @pl.when(pl.program_id(1) == 0)
    def _():
        w_bf[...] = wt_ref[...].astype(jnp.bfloat16)

    o_ref[...] = lax.dot_general(h_ref[...], w_bf[...], (((1,), (1,)), ((), ())),
                                 preferred_element_type=jnp.float32).astype(o_ref.dtype)


def _colmm(h, wt, col0, ncols, tn, out_dtype, name, tm=1024):
    t, d = h.shape
    assert col0 % tn == 0 and ncols % tn == 0
    return pl.pallas_call(
        _colmm_kernel,
        grid=(ncols // tn, t // tm),
        in_specs=[pl.BlockSpec((tm, d), lambda n, m: (m, 0)),
                  pl.BlockSpec((tn, d), lambda n, m: (col0 // tn + n, 0))],
        out_specs=pl.BlockSpec((tm, tn), lambda n, m: (m, n)),
        out_shape=jax.ShapeDtypeStruct((t, ncols), out_dtype),
        scratch_shapes=[pltpu.VMEM((tn, d), jnp.bfloat16)],
        compiler_params=_cparams(("arbitrary", "arbitrary")),
        name=name,
    )(h, wt)


def _projconv_kernel(h_ref, wt_hbm, cw_ref, g_ref, o_ref, w_bf, stage, halo_scr, sem, *, per_b):
    m = pl.program_id(0)

    @pl.when(m == 0)
    def _():
        rows = stage.shape[0]
        for k in range(w_bf.shape[0] // rows):
            cp = pltpu.make_async_copy(wt_hbm.at[pl.ds(k * rows, rows)], stage, sem)
            cp.start()
            cp.wait()
            w_bf[pl.ds(k * rows, rows), :] = stage[...].astype(jnp.bfloat16)

    nt = (((1,), (1,)), ((), ()))
    h = h_ref[...]
    bg, cg, xv = (lax.dot_general(h, w_bf[k * CONV_DIM:(k + 1) * CONV_DIM, :], nt,
                                  preferred_element_type=jnp.float32) for k in range(3))
    u = cg * xv
    halo = jnp.where(m % per_b == 0, 0.0, halo_scr[...])
    halo_scr[...] = u[u.shape[0] - SUBLANES:, :]
    row = lax.broadcasted_iota(jnp.int32, u.shape, 0)
    h1 = halo[SUBLANES - 1:SUBLANES, :]
    h2 = halo[SUBLANES - 2:SUBLANES - 1, :]
    u1 = jnp.where(row == 0, h1, pltpu.roll(u, 1, 0))
    u2 = jnp.where(row == 0, h2, jnp.where(row == 1, h1, pltpu.roll(u, 2, 0)))
    w = cw_ref[...]
    y = bg * (w[0:1, :] * u2 + w[1:2, :] * u1 + w[2:3, :] * u)
    o_ref[...] = (_rms(y) * g_ref[...]).astype(o_ref.dtype)


def _projconv(h, wt, conv_w, g_conv_out, seq, tm=1024):
    t, d = h.shape
    c = CONV_DIM
    kern = functools.partial(_projconv_kernel, per_b=seq // tm)
    return pl.pallas_call(
        kern,
        grid=(t // tm,),
        in_specs=[pl.BlockSpec((tm, d), lambda m: (m, 0)),
                  pl.BlockSpec(memory_space=pl.ANY),
                  pl.BlockSpec((CONV_WIDTH, c), lambda m: (0, 0)),
                  pl.BlockSpec((1, c), lambda m: (0, 0))],
        out_specs=pl.BlockSpec((tm, c), lambda m: (m, 0)),
        out_shape=jax.ShapeDtypeStruct((t, c), jnp.bfloat16),
        scratch_shapes=[pltpu.VMEM((3 * c, d), jnp.bfloat16),
                        pltpu.VMEM((STAGE_ROWS, d), jnp.float32),
                        pltpu.VMEM((SUBLANES, c), jnp.float32),
                        pltpu.SemaphoreType.DMA(())],
        compiler_params=_cparams(("arbitrary",)),
        name="proj_conv",
    )(h, wt, conv_w, g_conv_out)


def _ordered_bits(v):
    return v ^ ((v >> 31) & jnp.int32(0x7FFFFFFF))


def _attn_search(nch, sc_scr, k_sel):
    n_keys = float(nch * ATTN_KC)
    half = ATTN_Q // 2

    def half_step(bit, thr, rows):
        cand = thr + bit
        cand_f = lax.bitcast_convert_type(_ordered_bits(cand), jnp.float32)
        hits = jnp.where(sc_scr[0, rows, :] >= cand_f, 1.0, 0.0)
        for c in range(1, nch):
            hits = hits + jnp.where(sc_scr[c, rows, :] >= cand_f, 1.0, 0.0)
        cnt = jnp.sum(hits, axis=1, keepdims=True)
        cnt = jnp.where(cand < KEY_NEG_INF, n_keys, cnt)
        return jnp.where(cnt >= k_sel, cand, thr)

    def bit_step(i, thrs):
        bit = lax.shift_left(jnp.int32(1), 31 - i)
        return tuple(half_step(bit, thr, pl.ds(k * half, half)) for k, thr in enumerate(thrs))

    init = jnp.full((half, 1), INT_MIN, jnp.int32)
    thrs = lax.fori_loop(0, 32, bit_step, (init, init), unroll=2)
    thr = jnp.concatenate(thrs, axis=0)
    return lax.bitcast_convert_type(_ordered_bits(thr), jnp.float32)


def _attn_tiebreak(nch, sc_scr, thr_f, k_sel, tie_scr):
    kloc = lax.broadcasted_iota(jnp.int32, (ATTN_Q, ATTN_KC), 1)

    def count(pred):
        hits = pred(sc_scr[0], 0)
        for c in range(1, nch):
            hits = hits + pred(sc_scr[c], c)
        return jnp.sum(hits, axis=1, keepdims=True)

    tie_scr[...] = jnp.full(tie_scr.shape, TIE_ALL, jnp.int32)
    n_ge = count(lambda sc, c: jnp.where(sc >= thr_f, 1.0, 0.0))
    excess = jnp.where(thr_f > -jnp.inf, n_ge - k_sel, 0.0)

    @pl.when(jnp.max(excess) > 0.0)
    def _():
        need = k_sel - count(lambda sc, c: jnp.where(sc > thr_f, 1.0, 0.0))
        n_bits = (nch * ATTN_KC - 1).bit_length()

        def step(i, lim):
            cand = lim + lax.shift_left(jnp.int32(1), n_bits - 1 - i)
            below = count(lambda sc, c: jnp.where(
                sc == thr_f, jnp.where(c * ATTN_KC + kloc < cand, 1.0, 0.0), 0.0))
            return jnp.where(below < need, cand, lim)

        lim = lax.fori_loop(0, n_bits, step, jnp.zeros((ATTN_Q, 1), jnp.int32))
        tie_scr[...] = jnp.where(excess > 0.0, lim, TIE_ALL)


def _attn_kernel(iq_ref, ikwq_ref, ikwa_ref, kv_ref, q_ref, wuk_ref, wuv_ref, kvg_ref, go_ref,
                 o_ref, ckv_scr, ik_scr, sc_scr, thr_scr, tie_scr, ql_scr, m_scr, l_scr, acc_scr, y_scr,
                 *, seq, k_sel):
    j = pl.program_id(1)
    nch = j + 1
    nt = (((1,), (1,)), ((), ()))

    @pl.when(j == 0)
    def _():
        ckv_scr[...] = (_rms(kv_ref[0]) * kvg_ref[...]).astype(jnp.bfloat16)
        ik_scr[...] = ikwa_ref[0][:, :IDX_DIM].astype(jnp.bfloat16)

    iw = ikwq_ref[0][:, IDX_DIM:IDX_DIM + IDX_HEADS] * (IDX_HEADS ** -0.5 * IDX_DIM ** -0.5)
    qpos = j * ATTN_Q + lax.broadcasted_iota(jnp.int32, (ATTN_Q, ATTN_KC), 0)
    kloc = lax.broadcasted_iota(jnp.int32, (ATTN_Q, ATTN_KC), 1)

    def score_chunk(c, carry):
        k0 = pl.multiple_of(c * ATTN_KC, ATTN_KC)
        ik_c = ik_scr[pl.ds(k0, ATTN_KC), :]
        score = jnp.zeros((ATTN_Q, ATTN_KC), jnp.float32)
        for h in range(IDX_HEADS):
            s = lax.dot_general(iq_ref[0][:, h * IDX_DIM:(h + 1) * IDX_DIM], ik_c, nt,
                                preferred_element_type=jnp.float32)
            score = score + jnp.maximum(s, 0.0) * iw[:, h:h + 1]
        sc_scr[c] = jnp.where(k0 + kloc <= qpos, score, -jnp.inf)
        return carry

    lax.fori_loop(0, nch, score_chunk, 0)

    for v in range(seq // ATTN_KC):
        @pl.when(j == v)
        def _(v=v):
            thr_scr[...] = _attn_search(v + 1, sc_scr, k_sel)
            _attn_tiebreak(v + 1, sc_scr, thr_scr[...], k_sel, tie_scr)

    for h in range(N_HEADS):
        ql = lax.dot_general(q_ref[0][:, h * HEAD_DIM:(h + 1) * HEAD_DIM], wuk_ref[h], nt,
                             preferred_element_type=jnp.float32)
        ql_scr[h * ATTN_Q:(h + 1) * ATTN_Q, :] = ql.astype(jnp.bfloat16)
    m_scr[...] = jnp.full(m_scr.shape, MASKED, jnp.float32)
    l_scr[...] = jnp.zeros(l_scr.shape, jnp.float32)
    acc_scr[...] = jnp.zeros(acc_scr.shape, jnp.float32)
    thr_f = thr_scr[...]
    tie = tie_scr[...]
    scale = HEAD_DIM ** -0.5

    def attn_chunk(c, carry):
        k0 = pl.multiple_of(c * ATTN_KC, ATTN_KC)
        ckv_c = ckv_scr[pl.ds(k0, ATTN_KC), :]
        sc = sc_scr[c]
        kpos = k0 + kloc
        tied = jnp.where(sc == thr_f, jnp.where(kpos <= tie, 0.0, MASKED), MASKED)
        bias = jnp.where(kpos <= qpos, jnp.where(sc > thr_f, 0.0, tied), MASKED)
        for u in range(N_HEADS * ATTN_Q // ATTN_RB):
            rows = pl.ds(u * ATTN_RB, ATTN_RB)
            qrow = (u * ATTN_RB) % ATTN_Q
            lg = lax.dot_general(ql_scr[rows, :], ckv_c, nt,
                                 preferred_element_type=jnp.float32) * scale + bias[qrow:qrow + ATTN_RB]
            m_old = m_scr[rows, :]
            m_new = jnp.maximum(m_old, jnp.max(lg, axis=1, keepdims=True))
            alpha = jnp.exp(m_old - m_new)
            p = jnp.exp(lg - _tile_lanes(m_new, ATTN_KC // LANES))
            l_scr[rows, :] = alpha * l_scr[rows, :] + jnp.sum(p, axis=1, keepdims=True)
            pv = jnp.dot(p.astype(jnp.bfloat16), ckv_c, preferred_element_type=jnp.float32)
            acc_scr[rows, :] = _tile_lanes(alpha, KV_RANK // LANES) * acc_scr[rows, :] + pv
            m_scr[rows, :] = m_new
        return carry

    lax.fori_loop(0, nch, attn_chunk, 0)

    o = (acc_scr[...] * _tile_lanes(1.0 / l_scr[...], KV_RANK // LANES)).astype(jnp.bfloat16)
    for h in range(N_HEADS):
        y_scr[:, h * HEAD_DIM:(h + 1) * HEAD_DIM] = jnp.dot(
            o[h * ATTN_Q:(h + 1) * ATTN_Q], wuv_ref[h], preferred_element_type=jnp.float32)
    o_ref[0] = (_rms(y_scr[...]) * go_ref[...]).astype(o_ref.dtype)


def _attn(iq, ikw, kv, q, w_uk_bf, w_uv_bf, kv_norm_g, g_attn_out):
    b, s, _ = iq.shape
    assert ATTN_Q == ATTN_KC and s % ATTN_Q == 0
    k_sel = min(TOPK_MAX, s // 4)
    rows = N_HEADS * ATTN_Q
    kern = functools.partial(_attn_kernel, seq=s, k_sel=k_sel)
    return pl.pallas_call(
        kern,
        grid=(b, s // ATTN_Q),
        in_specs=[pl.BlockSpec((1, ATTN_Q, IDX_HEADS * IDX_DIM), lambda bi, j: (bi, j, 0)),
                  pl.BlockSpec((1, ATTN_Q, IKW_COLS), lambda bi, j: (bi, j, 0)),
                  pl.BlockSpec((1, s, IKW_COLS), lambda bi, j: (bi, 0, 0)),
                  pl.BlockSpec((1, s, KV_RANK), lambda bi, j: (bi, 0, 0)),
                  pl.BlockSpec((1, ATTN_Q, ATTN_DIM), lambda bi, j: (bi, j, 0)),
                  pl.BlockSpec((N_HEADS, KV_RANK, HEAD_DIM), lambda bi, j: (0, 0, 0)),
                  pl.BlockSpec((N_HEADS, KV_RANK, HEAD_DIM), lambda bi, j: (0, 0, 0)),
                  pl.BlockSpec((1, KV_RANK), lambda bi, j: (0, 0)),
                  pl.BlockSpec((1, ATTN_DIM), lambda bi, j: (0, 0))],
        out_specs=pl.BlockSpec((1, ATTN_Q, ATTN_DIM), lambda bi, j: (bi, j, 0)),
        out_shape=jax.ShapeDtypeStruct((b, s, ATTN_DIM), jnp.bfloat16),
        scratch_shapes=[pltpu.VMEM((s, KV_RANK), jnp.bfloat16),
                        pltpu.VMEM((s, IDX_DIM), jnp.bfloat16),
                        pltpu.VMEM((s // ATTN_KC, ATTN_Q, ATTN_KC), jnp.float32),
                        pltpu.VMEM((ATTN_Q, 1), jnp.float32),
                        pltpu.VMEM((ATTN_Q, 1), jnp.int32),
                        pltpu.VMEM((rows, KV_RANK), jnp.bfloat16),
                        pltpu.VMEM((rows, LANES), jnp.float32),
                        pltpu.VMEM((rows, LANES), jnp.float32),
                        pltpu.VMEM((rows, KV_RANK), jnp.float32),
                        pltpu.VMEM((ATTN_Q, ATTN_DIM), jnp.float32)],
        compiler_params=_cparams(("arbitrary", "arbitrary")),
        name="attn",
    )(iq, ikw, ikw, kv, q, w_uk_bf, w_uv_bf, kv_norm_g, g_attn_out)


R_E1, R_E2, R_W1, R_W2, R_RANK1, R_RANK2 = range(6)


def _out_kernel(mc_ref, ma_ref, w_hbm, x_ref, gt_ref, g_ref, sc_ref, sh_ref, wr_ref, br_ref,
                x1_ref, h2_ref, route_ref, cnt_ref, cnt_scr, tri_scr, w_bf, stage, sem):
    @pl.when(pl.program_id(0) == 0)
    def _():
        cnt_scr[...] = jnp.zeros(cnt_scr.shape, jnp.float32)
        earlier = (lax.broadcasted_iota(jnp.int32, tri_scr.shape, 1)
                   < lax.broadcasted_iota(jnp.int32, tri_scr.shape, 0))
        tri_scr[...] = jnp.where(earlier, 1.0, 0.0).astype(tri_scr.dtype)
        rows = stage.shape[0]
        for k in range(w_bf.shape[0] // rows):
            cp = pltpu.make_async_copy(w_hbm.at[pl.ds(k * rows, rows)], stage, sem)
            cp.start()
            cp.wait()
            w_bf[pl.ds(k * rows, rows), :] = stage[...].astype(jnp.bfloat16)

    rs = x_ref.shape[0] // OUT_SUB
    for sub in range(OUT_SUB):
        rows = pl.ds(sub * rs, rs)
        _out_rows(mc_ref[rows, :], ma_ref[rows, :], w_bf, x_ref[rows, :], gt_ref, g_ref, sc_ref, sh_ref,
                  wr_ref, br_ref, x1_ref.at[rows], h2_ref.at[pl.ds(sub * rs * ROW_SUB, rs * ROW_SUB)],
                  route_ref.at[rows], cnt_scr, tri_scr[0:rs, 0:rs])
    cnt_ref[...] = cnt_scr[...]


def _out_rows(mc, ma, w_bf, x, gt_ref, g_ref, sc_ref, sh_ref, wr_ref, br_ref,
              x1_ref, h2_ref, route_ref, cnt_scr, tri):
    mix = (jnp.dot(mc, w_bf[0:CONV_DIM, :], preferred_element_type=jnp.float32)
           + jnp.dot(ma, w_bf[CONV_DIM:, :], preferred_element_type=jnp.float32))
    x1 = x + gt_ref[0] * mix
    x1_ref[...] = x1
    h2 = _rms(x1) * g_ref[...] * (1.0 + sc_ref[0]) + sh_ref[0]
    _rows_from_2d(h2_ref, h2)

    logit = jnp.dot(h2.astype(jnp.bfloat16), wr_ref[...],
                    preferred_element_type=jnp.float32) + br_ref[...]
    lane = lax.broadcasted_iota(jnp.int32, logit.shape, 1).astype(jnp.float32)
    neg = -jnp.inf
    big = float(ROUTE_LANES)
    is_g = lane < N_GROUPS
    gl = jnp.where(is_g, logit, neg)
    gmax = jnp.max(gl, axis=1, keepdims=True)
    p_group = 1.0 / jnp.sum(jnp.exp(gl - gmax), axis=1, keepdims=True)
    g_sel = jnp.min(jnp.where(is_g & (gl == gmax), lane, big), axis=1, keepdims=True)
    lo = N_GROUPS + g_sel * EXPERTS_PER_GROUP
    in_grp = (lane >= lo) & (lane < lo + EXPERTS_PER_GROUP)
    el = jnp.where(in_grp, logit, neg)
    m1 = jnp.max(el, axis=1, keepdims=True)
    i1 = jnp.min(jnp.where(in_grp & (el == m1), lane, big), axis=1, keepdims=True)
    el2 = jnp.where(lane == i1, neg, el)
    m2 = jnp.max(el2, axis=1, keepdims=True)
    i2 = jnp.min(jnp.where(in_grp & (lane != i1) & (el2 == m2), lane, big), axis=1, keepdims=True)
    r = jnp.exp(m2 - m1)
    w1 = p_group / (1.0 + r)
    w2 = p_group * r / (1.0 + r)

    member = jnp.where(lane == i1, 1.0, jnp.where(lane == i2, 1.0, 0.0))
    before = jnp.dot(tri, member.astype(jnp.bfloat16),
                     preferred_element_type=jnp.float32) + cnt_scr[...]
    rank1 = jnp.sum(jnp.where(lane == i1, before, 0.0), axis=1, keepdims=True)
    rank2 = jnp.sum(jnp.where(lane == i2, before, 0.0), axis=1, keepdims=True)
    cnt_scr[...] = cnt_scr[...] + jnp.sum(member, axis=0, keepdims=True)

    route = jnp.zeros_like(logit)
    for idx, val in ((R_E1, i1 - N_GROUPS), (R_E2, i2 - N_GROUPS), (R_W1, w1), (R_W2, w2),
                     (R_RANK1, rank1), (R_RANK2, rank2)):
        route = jnp.where(lane == idx, val, route)
    route_ref[...] = route


def _out(mix_c, mix_a, w_out, x2, gt1, g_ffn, sc2, sh2, w_route, b_route, seq, tm=512):
    t, d = x2.shape
    per_b = seq // tm
    bmap = lambda m: (m // per_b, 0, 0)
    return pl.pallas_call(
        _out_kernel,
        grid=(t // tm,),
        in_specs=[pl.BlockSpec((tm, CONV_DIM), lambda m: (m, 0)),
                  pl.BlockSpec((tm, ATTN_DIM), lambda m: (m, 0)),
                  pl.BlockSpec(memory_space=pl.ANY),
                  pl.BlockSpec((tm, d), lambda m: (m, 0)),
                  pl.BlockSpec((1, 1, d), bmap),
                  pl.BlockSpec((1, d), lambda m: (0, 0)),
                  pl.BlockSpec((1, 1, d), bmap),
                  pl.BlockSpec((1, 1, d), bmap),
                  pl.BlockSpec((d, ROUTE_LANES), lambda m: (0, 0)),
                  pl.BlockSpec((1, ROUTE_LANES), lambda m: (0, 0))],
        out_specs=[pl.BlockSpec((tm, d), lambda m: (m, 0)),
                   pl.BlockSpec((tm * ROW_SUB, LANES), lambda m: (m, 0)),
                   pl.BlockSpec((tm, ROUTE_LANES), lambda m: (m, 0)),
                   pl.BlockSpec((1, ROUTE_LANES), lambda m: (0, 0))],
        out_shape=[jax.ShapeDtypeStruct((t, d), jnp.float32),
                   jax.ShapeDtypeStruct((t * ROW_SUB, LANES), jnp.float32),
                   jax.ShapeDtypeStruct((t, ROUTE_LANES), jnp.float32),
                   jax.ShapeDtypeStruct((1, ROUTE_LANES), jnp.float32)],
        scratch_shapes=[pltpu.VMEM((1, ROUTE_LANES), jnp.float32),
                        pltpu.VMEM((tm, tm), jnp.bfloat16),
                        pltpu.VMEM(w_out.shape, jnp.bfloat16),
                        pltpu.VMEM((STAGE_ROWS, d), jnp.float32),
                        pltpu.SemaphoreType.DMA(())],
        compiler_params=_cparams(("arbitrary",)),
        name="out",
    )(mix_c, mix_a, w_out, x2, gt1, g_ffn, sc2, sh2, w_route, b_route)


def _slot(off_ref, e_ref, r_ref, tok):
    return off_ref[e_ref[tok]] + r_ref[tok]


def _dispatch_kernel(e1_ref, e2_ref, r1_ref, r2_ref, off_ref, zs_ref, nu_ref, h2_ref, xs_ref,
                     zbuf, sem_z, sem_r, *, tm, n_tiles):
    i = pl.program_id(0)

    @pl.when(i == 0)
    def _():
        zbuf[...] = jnp.zeros(zbuf.shape, zbuf.dtype)

        def zero_copy(start):
            rows = pl.ds(pl.multiple_of(start * ROW_SUB, MOE_TM * ROW_SUB), MOE_TM * ROW_SUB)
            return pltpu.make_async_copy(zbuf, xs_ref.at[rows], sem_z)

        for e in range(N_EXPERTS):
            zero_copy(zs_ref[e]).start()
        for e in range(N_EXPERTS):
            zero_copy(zs_ref[e]).wait()

        def zero_tile(k, carry):
            zero_copy(k * MOE_TM).start()
            zero_copy(k * MOE_TM).wait()
            return carry

        lax.fori_loop(nu_ref[0], n_tiles, zero_tile, 0)

    def issue_row(r, carry):
        tok = i * tm + r
        for e_ref, r_ref in ((e1_ref, r1_ref), (e2_ref, r2_ref)):
            pltpu.make_async_copy(_row(h2_ref, r), _row(xs_ref, _slot(off_ref, e_ref, r_ref, tok)),
                                  sem_r).start()
        return carry

    lax.fori_loop(0, tm, issue_row, 0)
    for _ in range(2):
        pltpu.make_async_copy(h2_ref, xs_ref.at[pl.ds(0, tm * ROW_SUB)], sem_r).wait()


def _dispatch(plan, h2, n_rows, tm=256):
    t = h2.shape[0] // ROW_SUB
    kern = functools.partial(_dispatch_kernel, tm=tm, n_tiles=n_rows // MOE_TM)
    return pl.pallas_call(
        kern,
        grid_spec=pltpu.PrefetchScalarGridSpec(
            num_scalar_prefetch=len(plan),
            grid=(t // tm,),
            in_specs=[pl.BlockSpec((tm * ROW_SUB, LANES), lambda i, *_: (i, 0))],
            out_specs=pl.BlockSpec(memory_space=pl.ANY),
            scratch_shapes=[pltpu.VMEM((MOE_TM * ROW_SUB, LANES), h2.dtype),
                            pltpu.SemaphoreType.DMA(()),
                            pltpu.SemaphoreType.DMA(())]),
        out_shape=jax.ShapeDtypeStruct((n_rows * ROW_SUB, LANES), h2.dtype),
        compiler_params=_cparams(("arbitrary",)),
        name="dispatch",
    )(*plan, h2)


def _mlp_kernel(te_ref, nu_ref, first_ref, nxt_ref, slot_ref, xs_ref, wg_hbm, wu_hbm, wd_hbm, ys_ref,
                wg_f32, wu_f32, wd_f32, wg_bf, wu_bf, wd_bf, sem):
    i = pl.program_id(0)

    def copies(e, s):
        return [pltpu.make_async_copy(hbm.at[e], buf.at[s], sem.at[s])
                for hbm, buf in ((wg_hbm, wg_f32), (wu_hbm, wu_f32), (wd_hbm, wd_f32))]

    @pl.when(i == 0)
    def _():
        for cp in copies(te_ref[0], 0):
            cp.start()

    @pl.when((i < nu_ref[0]) & (first_ref[i] == 1))
    def _():
        s = slot_ref[i]
        for cp in copies(te_ref[i], s):
            cp.wait()
        wg_bf[...] = wg_f32[s].astype(jnp.bfloat16)
        wu_bf[...] = wu_f32[s].astype(jnp.bfloat16)
        wd_bf[...] = wd_f32[s].astype(jnp.bfloat16)

        @pl.when(nxt_ref[i] >= 0)
        def _():
            for cp in copies(nxt_ref[i], 1 - s):
                cp.start()

    @pl.when(i < nu_ref[0])
    def _():
        x = _rows_to_2d(xs_ref, MOE_TM).astype(jnp.bfloat16)
        a = jnp.dot(x, wg_bf[...], preferred_element_type=jnp.float32)
        u = jnp.dot(x, wu_bf[...], preferred_element_type=jnp.float32)
        hid = (a * jax.nn.sigmoid(a)) * u
        _rows_from_2d(ys_ref, jnp.dot(hid.astype(jnp.bfloat16), wd_bf[...],
                                      preferred_element_type=jnp.float32))

    @pl.when(i >= nu_ref[0])
    def _():
        ys_ref[...] = jnp.zeros(ys_ref.shape, ys_ref.dtype)


def _mlp(tiles, xs, w_gate, w_up, w_down):
    n_rows = xs.shape[0] // ROW_SUB
    _, d, f = w_gate.shape
    used = lambda i, te, nu, *_: (jnp.minimum(i, nu[0] - 1), 0)
    return pl.pallas_call(
        _mlp_kernel,
        grid_spec=pltpu.PrefetchScalarGridSpec(
            num_scalar_prefetch=len(tiles),
            grid=(n_rows // MOE_TM,),
            in_specs=[pl.BlockSpec((MOE_TM * ROW_SUB, LANES), used),
                      pl.BlockSpec(memory_space=pl.ANY),
                      pl.BlockSpec(memory_space=pl.ANY),
                      pl.BlockSpec(memory_space=pl.ANY)],
            out_specs=pl.BlockSpec((MOE_TM * ROW_SUB, LANES), lambda i, *_: (i, 0)),
            scratch_shapes=[pltpu.VMEM((2, d, f), jnp.float32),
                            pltpu.VMEM((2, d, f), jnp.float32),
                            pltpu.VMEM((2, f, d), jnp.float32),
                            pltpu.VMEM((d, f), jnp.bfloat16),
                            pltpu.VMEM((d, f), jnp.bfloat16),
                            pltpu.VMEM((f, d), jnp.bfloat16),
                            pltpu.SemaphoreType.DMA((2,))]),
        out_shape=jax.ShapeDtypeStruct((n_rows * ROW_SUB, LANES), jnp.float32),
        compiler_params=_cparams(("arbitrary",)),
        name="mlp",
    )(*tiles, xs, w_gate, w_up, w_down)


def _combine_kernel(e1_ref, e2_ref, r1_ref, r2_ref, off_ref, ys_ref, x1_ref, route_ref, gt_ref, g_ref,
                    sc_ref, sh_ref, o_ref, a0, b0, a1, b1, sem, *, tm, n_tiles):
    i = pl.program_id(0)
    bufs = ((a0, b0), (a1, b1))

    def issue(tile, sl):
        def issue_row(r, carry):
            tok = tile * tm + r
            for buf, e_ref, r_ref in ((bufs[sl][0], e1_ref, r1_ref), (bufs[sl][1], e2_ref, r2_ref)):
                pltpu.make_async_copy(_row(ys_ref, _slot(off_ref, e_ref, r_ref, tok)),
                                      _row(buf, r), sem.at[sl]).start()
            return carry

        lax.fori_loop(0, tm, issue_row, 0)

    @pl.when(i == 0)
    def _():
        issue(0, 0)

    for sl in range(2):
        @pl.when(i % 2 == sl)
        def _(sl=sl):
            @pl.when(i + 1 < n_tiles)
            def _():
                issue(i + 1, 1 - sl)

            for buf in bufs[sl]:
                pltpu.make_async_copy(ys_ref.at[pl.ds(0, tm * ROW_SUB)], buf, sem.at[sl]).wait()

            route = route_ref[...]
            y = (route[:, R_W1:R_W1 + 1] * _rows_to_2d(bufs[sl][0], tm)
                 + route[:, R_W2:R_W2 + 1] * _rows_to_2d(bufs[sl][1], tm))
            x = x1_ref[...] + gt_ref[0] * y
            o_ref[...] = _rms(x) * g_ref[...] * (1.0 + sc_ref[0]) + sh_ref[0]


def _combine(plan, ys, x1, route, gt2, g_final, scf, shf, seq, tm=256):
    t, d = x1.shape
    per_b = seq // tm
    n_tiles = t // tm
    bmap = lambda m, *_: (m // per_b, 0, 0)
    kern = functools.partial(_combine_kernel, tm=tm, n_tiles=n_tiles)
    return pl.pallas_call(
        kern,
        grid_spec=pltpu.PrefetchScalarGridSpec(
            num_scalar_prefetch=len(plan),
            grid=(n_tiles,),
            in_specs=[pl.BlockSpec(memory_space=pl.ANY),
                      pl.BlockSpec((tm, d), lambda m, *_: (m, 0)),
                      pl.BlockSpec((tm, ROUTE_LANES), lambda m, *_: (m, 0)),
                      pl.BlockSpec((1, 1, d), bmap),
                      pl.BlockSpec((1, d), lambda m, *_: (0, 0)),
                      pl.BlockSpec((1, 1, d), bmap),
                      pl.BlockSpec((1, 1, d), bmap)],
            out_specs=pl.BlockSpec((tm, d), lambda m, *_: (m, 0)),
            scratch_shapes=[pltpu.VMEM((tm * ROW_SUB, LANES), jnp.float32) for _ in range(4)]
            + [pltpu.SemaphoreType.DMA((2,))]),
        out_shape=jax.ShapeDtypeStruct((t, d), jnp.float32),
        compiler_params=_cparams(("arbitrary",)),
        name="combine",
    )(*plan, ys, x1, route, gt2, g_final, scf, shf)


def _moe_plan(route, counts, t):
    i32 = jnp.int32
    cnt = counts[0, N_GROUPS:N_GROUPS + N_EXPERTS].astype(i32)
    padded = (cnt + MOE_TM - 1) // MOE_TM * MOE_TM
    off_end = jnp.cumsum(padded)
    off = off_end - padded
    n_tiles = (2 * t + N_EXPERTS * MOE_TM) // MOE_TM
    n_used = off_end[-1] // MOE_TM
    tile_start = jnp.arange(n_tiles, dtype=i32) * MOE_TM
    te = jnp.sum((off_end[None, :] <= tile_start[:, None]).astype(i32), axis=1)
    te = jnp.minimum(te, N_EXPERTS - 1)
    tile = jnp.arange(n_tiles, dtype=i32)
    te = jnp.where(tile < n_used, te, te[n_used - 1])
    first = jnp.concatenate([jnp.ones((1,), i32), (te[1:] != te[:-1]).astype(i32)])
    nxt_tile = off_end[te] // MOE_TM
    nxt = jnp.where(nxt_tile < n_used, te[jnp.minimum(nxt_tile, n_tiles - 1)], -1)
    slot = (jnp.cumsum(first) - 1) % 2
    assign = tuple(route[:, k].astype(i32) for k in (R_E1, R_E2, R_RANK1, R_RANK2))
    zstart = jnp.maximum(off_end - MOE_TM, 0)
    n_used = n_used.reshape(1)
    return assign, off, zstart, (te, n_used, first, nxt, slot), n_tiles * MOE_TM


def kernel(x, c, w_ada, b_ada, g_mix, w_in, conv_w, w_uk, kv_norm_g, w_uv, g_conv_out, g_attn_out,
           w_out, g_ffn, w_rg, b_rg, w_re, b_re, w_gate, w_up, w_down, w_ada_f, b_ada_f, g_final):
    b, s, d = x.shape
    assert w_ada.shape[0] == 1, "single layer"
    bf = jnp.bfloat16
    x2 = x.reshape(b * s, d)

    c_pad = jnp.zeros((SUBLANES, d), jnp.float32).at[:b].set(c)
    mod = _mod(c_pad, w_ada[0], b_ada[0])[:b]
    modf = _mod(c_pad, w_ada_f, b_ada_f)[:b]
    vec = lambda a, i: a[:, i * d:(i + 1) * d].reshape(b, 1, d)
    sh1, sc1, gt1, sh2, sc2, gt2 = (vec(mod, i) for i in range(N_MOD))
    shf, scf = vec(modf, 0), vec(modf, 1)
    row = lambda a: a.reshape(1, -1)

    w_in_t = jnp.swapaxes(w_in[0], 0, 1)
    h, kv = _hnorm(x2, row(g_mix[0]), sc1, sh1, w_in_t, OFF_KV, KV_RANK, s, jnp.float32)
    w_tail = jnp.pad(w_in_t[OFF_IK:], ((0, IKW_COLS - (IN_COLS - OFF_IK)), (0, 0)))
    mix_c = _projconv(h, w_in_t, conv_w[0], row(g_conv_out[0]), s)
    q = _colmm(h, w_in_t, OFF_Q, ATTN_DIM, 1024, bf, "proj_q")
    iq = _colmm(h, w_in_t, OFF_IQ, IDX_HEADS * IDX_DIM, 512, bf, "proj_iq")
    ikw = _colmm(h, w_tail, 0, IKW_COLS, IKW_COLS, jnp.float32, "proj_ikw")

    mix_a = _attn(iq.reshape(b, s, -1), ikw.reshape(b, s, -1), kv.reshape(b, s, -1),
                  q.reshape(b, s, -1), w_uk[0].astype(bf), w_uv[0].astype(bf),
                  row(kv_norm_g[0]), row(g_attn_out[0]))

    w_route = jnp.zeros((d, ROUTE_LANES), bf).at[:, :N_GROUPS].set(w_rg[0].astype(bf))
    w_route = w_route.at[:, N_GROUPS:N_GROUPS + N_EXPERTS].set(w_re[0].astype(bf))
    b_route = jnp.zeros((1, ROUTE_LANES), jnp.float32).at[0, :N_GROUPS].set(b_rg[0])
    b_route = b_route.at[0, N_GROUPS:N_GROUPS + N_EXPERTS].set(b_re[0])
    x1, h2, route, counts = _out(mix_c, mix_a.reshape(b * s, -1),
                                 w_out[0], x2, gt1, row(g_ffn[0]),
                                 sc2, sh2, w_route, b_route, s)

    assign, off, zstart, tiles, n_rows = _moe_plan(route, counts, b * s)
    xs = _dispatch((*assign, off, zstart, tiles[1]), h2, n_rows)
    ys = _mlp(tiles, xs, w_gate[0], w_up[0], w_down[0])
    out = _combine((*assign, off), ys, x1, route, gt2, row(g_final), scf, shf, s)
    return out.reshape(b, s, d)
```

```python
import functools

import jax
import jax.numpy as jnp
from jax import lax
from jax.experimental import pallas as pl
from jax.experimental.pallas import tpu as pltpu

D_MODEL = 2048
CONV_DIM = 1024
CONV_WIDTH = 3
N_HEADS = 8
HEAD_DIM = 128
ATTN_DIM = N_HEADS * HEAD_DIM
KV_RANK = 512
IDX_HEADS = 16
IDX_DIM = 128
TOPK_MAX = 256
N_GROUPS = 4
EXPERTS_PER_GROUP = 8
N_EXPERTS = N_GROUPS * EXPERTS_PER_GROUP
EXPERT_FF = 512
N_MOD = 6
EPS = 1e-6

OFF_Q = 3 * CONV_DIM
OFF_KV = OFF_Q + ATTN_DIM
OFF_IQ = OFF_KV + KV_RANK
OFF_IK = OFF_IQ + IDX_HEADS * IDX_DIM
OFF_IW = OFF_IK + IDX_DIM
IN_COLS = OFF_IW + IDX_HEADS

LANES = 128
SUBLANES = 8
VMEM_LIMIT = 56 * 1024 * 1024

IKW_COLS = 256
ATTN_Q = 256
ATTN_KC = 256
ATTN_RB = 256
MASKED = -1e30
TIE_ALL = 2 ** 30
ROUTE_LANES = 128
STAGE_ROWS = 512
OUT_SUB = 2
MOE_TM = 256
ZERO_ROWS = 64
ROW_SUB = D_MODEL // LANES
INT_MIN = -2 ** 31
KEY_NEG_INF = INT_MIN + 0x7FFFFF


def _cparams(sem):
    return pltpu.CompilerParams(dimension_semantics=sem, vmem_limit_bytes=VMEM_LIMIT)


def _rms(v, axis=-1):
    return v * lax.rsqrt(jnp.mean(v * v, axis=axis, keepdims=True) + EPS)


def _tile_lanes(v, n):
    return jnp.concatenate([v] * n, axis=1)


def _rows_to_2d(ref, n):
    return jnp.concatenate([ref[pl.ds(c, n, stride=ROW_SUB), :] for c in range(ROW_SUB)], axis=1)


def _rows_from_2d(ref, val):
    n = val.shape[0]
    for c in range(ROW_SUB):
        ref[pl.ds(c, n, stride=ROW_SUB), :] = val[:, c * LANES:(c + 1) * LANES]


def _row(ref, i):
    return ref.at[pl.ds(pl.multiple_of(i * ROW_SUB, ROW_SUB), ROW_SUB)]


def _mod_kernel(c_ref, w_ref, b_ref, o_ref):
    c = c_ref[...]
    ca = (c * jax.nn.sigmoid(c)).astype(jnp.bfloat16)
    o_ref[...] = jnp.dot(ca, w_ref[...].astype(jnp.bfloat16),
                         preferred_element_type=jnp.float32) + b_ref[...]


def _mod(c_pad, w, b, tn=1024):
    d, n = w.shape
    return pl.pallas_call(
        _mod_kernel,
        grid=(n // tn,),
        in_specs=[pl.BlockSpec((c_pad.shape[0], d), lambda j: (0, 0)),
                  pl.BlockSpec((d, tn), lambda j: (0, j)),
                  pl.BlockSpec((1, tn), lambda j: (0, j))],
        out_specs=pl.BlockSpec((c_pad.shape[0], tn), lambda j: (0, j)),
        out_shape=jax.ShapeDtypeStruct((c_pad.shape[0], n), jnp.float32),
        compiler_params=_cparams(("arbitrary",)),
        name="mod",
    )(c_pad, w, b.reshape(1, n))


def _hnorm_kernel(x_ref, g_ref, sc_ref, sh_ref, wt_ref, h_ref, o_ref, w_bf):
    @pl.when(pl.program_id(0) == 0)
    def _():
        w_bf[...] = wt_ref[...].astype(jnp.bfloat16)

    h = (_rms(x_ref[...]) * g_ref[...] * (1.0 + sc_ref[0]) + sh_ref[0]).astype(h_ref.dtype)
    h_ref[...] = h
    o_ref[...] = lax.dot_general(h, w_bf[...], (((1,), (1,)), ((), ())),
                                 preferred_element_type=jnp.float32).astype(o_ref.dtype)


def _hnorm(x2, g_mix, sc1, sh1, wt, col0, ncols, seq, out_dtype, tm=1024):
    t, d = x2.shape
    per_b = seq // tm
    bmap = lambda m: (m // per_b, 0, 0)
    assert col0 % ncols == 0
    return pl.pallas_call(
        _hnorm_kernel,
        grid=(t // tm,),
        in_specs=[pl.BlockSpec((tm, d), lambda m: (m, 0)),
                  pl.BlockSpec((1, d), lambda m: (0, 0)),
                  pl.BlockSpec((1, 1, d), bmap),
                  pl.BlockSpec((1, 1, d), bmap),
                  pl.BlockSpec((ncols, d), lambda m: (col0 // ncols, 0))],
        out_specs=[pl.BlockSpec((tm, d), lambda m: (m, 0)),
                   pl.BlockSpec((tm, ncols), lambda m: (m, 0))],
        out_shape=[jax.ShapeDtypeStruct((t, d), jnp.bfloat16),
                   jax.ShapeDtypeStruct((t, ncols), out_dtype)],
        scratch_shapes=[pltpu.VMEM((ncols, d), jnp.bfloat16)],
        compiler_params=_cparams(("arbitrary",)),
        name="hnorm",
    )(x2, g_mix, sc1, sh1, wt)


def _colmm_kernel(h_ref, wt_ref, o_ref, w_bf):
    @pl.when(pl.program_id(1) == 0)
    def _():
        w_bf[...] = wt_ref[...].astype(jnp.bfloat16)

    o_ref[...] = lax.dot_general(h_ref[...], w_bf[...], (((1,), (1,)), ((), ())),
                                 preferred_element_type=jnp.float32).astype(o_ref.dtype)


def _colmm(h, wt, col0, ncols, tn, out_dtype, name, tm=1024):
    t, d = h.shape
    assert col0 % tn == 0 and ncols % tn == 0
    return pl.pallas_call(
        _colmm_kernel,
        grid=(ncols // tn, t // tm),
        in_specs=[pl.BlockSpec((tm, d), lambda n, m: (m, 0)),
                  pl.BlockSpec((tn, d), lambda n, m: (col0 // tn + n, 0))],
        out_specs=pl.BlockSpec((tm, tn), lambda n, m: (m, n)),
        out_shape=jax.ShapeDtypeStruct((t, ncols), out_dtype),
        scratch_shapes=[pltpu.VMEM((tn, d), jnp.bfloat16)],
        compiler_params=_cparams(("arbitrary", "arbitrary")),
        name=name,
    )(h, wt)


def _projconv_kernel(h_ref, wt_hbm, cw_ref, g_ref, o_ref, w_bf, stage, halo_scr, sem, *, per_b):
    m = pl.program_id(0)

    @pl.when(m == 0)
    def _():
        rows = stage.shape[0]
        for k in range(w_bf.shape[0] // rows):
            cp = pltpu.make_async_copy(wt_hbm.at[pl.ds(k * rows, rows)], stage, sem)
            cp.start()
            cp.wait()
            w_bf[pl.ds(k * rows, rows), :] = stage[...].astype(jnp.bfloat16)

    nt = (((1,), (1,)), ((), ()))
    h = h_ref[...]
    bg, cg, xv = (lax.dot_general(h, w_bf[k * CONV_DIM:(k + 1) * CONV_DIM, :], nt,
                                  preferred_element_type=jnp.float32) for k in range(3))
    u = cg * xv
    halo = jnp.where(m % per_b == 0, 0.0, halo_scr[...])
    halo_scr[...] = u[u.shape[0] - SUBLANES:, :]
    row = lax.broadcasted_iota(jnp.int32, u.shape, 0)
    h1 = halo[SUBLANES - 1:SUBLANES, :]
    h2 = halo[SUBLANES - 2:SUBLANES - 1, :]
    u1 = jnp.where(row == 0, h1, pltpu.roll(u, 1, 0))
    u2 = jnp.where(row == 0, h2, jnp.where(row == 1, h1, pltpu.roll(u, 2, 0)))
    w = cw_ref[...]
    y = bg * (w[0:1, :] * u2 + w[1:2, :] * u1 + w[2:3, :] * u)
    o_ref[...] = (_rms(y) * g_ref[...]).astype(o_ref.dtype)


def _projconv(h, wt, conv_w, g_conv_out, seq, tm=1024):
    t, d = h.shape
    c = CONV_DIM
    kern = functools.partial(_projconv_kernel, per_b=seq // tm)
    return pl.pallas_call(
        kern,
        grid=(t // tm,),
        in_specs=[pl.BlockSpec((tm, d), lambda m: (m, 0)),
                  pl.BlockSpec(memory_space=pl.ANY),
                  pl.BlockSpec((CONV_WIDTH, c), lambda m: (0, 0)),
                  pl.BlockSpec((1, c), lambda m: (0, 0))],
        out_specs=pl.BlockSpec((tm, c), lambda m: (m, 0)),
        out_shape=jax.ShapeDtypeStruct((t, c), jnp.bfloat16),
        scratch_shapes=[pltpu.VMEM((3 * c, d), jnp.bfloat16),
                        pltpu.VMEM((STAGE_ROWS, d), jnp.float32),
                        pltpu.VMEM((SUBLANES, c), jnp.float32),
                        pltpu.SemaphoreType.DMA(())],
        compiler_params=_cparams(("arbitrary",)),
        name="proj_conv",
    )(h, wt, conv_w, g_conv_out)


def _ordered_bits(v):
    return v ^ ((v >> 31) & jnp.int32(0x7FFFFFFF))


def _attn_search(nch, sc_scr, k_sel):
    n_keys = float(nch * ATTN_KC)
    half = ATTN_Q // 2

    def half_step(bit, thr, rows):
        cand = thr + bit
        cand_f = lax.bitcast_convert_type(_ordered_bits(cand), jnp.float32)
        hits = jnp.where(sc_scr[0, rows, :] >= cand_f, 1.0, 0.0)
        for c in range(1, nch):
            hits = hits + jnp.where(sc_scr[c, rows, :] >= cand_f, 1.0, 0.0)
        cnt = jnp.sum(hits, axis=1, keepdims=True)
        cnt = jnp.where(cand < KEY_NEG_INF, n_keys, cnt)
        return jnp.where(cnt >= k_sel, cand, thr)

    def bit_step(i, thrs):
        bit = lax.shift_left(jnp.int32(1), 31 - i)
        return tuple(half_step(bit, thr, pl.ds(k * half, half)) for k, thr in enumerate(thrs))

    init = jnp.full((half, 1), INT_MIN, jnp.int32)
    thrs = lax.fori_loop(0, 32, bit_step, (init, init), unroll=2)
    thr = jnp.concatenate(thrs, axis=0)
    return lax.bitcast_convert_type(_ordered_bits(thr), jnp.float32)


def _attn_tiebreak(nch, sc_scr, thr_f, k_sel, tie_scr):
    kloc = lax.broadcasted_iota(jnp.int32, (ATTN_Q, ATTN_KC), 1)

    def count(pred):
        hits = pred(sc_scr[0], 0)
        for c in range(1, nch):
            hits = hits + pred(sc_scr[c], c)
        return jnp.sum(hits, axis=1, keepdims=True)

    tie_scr[...] = jnp.full(tie_scr.shape, TIE_ALL, jnp.int32)
    n_ge = count(lambda sc, c: jnp.where(sc >= thr_f, 1.0, 0.0))
    excess = jnp.where(thr_f > -jnp.inf, n_ge - k_sel, 0.0)

    @pl.when(jnp.max(excess) > 0.0)
    def _():
        need = k_sel - count(lambda sc, c: jnp.where(sc > thr_f, 1.0, 0.0))
        n_bits = (nch * ATTN_KC - 1).bit_length()

        def step(i, lim):
            cand = lim + lax.shift_left(jnp.int32(1), n_bits - 1 - i)
            below = count(lambda sc, c: jnp.where(
                sc == thr_f, jnp.where(c * ATTN_KC + kloc < cand, 1.0, 0.0), 0.0))
            return jnp.where(below < need, cand, lim)

        lim = lax.fori_loop(0, n_bits, step, jnp.zeros((ATTN_Q, 1), jnp.int32))
        tie_scr[...] = jnp.where(excess > 0.0, lim, TIE_ALL)


def _attn_kernel(iq_ref, ikwq_ref, ikwa_ref, kv_ref, q_ref, wuk_ref, wuv_ref, kvg_ref, go_ref,
                 o_ref, ckv_scr, ik_scr, sc_scr, thr_scr, tie_scr, ql_scr, m_scr, l_scr, acc_scr, y_scr,
                 *, seq, k_sel):
    j = pl.program_id(1)
    nch = j + 1
    nt = (((1,), (1,)), ((), ()))

    @pl.when(j == 0)
    def _():
        ckv_scr[...] = (_rms(kv_ref[0]) * kvg_ref[...]).astype(jnp.bfloat16)
        ik_scr[...] = ikwa_ref[0][:, :IDX_DIM].astype(jnp.bfloat16)

    iw = ikwq_ref[0][:, IDX_DIM:IDX_DIM + IDX_HEADS] * (IDX_HEADS ** -0.5 * IDX_DIM ** -0.5)
    qpos = j * ATTN_Q + lax.broadcasted_iota(jnp.int32, (ATTN_Q, ATTN_KC), 0)
    kloc = lax.broadcasted_iota(jnp.int32, (ATTN_Q, ATTN_KC), 1)

    def score_chunk(c, carry):
        k0 = pl.multiple_of(c * ATTN_KC, ATTN_KC)
        ik_c = ik_scr[pl.ds(k0, ATTN_KC), :]
        score = jnp.zeros((ATTN_Q, ATTN_KC), jnp.float32)
        for h in range(IDX_HEADS):
            s = lax.dot_general(iq_ref[0][:, h * IDX_DIM:(h + 1) * IDX_DIM], ik_c, nt,
                                preferred_element_type=jnp.float32)
            score = score + jnp.maximum(s, 0.0) * iw[:, h:h + 1]
        sc_scr[c] = jnp.where(k0 + kloc <= qpos, score, -jnp.inf)
        return carry

    lax.fori_loop(0, nch, score_chunk, 0)

    for v in range(seq // ATTN_KC):
        @pl.when(j == v)
        def _(v=v):
            thr_scr[...] = _attn_search(v + 1, sc_scr, k_sel)
            _attn_tiebreak(v + 1, sc_scr, thr_scr[...], k_sel, tie_scr)

    for h in range(N_HEADS):
        ql = lax.dot_general(q_ref[0][:, h * HEAD_DIM:(h + 1) * HEAD_DIM], wuk_ref[h], nt,
                             preferred_element_type=jnp.float32)
        ql_scr[h * ATTN_Q:(h + 1) * ATTN_Q, :] = ql.astype(jnp.bfloat16)
    thr_f = thr_scr[...]
    tie = tie_scr[...]
    scale = HEAD_DIM ** -0.5

    def attn_chunk(c, first):
        k0 = pl.multiple_of(c * ATTN_KC, ATTN_KC)
        ckv_c = ckv_scr[pl.ds(k0, ATTN_KC), :]
        sc = sc_scr[c]
        kpos = k0 + kloc
        tied = jnp.where(sc == thr_f, jnp.where(kpos <= tie, 0.0, MASKED), MASKED)
        bias = jnp.where(kpos <= qpos, jnp.where(sc > thr_f, 0.0, tied), MASKED)
        for u in range(N_HEADS * ATTN_Q // ATTN_RB):
            rows = pl.ds(u * ATTN_RB, ATTN_RB)
            qrow = (u * ATTN_RB) % ATTN_Q
            lg = lax.dot_general(ql_scr[rows, :], ckv_c, nt,
                                 preferred_element_type=jnp.float32) * scale + bias[qrow:qrow + ATTN_RB]
            row_max = jnp.max(lg, axis=1, keepdims=True)
            if first:
                m_new = jnp.maximum(jnp.full((ATTN_RB, LANES), MASKED, jnp.float32), row_max)
            else:
                m_old = m_scr[rows, :]
                m_new = jnp.maximum(m_old, row_max)
                alpha = jnp.exp(m_old - m_new)
            p = jnp.exp(lg - _tile_lanes(m_new, ATTN_KC // LANES))
            p_sum = jnp.sum(p, axis=1, keepdims=True)
            pv = jnp.dot(p.astype(jnp.bfloat16), ckv_c, preferred_element_type=jnp.float32)
            if first:
                l_scr[rows, :] = jnp.broadcast_to(p_sum, (ATTN_RB, LANES))
                acc_scr[rows, :] = pv
            else:
                l_scr[rows, :] = alpha * l_scr[rows, :] + p_sum
                acc_scr[rows, :] = _tile_lanes(alpha, KV_RANK // LANES) * acc_scr[rows, :] + pv
            m_scr[rows, :] = m_new

    attn_chunk(0, True)

    def later_chunk(c, carry):
        attn_chunk(c, False)
        return carry

    lax.fori_loop(1, nch, later_chunk, 0)

    o = (acc_scr[...] * _tile_lanes(1.0 / l_scr[...], KV_RANK // LANES)).astype(jnp.bfloat16)
    for h in range(N_HEADS):
        y_scr[:, h * HEAD_DIM:(h + 1) * HEAD_DIM] = jnp.dot(
            o[h * ATTN_Q:(h + 1) * ATTN_Q], wuv_ref[h], preferred_element_type=jnp.float32)
    o_ref[0] = (_rms(y_scr[...]) * go_ref[...]).astype(o_ref.dtype)


def _attn(iq, ikw, kv, q, w_uk_bf, w_uv_bf, kv_norm_g, g_attn_out):
    b, s, _ = iq.shape
    assert ATTN_Q == ATTN_KC and s % ATTN_Q == 0
    k_sel = min(TOPK_MAX, s // 4)
    rows = N_HEADS * ATTN_Q
    kern = functools.partial(_attn_kernel, seq=s, k_sel=k_sel)
    return pl.pallas_call(
        kern,
        grid=(b, s // ATTN_Q),
        in_specs=[pl.BlockSpec((1, ATTN_Q, IDX_HEADS * IDX_DIM), lambda bi, j: (bi, j, 0)),
                  pl.BlockSpec((1, ATTN_Q, IKW_COLS), lambda bi, j: (bi, j, 0)),
                  pl.BlockSpec((1, s, IKW_COLS), lambda bi, j: (bi, 0, 0)),
                  pl.BlockSpec((1, s, KV_RANK), lambda bi, j: (bi, 0, 0)),
                  pl.BlockSpec((1, ATTN_Q, ATTN_DIM), lambda bi, j: (bi, j, 0)),
                  pl.BlockSpec((N_HEADS, KV_RANK, HEAD_DIM), lambda bi, j: (0, 0, 0)),
                  pl.BlockSpec((N_HEADS, KV_RANK, HEAD_DIM), lambda bi, j: (0, 0, 0)),
                  pl.BlockSpec((1, KV_RANK), lambda bi, j: (0, 0)),
                  pl.BlockSpec((1, ATTN_DIM), lambda bi, j: (0, 0))],
        out_specs=pl.BlockSpec((1, ATTN_Q, ATTN_DIM), lambda bi, j: (bi, j, 0)),
        out_shape=jax.ShapeDtypeStruct((b, s, ATTN_DIM), jnp.bfloat16),
        scratch_shapes=[pltpu.VMEM((s, KV_RANK), jnp.bfloat16),
                        pltpu.VMEM((s, IDX_DIM), jnp.bfloat16),
                        pltpu.VMEM((s // ATTN_KC, ATTN_Q, ATTN_KC), jnp.float32),
                        pltpu.VMEM((ATTN_Q, 1), jnp.float32),
                        pltpu.VMEM((ATTN_Q, 1), jnp.int32),
                        pltpu.VMEM((rows, KV_RANK), jnp.bfloat16),
                        pltpu.VMEM((rows, LANES), jnp.float32),
                        pltpu.VMEM((rows, LANES), jnp.float32),
                        pltpu.VMEM((rows, KV_RANK), jnp.float32),
                        pltpu.VMEM((ATTN_Q, ATTN_DIM), jnp.float32)],
        compiler_params=_cparams(("arbitrary", "arbitrary")),
        name="attn",
    )(iq, ikw, ikw, kv, q, w_uk_bf, w_uv_bf, kv_norm_g, g_attn_out)


R_E1, R_E2, R_W1, R_W2, R_RANK1, R_RANK2 = range(6)


def _out_kernel(mc_ref, ma_ref, w_hbm, x_ref, gt_ref, g_ref, sc_ref, sh_ref, wr_ref, br_ref,
                x1_ref, h2_ref, route_ref, cnt_ref, cnt_scr, tri_scr, w_bf, stage, sem):
    @pl.when(pl.program_id(0) == 0)
    def _():
        cnt_scr[...] = jnp.zeros(cnt_scr.shape, jnp.float32)
        earlier = (lax.broadcasted_iota(jnp.int32, tri_scr.shape, 1)
                   < lax.broadcasted_iota(jnp.int32, tri_scr.shape, 0))
        tri_scr[...] = jnp.where(earlier, 1.0, 0.0).astype(tri_scr.dtype)
        rows = stage.shape[0]
        for k in range(w_bf.shape[0] // rows):
            cp = pltpu.make_async_copy(w_hbm.at[pl.ds(k * rows, rows)], stage, sem)
            cp.start()
            cp.wait()
            w_bf[pl.ds(k * rows, rows), :] = stage[...].astype(jnp.bfloat16)

    rs = x_ref.shape[0] // OUT_SUB
    for sub in range(OUT_SUB):
        rows = pl.ds(sub * rs, rs)
        _out_rows(mc_ref[rows, :], ma_ref[rows, :], w_bf, x_ref[rows, :], gt_ref, g_ref, sc_ref, sh_ref,
                  wr_ref, br_ref, x1_ref.at[rows], h2_ref.at[pl.ds(sub * rs * ROW_SUB, rs * ROW_SUB)],
                  route_ref.at[rows], cnt_scr, tri_scr[0:rs, 0:rs])
    cnt_ref[...] = cnt_scr[...]


def _out_rows(mc, ma, w_bf, x, gt_ref, g_ref, sc_ref, sh_ref, wr_ref, br_ref,
              x1_ref, h2_ref, route_ref, cnt_scr, tri):
    mix = (jnp.dot(mc, w_bf[0:CONV_DIM, :], preferred_element_type=jnp.float32)
           + jnp.dot(ma, w_bf[CONV_DIM:, :], preferred_element_type=jnp.float32))
    x1 = x + gt_ref[0] * mix
    x1_ref[...] = x1
    h2 = _rms(x1) * g_ref[...] * (1.0 + sc_ref[0]) + sh_ref[0]
    _rows_from_2d(h2_ref, h2)

    logit = jnp.dot(h2.astype(jnp.bfloat16), wr_ref[...],
                    preferred_element_type=jnp.float32) + br_ref[...]
    lane = lax.broadcasted_iota(jnp.int32, logit.shape, 1).astype(jnp.float32)
    neg = -jnp.inf
    big = float(ROUTE_LANES)
    is_g = lane < N_GROUPS
    gl = jnp.where(is_g, logit, neg)
    gmax = jnp.max(gl, axis=1, keepdims=True)
    p_group = 1.0 / jnp.sum(jnp.exp(gl - gmax), axis=1, keepdims=True)
    g_sel = jnp.min(jnp.where(is_g & (gl == gmax), lane, big), axis=1, keepdims=True)
    lo = N_GROUPS + g_sel * EXPERTS_PER_GROUP
    in_grp = (lane >= lo) & (lane < lo + EXPERTS_PER_GROUP)
    el = jnp.where(in_grp, logit, neg)
    m1 = jnp.max(el, axis=1, keepdims=True)
    i1 = jnp.min(jnp.where(in_grp & (el == m1), lane, big), axis=1, keepdims=True)
    el2 = jnp.where(lane == i1, neg, el)
    m2 = jnp.max(el2, axis=1, keepdims=True)
    i2 = jnp.min(jnp.where(in_grp & (lane != i1) & (el2 == m2), lane, big), axis=1, keepdims=True)
    r = jnp.exp(m2 - m1)
    w1 = p_group / (1.0 + r)
    w2 = p_group * r / (1.0 + r)

    member = jnp.where(lane == i1, 1.0, jnp.where(lane == i2, 1.0, 0.0))
    before = jnp.dot(tri, member.astype(jnp.bfloat16),
                     preferred_element_type=jnp.float32) + cnt_scr[...]
    rank1 = jnp.sum(jnp.where(lane == i1, before, 0.0), axis=1, keepdims=True)
    rank2 = jnp.sum(jnp.where(lane == i2, before, 0.0), axis=1, keepdims=True)
    cnt_scr[...] = cnt_scr[...] + jnp.sum(member, axis=0, keepdims=True)

    route = jnp.zeros_like(logit)
    for idx, val in ((R_E1, i1 - N_GROUPS), (R_E2, i2 - N_GROUPS), (R_W1, w1), (R_W2, w2),
                     (R_RANK1, rank1), (R_RANK2, rank2)):
        route = jnp.where(lane == idx, val, route)
    route_ref[...] = route


def _out(mix_c, mix_a, w_out, x2, gt1, g_ffn, sc2, sh2, w_route, b_route, seq, tm=512):
    t, d = x2.shape
    per_b = seq // tm
    bmap = lambda m: (m // per_b, 0, 0)
    return pl.pallas_call(
        _out_kernel,
        grid=(t // tm,),
        in_specs=[pl.BlockSpec((tm, CONV_DIM), lambda m: (m, 0)),
                  pl.BlockSpec((tm, ATTN_DIM), lambda m: (m, 0)),
                  pl.BlockSpec(memory_space=pl.ANY),
                  pl.BlockSpec((tm, d), lambda m: (m, 0)),
                  pl.BlockSpec((1, 1, d), bmap),
                  pl.BlockSpec((1, d), lambda m: (0, 0)),
                  pl.BlockSpec((1, 1, d), bmap),
                  pl.BlockSpec((1, 1, d), bmap),
                  pl.BlockSpec((d, ROUTE_LANES), lambda m: (0, 0)),
                  pl.BlockSpec((1, ROUTE_LANES), lambda m: (0, 0))],
        out_specs=[pl.BlockSpec((tm, d), lambda m: (m, 0)),
                   pl.BlockSpec((tm * ROW_SUB, LANES), lambda m: (m, 0)),
                   pl.BlockSpec((tm, ROUTE_LANES), lambda m: (m, 0)),
                   pl.BlockSpec((1, ROUTE_LANES), lambda m: (0, 0))],
        out_shape=[jax.ShapeDtypeStruct((t, d), jnp.float32),
                   jax.ShapeDtypeStruct((t * ROW_SUB, LANES), jnp.float32),
                   jax.ShapeDtypeStruct((t, ROUTE_LANES), jnp.float32),
                   jax.ShapeDtypeStruct((1, ROUTE_LANES), jnp.float32)],
        scratch_shapes=[pltpu.VMEM((1, ROUTE_LANES), jnp.float32),
                        pltpu.VMEM((tm, tm), jnp.bfloat16),
                        pltpu.VMEM(w_out.shape, jnp.bfloat16),
                        pltpu.VMEM((STAGE_ROWS, d), jnp.float32),
                        pltpu.SemaphoreType.DMA(())],
        compiler_params=_cparams(("arbitrary",)),
        name="out",
    )(mix_c, mix_a, w_out, x2, gt1, g_ffn, sc2, sh2, w_route, b_route)


def _slot(off_ref, e_ref, r_ref, tok):
    return off_ref[e_ref[tok]] + r_ref[tok]


def _dispatch_kernel(e1_ref, e2_ref, r1_ref, r2_ref, off_ref, zs_ref, zc_ref, nu_ref, h2_ref, xs_ref,
                     zbuf, sem_z, sem_r, *, tm, n_tiles):
    i = pl.program_id(0)

    @pl.when(i == 0)
    def _():
        zbuf[...] = jnp.zeros(zbuf.shape, zbuf.dtype)

        def zero_copy(start):
            rows = pl.ds(pl.multiple_of(start * ROW_SUB, ZERO_ROWS * ROW_SUB), ZERO_ROWS * ROW_SUB)
            return pltpu.make_async_copy(zbuf, xs_ref.at[rows], sem_z)

        def zero_fill(first_row, n_granules):
            def start(k, carry):
                zero_copy(first_row + k * ZERO_ROWS).start()
                return carry

            def wait(k, carry):
                zero_copy(first_row + k * ZERO_ROWS).wait()
                return carry

            lax.fori_loop(0, n_granules, start, 0)
            lax.fori_loop(0, n_granules, wait, 0)

        for e in range(N_EXPERTS):
            zero_fill(zs_ref[e], zc_ref[e])
        zero_fill(nu_ref[0] * MOE_TM, (n_tiles - nu_ref[0]) * (MOE_TM // ZERO_ROWS))

    def issue_row(r, carry):
        tok = i * tm + r
        for e_ref, r_ref in ((e1_ref, r1_ref), (e2_ref, r2_ref)):
            pltpu.make_async_copy(_row(h2_ref, r), _row(xs_ref, _slot(off_ref, e_ref, r_ref, tok)),
                                  sem_r).start()
        return carry

    lax.fori_loop(0, tm, issue_row, 0)
    for _ in range(2):
        pltpu.make_async_copy(h2_ref, xs_ref.at[pl.ds(0, tm * ROW_SUB)], sem_r).wait()


def _dispatch(plan, h2, n_rows, tm=256):
    t = h2.shape[0] // ROW_SUB
    kern = functools.partial(_dispatch_kernel, tm=tm, n_tiles=n_rows // MOE_TM)
    return pl.pallas_call(
        kern,
        grid_spec=pltpu.PrefetchScalarGridSpec(
            num_scalar_prefetch=len(plan),
            grid=(t // tm,),
            in_specs=[pl.BlockSpec((tm * ROW_SUB, LANES), lambda i, *_: (i, 0))],
            out_specs=pl.BlockSpec(memory_space=pl.ANY),
            scratch_shapes=[pltpu.VMEM((ZERO_ROWS * ROW_SUB, LANES), h2.dtype),
                            pltpu.SemaphoreType.DMA(()),
                            pltpu.SemaphoreType.DMA(())]),
        out_shape=jax.ShapeDtypeStruct((n_rows * ROW_SUB, LANES), h2.dtype),
        compiler_params=_cparams(("arbitrary",)),
        name="dispatch",
    )(*plan, h2)


def _mlp_kernel(te_ref, nu_ref, first_ref, nxt_ref, slot_ref, xs_ref, wg_hbm, wu_hbm, wd_hbm, ys_ref,
                wg_f32, wu_f32, wd_f32, wg_bf, wu_bf, wd_bf, sem):
    i = pl.program_id(0)

    def copies(e, s):
        return [pltpu.make_async_copy(hbm.at[e], buf.at[s], sem.at[s])
                for hbm, buf in ((wg_hbm, wg_f32), (wu_hbm, wu_f32), (wd_hbm, wd_f32))]

    @pl.when(i == 0)
    def _():
        for cp in copies(te_ref[0], 0):
            cp.start()

    @pl.when((i < nu_ref[0]) & (first_ref[i] == 1))
    def _():
        s = slot_ref[i]
        for cp in copies(te_ref[i], s):
            cp.wait()
        wg_bf[...] = wg_f32[s].astype(jnp.bfloat16)
        wu_bf[...] = wu_f32[s].astype(jnp.bfloat16)
        wd_bf[...] = wd_f32[s].astype(jnp.bfloat16)

        @pl.when(nxt_ref[i] >= 0)
        def _():
            for cp in copies(nxt_ref[i], 1 - s):
                cp.start()

    @pl.when(i < nu_ref[0])
    def _():
        x = _rows_to_2d(xs_ref, MOE_TM).astype(jnp.bfloat16)
        a = jnp.dot(x, wg_bf[...], preferred_element_type=jnp.float32)
        u = jnp.dot(x, wu_bf[...], preferred_element_type=jnp.float32)
        hid = (a * jax.nn.sigmoid(a)) * u
        _rows_from_2d(ys_ref, jnp.dot(hid.astype(jnp.bfloat16), wd_bf[...],
                                      preferred_element_type=jnp.float32))

    @pl.when(i >= nu_ref[0])
    def _():
        ys_ref[...] = jnp.zeros(ys_ref.shape, ys_ref.dtype)


def _mlp(tiles, xs, w_gate, w_up, w_down):
    n_rows = xs.shape[0] // ROW_SUB
    _, d, f = w_gate.shape
    used = lambda i, te, nu, *_: (jnp.minimum(i, nu[0] - 1), 0)
    return pl.pallas_call(
        _mlp_kernel,
        grid_spec=pltpu.PrefetchScalarGridSpec(
            num_scalar_prefetch=len(tiles),
            grid=(n_rows // MOE_TM,),
            in_specs=[pl.BlockSpec((MOE_TM * ROW_SUB, LANES), used),
                      pl.BlockSpec(memory_space=pl.ANY),
                      pl.BlockSpec(memory_space=pl.ANY),
                      pl.BlockSpec(memory_space=pl.ANY)],
            out_specs=pl.BlockSpec((MOE_TM * ROW_SUB, LANES), lambda i, *_: (i, 0)),
            scratch_shapes=[pltpu.VMEM((2, d, f), jnp.float32),
                            pltpu.VMEM((2, d, f), jnp.float32),
                            pltpu.VMEM((2, f, d), jnp.float32),
                            pltpu.VMEM((d, f), jnp.bfloat16),
                            pltpu.VMEM((d, f), jnp.bfloat16),
                            pltpu.VMEM((f, d), jnp.bfloat16),
                            pltpu.SemaphoreType.DMA((2,))]),
        out_shape=jax.ShapeDtypeStruct((n_rows * ROW_SUB, LANES), jnp.float32),
        compiler_params=_cparams(("arbitrary",)),
        name="mlp",
    )(*tiles, xs, w_gate, w_up, w_down)


def _combine_kernel(e1_ref, e2_ref, r1_ref, r2_ref, off_ref, ys_ref, x1_ref, route_ref, gt_ref, g_ref,
                    sc_ref, sh_ref, o_ref, a0, b0, a1, b1, sem, *, tm, n_tiles):
    i = pl.program_id(0)
    bufs = ((a0, b0), (a1, b1))

    def issue(tile, sl):
        def issue_row(r, carry):
            tok = tile * tm + r
            for buf, e_ref, r_ref in ((bufs[sl][0], e1_ref, r1_ref), (bufs[sl][1], e2_ref, r2_ref)):
                pltpu.make_async_copy(_row(ys_ref, _slot(off_ref, e_ref, r_ref, tok)),
                                      _row(buf, r), sem.at[sl]).start()
            return carry

        lax.fori_loop(0, tm, issue_row, 0)

    @pl.when(i == 0)
    def _():
        issue(0, 0)

    for sl in range(2):
        @pl.when(i % 2 == sl)
        def _(sl=sl):
            @pl.when(i + 1 < n_tiles)
            def _():
                issue(i + 1, 1 - sl)

            for buf in bufs[sl]:
                pltpu.make_async_copy(ys_ref.at[pl.ds(0, tm * ROW_SUB)], buf, sem.at[sl]).wait()

            route = route_ref[...]
            y = (route[:, R_W1:R_W1 + 1] * _rows_to_2d(bufs[sl][0], tm)
                 + route[:, R_W2:R_W2 + 1] * _rows_to_2d(bufs[sl][1], tm))
            x = x1_ref[...] + gt_ref[0] * y
            o_ref[...] = _rms(x) * g_ref[...] * (1.0 + sc_ref[0]) + sh_ref[0]


def _combine(plan, ys, x1, route, gt2, g_final, scf, shf, seq, tm=256):
    t, d = x1.shape
    per_b = seq // tm
    n_tiles = t // tm
    bmap = lambda m, *_: (m // per_b, 0, 0)
    kern = functools.partial(_combine_kernel, tm=tm, n_tiles=n_tiles)
    return pl.pallas_call(
        kern,
        grid_spec=pltpu.PrefetchScalarGridSpec(
            num_scalar_prefetch=len(plan),
            grid=(n_tiles,),
            in_specs=[pl.BlockSpec(memory_space=pl.ANY),
                      pl.BlockSpec((tm, d), lambda m, *_: (m, 0)),
                      pl.BlockSpec((tm, ROUTE_LANES), lambda m, *_: (m, 0)),
                      pl.BlockSpec((1, 1, d), bmap),
                      pl.BlockSpec((1, d), lambda m, *_: (0, 0)),
                      pl.BlockSpec((1, 1, d), bmap),
                      pl.BlockSpec((1, 1, d), bmap)],
            out_specs=pl.BlockSpec((tm, d), lambda m, *_: (m, 0)),
            scratch_shapes=[pltpu.VMEM((tm * ROW_SUB, LANES), jnp.float32) for _ in range(4)]
            + [pltpu.SemaphoreType.DMA((2,))]),
        out_shape=jax.ShapeDtypeStruct((t, d), jnp.float32),
        compiler_params=_cparams(("arbitrary",)),
        name="combine",
    )(*plan, ys, x1, route, gt2, g_final, scf, shf)


def _moe_plan(route, counts, t):
    i32 = jnp.int32
    cnt = counts[0, N_GROUPS:N_GROUPS + N_EXPERTS].astype(i32)
    padded = (cnt + MOE_TM - 1) // MOE_TM * MOE_TM
    off_end = jnp.cumsum(padded)
    off = off_end - padded
    n_tiles = (2 * t + N_EXPERTS * MOE_TM) // MOE_TM
    n_used = off_end[-1] // MOE_TM
    tile_start = jnp.arange(n_tiles, dtype=i32) * MOE_TM
    te = jnp.sum((off_end[None, :] <= tile_start[:, None]).astype(i32), axis=1)
    te = jnp.minimum(te, N_EXPERTS - 1)
    tile = jnp.arange(n_tiles, dtype=i32)
    te = jnp.where(tile < n_used, te, te[n_used - 1])
    first = jnp.concatenate([jnp.ones((1,), i32), (te[1:] != te[:-1]).astype(i32)])
    nxt_tile = off_end[te] // MOE_TM
    nxt = jnp.where(nxt_tile < n_used, te[jnp.minimum(nxt_tile, n_tiles - 1)], -1)
    slot = (jnp.cumsum(first) - 1) % 2
    assign = tuple(route[:, k].astype(i32) for k in (R_E1, R_E2, R_RANK1, R_RANK2))
    zstart = (off + cnt) // ZERO_ROWS * ZERO_ROWS
    zcount = (off_end - zstart) // ZERO_ROWS
    n_used = n_used.reshape(1)
    return assign, off, (zstart, zcount), (te, n_used, first, nxt, slot), n_tiles * MOE_TM


def kernel(x, c, w_ada, b_ada, g_mix, w_in, conv_w, w_uk, kv_norm_g, w_uv, g_conv_out, g_attn_out,
           w_out, g_ffn, w_rg, b_rg, w_re, b_re, w_gate, w_up, w_down, w_ada_f, b_ada_f, g_final):
    b, s, d = x.shape
    assert w_ada.shape[0] == 1, "single layer"
    bf = jnp.bfloat16
    x2 = x.reshape(b * s, d)

    c_pad = jnp.zeros((SUBLANES, d), jnp.float32).at[:b].set(c)
    mod = _mod(c_pad, w_ada[0], b_ada[0])[:b]
    modf = _mod(c_pad, w_ada_f, b_ada_f)[:b]
    vec = lambda a, i: a[:, i * d:(i + 1) * d].reshape(b, 1, d)
    sh1, sc1, gt1, sh2, sc2, gt2 = (vec(mod, i) for i in range(N_MOD))
    shf, scf = vec(modf, 0), vec(modf, 1)
    row = lambda a: a.reshape(1, -1)

    w_in_t = jnp.swapaxes(w_in[0], 0, 1)
    h, kv = _hnorm(x2, row(g_mix[0]), sc1, sh1, w_in_t, OFF_KV, KV_RANK, s, jnp.float32)
    w_tail = jnp.pad(w_in_t[OFF_IK:], ((0, IKW_COLS - (IN_COLS - OFF_IK)), (0, 0)))
    mix_c = _projconv(h, w_in_t, conv_w[0], row(g_conv_out[0]), s)
    q = _colmm(h, w_in_t, OFF_Q, ATTN_DIM, 1024, bf, "proj_q")
    iq = _colmm(h, w_in_t, OFF_IQ, IDX_HEADS * IDX_DIM, 512, bf, "proj_iq")
    ikw = _colmm(h, w_tail, 0, IKW_COLS, IKW_COLS, jnp.float32, "proj_ikw")

    mix_a = _attn(iq.reshape(b, s, -1), ikw.reshape(b, s, -1), kv.reshape(b, s, -1),
                  q.reshape(b, s, -1), w_uk[0].astype(bf), w_uv[0].astype(bf),
                  row(kv_norm_g[0]), row(g_attn_out[0]))

    w_route = jnp.zeros((d, ROUTE_LANES), bf).at[:, :N_GROUPS].set(w_rg[0].astype(bf))
    w_route = w_route.at[:, N_GROUPS:N_GROUPS + N_EXPERTS].set(w_re[0].astype(bf))
    b_route = jnp.zeros((1, ROUTE_LANES), jnp.float32).at[0, :N_GROUPS].set(b_rg[0])
    b_route = b_route.at[0, N_GROUPS:N_GROUPS + N_EXPERTS].set(b_re[0])
    x1, h2, route, counts = _out(mix_c, mix_a.reshape(b * s, -1),
                                 w_out[0], x2, gt1, row(g_ffn[0]),
                                 sc2, sh2, w_route, b_route, s)

    assign, off, zero, tiles, n_rows = _moe_plan(route, counts, b * s)
    xs = _dispatch((*assign, off, *zero, tiles[1]), h2, n_rows)
    ys = _mlp(tiles, xs, w_gate[0], w_up[0], w_down[0])
    out = _combine((*assign, off), ys, x1, route, gt2, row(g_final), scf, shf, s)
    return out.reshape(b, s, d)
```

```python
import functools

import jax
import jax.numpy as jnp
from jax import lax
from jax.experimental import pallas as pl
from jax.experimental.pallas import tpu as pltpu

D_MODEL = 2048
CONV_DIM = 1024
CONV_WIDTH = 3
N_HEADS = 8
HEAD_DIM = 128
ATTN_DIM = N_HEADS * HEAD_DIM
KV_RANK = 512
IDX_HEADS = 16
IDX_DIM = 128
TOPK_MAX = 256
N_GROUPS = 4
EXPERTS_PER_GROUP = 8
N_EXPERTS = N_GROUPS * EXPERTS_PER_GROUP
EXPERT_FF = 512
N_MOD = 6
EPS = 1e-6

OFF_Q = 3 * CONV_DIM
OFF_KV = OFF_Q + ATTN_DIM
OFF_IQ = OFF_KV + KV_RANK
OFF_IK = OFF_IQ + IDX_HEADS * IDX_DIM
OFF_IW = OFF_IK + IDX_DIM
IN_COLS = OFF_IW + IDX_HEADS

LANES = 128
SUBLANES = 8
VMEM_LIMIT = 56 * 1024 * 1024

IKW_COLS = 256
ATTN_Q = 256
ATTN_KC = 256
ATTN_RB = 256
MASKED = -1e30
TIE_ALL = 2 ** 30
ROUTE_LANES = 128
STAGE_ROWS = 512
OUT_SUB = 2
MOE_TM = 256
ZERO_ROWS = 64
ROW_SUB = D_MODEL // LANES
INT_MIN = -2 ** 31
KEY_NEG_INF = INT_MIN + 0x7FFFFF


def _cparams(sem):
    return pltpu.CompilerParams(dimension_semantics=sem, vmem_limit_bytes=VMEM_LIMIT)


def _rms(v, axis=-1):
    return v * lax.rsqrt(jnp.mean(v * v, axis=axis, keepdims=True) + EPS)


def _tile_lanes(v, n):
    return jnp.concatenate([v] * n, axis=1)


def _rows_to_2d(ref, n):
    return jnp.concatenate([ref[pl.ds(c, n, stride=ROW_SUB), :] for c in range(ROW_SUB)], axis=1)


def _rows_from_2d(ref, val):
    n = val.shape[0]
    for c in range(ROW_SUB):
        ref[pl.ds(c, n, stride=ROW_SUB), :] = val[:, c * LANES:(c + 1) * LANES]


def _row(ref, i):
    return ref.at[pl.ds(pl.multiple_of(i * ROW_SUB, ROW_SUB), ROW_SUB)]


def _mod_kernel(c_ref, w_ref, b_ref, o_ref):
    c = c_ref[...]
    ca = (c * jax.nn.sigmoid(c)).astype(jnp.bfloat16)
    o_ref[...] = jnp.dot(ca, w_ref[...].astype(jnp.bfloat16),
                         preferred_element_type=jnp.float32) + b_ref[...]


def _mod(c_pad, w, b, tn=1024):
    d, n = w.shape
    return pl.pallas_call(
        _mod_kernel,
        grid=(n // tn,),
        in_specs=[pl.BlockSpec((c_pad.shape[0], d), lambda j: (0, 0)),
                  pl.BlockSpec((d, tn), lambda j: (0, j)),
                  pl.BlockSpec((1, tn), lambda j: (0, j))],
        out_specs=pl.BlockSpec((c_pad.shape[0], tn), lambda j: (0, j)),
        out_shape=jax.ShapeDtypeStruct((c_pad.shape[0], n), jnp.float32),
        compiler_params=_cparams(("arbitrary",)),
        name="mod",
    )(c_pad, w, b.reshape(1, n))


ATT_COLS = IN_COLS - OFF_Q
ATT_PAD = OFF_IK - OFF_Q + IKW_COLS


def _hproj_kernel(x_ref, g_ref, sc_ref, sh_ref, wt_hbm, h_ref, q_ref, kv_ref, iq_ref, ikw_ref,
                  w_bf, stage, sem):
    @pl.when(pl.program_id(0) == 0)
    def _():
        def load(src_row, n_rows, dst_row):
            cp = pltpu.make_async_copy(wt_hbm.at[pl.ds(src_row, n_rows)],
                                       stage.at[pl.ds(0, n_rows)], sem)
            cp.start()
            cp.wait()
            w_bf[pl.ds(dst_row, n_rows), :] = stage[pl.ds(0, n_rows), :].astype(jnp.bfloat16)

        full = ATT_COLS // STAGE_ROWS * STAGE_ROWS
        for k in range(ATT_COLS // STAGE_ROWS):
            load(OFF_Q + k * STAGE_ROWS, STAGE_ROWS, k * STAGE_ROWS)
        load(OFF_Q + full, ATT_COLS - full, full)
        w_bf[pl.ds(ATT_COLS, ATT_PAD - ATT_COLS), :] = jnp.zeros((ATT_PAD - ATT_COLS, w_bf.shape[1]),
                                                                 jnp.bfloat16)

    h = (_rms(x_ref[...]) * g_ref[...] * (1.0 + sc_ref[0]) + sh_ref[0]).astype(h_ref.dtype)
    h_ref[...] = h
    nt = (((1,), (1,)), ((), ()))
    row0 = 0
    for o_ref in (q_ref, kv_ref, iq_ref, ikw_ref):
        n = o_ref.shape[1]
        o_ref[...] = lax.dot_general(h, w_bf[row0:row0 + n, :], nt,
                                     preferred_element_type=jnp.float32).astype(o_ref.dtype)
        row0 += n


def _hproj(x2, g_mix, sc1, sh1, wt, seq, tm=512):
    t, d = x2.shape
    per_b = seq // tm
    bmap = lambda m: (m // per_b, 0, 0)
    widths = (ATTN_DIM, KV_RANK, IDX_HEADS * IDX_DIM, IKW_COLS)
    dtypes = (jnp.bfloat16, jnp.float32, jnp.bfloat16, jnp.float32)
    assert sum(widths) == ATT_PAD and (ATT_COLS % STAGE_ROWS) % SUBLANES == 0
    return pl.pallas_call(
        _hproj_kernel,
        grid=(t // tm,),
        in_specs=[pl.BlockSpec((tm, d), lambda m: (m, 0)),
                  pl.BlockSpec((1, d), lambda m: (0, 0)),
                  pl.BlockSpec((1, 1, d), bmap),
                  pl.BlockSpec((1, 1, d), bmap),
                  pl.BlockSpec(memory_space=pl.ANY)],
        out_specs=[pl.BlockSpec((tm, d), lambda m: (m, 0))]
        + [pl.BlockSpec((tm, w), lambda m: (m, 0)) for w in widths],
        out_shape=[jax.ShapeDtypeStruct((t, d), jnp.bfloat16)]
        + [jax.ShapeDtypeStruct((t, w), dt) for w, dt in zip(widths, dtypes)],
        scratch_shapes=[pltpu.VMEM((ATT_PAD, d), jnp.bfloat16),
                        pltpu.VMEM((STAGE_ROWS, d), jnp.float32),
                        pltpu.SemaphoreType.DMA(())],
        compiler_params=_cparams(("arbitrary",)),
        name="hproj",
    )(x2, g_mix, sc1, sh1, wt)


def _projconv_kernel(h_ref, wt_hbm, cw_ref, g_ref, o_ref, w_bf, stage, halo_scr, sem, *, per_b):
    m = pl.program_id(0)

    @pl.when(m == 0)
    def _():
        rows = stage.shape[0]
        for k in range(w_bf.shape[0] // rows):
            cp = pltpu.make_async_copy(wt_hbm.at[pl.ds(k * rows, rows)], stage, sem)
            cp.start()
            cp.wait()
            w_bf[pl.ds(k * rows, rows), :] = stage[...].astype(jnp.bfloat16)

    nt = (((1,), (1,)), ((), ()))
    h = h_ref[...]
    bg, cg, xv = (lax.dot_general(h, w_bf[k * CONV_DIM:(k + 1) * CONV_DIM, :], nt,
                                  preferred_element_type=jnp.float32) for k in range(3))
    u = cg * xv
    halo = jnp.where(m % per_b == 0, 0.0, halo_scr[...])
    halo_scr[...] = u[u.shape[0] - SUBLANES:, :]
    row = lax.broadcasted_iota(jnp.int32, u.shape, 0)
    h1 = halo[SUBLANES - 1:SUBLANES, :]
    h2 = halo[SUBLANES - 2:SUBLANES - 1, :]
    u1 = jnp.where(row == 0, h1, pltpu.roll(u, 1, 0))
    u2 = jnp.where(row == 0, h2, jnp.where(row == 1, h1, pltpu.roll(u, 2, 0)))
    w = cw_ref[...]
    y = bg * (w[0:1, :] * u2 + w[1:2, :] * u1 + w[2:3, :] * u)
    o_ref[...] = (_rms(y) * g_ref[...]).astype(o_ref.dtype)


def _projconv(h, wt, conv_w, g_conv_out, seq, tm=1024):
    t, d = h.shape
    c = CONV_DIM
    kern = functools.partial(_projconv_kernel, per_b=seq // tm)
    return pl.pallas_call(
        kern,
        grid=(t // tm,),
        in_specs=[pl.BlockSpec((tm, d), lambda m: (m, 0)),
                  pl.BlockSpec(memory_space=pl.ANY),
                  pl.BlockSpec((CONV_WIDTH, c), lambda m: (0, 0)),
                  pl.BlockSpec((1, c), lambda m: (0, 0))],
        out_specs=pl.BlockSpec((tm, c), lambda m: (m, 0)),
        out_shape=jax.ShapeDtypeStruct((t, c), jnp.bfloat16),
        scratch_shapes=[pltpu.VMEM((3 * c, d), jnp.bfloat16),
                        pltpu.VMEM((STAGE_ROWS, d), jnp.float32),
                        pltpu.VMEM((SUBLANES, c), jnp.float32),
                        pltpu.SemaphoreType.DMA(())],
        compiler_params=_cparams(("arbitrary",)),
        name="proj_conv",
    )(h, wt, conv_w, g_conv_out)


def _ordered_bits(v):
    return v ^ ((v >> 31) & jnp.int32(0x7FFFFFFF))


def _attn_search(nch, sc_scr, k_sel):
    n_keys = float(nch * ATTN_KC)
    half = ATTN_Q // 2

    def half_step(bit, thr, rows):
        cand = thr + bit
        cand_f = lax.bitcast_convert_type(_ordered_bits(cand), jnp.float32)
        hits = jnp.where(sc_scr[0, rows, :] >= cand_f, 1.0, 0.0)
        for c in range(1, nch):
            hits = hits + jnp.where(sc_scr[c, rows, :] >= cand_f, 1.0, 0.0)
        cnt = jnp.sum(hits, axis=1, keepdims=True)
        cnt = jnp.where(cand < KEY_NEG_INF, n_keys, cnt)
        return jnp.where(cnt >= k_sel, cand, thr)

    def bit_step(i, thrs):
        bit = lax.shift_left(jnp.int32(1), 31 - i)
        return tuple(half_step(bit, thr, pl.ds(k * half, half)) for k, thr in enumerate(thrs))

    init = jnp.full((half, 1), INT_MIN, jnp.int32)
    thrs = lax.fori_loop(0, 32, bit_step, (init, init), unroll=2)
    thr = jnp.concatenate(thrs, axis=0)
    return lax.bitcast_convert_type(_ordered_bits(thr), jnp.float32)


def _attn_tiebreak(nch, sc_scr, thr_f, k_sel, tie_scr):
    kloc = lax.broadcasted_iota(jnp.int32, (ATTN_Q, ATTN_KC), 1)

    def count(pred):
        hits = pred(sc_scr[0], 0)
        for c in range(1, nch):
            hits = hits + pred(sc_scr[c], c)
        return jnp.sum(hits, axis=1, keepdims=True)

    tie_scr[...] = jnp.full(tie_scr.shape, TIE_ALL, jnp.int32)
    n_ge = count(lambda sc, c: jnp.where(sc >= thr_f, 1.0, 0.0))
    excess = jnp.where(thr_f > -jnp.inf, n_ge - k_sel, 0.0)

    @pl.when(jnp.max(excess) > 0.0)
    def _():
        need = k_sel - count(lambda sc, c: jnp.where(sc > thr_f, 1.0, 0.0))
        n_bits = (nch * ATTN_KC - 1).bit_length()

        def step(i, lim):
            cand = lim + lax.shift_left(jnp.int32(1), n_bits - 1 - i)
            below = count(lambda sc, c: jnp.where(
                sc == thr_f, jnp.where(c * ATTN_KC + kloc < cand, 1.0, 0.0), 0.0))
            return jnp.where(below < need, cand, lim)

        lim = lax.fori_loop(0, n_bits, step, jnp.zeros((ATTN_Q, 1), jnp.int32))
        tie_scr[...] = jnp.where(excess > 0.0, lim, TIE_ALL)


def _attn_kernel(iq_ref, ikwq_ref, ikwa_ref, kv_ref, q_ref, wuk_ref, wuv_ref, kvg_ref, go_ref,
                 o_ref, ckv_scr, ik_scr, sc_scr, thr_scr, tie_scr, ql_scr, m_scr, l_scr, acc_scr, y_scr,
                 *, seq, k_sel):
    j = pl.program_id(1)
    nch = j + 1
    nt = (((1,), (1,)), ((), ()))

    @pl.when(j == 0)
    def _():
        ckv_scr[...] = (_rms(kv_ref[0]) * kvg_ref[...]).astype(jnp.bfloat16)
        ik_scr[...] = ikwa_ref[0][:, :IDX_DIM].astype(jnp.bfloat16)

    iw = ikwq_ref[0][:, IDX_DIM:IDX_DIM + IDX_HEADS] * (IDX_HEADS ** -0.5 * IDX_DIM ** -0.5)
    qpos = j * ATTN_Q + lax.broadcasted_iota(jnp.int32, (ATTN_Q, ATTN_KC), 0)
    kloc = lax.broadcasted_iota(jnp.int32, (ATTN_Q, ATTN_KC), 1)

    def score_chunk(c, carry):
        k0 = pl.multiple_of(c * ATTN_KC, ATTN_KC)
        ik_c = ik_scr[pl.ds(k0, ATTN_KC), :]
        score = jnp.zeros((ATTN_Q, ATTN_KC), jnp.float32)
        for h in range(IDX_HEADS):
            s = lax.dot_general(iq_ref[0][:, h * IDX_DIM:(h + 1) * IDX_DIM], ik_c, nt,
                                preferred_element_type=jnp.float32)
            score = score + jnp.maximum(s, 0.0) * iw[:, h:h + 1]
        sc_scr[c] = jnp.where(k0 + kloc <= qpos, score, -jnp.inf)
        return carry

    lax.fori_loop(0, nch, score_chunk, 0)

    for v in range(seq // ATTN_KC):
        @pl.when(j == v)
        def _(v=v):
            thr_scr[...] = _attn_search(v + 1, sc_scr, k_sel)
            _attn_tiebreak(v + 1, sc_scr, thr_scr[...], k_sel, tie_scr)

    for h in range(N_HEADS):
        ql = lax.dot_general(q_ref[0][:, h * HEAD_DIM:(h + 1) * HEAD_DIM], wuk_ref[h], nt,
                             preferred_element_type=jnp.float32)
        ql_scr[h * ATTN_Q:(h + 1) * ATTN_Q, :] = ql.astype(jnp.bfloat16)
    thr_f = thr_scr[...]
    tie = tie_scr[...]
    scale = HEAD_DIM ** -0.5

    def attn_chunk(c, first):
        k0 = pl.multiple_of(c * ATTN_KC, ATTN_KC)
        ckv_c = ckv_scr[pl.ds(k0, ATTN_KC), :]
        sc = sc_scr[c]
        kpos = k0 + kloc
        tied = jnp.where(sc == thr_f, jnp.where(kpos <= tie, 0.0, MASKED), MASKED)
        bias = jnp.where(kpos <= qpos, jnp.where(sc > thr_f, 0.0, tied), MASKED)
        for u in range(N_HEADS * ATTN_Q // ATTN_RB):
            rows = pl.ds(u * ATTN_RB, ATTN_RB)
            qrow = (u * ATTN_RB) % ATTN_Q
            lg = lax.dot_general(ql_scr[rows, :], ckv_c, nt,
                                 preferred_element_type=jnp.float32) * scale + bias[qrow:qrow + ATTN_RB]
            row_max = jnp.max(lg, axis=1, keepdims=True)
            if first:
                m_new = jnp.maximum(jnp.full((ATTN_RB, LANES), MASKED, jnp.float32), row_max)
            else:
                m_old = m_scr[rows, :]
                m_new = jnp.maximum(m_old, row_max)
                alpha = jnp.exp(m_old - m_new)
            p = jnp.exp(lg - _tile_lanes(m_new, ATTN_KC // LANES))
            p_sum = jnp.sum(p, axis=1, keepdims=True)
            pv = jnp.dot(p.astype(jnp.bfloat16), ckv_c, preferred_element_type=jnp.float32)
            if first:
                l_scr[rows, :] = jnp.broadcast_to(p_sum, (ATTN_RB, LANES))
                acc_scr[rows, :] = pv
            else:
                l_scr[rows, :] = alpha * l_scr[rows, :] + p_sum
                acc_scr[rows, :] = _tile_lanes(alpha, KV_RANK // LANES) * acc_scr[rows, :] + pv
            m_scr[rows, :] = m_new

    attn_chunk(0, True)

    def later_chunk(c, carry):
        attn_chunk(c, False)
        return carry

    lax.fori_loop(1, nch, later_chunk, 0)

    o = (acc_scr[...] * _tile_lanes(1.0 / l_scr[...], KV_RANK // LANES)).astype(jnp.bfloat16)
    for h in range(N_HEADS):
        y_scr[:, h * HEAD_DIM:(h + 1) * HEAD_DIM] = jnp.dot(
            o[h * ATTN_Q:(h + 1) * ATTN_Q], wuv_ref[h], preferred_element_type=jnp.float32)
    o_ref[0] = (_rms(y_scr[...]) * go_ref[...]).astype(o_ref.dtype)


def _attn(iq, ikw, kv, q, w_uk_bf, w_uv_bf, kv_norm_g, g_attn_out):
    b, s, _ = iq.shape
    assert ATTN_Q == ATTN_KC and s % ATTN_Q == 0
    k_sel = min(TOPK_MAX, s // 4)
    rows = N_HEADS * ATTN_Q
    kern = functools.partial(_attn_kernel, seq=s, k_sel=k_sel)
    return pl.pallas_call(
        kern,
        grid=(b, s // ATTN_Q),
        in_specs=[pl.BlockSpec((1, ATTN_Q, IDX_HEADS * IDX_DIM), lambda bi, j: (bi, j, 0)),
                  pl.BlockSpec((1, ATTN_Q, IKW_COLS), lambda bi, j: (bi, j, 0)),
                  pl.BlockSpec((1, s, IKW_COLS), lambda bi, j: (bi, 0, 0)),
                  pl.BlockSpec((1, s, KV_RANK), lambda bi, j: (bi, 0, 0)),
                  pl.BlockSpec((1, ATTN_Q, ATTN_DIM), lambda bi, j: (bi, j, 0)),
                  pl.BlockSpec((N_HEADS, KV_RANK, HEAD_DIM), lambda bi, j: (0, 0, 0)),
                  pl.BlockSpec((N_HEADS, KV_RANK, HEAD_DIM), lambda bi, j: (0, 0, 0)),
                  pl.BlockSpec((1, KV_RANK), lambda bi, j: (0, 0)),
                  pl.BlockSpec((1, ATTN_DIM), lambda bi, j: (0, 0))],
        out_specs=pl.BlockSpec((1, ATTN_Q, ATTN_DIM), lambda bi, j: (bi, j, 0)),
        out_shape=jax.ShapeDtypeStruct((b, s, ATTN_DIM), jnp.bfloat16),
        scratch_shapes=[pltpu.VMEM((s, KV_RANK), jnp.bfloat16),
                        pltpu.VMEM((s, IDX_DIM), jnp.bfloat16),
                        pltpu.VMEM((s // ATTN_KC, ATTN_Q, ATTN_KC), jnp.float32),
                        pltpu.VMEM((ATTN_Q, 1), jnp.float32),
                        pltpu.VMEM((ATTN_Q, 1), jnp.int32),
                        pltpu.VMEM((rows, KV_RANK), jnp.bfloat16),
                        pltpu.VMEM((rows, LANES), jnp.float32),
                        pltpu.VMEM((rows, LANES), jnp.float32),
                        pltpu.VMEM((rows, KV_RANK), jnp.float32),
                        pltpu.VMEM((ATTN_Q, ATTN_DIM), jnp.float32)],
        compiler_params=_cparams(("arbitrary", "arbitrary")),
        name="attn",
    )(iq, ikw, ikw, kv, q, w_uk_bf, w_uv_bf, kv_norm_g, g_attn_out)


R_E1, R_E2, R_W1, R_W2, R_RANK1, R_RANK2 = range(6)


def _out_kernel(mc_ref, ma_ref, w_hbm, x_ref, gt_ref, g_ref, sc_ref, sh_ref, wr_ref, br_ref,
                x1_ref, h2_ref, route_ref, cnt_ref, cnt_scr, tri_scr, w_bf, stage, sem):
    @pl.when(pl.program_id(0) == 0)
    def _():
        cnt_scr[...] = jnp.zeros(cnt_scr.shape, jnp.float32)
        earlier = (lax.broadcasted_iota(jnp.int32, tri_scr.shape, 1)
                   < lax.broadcasted_iota(jnp.int32, tri_scr.shape, 0))
        tri_scr[...] = jnp.where(earlier, 1.0, 0.0).astype(tri_scr.dtype)
        rows = stage.shape[0]
        for k in range(w_bf.shape[0] // rows):
            cp = pltpu.make_async_copy(w_hbm.at[pl.ds(k * rows, rows)], stage, sem)
            cp.start()
            cp.wait()
            w_bf[pl.ds(k * rows, rows), :] = stage[...].astype(jnp.bfloat16)

    rs = x_ref.shape[0] // OUT_SUB
    for sub in range(OUT_SUB):
        rows = pl.ds(sub * rs, rs)
        _out_rows(mc_ref[rows, :], ma_ref[rows, :], w_bf, x_ref[rows, :], gt_ref, g_ref, sc_ref, sh_ref,
                  wr_ref, br_ref, x1_ref.at[rows], h2_ref.at[pl.ds(sub * rs * ROW_SUB, rs * ROW_SUB)],
                  route_ref.at[rows], cnt_scr, tri_scr[0:rs, 0:rs])
    cnt_ref[...] = cnt_scr[...]


def _out_rows(mc, ma, w_bf, x, gt_ref, g_ref, sc_ref, sh_ref, wr_ref, br_ref,
              x1_ref, h2_ref, route_ref, cnt_scr, tri):
    mix = (jnp.dot(mc, w_bf[0:CONV_DIM, :], preferred_element_type=jnp.float32)
           + jnp.dot(ma, w_bf[CONV_DIM:, :], preferred_element_type=jnp.float32))
    x1 = x + gt_ref[0] * mix
    x1_ref[...] = x1
    h2 = _rms(x1) * g_ref[...] * (1.0 + sc_ref[0]) + sh_ref[0]
    _rows_from_2d(h2_ref, h2)

    logit = jnp.dot(h2.astype(jnp.bfloat16), wr_ref[...],
                    preferred_element_type=jnp.float32) + br_ref[...]
    lane = lax.broadcasted_iota(jnp.int32, logit.shape, 1).astype(jnp.float32)
    neg = -jnp.inf
    big = float(ROUTE_LANES)
    is_g = lane < N_GROUPS
    gl = jnp.where(is_g, logit, neg)
    gmax = jnp.max(gl, axis=1, keepdims=True)
    p_group = 1.0 / jnp.sum(jnp.exp(gl - gmax), axis=1, keepdims=True)
    g_sel = jnp.min(jnp.where(is_g & (gl == gmax), lane, big), axis=1, keepdims=True)
    lo = N_GROUPS + g_sel * EXPERTS_PER_GROUP
    in_grp = (lane >= lo) & (lane < lo + EXPERTS_PER_GROUP)
    el = jnp.where(in_grp, logit, neg)
    m1 = jnp.max(el, axis=1, keepdims=True)
    i1 = jnp.min(jnp.where(in_grp & (el == m1), lane, big), axis=1, keepdims=True)
    el2 = jnp.where(lane == i1, neg, el)
    m2 = jnp.max(el2, axis=1, keepdims=True)
    i2 = jnp.min(jnp.where(in_grp & (lane != i1) & (el2 == m2), lane, big), axis=1, keepdims=True)
    r = jnp.exp(m2 - m1)
    w1 = p_group / (1.0 + r)
    w2 = p_group * r / (1.0 + r)

    member = jnp.where(lane == i1, 1.0, jnp.where(lane == i2, 1.0, 0.0))
    before = jnp.dot(tri, member.astype(jnp.bfloat16),
                     preferred_element_type=jnp.float32) + cnt_scr[...]
    rank1 = jnp.sum(jnp.where(lane == i1, before, 0.0), axis=1, keepdims=True)
    rank2 = jnp.sum(jnp.where(lane == i2, before, 0.0), axis=1, keepdims=True)
    cnt_scr[...] = cnt_scr[...] + jnp.sum(member, axis=0, keepdims=True)

    route = jnp.zeros_like(logit)
    for idx, val in ((R_E1, i1 - N_GROUPS), (R_E2, i2 - N_GROUPS), (R_W1, w1), (R_W2, w2),
                     (R_RANK1, rank1), (R_RANK2, rank2)):
        route = jnp.where(lane == idx, val, route)
    route_ref[...] = route


def _out(mix_c, mix_a, w_out, x2, gt1, g_ffn, sc2, sh2, w_route, b_route, seq, tm=512):
    t, d = x2.shape
    per_b = seq // tm
    bmap = lambda m: (m // per_b, 0, 0)
    return pl.pallas_call(
        _out_kernel,
        grid=(t // tm,),
        in_specs=[pl.BlockSpec((tm, CONV_DIM), lambda m: (m, 0)),
                  pl.BlockSpec((tm, ATTN_DIM), lambda m: (m, 0)),
                  pl.BlockSpec(memory_space=pl.ANY),
                  pl.BlockSpec((tm, d), lambda m: (m, 0)),
                  pl.BlockSpec((1, 1, d), bmap),
                  pl.BlockSpec((1, d), lambda m: (0, 0)),
                  pl.BlockSpec((1, 1, d), bmap),
                  pl.BlockSpec((1, 1, d), bmap),
                  pl.BlockSpec((d, ROUTE_LANES), lambda m: (0, 0)),
                  pl.BlockSpec((1, ROUTE_LANES), lambda m: (0, 0))],
        out_specs=[pl.BlockSpec((tm, d), lambda m: (m, 0)),
                   pl.BlockSpec((tm * ROW_SUB, LANES), lambda m: (m, 0)),
                   pl.BlockSpec((tm, ROUTE_LANES), lambda m: (m, 0)),
                   pl.BlockSpec((1, ROUTE_LANES), lambda m: (0, 0))],
        out_shape=[jax.ShapeDtypeStruct((t, d), jnp.float32),
                   jax.ShapeDtypeStruct((t * ROW_SUB, LANES), jnp.float32),
                   jax.ShapeDtypeStruct((t, ROUTE_LANES), jnp.float32),
                   jax.ShapeDtypeStruct((1, ROUTE_LANES), jnp.float32)],
        scratch_shapes=[pltpu.VMEM((1, ROUTE_LANES), jnp.float32),
                        pltpu.VMEM((tm, tm), jnp.bfloat16),
                        pltpu.VMEM(w_out.shape, jnp.bfloat16),
                        pltpu.VMEM((STAGE_ROWS, d), jnp.float32),
                        pltpu.SemaphoreType.DMA(())],
        compiler_params=_cparams(("arbitrary",)),
        name="out",
    )(mix_c, mix_a, w_out, x2, gt1, g_ffn, sc2, sh2, w_route, b_route)


def _slot(off_ref, e_ref, r_ref, tok):
    return off_ref[e_ref[tok]] + r_ref[tok]


def _dispatch_kernel(e1_ref, e2_ref, r1_ref, r2_ref, off_ref, zs_ref, zc_ref, nu_ref, h2_ref, xs_ref,
                     zbuf, sem_z, sem_r, *, tm, n_tiles):
    i = pl.program_id(0)

    @pl.when(i == 0)
    def _():
        zbuf[...] = jnp.zeros(zbuf.shape, zbuf.dtype)

        def zero_copy(start):
            rows = pl.ds(pl.multiple_of(start * ROW_SUB, ZERO_ROWS * ROW_SUB), ZERO_ROWS * ROW_SUB)
            return pltpu.make_async_copy(zbuf, xs_ref.at[rows], sem_z)

        def zero_fill(first_row, n_granules):
            def start(k, carry):
                zero_copy(first_row + k * ZERO_ROWS).start()
                return carry

            def wait(k, carry):
                zero_copy(first_row + k * ZERO_ROWS).wait()
                return carry

            lax.fori_loop(0, n_granules, start, 0)
            lax.fori_loop(0, n_granules, wait, 0)

        for e in range(N_EXPERTS):
            zero_fill(zs_ref[e], zc_ref[e])
        zero_fill(nu_ref[0] * MOE_TM, (n_tiles - nu_ref[0]) * (MOE_TM // ZERO_ROWS))

    def issue_row(r, carry):
        tok = i * tm + r
        for e_ref, r_ref in ((e1_ref, r1_ref), (e2_ref, r2_ref)):
            pltpu.make_async_copy(_row(h2_ref, r), _row(xs_ref, _slot(off_ref, e_ref, r_ref, tok)),
                                  sem_r).start()
        return carry

    lax.fori_loop(0, tm, issue_row, 0)
    for _ in range(2):
        pltpu.make_async_copy(h2_ref, xs_ref.at[pl.ds(0, tm * ROW_SUB)], sem_r).wait()


def _dispatch(plan, h2, n_rows, tm=256):
    t = h2.shape[0] // ROW_SUB
    kern = functools.partial(_dispatch_kernel, tm=tm, n_tiles=n_rows // MOE_TM)
    return pl.pallas_call(
        kern,
        grid_spec=pltpu.PrefetchScalarGridSpec(
            num_scalar_prefetch=len(plan),
            grid=(t // tm,),
            in_specs=[pl.BlockSpec((tm * ROW_SUB, LANES), lambda i, *_: (i, 0))],
            out_specs=pl.BlockSpec(memory_space=pl.ANY),
            scratch_shapes=[pltpu.VMEM((ZERO_ROWS * ROW_SUB, LANES), h2.dtype),
                            pltpu.SemaphoreType.DMA(()),
                            pltpu.SemaphoreType.DMA(())]),
        out_shape=jax.ShapeDtypeStruct((n_rows * ROW_SUB, LANES), h2.dtype),
        compiler_params=_cparams(("arbitrary",)),
        name="dispatch",
    )(*plan, h2)


def _mlp_kernel(te_ref, nu_ref, first_ref, nxt_ref, slot_ref, xs_ref, wg_hbm, wu_hbm, wd_hbm, ys_ref,
                wg_f32, wu_f32, wd_f32, wg_bf, wu_bf, wd_bf, sem):
    i = pl.program_id(0)

    def copies(e, s):
        return [pltpu.make_async_copy(hbm.at[e], buf.at[s], sem.at[s])
                for hbm, buf in ((wg_hbm, wg_f32), (wu_hbm, wu_f32), (wd_hbm, wd_f32))]

    @pl.when(i == 0)
    def _():
        for cp in copies(te_ref[0], 0):
            cp.start()

    @pl.when((i < nu_ref[0]) & (first_ref[i] == 1))
    def _():
        s = slot_ref[i]
        for cp in copies(te_ref[i], s):
            cp.wait()
        wg_bf[...] = wg_f32[s].astype(jnp.bfloat16)
        wu_bf[...] = wu_f32[s].astype(jnp.bfloat16)
        wd_bf[...] = wd_f32[s].astype(jnp.bfloat16)

        @pl.when(nxt_ref[i] >= 0)
        def _():
            for cp in copies(nxt_ref[i], 1 - s):
                cp.start()

    @pl.when(i < nu_ref[0])
    def _():
        x = _rows_to_2d(xs_ref, MOE_TM).astype(jnp.bfloat16)
        a = jnp.dot(x, wg_bf[...], preferred_element_type=jnp.float32)
        u = jnp.dot(x, wu_bf[...], preferred_element_type=jnp.float32)
        hid = (a * jax.nn.sigmoid(a)) * u
        _rows_from_2d(ys_ref, jnp.dot(hid.astype(jnp.bfloat16), wd_bf[...],
                                      preferred_element_type=jnp.float32))

    @pl.when(i >= nu_ref[0])
    def _():
        ys_ref[...] = jnp.zeros(ys_ref.shape, ys_ref.dtype)


def _mlp(tiles, xs, w_gate, w_up, w_down):
    n_rows = xs.shape[0] // ROW_SUB
    _, d, f = w_gate.shape
    used = lambda i, te, nu, *_: (jnp.minimum(i, nu[0] - 1), 0)
    return pl.pallas_call(
        _mlp_kernel,
        grid_spec=pltpu.PrefetchScalarGridSpec(
            num_scalar_prefetch=len(tiles),
            grid=(n_rows // MOE_TM,),
            in_specs=[pl.BlockSpec((MOE_TM * ROW_SUB, LANES), used),
                      pl.BlockSpec(memory_space=pl.ANY),
                      pl.BlockSpec(memory_space=pl.ANY),
                      pl.BlockSpec(memory_space=pl.ANY)],
            out_specs=pl.BlockSpec((MOE_TM * ROW_SUB, LANES), lambda i, *_: (i, 0)),
            scratch_shapes=[pltpu.VMEM((2, d, f), jnp.float32),
                            pltpu.VMEM((2, d, f), jnp.float32),
                            pltpu.VMEM((2, f, d), jnp.float32),
                            pltpu.VMEM((d, f), jnp.bfloat16),
                            pltpu.VMEM((d, f), jnp.bfloat16),
                            pltpu.VMEM((f, d), jnp.bfloat16),
                            pltpu.SemaphoreType.DMA((2,))]),
        out_shape=jax.ShapeDtypeStruct((n_rows * ROW_SUB, LANES), jnp.float32),
        compiler_params=_cparams(("arbitrary",)),
        name="mlp",
    )(*tiles, xs, w_gate, w_up, w_down)


def _combine_kernel(e1_ref, e2_ref, r1_ref, r2_ref, off_ref, ys_ref, x1_ref, route_ref, gt_ref, g_ref,
                    sc_ref, sh_ref, o_ref, a0, b0, a1, b1, sem, *, tm, n_tiles):
    i = pl.program_id(0)
    bufs = ((a0, b0), (a1, b1))

    def issue(tile, sl):
        def issue_row(r, carry):
            tok = tile * tm + r
            for buf, e_ref, r_ref in ((bufs[sl][0], e1_ref, r1_ref), (bufs[sl][1], e2_ref, r2_ref)):
                pltpu.make_async_copy(_row(ys_ref, _slot(off_ref, e_ref, r_ref, tok)),
                                      _row(buf, r), sem.at[sl]).start()
            return carry

        lax.fori_loop(0, tm, issue_row, 0)

    @pl.when(i == 0)
    def _():
        issue(0, 0)

    for sl in range(2):
        @pl.when(i % 2 == sl)
        def _(sl=sl):
            @pl.when(i + 1 < n_tiles)
            def _():
                issue(i + 1, 1 - sl)

            for buf in bufs[sl]:
                pltpu.make_async_copy(ys_ref.at[pl.ds(0, tm * ROW_SUB)], buf, sem.at[sl]).wait()

            route = route_ref[...]
            y = (route[:, R_W1:R_W1 + 1] * _rows_to_2d(bufs[sl][0], tm)
                 + route[:, R_W2:R_W2 + 1] * _rows_to_2d(bufs[sl][1], tm))
            x = x1_ref[...] + gt_ref[0] * y
            o_ref[...] = _rms(x) * g_ref[...] * (1.0 + sc_ref[0]) + sh_ref[0]


def _combine(plan, ys, x1, route, gt2, g_final, scf, shf, seq, tm=256):
    t, d = x1.shape
    per_b = seq // tm
    n_tiles = t // tm
    bmap = lambda m, *_: (m // per_b, 0, 0)
    kern = functools.partial(_combine_kernel, tm=tm, n_tiles=n_tiles)
    return pl.pallas_call(
        kern,
        grid_spec=pltpu.PrefetchScalarGridSpec(
            num_scalar_prefetch=len(plan),
            grid=(n_tiles,),
            in_specs=[pl.BlockSpec(memory_space=pl.ANY),
                      pl.BlockSpec((tm, d), lambda m, *_: (m, 0)),
                      pl.BlockSpec((tm, ROUTE_LANES), lambda m, *_: (m, 0)),
                      pl.BlockSpec((1, 1, d), bmap),
                      pl.BlockSpec((1, d), lambda m, *_: (0, 0)),
                      pl.BlockSpec((1, 1, d), bmap),
                      pl.BlockSpec((1, 1, d), bmap)],
            out_specs=pl.BlockSpec((tm, d), lambda m, *_: (m, 0)),
            scratch_shapes=[pltpu.VMEM((tm * ROW_SUB, LANES), jnp.float32) for _ in range(4)]
            + [pltpu.SemaphoreType.DMA((2,))]),
        out_shape=jax.ShapeDtypeStruct((t, d), jnp.float32),
        compiler_params=_cparams(("arbitrary",)),
        name="combine",
    )(*plan, ys, x1, route, gt2, g_final, scf, shf)


def _moe_plan(route, counts, t):
    i32 = jnp.int32
    cnt = counts[0, N_GROUPS:N_GROUPS + N_EXPERTS].astype(i32)
    padded = (cnt + MOE_TM - 1) // MOE_TM * MOE_TM
    off_end = jnp.cumsum(padded)
    off = off_end - padded
    n_tiles = (2 * t + N_EXPERTS * MOE_TM) // MOE_TM
    n_used = off_end[-1] // MOE_TM
    tile_start = jnp.arange(n_tiles, dtype=i32) * MOE_TM
    te = jnp.sum((off_end[None, :] <= tile_start[:, None]).astype(i32), axis=1)
    te = jnp.minimum(te, N_EXPERTS - 1)
    tile = jnp.arange(n_tiles, dtype=i32)
    te = jnp.where(tile < n_used, te, te[n_used - 1])
    first = jnp.concatenate([jnp.ones((1,), i32), (te[1:] != te[:-1]).astype(i32)])
    nxt_tile = off_end[te] // MOE_TM
    nxt = jnp.where(nxt_tile < n_used, te[jnp.minimum(nxt_tile, n_tiles - 1)], -1)
    slot = (jnp.cumsum(first) - 1) % 2
    assign = tuple(route[:, k].astype(i32) for k in (R_E1, R_E2, R_RANK1, R_RANK2))
    zstart = (off + cnt) // ZERO_ROWS * ZERO_ROWS
    zcount = (off_end - zstart) // ZERO_ROWS
    n_used = n_used.reshape(1)
    return assign, off, (zstart, zcount), (te, n_used, first, nxt, slot), n_tiles * MOE_TM


def kernel(x, c, w_ada, b_ada, g_mix, w_in, conv_w, w_uk, kv_norm_g, w_uv, g_conv_out, g_attn_out,
           w_out, g_ffn, w_rg, b_rg, w_re, b_re, w_gate, w_up, w_down, w_ada_f, b_ada_f, g_final):
    b, s, d = x.shape
    assert w_ada.shape[0] == 1, "single layer"
    bf = jnp.bfloat16
    x2 = x.reshape(b * s, d)

    c_pad = jnp.zeros((SUBLANES, d), jnp.float32).at[:b].set(c)
    mod = _mod(c_pad, w_ada[0], b_ada[0])[:b]
    modf = _mod(c_pad, w_ada_f, b_ada_f)[:b]
    vec = lambda a, i: a[:, i * d:(i + 1) * d].reshape(b, 1, d)
    sh1, sc1, gt1, sh2, sc2, gt2 = (vec(mod, i) for i in range(N_MOD))
    shf, scf = vec(modf, 0), vec(modf, 1)
    row = lambda a: a.reshape(1, -1)

    w_in_t = jnp.swapaxes(w_in[0], 0, 1)
    h, q, kv, iq, ikw = _hproj(x2, row(g_mix[0]), sc1, sh1, w_in_t, s)
    mix_c = _projconv(h, w_in_t, conv_w[0], row(g_conv_out[0]), s)

    mix_a = _attn(iq.reshape(b, s, -1), ikw.reshape(b, s, -1), kv.reshape(b, s, -1),
                  q.reshape(b, s, -1), w_uk[0].astype(bf), w_uv[0].astype(bf),
                  row(kv_norm_g[0]), row(g_attn_out[0]))

    w_route = jnp.zeros((d, ROUTE_LANES), bf).at[:, :N_GROUPS].set(w_rg[0].astype(bf))
    w_route = w_route.at[:, N_GROUPS:N_GROUPS + N_EXPERTS].set(w_re[0].astype(bf))
    b_route = jnp.zeros((1, ROUTE_LANES), jnp.float32).at[0, :N_GROUPS].set(b_rg[0])
    b_route = b_route.at[0, N_GROUPS:N_GROUPS + N_EXPERTS].set(b_re[0])
    x1, h2, route, counts = _out(mix_c, mix_a.reshape(b * s, -1),
                                 w_out[0], x2, gt1, row(g_ffn[0]),
                                 sc2, sh2, w_route, b_route, s)

    assign, off, zero, tiles, n_rows = _moe_plan(route, counts, b * s)
    xs = _dispatch((*assign, off, *zero, tiles[1]), h2, n_rows)
    ys = _mlp(tiles, xs, w_gate[0], w_up[0], w_down[0])
    out = _combine((*assign, off), ys, x1, route, gt2, row(g_final), scf, shf, s)
    return out.reshape(b, s, d)
```

```python
import functools

import jax
import jax.numpy as jnp
from jax import lax
from jax.experimental import pallas as pl
from jax.experimental.pallas import tpu as pltpu

D_MODEL = 2048
CONV_DIM = 1024
CONV_WIDTH = 3
N_HEADS = 8
HEAD_DIM = 128
ATTN_DIM = N_HEADS * HEAD_DIM
KV_RANK = 512
IDX_HEADS = 16
IDX_DIM = 128
TOPK_MAX = 256
N_GROUPS = 4
EXPERTS_PER_GROUP = 8
N_EXPERTS = N_GROUPS * EXPERTS_PER_GROUP
EXPERT_FF = 512
N_MOD = 6
EPS = 1e-6

OFF_Q = 3 * CONV_DIM
OFF_KV = OFF_Q + ATTN_DIM
OFF_IQ = OFF_KV + KV_RANK
OFF_IK = OFF_IQ + IDX_HEADS * IDX_DIM
OFF_IW = OFF_IK + IDX_DIM
IN_COLS = OFF_IW + IDX_HEADS

LANES = 128
SUBLANES = 8
VMEM_LIMIT = 56 * 1024 * 1024

IKW_COLS = 256
ATTN_Q = 256
ATTN_KC = 256
ATTN_RB = 256
MASKED = -1e30
TIE_ALL = 2 ** 30
ROUTE_LANES = 128
STAGE_ROWS = 256
OUT_SUB = 2
MOE_TM = 256
ZERO_ROWS = 64
ROW_SUB = D_MODEL // LANES
INT_MIN = -2 ** 31
KEY_NEG_INF = INT_MIN + 0x7FFFFF


def _cparams(sem):
    return pltpu.CompilerParams(dimension_semantics=sem, vmem_limit_bytes=VMEM_LIMIT)


def _rms(v, axis=-1):
    return v * lax.rsqrt(jnp.mean(v * v, axis=axis, keepdims=True) + EPS)


def _tile_lanes(v, n):
    return jnp.concatenate([v] * n, axis=1)


def _rows_to_2d(ref, n):
    return jnp.concatenate([ref[pl.ds(c, n, stride=ROW_SUB), :] for c in range(ROW_SUB)], axis=1)


def _rows_from_2d(ref, val):
    n = val.shape[0]
    for c in range(ROW_SUB):
        ref[pl.ds(c, n, stride=ROW_SUB), :] = val[:, c * LANES:(c + 1) * LANES]


def _row(ref, i):
    return ref.at[pl.ds(pl.multiple_of(i * ROW_SUB, ROW_SUB), ROW_SUB)]


def _load_bf16(w_hbm, pieces, w_bf, stage, sem):
    def copy(i):
        src, n, _ = pieces[i]
        return pltpu.make_async_copy(w_hbm.at[pl.ds(src, n)], stage.at[i % 2, pl.ds(0, n)],
                                     sem.at[i % 2])

    copy(0).start()
    for i, (_, n, dst) in enumerate(pieces):
        if i + 1 < len(pieces):
            copy(i + 1).start()
        copy(i).wait()
        w_bf[pl.ds(dst, n), :] = stage[i % 2, pl.ds(0, n), :].astype(jnp.bfloat16)


def _mod_kernel(c_ref, w_ref, b_ref, o_ref):
    c = c_ref[...]
    ca = (c * jax.nn.sigmoid(c)).astype(jnp.bfloat16)
    o_ref[...] = jnp.dot(ca, w_ref[...].astype(jnp.bfloat16),
                         preferred_element_type=jnp.float32) + b_ref[...]


def _mod(c_pad, w, b, tn=1024):
    d, n = w.shape
    return pl.pallas_call(
        _mod_kernel,
        grid=(n // tn,),
        in_specs=[pl.BlockSpec((c_pad.shape[0], d), lambda j: (0, 0)),
                  pl.BlockSpec((d, tn), lambda j: (0, j)),
                  pl.BlockSpec((1, tn), lambda j: (0, j))],
        out_specs=pl.BlockSpec((c_pad.shape[0], tn), lambda j: (0, j)),
        out_shape=jax.ShapeDtypeStruct((c_pad.shape[0], n), jnp.float32),
        compiler_params=_cparams(("arbitrary",)),
        name="mod",
    )(c_pad, w, b.reshape(1, n))


ATT_COLS = IN_COLS - OFF_Q
ATT_PAD = OFF_IK - OFF_Q + IKW_COLS


def _hproj_kernel(x_ref, g_ref, sc_ref, sh_ref, wt_hbm, h_ref, q_ref, kv_ref, iq_ref, ikw_ref,
                  w_bf, stage, sem):
    @pl.when(pl.program_id(0) == 0)
    def _():
        full = ATT_COLS // STAGE_ROWS
        pieces = [(OFF_Q + k * STAGE_ROWS, STAGE_ROWS, k * STAGE_ROWS) for k in range(full)]
        pieces.append((OFF_Q + full * STAGE_ROWS, ATT_COLS - full * STAGE_ROWS, full * STAGE_ROWS))
        _load_bf16(wt_hbm, pieces, w_bf, stage, sem)
        w_bf[pl.ds(ATT_COLS, ATT_PAD - ATT_COLS), :] = jnp.zeros((ATT_PAD - ATT_COLS, w_bf.shape[1]),
                                                                 jnp.bfloat16)

    h = (_rms(x_ref[...]) * g_ref[...] * (1.0 + sc_ref[0]) + sh_ref[0]).astype(h_ref.dtype)
    h_ref[...] = h
    nt = (((1,), (1,)), ((), ()))
    row0 = 0
    for o_ref in (q_ref, kv_ref, iq_ref, ikw_ref):
        n = o_ref.shape[1]
        o_ref[...] = lax.dot_general(h, w_bf[row0:row0 + n, :], nt,
                                     preferred_element_type=jnp.float32).astype(o_ref.dtype)
        row0 += n


def _hproj(x2, g_mix, sc1, sh1, wt, seq, tm=512):
    t, d = x2.shape
    per_b = seq // tm
    bmap = lambda m: (m // per_b, 0, 0)
    widths = (ATTN_DIM, KV_RANK, IDX_HEADS * IDX_DIM, IKW_COLS)
    dtypes = (jnp.bfloat16, jnp.float32, jnp.bfloat16, jnp.float32)
    assert sum(widths) == ATT_PAD and (ATT_COLS % STAGE_ROWS) % SUBLANES == 0
    return pl.pallas_call(
        _hproj_kernel,
        grid=(t // tm,),
        in_specs=[pl.BlockSpec((tm, d), lambda m: (m, 0)),
                  pl.BlockSpec((1, d), lambda m: (0, 0)),
                  pl.BlockSpec((1, 1, d), bmap),
                  pl.BlockSpec((1, 1, d), bmap),
                  pl.BlockSpec(memory_space=pl.ANY)],
        out_specs=[pl.BlockSpec((tm, d), lambda m: (m, 0))]
        + [pl.BlockSpec((tm, w), lambda m: (m, 0)) for w in widths],
        out_shape=[jax.ShapeDtypeStruct((t, d), jnp.bfloat16)]
        + [jax.ShapeDtypeStruct((t, w), dt) for w, dt in zip(widths, dtypes)],
        scratch_shapes=[pltpu.VMEM((ATT_PAD, d), jnp.bfloat16),
                        pltpu.VMEM((2, STAGE_ROWS, d), jnp.float32),
                        pltpu.SemaphoreType.DMA((2,))],
        compiler_params=_cparams(("arbitrary",)),
        name="hproj",
    )(x2, g_mix, sc1, sh1, wt)


def _projconv_kernel(h_ref, wt_hbm, cw_ref, g_ref, o_ref, w_bf, stage, halo_scr, sem, *, per_b):
    m = pl.program_id(0)

    @pl.when(m == 0)
    def _():
        _load_bf16(wt_hbm, [(r, STAGE_ROWS, r) for r in range(0, w_bf.shape[0], STAGE_ROWS)],
                   w_bf, stage, sem)

    nt = (((1,), (1,)), ((), ()))
    h = h_ref[...]
    bg, cg, xv = (lax.dot_general(h, w_bf[k * CONV_DIM:(k + 1) * CONV_DIM, :], nt,
                                  preferred_element_type=jnp.float32) for k in range(3))
    u = cg * xv
    halo = jnp.where(m % per_b == 0, 0.0, halo_scr[...])
    halo_scr[...] = u[u.shape[0] - SUBLANES:, :]
    row = lax.broadcasted_iota(jnp.int32, u.shape, 0)
    h1 = halo[SUBLANES - 1:SUBLANES, :]
    h2 = halo[SUBLANES - 2:SUBLANES - 1, :]
    u1 = jnp.where(row == 0, h1, pltpu.roll(u, 1, 0))
    u2 = jnp.where(row == 0, h2, jnp.where(row == 1, h1, pltpu.roll(u, 2, 0)))
    w = cw_ref[...]
    y = bg * (w[0:1, :] * u2 + w[1:2, :] * u1 + w[2:3, :] * u)
    o_ref[...] = (_rms(y) * g_ref[...]).astype(o_ref.dtype)


def _projconv(h, wt, conv_w, g_conv_out, seq, tm=1024):
    t, d = h.shape
    c = CONV_DIM
    kern = functools.partial(_projconv_kernel, per_b=seq // tm)
    return pl.pallas_call(
        kern,
        grid=(t // tm,),
        in_specs=[pl.BlockSpec((tm, d), lambda m: (m, 0)),
                  pl.BlockSpec(memory_space=pl.ANY),
                  pl.BlockSpec((CONV_WIDTH, c), lambda m: (0, 0)),
                  pl.BlockSpec((1, c), lambda m: (0, 0))],
        out_specs=pl.BlockSpec((tm, c), lambda m: (m, 0)),
        out_shape=jax.ShapeDtypeStruct((t, c), jnp.bfloat16),
        scratch_shapes=[pltpu.VMEM((3 * c, d), jnp.bfloat16),
                        pltpu.VMEM((2, STAGE_ROWS, d), jnp.float32),
                        pltpu.VMEM((SUBLANES, c), jnp.float32),
                        pltpu.SemaphoreType.DMA((2,))],
        compiler_params=_cparams(("arbitrary",)),
        name="proj_conv",
    )(h, wt, conv_w, g_conv_out)


def _ordered_bits(v):
    return v ^ ((v >> 31) & jnp.int32(0x7FFFFFFF))


def _attn_search(nch, sc_scr, k_sel):
    n_keys = float(nch * ATTN_KC)
    half = ATTN_Q // 2

    def half_step(bit, thr, rows):
        cand = thr + bit
        cand_f = lax.bitcast_convert_type(_ordered_bits(cand), jnp.float32)
        hits = jnp.where(sc_scr[0, rows, :] >= cand_f, 1.0, 0.0)
        for c in range(1, nch):
            hits = hits + jnp.where(sc_scr[c, rows, :] >= cand_f, 1.0, 0.0)
        cnt = jnp.sum(hits, axis=1, keepdims=True)
        cnt = jnp.where(cand < KEY_NEG_INF, n_keys, cnt)
        return jnp.where(cnt >= k_sel, cand, thr)

    def bit_step(i, thrs):
        bit = lax.shift_left(jnp.int32(1), 31 - i)
        return tuple(half_step(bit, thr, pl.ds(k * half, half)) for k, thr in enumerate(thrs))

    init = jnp.full((half, 1), INT_MIN, jnp.int32)
    thrs = lax.fori_loop(0, 32, bit_step, (init, init), unroll=2)
    thr = jnp.concatenate(thrs, axis=0)
    return lax.bitcast_convert_type(_ordered_bits(thr), jnp.float32)


def _attn_tiebreak(nch, sc_scr, thr_f, k_sel, tie_scr):
    kloc = lax.broadcasted_iota(jnp.int32, (ATTN_Q, ATTN_KC), 1)

    def count(pred):
        hits = pred(sc_scr[0], 0)
        for c in range(1, nch):
            hits = hits + pred(sc_scr[c], c)
        return jnp.sum(hits, axis=1, keepdims=True)

    tie_scr[...] = jnp.full(tie_scr.shape, TIE_ALL, jnp.int32)
    n_ge = count(lambda sc, c: jnp.where(sc >= thr_f, 1.0, 0.0))
    excess = jnp.where(thr_f > -jnp.inf, n_ge - k_sel, 0.0)

    @pl.when(jnp.max(excess) > 0.0)
    def _():
        need = k_sel - count(lambda sc, c: jnp.where(sc > thr_f, 1.0, 0.0))
        n_bits = (nch * ATTN_KC - 1).bit_length()

        def step(i, lim):
            cand = lim + lax.shift_left(jnp.int32(1), n_bits - 1 - i)
            below = count(lambda sc, c: jnp.where(
                sc == thr_f, jnp.where(c * ATTN_KC + kloc < cand, 1.0, 0.0), 0.0))
            return jnp.where(below < need, cand, lim)

        lim = lax.fori_loop(0, n_bits, step, jnp.zeros((ATTN_Q, 1), jnp.int32))
        tie_scr[...] = jnp.where(excess > 0.0, lim, TIE_ALL)


def _attn_kernel(iq_ref, ikwq_ref, ikwa_ref, kv_ref, q_ref, wuk_ref, wuv_ref, kvg_ref, go_ref,
                 o_ref, ckv_scr, ik_scr, sc_scr, thr_scr, tie_scr, ql_scr, m_scr, l_scr, acc_scr, y_scr,
                 *, seq, k_sel):
    j = pl.program_id(1)
    nch = j + 1
    nt = (((1,), (1,)), ((), ()))

    @pl.when(j == 0)
    def _():
        ckv_scr[...] = (_rms(kv_ref[0]) * kvg_ref[...]).astype(jnp.bfloat16)
        ik_scr[...] = ikwa_ref[0][:, :IDX_DIM].astype(jnp.bfloat16)

    iw = ikwq_ref[0][:, IDX_DIM:IDX_DIM + IDX_HEADS] * (IDX_HEADS ** -0.5 * IDX_DIM ** -0.5)
    qpos = j * ATTN_Q + lax.broadcasted_iota(jnp.int32, (ATTN_Q, ATTN_KC), 0)
    kloc = lax.broadcasted_iota(jnp.int32, (ATTN_Q, ATTN_KC), 1)

    def score_chunk(c, carry):
        k0 = pl.multiple_of(c * ATTN_KC, ATTN_KC)
        ik_c = ik_scr[pl.ds(k0, ATTN_KC), :]
        score = jnp.zeros((ATTN_Q, ATTN_KC), jnp.float32)
        for h in range(IDX_HEADS):
            s = lax.dot_general(iq_ref[0][:, h * IDX_DIM:(h + 1) * IDX_DIM], ik_c, nt,
                                preferred_element_type=jnp.float32)
            score = score + jnp.maximum(s, 0.0) * iw[:, h:h + 1]
        sc_scr[c] = jnp.where(k0 + kloc <= qpos, score, -jnp.inf)
        return carry

    lax.fori_loop(0, nch, score_chunk, 0)

    for v in range(seq // ATTN_KC):
        @pl.when(j == v)
        def _(v=v):
            thr_scr[...] = _attn_search(v + 1, sc_scr, k_sel)
            _attn_tiebreak(v + 1, sc_scr, thr_scr[...], k_sel, tie_scr)

    for h in range(N_HEADS):
        ql = lax.dot_general(q_ref[0][:, h * HEAD_DIM:(h + 1) * HEAD_DIM], wuk_ref[h], nt,
                             preferred_element_type=jnp.float32)
        ql_scr[h * ATTN_Q:(h + 1) * ATTN_Q, :] = ql.astype(jnp.bfloat16)
    thr_f = thr_scr[...]
    tie = tie_scr[...]
    scale = HEAD_DIM ** -0.5

    def attn_chunk(c, first):
        k0 = pl.multiple_of(c * ATTN_KC, ATTN_KC)
        ckv_c = ckv_scr[pl.ds(k0, ATTN_KC), :]
        sc = sc_scr[c]
        kpos = k0 + kloc
        tied = jnp.where(sc == thr_f, jnp.where(kpos <= tie, 0.0, MASKED), MASKED)
        bias = jnp.where(kpos <= qpos, jnp.where(sc > thr_f, 0.0, tied), MASKED)
        for u in range(N_HEADS * ATTN_Q // ATTN_RB):
            rows = pl.ds(u * ATTN_RB, ATTN_RB)
            qrow = (u * ATTN_RB) % ATTN_Q
            lg = lax.dot_general(ql_scr[rows, :], ckv_c, nt,
                                 preferred_element_type=jnp.float32) * scale + bias[qrow:qrow + ATTN_RB]
            row_max = jnp.max(lg, axis=1, keepdims=True)
            if first:
                m_new = jnp.maximum(jnp.full((ATTN_RB, LANES), MASKED, jnp.float32), row_max)
            else:
                m_old = m_scr[rows, :]
                m_new = jnp.maximum(m_old, row_max)
                alpha = jnp.exp(m_old - m_new)
            p = jnp.exp(lg - _tile_lanes(m_new, ATTN_KC // LANES))
            p_sum = jnp.sum(p, axis=1, keepdims=True)
            pv = jnp.dot(p.astype(jnp.bfloat16), ckv_c, preferred_element_type=jnp.float32)
            if first:
                l_scr[rows, :] = jnp.broadcast_to(p_sum, (ATTN_RB, LANES))
                acc_scr[rows, :] = pv
            else:
                l_scr[rows, :] = alpha * l_scr[rows, :] + p_sum
                acc_scr[rows, :] = _tile_lanes(alpha, KV_RANK // LANES) * acc_scr[rows, :] + pv
            m_scr[rows, :] = m_new

    attn_chunk(0, True)

    def later_chunk(c, carry):
        attn_chunk(c, False)
        return carry

    lax.fori_loop(1, nch, later_chunk, 0)

    o = (acc_scr[...] * _tile_lanes(1.0 / l_scr[...], KV_RANK // LANES)).astype(jnp.bfloat16)
    for h in range(N_HEADS):
        y_scr[:, h * HEAD_DIM:(h + 1) * HEAD_DIM] = jnp.dot(
            o[h * ATTN_Q:(h + 1) * ATTN_Q], wuv_ref[h], preferred_element_type=jnp.float32)
    o_ref[0] = (_rms(y_scr[...]) * go_ref[...]).astype(o_ref.dtype)


def _attn(iq, ikw, kv, q, w_uk_bf, w_uv_bf, kv_norm_g, g_attn_out):
    b, s, _ = iq.shape
    assert ATTN_Q == ATTN_KC and s % ATTN_Q == 0
    k_sel = min(TOPK_MAX, s // 4)
    rows = N_HEADS * ATTN_Q
    kern = functools.partial(_attn_kernel, seq=s, k_sel=k_sel)
    return pl.pallas_call(
        kern,
        grid=(b, s // ATTN_Q),
        in_specs=[pl.BlockSpec((1, ATTN_Q, IDX_HEADS * IDX_DIM), lambda bi, j: (bi, j, 0)),
                  pl.BlockSpec((1, ATTN_Q, IKW_COLS), lambda bi, j: (bi, j, 0)),
                  pl.BlockSpec((1, s, IKW_COLS), lambda bi, j: (bi, 0, 0)),
                  pl.BlockSpec((1, s, KV_RANK), lambda bi, j: (bi, 0, 0)),
                  pl.BlockSpec((1, ATTN_Q, ATTN_DIM), lambda bi, j: (bi, j, 0)),
                  pl.BlockSpec((N_HEADS, KV_RANK, HEAD_DIM), lambda bi, j: (0, 0, 0)),
                  pl.BlockSpec((N_HEADS, KV_RANK, HEAD_DIM), lambda bi, j: (0, 0, 0)),
                  pl.BlockSpec((1, KV_RANK), lambda bi, j: (0, 0)),
                  pl.BlockSpec((1, ATTN_DIM), lambda bi, j: (0, 0))],
        out_specs=pl.BlockSpec((1, ATTN_Q, ATTN_DIM), lambda bi, j: (bi, j, 0)),
        out_shape=jax.ShapeDtypeStruct((b, s, ATTN_DIM), jnp.bfloat16),
        scratch_shapes=[pltpu.VMEM((s, KV_RANK), jnp.bfloat16),
                        pltpu.VMEM((s, IDX_DIM), jnp.bfloat16),
                        pltpu.VMEM((s // ATTN_KC, ATTN_Q, ATTN_KC), jnp.float32),
                        pltpu.VMEM((ATTN_Q, 1), jnp.float32),
                        pltpu.VMEM((ATTN_Q, 1), jnp.int32),
                        pltpu.VMEM((rows, KV_RANK), jnp.bfloat16),
                        pltpu.VMEM((rows, LANES), jnp.float32),
                        pltpu.VMEM((rows, LANES), jnp.float32),
                        pltpu.VMEM((rows, KV_RANK), jnp.float32),
                        pltpu.VMEM((ATTN_Q, ATTN_DIM), jnp.float32)],
        compiler_params=_cparams(("arbitrary", "arbitrary")),
        name="attn",
    )(iq, ikw, ikw, kv, q, w_uk_bf, w_uv_bf, kv_norm_g, g_attn_out)


R_E1, R_E2, R_W1, R_W2, R_RANK1, R_RANK2 = range(6)


def _out_kernel(mc_ref, ma_ref, w_hbm, x_ref, gt_ref, g_ref, sc_ref, sh_ref, wr_ref, br_ref,
                x1_ref, h2_ref, route_ref, cnt_ref, cnt_scr, tri_scr, w_bf, stage, sem):
    @pl.when(pl.program_id(0) == 0)
    def _():
        cnt_scr[...] = jnp.zeros(cnt_scr.shape, jnp.float32)
        earlier = (lax.broadcasted_iota(jnp.int32, tri_scr.shape, 1)
                   < lax.broadcasted_iota(jnp.int32, tri_scr.shape, 0))
        tri_scr[...] = jnp.where(earlier, 1.0, 0.0).astype(tri_scr.dtype)
        _load_bf16(w_hbm, [(r, STAGE_ROWS, r) for r in range(0, w_bf.shape[0], STAGE_ROWS)],
                   w_bf, stage, sem)

    rs = x_ref.shape[0] // OUT_SUB
    for sub in range(OUT_SUB):
        rows = pl.ds(sub * rs, rs)
        _out_rows(mc_ref[rows, :], ma_ref[rows, :], w_bf, x_ref[rows, :], gt_ref, g_ref, sc_ref, sh_ref,
                  wr_ref, br_ref, x1_ref.at[rows], h2_ref.at[pl.ds(sub * rs * ROW_SUB, rs * ROW_SUB)],
                  route_ref.at[rows], cnt_scr, tri_scr[0:rs, 0:rs])
    cnt_ref[...] = cnt_scr[...]


def _out_rows(mc, ma, w_bf, x, gt_ref, g_ref, sc_ref, sh_ref, wr_ref, br_ref,
              x1_ref, h2_ref, route_ref, cnt_scr, tri):
    mix = (jnp.dot(mc, w_bf[0:CONV_DIM, :], preferred_element_type=jnp.float32)
           + jnp.dot(ma, w_bf[CONV_DIM:, :], preferred_element_type=jnp.float32))
    x1 = x + gt_ref[0] * mix
    x1_ref[...] = x1
    h2 = _rms(x1) * g_ref[...] * (1.0 + sc_ref[0]) + sh_ref[0]
    _rows_from_2d(h2_ref, h2)

    logit = jnp.dot(h2.astype(jnp.bfloat16), wr_ref[...],
                    preferred_element_type=jnp.float32) + br_ref[...]
    lane = lax.broadcasted_iota(jnp.int32, logit.shape, 1).astype(jnp.float32)
    neg = -jnp.inf
    big = float(ROUTE_LANES)
    is_g = lane < N_GROUPS
    gl = jnp.where(is_g, logit, neg)
    gmax = jnp.max(gl, axis=1, keepdims=True)
    p_group = 1.0 / jnp.sum(jnp.exp(gl - gmax), axis=1, keepdims=True)
    g_sel = jnp.min(jnp.where(is_g & (gl == gmax), lane, big), axis=1, keepdims=True)
    lo = N_GROUPS + g_sel * EXPERTS_PER_GROUP
    in_grp = (lane >= lo) & (lane < lo + EXPERTS_PER_GROUP)
    el = jnp.where(in_grp, logit, neg)
    m1 = jnp.max(el, axis=1, keepdims=True)
    i1 = jnp.min(jnp.where(in_grp & (el == m1), lane, big), axis=1, keepdims=True)
    el2 = jnp.where(lane == i1, neg, el)
    m2 = jnp.max(el2, axis=1, keepdims=True)
    i2 = jnp.min(jnp.where(in_grp & (lane != i1) & (el2 == m2), lane, big), axis=1, keepdims=True)
    r = jnp.exp(m2 - m1)
    w1 = p_group / (1.0 + r)
    w2 = p_group * r / (1.0 + r)

    member = jnp.where(lane == i1, 1.0, jnp.where(lane == i2, 1.0, 0.0))
    before = jnp.dot(tri, member.astype(jnp.bfloat16),
                     preferred_element_type=jnp.float32) + cnt_scr[...]
    rank1 = jnp.sum(jnp.where(lane == i1, before, 0.0), axis=1, keepdims=True)
    rank2 = jnp.sum(jnp.where(lane == i2, before, 0.0), axis=1, keepdims=True)
    cnt_scr[...] = cnt_scr[...] + jnp.sum(member, axis=0, keepdims=True)

    route = jnp.zeros_like(logit)
    for idx, val in ((R_E1, i1 - N_GROUPS), (R_E2, i2 - N_GROUPS), (R_W1, w1), (R_W2, w2),
                     (R_RANK1, rank1), (R_RANK2, rank2)):
        route = jnp.where(lane == idx, val, route)
    route_ref[...] = route


def _out(mix_c, mix_a, w_out, x2, gt1, g_ffn, sc2, sh2, w_route, b_route, seq, tm=512):
    t, d = x2.shape
    per_b = seq // tm
    bmap = lambda m: (m // per_b, 0, 0)
    return pl.pallas_call(
        _out_kernel,
        grid=(t // tm,),
        in_specs=[pl.BlockSpec((tm, CONV_DIM), lambda m: (m, 0)),
                  pl.BlockSpec((tm, ATTN_DIM), lambda m: (m, 0)),
                  pl.BlockSpec(memory_space=pl.ANY),
                  pl.BlockSpec((tm, d), lambda m: (m, 0)),
                  pl.BlockSpec((1, 1, d), bmap),
                  pl.BlockSpec((1, d), lambda m: (0, 0)),
                  pl.BlockSpec((1, 1, d), bmap),
                  pl.BlockSpec((1, 1, d), bmap),
                  pl.BlockSpec((d, ROUTE_LANES), lambda m: (0, 0)),
                  pl.BlockSpec((1, ROUTE_LANES), lambda m: (0, 0))],
        out_specs=[pl.BlockSpec((tm, d), lambda m: (m, 0)),
                   pl.BlockSpec((tm * ROW_SUB, LANES), lambda m: (m, 0)),
                   pl.BlockSpec((tm, ROUTE_LANES), lambda m: (m, 0)),
                   pl.BlockSpec((1, ROUTE_LANES), lambda m: (0, 0))],
        out_shape=[jax.ShapeDtypeStruct((t, d), jnp.float32),
                   jax.ShapeDtypeStruct((t * ROW_SUB, LANES), jnp.float32),
                   jax.ShapeDtypeStruct((t, ROUTE_LANES), jnp.float32),
                   jax.ShapeDtypeStruct((1, ROUTE_LANES), jnp.float32)],
        scratch_shapes=[pltpu.VMEM((1, ROUTE_LANES), jnp.float32),
                        pltpu.VMEM((tm, tm), jnp.bfloat16),
                        pltpu.VMEM(w_out.shape, jnp.bfloat16),
                        pltpu.VMEM((2, STAGE_ROWS, d), jnp.float32),
                        pltpu.SemaphoreType.DMA((2,))],
        compiler_params=_cparams(("arbitrary",)),
        name="out",
    )(mix_c, mix_a, w_out, x2, gt1, g_ffn, sc2, sh2, w_route, b_route)


def _slot(off_ref, e_ref, r_ref, tok):
    return off_ref[e_ref[tok]] + r_ref[tok]


def _dispatch_kernel(e1_ref, e2_ref, r1_ref, r2_ref, off_ref, zs_ref, zc_ref, nu_ref, h2_ref, xs_ref,
                     zbuf, sem_z, sem_r, *, tm, n_tiles):
    i = pl.program_id(0)

    @pl.when(i == 0)
    def _():
        zbuf[...] = jnp.zeros(zbuf.shape, zbuf.dtype)

        def zero_copy(start):
            rows = pl.ds(pl.multiple_of(start * ROW_SUB, ZERO_ROWS * ROW_SUB), ZERO_ROWS * ROW_SUB)
            return pltpu.make_async_copy(zbuf, xs_ref.at[rows], sem_z)

        def zero_fill(first_row, n_granules):
            def start(k, carry):
                zero_copy(first_row + k * ZERO_ROWS).start()
                return carry

            def wait(k, carry):
                zero_copy(first_row + k * ZERO_ROWS).wait()
                return carry

            lax.fori_loop(0, n_granules, start, 0)
            lax.fori_loop(0, n_granules, wait, 0)

        for e in range(N_EXPERTS):
            zero_fill(zs_ref[e], zc_ref[e])
        zero_fill(nu_ref[0] * MOE_TM, (n_tiles - nu_ref[0]) * (MOE_TM // ZERO_ROWS))

    def issue_row(r, carry):
        tok = i * tm + r
        for e_ref, r_ref in ((e1_ref, r1_ref), (e2_ref, r2_ref)):
            pltpu.make_async_copy(_row(h2_ref, r), _row(xs_ref, _slot(off_ref, e_ref, r_ref, tok)),
                                  sem_r).start()
        return carry

    lax.fori_loop(0, tm, issue_row, 0)
    for _ in range(2):
        pltpu.make_async_copy(h2_ref, xs_ref.at[pl.ds(0, tm * ROW_SUB)], sem_r).wait()


def _dispatch(plan, h2, n_rows, tm=512):
    t = h2.shape[0] // ROW_SUB
    kern = functools.partial(_dispatch_kernel, tm=tm, n_tiles=n_rows // MOE_TM)
    return pl.pallas_call(
        kern,
        grid_spec=pltpu.PrefetchScalarGridSpec(
            num_scalar_prefetch=len(plan),
            grid=(t // tm,),
            in_specs=[pl.BlockSpec((tm * ROW_SUB, LANES), lambda i, *_: (i, 0))],
            out_specs=pl.BlockSpec(memory_space=pl.ANY),
            scratch_shapes=[pltpu.VMEM((ZERO_ROWS * ROW_SUB, LANES), h2.dtype),
                            pltpu.SemaphoreType.DMA(()),
                            pltpu.SemaphoreType.DMA(())]),
        out_shape=jax.ShapeDtypeStruct((n_rows * ROW_SUB, LANES), h2.dtype),
        compiler_params=_cparams(("arbitrary",)),
        name="dispatch",
    )(*plan, h2)


def _mlp_kernel(te_ref, nu_ref, first_ref, nxt_ref, slot_ref, xs_ref, wg_hbm, wu_hbm, wd_hbm, ys_ref,
                wg_f32, wu_f32, wd_f32, wg_bf, wu_bf, wd_bf, sem):
    i = pl.program_id(0)

    def copies(e, s):
        return [pltpu.make_async_copy(hbm.at[e], buf.at[s], sem.at[s])
                for hbm, buf in ((wg_hbm, wg_f32), (wu_hbm, wu_f32), (wd_hbm, wd_f32))]

    @pl.when(i == 0)
    def _():
        for cp in copies(te_ref[0], 0):
            cp.start()

    @pl.when((i < nu_ref[0]) & (first_ref[i] == 1))
    def _():
        s = slot_ref[i]
        for cp in copies(te_ref[i], s):
            cp.wait()
        wg_bf[...] = wg_f32[s].astype(jnp.bfloat16)
        wu_bf[...] = wu_f32[s].astype(jnp.bfloat16)
        wd_bf[...] = wd_f32[s].astype(jnp.bfloat16)

        @pl.when(nxt_ref[i] >= 0)
        def _():
            for cp in copies(nxt_ref[i], 1 - s):
                cp.start()

    @pl.when(i < nu_ref[0])
    def _():
        x = _rows_to_2d(xs_ref, MOE_TM).astype(jnp.bfloat16)
        a = jnp.dot(x, wg_bf[...], preferred_element_type=jnp.float32)
        u = jnp.dot(x, wu_bf[...], preferred_element_type=jnp.float32)
        hid = (a * jax.nn.sigmoid(a)) * u
        _rows_from_2d(ys_ref, jnp.dot(hid.astype(jnp.bfloat16), wd_bf[...],
                                      preferred_element_type=jnp.float32))

    @pl.when(i >= nu_ref[0])
    def _():
        ys_ref[...] = jnp.zeros(ys_ref.shape, ys_ref.dtype)


def _mlp(tiles, xs, w_gate, w_up, w_down):
    n_rows = xs.shape[0] // ROW_SUB
    _, d, f = w_gate.shape
    used = lambda i, te, nu, *_: (jnp.minimum(i, nu[0] - 1), 0)
    return pl.pallas_call(
        _mlp_kernel,
        grid_spec=pltpu.PrefetchScalarGridSpec(
            num_scalar_prefetch=len(tiles),
            grid=(n_rows // MOE_TM,),
            in_specs=[pl.BlockSpec((MOE_TM * ROW_SUB, LANES), used),
                      pl.BlockSpec(memory_space=pl.ANY),
                      pl.BlockSpec(memory_space=pl.ANY),
                      pl.BlockSpec(memory_space=pl.ANY)],
            out_specs=pl.BlockSpec((MOE_TM * ROW_SUB, LANES), lambda i, *_: (i, 0)),
            scratch_shapes=[pltpu.VMEM((2, d, f), jnp.float32),
                            pltpu.VMEM((2, d, f), jnp.float32),
                            pltpu.VMEM((2, f, d), jnp.float32),
                            pltpu.VMEM((d, f), jnp.bfloat16),
                            pltpu.VMEM((d, f), jnp.bfloat16),
                            pltpu.VMEM((f, d), jnp.bfloat16),
                            pltpu.SemaphoreType.DMA((2,))]),
        out_shape=jax.ShapeDtypeStruct((n_rows * ROW_SUB, LANES), jnp.float32),
        compiler_params=_cparams(("arbitrary",)),
        name="mlp",
    )(*tiles, xs, w_gate, w_up, w_down)


def _combine_kernel(e1_ref, e2_ref, r1_ref, r2_ref, off_ref, ys_ref, x1_ref, route_ref, gt_ref, g_ref,
                    sc_ref, sh_ref, o_ref, a0, b0, a1, b1, sem, *, tm, n_tiles):
    i = pl.program_id(0)
    bufs = ((a0, b0), (a1, b1))

    def issue(tile, sl):
        def issue_row(r, carry):
            tok = tile * tm + r
            for buf, e_ref, r_ref in ((bufs[sl][0], e1_ref, r1_ref), (bufs[sl][1], e2_ref, r2_ref)):
                pltpu.make_async_copy(_row(ys_ref, _slot(off_ref, e_ref, r_ref, tok)),
                                      _row(buf, r), sem.at[sl]).start()
            return carry

        lax.fori_loop(0, tm, issue_row, 0)

    @pl.when(i == 0)
    def _():
        issue(0, 0)

    for sl in range(2):
        @pl.when(i % 2 == sl)
        def _(sl=sl):
            @pl.when(i + 1 < n_tiles)
            def _():
                issue(i + 1, 1 - sl)

            for buf in bufs[sl]:
                pltpu.make_async_copy(ys_ref.at[pl.ds(0, tm * ROW_SUB)], buf, sem.at[sl]).wait()

            route = route_ref[...]
            y = (route[:, R_W1:R_W1 + 1] * _rows_to_2d(bufs[sl][0], tm)
                 + route[:, R_W2:R_W2 + 1] * _rows_to_2d(bufs[sl][1], tm))
            x = x1_ref[...] + gt_ref[0] * y
            o_ref[...] = _rms(x) * g_ref[...] * (1.0 + sc_ref[0]) + sh_ref[0]


def _combine(plan, ys, x1, route, gt2, g_final, scf, shf, seq, tm=512):
    t, d = x1.shape
    per_b = seq // tm
    n_tiles = t // tm
    bmap = lambda m, *_: (m // per_b, 0, 0)
    kern = functools.partial(_combine_kernel, tm=tm, n_tiles=n_tiles)
    return pl.pallas_call(
        kern,
        grid_spec=pltpu.PrefetchScalarGridSpec(
            num_scalar_prefetch=len(plan),
            grid=(n_tiles,),
            in_specs=[pl.BlockSpec(memory_space=pl.ANY),
                      pl.BlockSpec((tm, d), lambda m, *_: (m, 0)),
                      pl.BlockSpec((tm, ROUTE_LANES), lambda m, *_: (m, 0)),
                      pl.BlockSpec((1, 1, d), bmap),
                      pl.BlockSpec((1, d), lambda m, *_: (0, 0)),
                      pl.BlockSpec((1, 1, d), bmap),
                      pl.BlockSpec((1, 1, d), bmap)],
            out_specs=pl.BlockSpec((tm, d), lambda m, *_: (m, 0)),
            scratch_shapes=[pltpu.VMEM((tm * ROW_SUB, LANES), jnp.float32) for _ in range(4)]
            + [pltpu.SemaphoreType.DMA((2,))]),
        out_shape=jax.ShapeDtypeStruct((t, d), jnp.float32),
        compiler_params=_cparams(("arbitrary",)),
        name="combine",
    )(*plan, ys, x1, route, gt2, g_final, scf, shf)


def _moe_plan(route, counts, t):
    i32 = jnp.int32
    cnt = counts[0, N_GROUPS:N_GROUPS + N_EXPERTS].astype(i32)
    padded = (cnt + MOE_TM - 1) // MOE_TM * MOE_TM
    off_end = jnp.cumsum(padded)
    off = off_end - padded
    n_tiles = (2 * t + N_EXPERTS * MOE_TM) // MOE_TM
    n_used = off_end[-1] // MOE_TM
    tile_start = jnp.arange(n_tiles, dtype=i32) * MOE_TM
    te = jnp.sum((off_end[None, :] <= tile_start[:, None]).astype(i32), axis=1)
    te = jnp.minimum(te, N_EXPERTS - 1)
    tile = jnp.arange(n_tiles, dtype=i32)
    te = jnp.where(tile < n_used, te, te[n_used - 1])
    first = jnp.concatenate([jnp.ones((1,), i32), (te[1:] != te[:-1]).astype(i32)])
    nxt_tile = off_end[te] // MOE_TM
    nxt = jnp.where(nxt_tile < n_used, te[jnp.minimum(nxt_tile, n_tiles - 1)], -1)
    slot = (jnp.cumsum(first) - 1) % 2
    assign = tuple(route[:, k].astype(i32) for k in (R_E1, R_E2, R_RANK1, R_RANK2))
    zstart = (off + cnt) // ZERO_ROWS * ZERO_ROWS
    zcount = (off_end - zstart) // ZERO_ROWS
    n_used = n_used.reshape(1)
    return assign, off, (zstart, zcount), (te, n_used, first, nxt, slot), n_tiles * MOE_TM


def kernel(x, c, w_ada, b_ada, g_mix, w_in, conv_w, w_uk, kv_norm_g, w_uv, g_conv_out, g_attn_out,
           w_out, g_ffn, w_rg, b_rg, w_re, b_re, w_gate, w_up, w_down, w_ada_f, b_ada_f, g_final):
    b, s, d = x.shape
    assert w_ada.shape[0] == 1, "single layer"
    bf = jnp.bfloat16
    x2 = x.reshape(b * s, d)

    c_pad = jnp.zeros((SUBLANES, d), jnp.float32).at[:b].set(c)
    mod = _mod(c_pad, w_ada[0], b_ada[0])[:b]
    modf = _mod(c_pad, w_ada_f, b_ada_f)[:b]
    vec = lambda a, i: a[:, i * d:(i + 1) * d].reshape(b, 1, d)
    sh1, sc1, gt1, sh2, sc2, gt2 = (vec(mod, i) for i in range(N_MOD))
    shf, scf = vec(modf, 0), vec(modf, 1)
    row = lambda a: a.reshape(1, -1)

    w_in_t = jnp.swapaxes(w_in[0], 0, 1)
    h, q, kv, iq, ikw = _hproj(x2, row(g_mix[0]), sc1, sh1, w_in_t, s)
    mix_c = _projconv(h, w_in_t, conv_w[0], row(g_conv_out[0]), s)

    mix_a = _attn(iq.reshape(b, s, -1), ikw.reshape(b, s, -1), kv.reshape(b, s, -1),
                  q.reshape(b, s, -1), w_uk[0].astype(bf), w_uv[0].astype(bf),
                  row(kv_norm_g[0]), row(g_attn_out[0]))

    w_route = jnp.zeros((d, ROUTE_LANES), bf).at[:, :N_GROUPS].set(w_rg[0].astype(bf))
    w_route = w_route.at[:, N_GROUPS:N_GROUPS + N_EXPERTS].set(w_re[0].astype(bf))
    b_route = jnp.zeros((1, ROUTE_LANES), jnp.float32).at[0, :N_GROUPS].set(b_rg[0])
    b_route = b_route.at[0, N_GROUPS:N_GROUPS + N_EXPERTS].set(b_re[0])
    x1, h2, route, counts = _out(mix_c, mix_a.reshape(b * s, -1),
                                 w_out[0], x2, gt1, row(g_ffn[0]),
                                 sc2, sh2, w_route, b_route, s)

    assign, off, zero, tiles, n_rows = _moe_plan(route, counts, b * s)
    xs = _dispatch((*assign, off, *zero, tiles[1]), h2, n_rows)
    ys = _mlp(tiles, xs, w_gate[0], w_up[0], w_down[0])
    out = _combine((*assign, off), ys, x1, route, gt2, row(g_final), scf, shf, s)
    return out.reshape(b, s, d)
```

```python
import functools

import jax
import jax.numpy as jnp
from jax import lax
from jax.experimental import pallas as pl
from jax.experimental.pallas import tpu as pltpu

D_MODEL = 2048
CONV_DIM = 1024
CONV_WIDTH = 3
N_HEADS = 8
HEAD_DIM = 128
ATTN_DIM = N_HEADS * HEAD_DIM
KV_RANK = 512
IDX_HEADS = 16
IDX_DIM = 128
TOPK_MAX = 256
N_GROUPS = 4
EXPERTS_PER_GROUP = 8
N_EXPERTS = N_GROUPS * EXPERTS_PER_GROUP
EXPERT_FF = 512
N_MOD = 6
EPS = 1e-6

OFF_Q = 3 * CONV_DIM
OFF_KV = OFF_Q + ATTN_DIM
OFF_IQ = OFF_KV + KV_RANK
OFF_IK = OFF_IQ + IDX_HEADS * IDX_DIM
OFF_IW = OFF_IK + IDX_DIM
IN_COLS = OFF_IW + IDX_HEADS

LANES = 128
SUBLANES = 8
VMEM_LIMIT = 56 * 1024 * 1024

IKW_COLS = 256
ATTN_Q = 256
ATTN_KC = 256
ATTN_RB = 256
MASKED = -1e30
TIE_ALL = 2 ** 30
ROUTE_LANES = 128
STAGE_ROWS = 256
OUT_SUB = 2
MOE_TM = 256
ZERO_ROWS = 64
ROW_SUB = D_MODEL // LANES
INT_MIN = -2 ** 31
KEY_NEG_INF = INT_MIN + 0x7FFFFF


def _cparams(sem):
    return pltpu.CompilerParams(dimension_semantics=sem, vmem_limit_bytes=VMEM_LIMIT)


def _rms(v, axis=-1):
    return v * lax.rsqrt(jnp.mean(v * v, axis=axis, keepdims=True) + EPS)


def _tile_lanes(v, n):
    return jnp.concatenate([v] * n, axis=1)


def _rows_to_2d(ref, n):
    return jnp.concatenate([ref[pl.ds(c, n, stride=ROW_SUB), :] for c in range(ROW_SUB)], axis=1)


def _rows_from_2d(ref, val):
    n = val.shape[0]
    for c in range(ROW_SUB):
        ref[pl.ds(c, n, stride=ROW_SUB), :] = val[:, c * LANES:(c + 1) * LANES]


def _row(ref, i):
    return ref.at[pl.ds(pl.multiple_of(i * ROW_SUB, ROW_SUB), ROW_SUB)]


def _load_bf16(w_hbm, pieces, w_bf, stage, sem):
    def copy(i):
        src, n, _ = pieces[i]
        return pltpu.make_async_copy(w_hbm.at[pl.ds(src, n)], stage.at[i % 2, pl.ds(0, n)],
                                     sem.at[i % 2])

    copy(0).start()
    for i, (_, n, dst) in enumerate(pieces):
        if i + 1 < len(pieces):
            copy(i + 1).start()
        copy(i).wait()
        w_bf[pl.ds(dst, n), :] = stage[i % 2, pl.ds(0, n), :].astype(jnp.bfloat16)


def _mod_kernel(c_ref, w_ref, b_ref, o_ref):
    c = c_ref[...]
    ca = (c * jax.nn.sigmoid(c)).astype(jnp.bfloat16)
    o_ref[...] = jnp.dot(ca, w_ref[...].astype(jnp.bfloat16),
                         preferred_element_type=jnp.float32) + b_ref[...]


def _mod(c_pad, w, b, tn=1024):
    d, n = w.shape
    return pl.pallas_call(
        _mod_kernel,
        grid=(n // tn,),
        in_specs=[pl.BlockSpec((c_pad.shape[0], d), lambda j: (0, 0)),
                  pl.BlockSpec((d, tn), lambda j: (0, j)),
                  pl.BlockSpec((1, tn), lambda j: (0, j))],
        out_specs=pl.BlockSpec((c_pad.shape[0], tn), lambda j: (0, j)),
        out_shape=jax.ShapeDtypeStruct((c_pad.shape[0], n), jnp.float32),
        compiler_params=_cparams(("arbitrary",)),
        name="mod",
    )(c_pad, w, b.reshape(1, n))


ATT_COLS = IN_COLS - OFF_Q
ATT_PAD = OFF_IK - OFF_Q + IKW_COLS


def _hproj_kernel(x_ref, g_ref, sc_ref, sh_ref, wt_hbm, h_ref, q_ref, kv_ref, iq_ref, ikw_ref,
                  w_bf, stage, sem):
    @pl.when(pl.program_id(0) == 0)
    def _():
        full = ATT_COLS // STAGE_ROWS
        pieces = [(OFF_Q + k * STAGE_ROWS, STAGE_ROWS, k * STAGE_ROWS) for k in range(full)]
        pieces.append((OFF_Q + full * STAGE_ROWS, ATT_COLS - full * STAGE_ROWS, full * STAGE_ROWS))
        _load_bf16(wt_hbm, pieces, w_bf, stage, sem)
        w_bf[pl.ds(ATT_COLS, ATT_PAD - ATT_COLS), :] = jnp.zeros((ATT_PAD - ATT_COLS, w_bf.shape[1]),
                                                                 jnp.bfloat16)

    h = (_rms(x_ref[...]) * g_ref[...] * (1.0 + sc_ref[0]) + sh_ref[0]).astype(h_ref.dtype)
    h_ref[...] = h
    nt = (((1,), (1,)), ((), ()))
    row0 = 0
    for o_ref in (q_ref, kv_ref, iq_ref, ikw_ref):
        n = o_ref.shape[1]
        o_ref[...] = lax.dot_general(h, w_bf[row0:row0 + n, :], nt,
                                     preferred_element_type=jnp.float32).astype(o_ref.dtype)
        row0 += n


def _hproj(x2, g_mix, sc1, sh1, wt, seq, tm=512):
    t, d = x2.shape
    per_b = seq // tm
    bmap = lambda m: (m // per_b, 0, 0)
    widths = (ATTN_DIM, KV_RANK, IDX_HEADS * IDX_DIM, IKW_COLS)
    dtypes = (jnp.bfloat16, jnp.float32, jnp.bfloat16, jnp.float32)
    assert sum(widths) == ATT_PAD and (ATT_COLS % STAGE_ROWS) % SUBLANES == 0
    return pl.pallas_call(
        _hproj_kernel,
        grid=(t // tm,),
        in_specs=[pl.BlockSpec((tm, d), lambda m: (m, 0)),
                  pl.BlockSpec((1, d), lambda m: (0, 0)),
                  pl.BlockSpec((1, 1, d), bmap),
                  pl.BlockSpec((1, 1, d), bmap),
                  pl.BlockSpec(memory_space=pl.ANY)],
        out_specs=[pl.BlockSpec((tm, d), lambda m: (m, 0))]
        + [pl.BlockSpec((tm, w), lambda m: (m, 0)) for w in widths],
        out_shape=[jax.ShapeDtypeStruct((t, d), jnp.bfloat16)]
        + [jax.ShapeDtypeStruct((t, w), dt) for w, dt in zip(widths, dtypes)],
        scratch_shapes=[pltpu.VMEM((ATT_PAD, d), jnp.bfloat16),
                        pltpu.VMEM((2, STAGE_ROWS, d), jnp.float32),
                        pltpu.SemaphoreType.DMA((2,))],
        compiler_params=_cparams(("arbitrary",)),
        name="hproj",
    )(x2, g_mix, sc1, sh1, wt)


def _projconv_kernel(h_ref, wt_hbm, cw_ref, g_ref, o_ref, w_bf, stage, halo_scr, sem, *, per_b):
    m = pl.program_id(0)

    @pl.when(m == 0)
    def _():
        _load_bf16(wt_hbm, [(r, STAGE_ROWS, r) for r in range(0, w_bf.shape[0], STAGE_ROWS)],
                   w_bf, stage, sem)

    nt = (((1,), (1,)), ((), ()))
    h = h_ref[...]
    bg, cg, xv = (lax.dot_general(h, w_bf[k * CONV_DIM:(k + 1) * CONV_DIM, :], nt,
                                  preferred_element_type=jnp.float32) for k in range(3))
    u = cg * xv
    halo = jnp.where(m % per_b == 0, 0.0, halo_scr[...])
    halo_scr[...] = u[u.shape[0] - SUBLANES:, :]
    row = lax.broadcasted_iota(jnp.int32, u.shape, 0)
    h1 = halo[SUBLANES - 1:SUBLANES, :]
    h2 = halo[SUBLANES - 2:SUBLANES - 1, :]
    u1 = jnp.where(row == 0, h1, pltpu.roll(u, 1, 0))
    u2 = jnp.where(row == 0, h2, jnp.where(row == 1, h1, pltpu.roll(u, 2, 0)))
    w = cw_ref[...]
    y = bg * (w[0:1, :] * u2 + w[1:2, :] * u1 + w[2:3, :] * u)
    o_ref[...] = (_rms(y) * g_ref[...]).astype(o_ref.dtype)


def _projconv(h, wt, conv_w, g_conv_out, seq, tm=1024):
    t, d = h.shape
    c = CONV_DIM
    kern = functools.partial(_projconv_kernel, per_b=seq // tm)
    return pl.pallas_call(
        kern,
        grid=(t // tm,),
        in_specs=[pl.BlockSpec((tm, d), lambda m: (m, 0)),
                  pl.BlockSpec(memory_space=pl.ANY),
                  pl.BlockSpec((CONV_WIDTH, c), lambda m: (0, 0)),
                  pl.BlockSpec((1, c), lambda m: (0, 0))],
        out_specs=pl.BlockSpec((tm, c), lambda m: (m, 0)),
        out_shape=jax.ShapeDtypeStruct((t, c), jnp.bfloat16),
        scratch_shapes=[pltpu.VMEM((3 * c, d), jnp.bfloat16),
                        pltpu.VMEM((2, STAGE_ROWS, d), jnp.float32),
                        pltpu.VMEM((SUBLANES, c), jnp.float32),
                        pltpu.SemaphoreType.DMA((2,))],
        compiler_params=_cparams(("arbitrary",)),
        name="proj_conv",
    )(h, wt, conv_w, g_conv_out)


def _ordered_bits(v):
    return v ^ ((v >> 31) & jnp.int32(0x7FFFFFFF))


def _attn_search(nch, sc_scr, k_sel):
    n_keys = float(nch * ATTN_KC)
    half = ATTN_Q // 2

    def half_step(bit, thr, rows):
        cand = thr + bit
        cand_f = lax.bitcast_convert_type(_ordered_bits(cand), jnp.float32)
        hits = jnp.where(sc_scr[0, rows, :] >= cand_f, 1.0, 0.0)
        for c in range(1, nch):
            hits = hits + jnp.where(sc_scr[c, rows, :] >= cand_f, 1.0, 0.0)
        cnt = jnp.sum(hits, axis=1, keepdims=True)
        cnt = jnp.where(cand < KEY_NEG_INF, n_keys, cnt)
        return jnp.where(cnt >= k_sel, cand, thr)

    def bit_step(i, thrs):
        bit = lax.shift_left(jnp.int32(1), 31 - i)
        return tuple(half_step(bit, thr, pl.ds(k * half, half)) for k, thr in enumerate(thrs))

    init = jnp.full((half, 1), INT_MIN, jnp.int32)
    thrs = lax.fori_loop(0, 32, bit_step, (init, init), unroll=2)
    thr = jnp.concatenate(thrs, axis=0)
    return lax.bitcast_convert_type(_ordered_bits(thr), jnp.float32)


def _attn_tiebreak(nch, sc_scr, thr_f, k_sel, tie_scr):
    kloc = lax.broadcasted_iota(jnp.int32, (ATTN_Q, ATTN_KC), 1)

    def count(pred):
        hits = pred(sc_scr[0], 0)
        for c in range(1, nch):
            hits = hits + pred(sc_scr[c], c)
        return jnp.sum(hits, axis=1, keepdims=True)

    tie_scr[...] = jnp.full(tie_scr.shape, TIE_ALL, jnp.int32)
    n_ge = count(lambda sc, c: jnp.where(sc >= thr_f, 1.0, 0.0))
    excess = jnp.where(thr_f > -jnp.inf, n_ge - k_sel, 0.0)

    @pl.when(jnp.max(excess) > 0.0)
    def _():
        need = k_sel - count(lambda sc, c: jnp.where(sc > thr_f, 1.0, 0.0))
        n_bits = (nch * ATTN_KC - 1).bit_length()

        def step(i, lim):
            cand = lim + lax.shift_left(jnp.int32(1), n_bits - 1 - i)
            below = count(lambda sc, c: jnp.where(
                sc == thr_f, jnp.where(c * ATTN_KC + kloc < cand, 1.0, 0.0), 0.0))
            return jnp.where(below < need, cand, lim)

        lim = lax.fori_loop(0, n_bits, step, jnp.zeros((ATTN_Q, 1), jnp.int32))
        tie_scr[...] = jnp.where(excess > 0.0, lim, TIE_ALL)


def _attn_kernel(iq_ref, ikwq_ref, ikwa_ref, kv_ref, q_ref, wuk_ref, wuv_ref, kvg_ref, go_ref,
                 o_ref, ckv_scr, ik_scr, sc_scr, thr_scr, tie_scr, ql_scr, m_scr, l_scr, acc_scr, y_scr,
                 *, seq, k_sel):
    j = pl.program_id(1)
    nch = j + 1
    nt = (((1,), (1,)), ((), ()))

    @pl.when(j == 0)
    def _():
        ckv_scr[...] = (_rms(kv_ref[0]) * kvg_ref[...]).astype(jnp.bfloat16)
        ik_scr[...] = ikwa_ref[0][:, :IDX_DIM].astype(jnp.bfloat16)

    iw = ikwq_ref[0][:, IDX_DIM:IDX_DIM + IDX_HEADS] * (IDX_HEADS ** -0.5 * IDX_DIM ** -0.5)
    qpos = j * ATTN_Q + lax.broadcasted_iota(jnp.int32, (ATTN_Q, ATTN_KC), 0)
    kloc = lax.broadcasted_iota(jnp.int32, (ATTN_Q, ATTN_KC), 1)

    def score_chunk(c, carry):
        k0 = pl.multiple_of(c * ATTN_KC, ATTN_KC)
        ik_c = ik_scr[pl.ds(k0, ATTN_KC), :]
        score = jnp.zeros((ATTN_Q, ATTN_KC), jnp.float32)
        for h in range(IDX_HEADS):
            s = lax.dot_general(iq_ref[0][:, h * IDX_DIM:(h + 1) * IDX_DIM], ik_c, nt,
                                preferred_element_type=jnp.float32)
            score = score + jnp.maximum(s, 0.0) * iw[:, h:h + 1]
        sc_scr[c] = jnp.where(k0 + kloc <= qpos, score, -jnp.inf)
        return carry

    lax.fori_loop(0, nch, score_chunk, 0)

    for v in range(seq // ATTN_KC):
        @pl.when(j == v)
        def _(v=v):
            thr_scr[...] = _attn_search(v + 1, sc_scr, k_sel)
            _attn_tiebreak(v + 1, sc_scr, thr_scr[...], k_sel, tie_scr)

    for h in range(N_HEADS):
        ql = lax.dot_general(q_ref[0][:, h * HEAD_DIM:(h + 1) * HEAD_DIM], wuk_ref[h], nt,
                             preferred_element_type=jnp.float32)
        ql_scr[h * ATTN_Q:(h + 1) * ATTN_Q, :] = ql.astype(jnp.bfloat16)
    thr_f = thr_scr[...]
    tie = tie_scr[...]
    scale = HEAD_DIM ** -0.5

    def attn_chunk(c, first):
        k0 = pl.multiple_of(c * ATTN_KC, ATTN_KC)
        ckv_c = ckv_scr[pl.ds(k0, ATTN_KC), :]
        sc = sc_scr[c]
        kpos = k0 + kloc
        tied = jnp.where(sc == thr_f, jnp.where(kpos <= tie, 0.0, MASKED), MASKED)
        bias = jnp.where(kpos <= qpos, jnp.where(sc > thr_f, 0.0, tied), MASKED)
        for u in range(N_HEADS * ATTN_Q // ATTN_RB):
            rows = pl.ds(u * ATTN_RB, ATTN_RB)
            qrow = (u * ATTN_RB) % ATTN_Q
            lg = lax.dot_general(ql_scr[rows, :], ckv_c, nt,
                                 preferred_element_type=jnp.float32) * scale + bias[qrow:qrow + ATTN_RB]
            row_max = jnp.max(lg, axis=1, keepdims=True)
            if first:
                m_new = jnp.maximum(jnp.full((ATTN_RB, LANES), MASKED, jnp.float32), row_max)
            else:
                m_old = m_scr[rows, :]
                m_new = jnp.maximum(m_old, row_max)
                alpha = jnp.exp(m_old - m_new)
            p = jnp.exp(lg - _tile_lanes(m_new, ATTN_KC // LANES))
            p_sum = jnp.sum(p, axis=1, keepdims=True)
            pv = jnp.dot(p.astype(jnp.bfloat16), ckv_c, preferred_element_type=jnp.float32)
            if first:
                l_scr[rows, :] = jnp.broadcast_to(p_sum, (ATTN_RB, LANES))
                acc_scr[rows, :] = pv
            else:
                l_scr[rows, :] = alpha * l_scr[rows, :] + p_sum
                acc_scr[rows, :] = _tile_lanes(alpha, KV_RANK // LANES) * acc_scr[rows, :] + pv
            m_scr[rows, :] = m_new

    attn_chunk(0, True)

    def later_chunk(c, carry):
        attn_chunk(c, False)
        return carry

    lax.fori_loop(1, nch, later_chunk, 0)

    o = (acc_scr[...] * _tile_lanes(1.0 / l_scr[...], KV_RANK // LANES)).astype(jnp.bfloat16)
    for h in range(N_HEADS):
        y_scr[:, h * HEAD_DIM:(h + 1) * HEAD_DIM] = jnp.dot(
            o[h * ATTN_Q:(h + 1) * ATTN_Q], wuv_ref[h], preferred_element_type=jnp.float32)
    o_ref[0] = (_rms(y_scr[...]) * go_ref[...]).astype(o_ref.dtype)


def _attn(iq, ikw, kv, q, w_uk_bf, w_uv_bf, kv_norm_g, g_attn_out):
    b, s, _ = iq.shape
    assert ATTN_Q == ATTN_KC and s % ATTN_Q == 0
    k_sel = min(TOPK_MAX, s // 4)
    rows = N_HEADS * ATTN_Q
    kern = functools.partial(_attn_kernel, seq=s, k_sel=k_sel)
    return pl.pallas_call(
        kern,
        grid=(b, s // ATTN_Q),
        in_specs=[pl.BlockSpec((1, ATTN_Q, IDX_HEADS * IDX_DIM), lambda bi, j: (bi, j, 0)),
                  pl.BlockSpec((1, ATTN_Q, IKW_COLS), lambda bi, j: (bi, j, 0)),
                  pl.BlockSpec((1, s, IKW_COLS), lambda bi, j: (bi, 0, 0)),
                  pl.BlockSpec((1, s, KV_RANK), lambda bi, j: (bi, 0, 0)),
                  pl.BlockSpec((1, ATTN_Q, ATTN_DIM), lambda bi, j: (bi, j, 0)),
                  pl.BlockSpec((N_HEADS, KV_RANK, HEAD_DIM), lambda bi, j: (0, 0, 0)),
                  pl.BlockSpec((N_HEADS, KV_RANK, HEAD_DIM), lambda bi, j: (0, 0, 0)),
                  pl.BlockSpec((1, KV_RANK), lambda bi, j: (0, 0)),
                  pl.BlockSpec((1, ATTN_DIM), lambda bi, j: (0, 0))],
        out_specs=pl.BlockSpec((1, ATTN_Q, ATTN_DIM), lambda bi, j: (bi, j, 0)),
        out_shape=jax.ShapeDtypeStruct((b, s, ATTN_DIM), jnp.bfloat16),
        scratch_shapes=[pltpu.VMEM((s, KV_RANK), jnp.bfloat16),
                        pltpu.VMEM((s, IDX_DIM), jnp.bfloat16),
                        pltpu.VMEM((s // ATTN_KC, ATTN_Q, ATTN_KC), jnp.float32),
                        pltpu.VMEM((ATTN_Q, 1), jnp.float32),
                        pltpu.VMEM((ATTN_Q, 1), jnp.int32),
                        pltpu.VMEM((rows, KV_RANK), jnp.bfloat16),
                        pltpu.VMEM((rows, LANES), jnp.float32),
                        pltpu.VMEM((rows, LANES), jnp.float32),
                        pltpu.VMEM((rows, KV_RANK), jnp.float32),
                        pltpu.VMEM((ATTN_Q, ATTN_DIM), jnp.float32)],
        compiler_params=_cparams(("arbitrary", "arbitrary")),
        name="attn",
    )(iq, ikw, ikw, kv, q, w_uk_bf, w_uv_bf, kv_norm_g, g_attn_out)


R_E1, R_E2, R_W1, R_W2, R_RANK1, R_RANK2 = range(6)


def _out_kernel(mc_ref, ma_ref, w_hbm, x_ref, gt_ref, g_ref, sc_ref, sh_ref, wr_ref, br_ref,
                x1_ref, h2_ref, route_ref, routet_ref, cnt_ref, cnt_scr, tri_scr, w_bf, stage, sem):
    @pl.when(pl.program_id(0) == 0)
    def _():
        cnt_scr[...] = jnp.zeros(cnt_scr.shape, jnp.float32)
        earlier = (lax.broadcasted_iota(jnp.int32, tri_scr.shape, 1)
                   < lax.broadcasted_iota(jnp.int32, tri_scr.shape, 0))
        tri_scr[...] = jnp.where(earlier, 1.0, 0.0).astype(tri_scr.dtype)
        _load_bf16(w_hbm, [(r, STAGE_ROWS, r) for r in range(0, w_bf.shape[0], STAGE_ROWS)],
                   w_bf, stage, sem)

    rs = x_ref.shape[0] // OUT_SUB
    for sub in range(OUT_SUB):
        rows = pl.ds(sub * rs, rs)
        _out_rows(mc_ref[rows, :], ma_ref[rows, :], w_bf, x_ref[rows, :], gt_ref, g_ref, sc_ref, sh_ref,
                  wr_ref, br_ref, x1_ref.at[rows], h2_ref.at[pl.ds(sub * rs * ROW_SUB, rs * ROW_SUB)],
                  route_ref.at[rows], routet_ref.at[:, rows], cnt_scr, tri_scr[0:rs, 0:rs])
    cnt_ref[...] = cnt_scr[...]


def _out_rows(mc, ma, w_bf, x, gt_ref, g_ref, sc_ref, sh_ref, wr_ref, br_ref,
              x1_ref, h2_ref, route_ref, routet_ref, cnt_scr, tri):
    mix = (jnp.dot(mc, w_bf[0:CONV_DIM, :], preferred_element_type=jnp.float32)
           + jnp.dot(ma, w_bf[CONV_DIM:, :], preferred_element_type=jnp.float32))
    x1 = x + gt_ref[0] * mix
    x1_ref[...] = x1
    h2 = _rms(x1) * g_ref[...] * (1.0 + sc_ref[0]) + sh_ref[0]
    _rows_from_2d(h2_ref, h2)

    logit = jnp.dot(h2.astype(jnp.bfloat16), wr_ref[...],
                    preferred_element_type=jnp.float32) + br_ref[...]
    lane = lax.broadcasted_iota(jnp.int32, logit.shape, 1).astype(jnp.float32)
    neg = -jnp.inf
    big = float(ROUTE_LANES)
    is_g = lane < N_GROUPS
    gl = jnp.where(is_g, logit, neg)
    gmax = jnp.max(gl, axis=1, keepdims=True)
    p_group = 1.0 / jnp.sum(jnp.exp(gl - gmax), axis=1, keepdims=True)
    g_sel = jnp.min(jnp.where(is_g & (gl == gmax), lane, big), axis=1, keepdims=True)
    lo = N_GROUPS + g_sel * EXPERTS_PER_GROUP
    in_grp = (lane >= lo) & (lane < lo + EXPERTS_PER_GROUP)
    el = jnp.where(in_grp, logit, neg)
    m1 = jnp.max(el, axis=1, keepdims=True)
    i1 = jnp.min(jnp.where(in_grp & (el == m1), lane, big), axis=1, keepdims=True)
    el2 = jnp.where(lane == i1, neg, el)
    m2 = jnp.max(el2, axis=1, keepdims=True)
    i2 = jnp.min(jnp.where(in_grp & (lane != i1) & (el2 == m2), lane, big), axis=1, keepdims=True)
    r = jnp.exp(m2 - m1)
    w1 = p_group / (1.0 + r)
    w2 = p_group * r / (1.0 + r)

    member = jnp.where(lane == i1, 1.0, jnp.where(lane == i2, 1.0, 0.0))
    before = jnp.dot(tri, member.astype(jnp.bfloat16),
                     preferred_element_type=jnp.float32) + cnt_scr[...]
    rank1 = jnp.sum(jnp.where(lane == i1, before, 0.0), axis=1, keepdims=True)
    rank2 = jnp.sum(jnp.where(lane == i2, before, 0.0), axis=1, keepdims=True)
    cnt_scr[...] = cnt_scr[...] + jnp.sum(member, axis=0, keepdims=True)

    route = jnp.zeros_like(logit)
    for idx, val in ((R_E1, i1 - N_GROUPS), (R_E2, i2 - N_GROUPS), (R_W1, w1), (R_W2, w2),
                     (R_RANK1, rank1), (R_RANK2, rank2)):
        route = jnp.where(lane == idx, val, route)
    route_ref[...] = route
    routet_ref[...] = route.T[0:SUBLANES, :]


def _out(mix_c, mix_a, w_out, x2, gt1, g_ffn, sc2, sh2, w_route, b_route, seq, tm=512):
    t, d = x2.shape
    per_b = seq // tm
    bmap = lambda m: (m // per_b, 0, 0)
    return pl.pallas_call(
        _out_kernel,
        grid=(t // tm,),
        in_specs=[pl.BlockSpec((tm, CONV_DIM), lambda m: (m, 0)),
                  pl.BlockSpec((tm, ATTN_DIM), lambda m: (m, 0)),
                  pl.BlockSpec(memory_space=pl.ANY),
                  pl.BlockSpec((tm, d), lambda m: (m, 0)),
                  pl.BlockSpec((1, 1, d), bmap),
                  pl.BlockSpec((1, d), lambda m: (0, 0)),
                  pl.BlockSpec((1, 1, d), bmap),
                  pl.BlockSpec((1, 1, d), bmap),
                  pl.BlockSpec((d, ROUTE_LANES), lambda m: (0, 0)),
                  pl.BlockSpec((1, ROUTE_LANES), lambda m: (0, 0))],
        out_specs=[pl.BlockSpec((tm, d), lambda m: (m, 0)),
                   pl.BlockSpec((tm * ROW_SUB, LANES), lambda m: (m, 0)),
                   pl.BlockSpec((tm, ROUTE_LANES), lambda m: (m, 0)),
                   pl.BlockSpec((SUBLANES, tm), lambda m: (0, m)),
                   pl.BlockSpec((1, ROUTE_LANES), lambda m: (0, 0))],
        out_shape=[jax.ShapeDtypeStruct((t, d), jnp.float32),
                   jax.ShapeDtypeStruct((t * ROW_SUB, LANES), jnp.float32),
                   jax.ShapeDtypeStruct((t, ROUTE_LANES), jnp.float32),
                   jax.ShapeDtypeStruct((SUBLANES, t), jnp.float32),
                   jax.ShapeDtypeStruct((1, ROUTE_LANES), jnp.float32)],
        scratch_shapes=[pltpu.VMEM((1, ROUTE_LANES), jnp.float32),
                        pltpu.VMEM((tm, tm), jnp.bfloat16),
                        pltpu.VMEM(w_out.shape, jnp.bfloat16),
                        pltpu.VMEM((2, STAGE_ROWS, d), jnp.float32),
                        pltpu.SemaphoreType.DMA((2,))],
        compiler_params=_cparams(("arbitrary",)),
        name="out",
    )(mix_c, mix_a, w_out, x2, gt1, g_ffn, sc2, sh2, w_route, b_route)


def _slot(off_ref, e_ref, r_ref, tok):
    return off_ref[e_ref[tok]] + r_ref[tok]


def _dispatch_kernel(e1_ref, e2_ref, r1_ref, r2_ref, off_ref, zs_ref, zc_ref, nu_ref, h2_ref, xs_ref,
                     zbuf, sem_z, sem_r, *, tm, n_tiles):
    i = pl.program_id(0)

    @pl.when(i == 0)
    def _():
        zbuf[...] = jnp.zeros(zbuf.shape, zbuf.dtype)

        def zero_copy(start):
            rows = pl.ds(pl.multiple_of(start * ROW_SUB, ZERO_ROWS * ROW_SUB), ZERO_ROWS * ROW_SUB)
            return pltpu.make_async_copy(zbuf, xs_ref.at[rows], sem_z)

        def zero_fill(first_row, n_granules):
            def start(k, carry):
                zero_copy(first_row + k * ZERO_ROWS).start()
                return carry

            def wait(k, carry):
                zero_copy(first_row + k * ZERO_ROWS).wait()
                return carry

            lax.fori_loop(0, n_granules, start, 0)
            lax.fori_loop(0, n_granules, wait, 0)

        for e in range(N_EXPERTS):
            zero_fill(zs_ref[e], zc_ref[e])
        zero_fill(nu_ref[0] * MOE_TM, (n_tiles - nu_ref[0]) * (MOE_TM // ZERO_ROWS))

    def issue_row(r, carry):
        tok = i * tm + r
        for e_ref, r_ref in ((e1_ref, r1_ref), (e2_ref, r2_ref)):
            pltpu.make_async_copy(_row(h2_ref, r), _row(xs_ref, _slot(off_ref, e_ref, r_ref, tok)),
                                  sem_r).start()
        return carry

    lax.fori_loop(0, tm, issue_row, 0)
    for _ in range(2):
        pltpu.make_async_copy(h2_ref, xs_ref.at[pl.ds(0, tm * ROW_SUB)], sem_r).wait()


def _dispatch(plan, h2, n_rows, tm=512):
    t = h2.shape[0] // ROW_SUB
    kern = functools.partial(_dispatch_kernel, tm=tm, n_tiles=n_rows // MOE_TM)
    return pl.pallas_call(
        kern,
        grid_spec=pltpu.PrefetchScalarGridSpec(
            num_scalar_prefetch=len(plan),
            grid=(t // tm,),
            in_specs=[pl.BlockSpec((tm * ROW_SUB, LANES), lambda i, *_: (i, 0))],
            out_specs=pl.BlockSpec(memory_space=pl.ANY),
            scratch_shapes=[pltpu.VMEM((ZERO_ROWS * ROW_SUB, LANES), h2.dtype),
                            pltpu.SemaphoreType.DMA(()),
                            pltpu.SemaphoreType.DMA(())]),
        out_shape=jax.ShapeDtypeStruct((n_rows * ROW_SUB, LANES), h2.dtype),
        compiler_params=_cparams(("arbitrary",)),
        name="dispatch",
    )(*plan, h2)


def _mlp_kernel(te_ref, nu_ref, first_ref, nxt_ref, slot_ref, xs_ref, wg_hbm, wu_hbm, wd_hbm, ys_ref,
                wg_f32, wu_f32, wd_f32, wg_bf, wu_bf, wd_bf, sem):
    i = pl.program_id(0)

    def copies(e, s):
        return [pltpu.make_async_copy(hbm.at[e], buf.at[s], sem.at[s])
                for hbm, buf in ((wg_hbm, wg_f32), (wu_hbm, wu_f32), (wd_hbm, wd_f32))]

    @pl.when(i == 0)
    def _():
        for cp in copies(te_ref[0], 0):
            cp.start()

    @pl.when((i < nu_ref[0]) & (first_ref[i] == 1))
    def _():
        s = slot_ref[i]
        for cp in copies(te_ref[i], s):
            cp.wait()
        wg_bf[...] = wg_f32[s].astype(jnp.bfloat16)
        wu_bf[...] = wu_f32[s].astype(jnp.bfloat16)
        wd_bf[...] = wd_f32[s].astype(jnp.bfloat16)

        @pl.when(nxt_ref[i] >= 0)
        def _():
            for cp in copies(nxt_ref[i], 1 - s):
                cp.start()

    @pl.when(i < nu_ref[0])
    def _():
        x = _rows_to_2d(xs_ref, MOE_TM).astype(jnp.bfloat16)
        a = jnp.dot(x, wg_bf[...], preferred_element_type=jnp.float32)
        u = jnp.dot(x, wu_bf[...], preferred_element_type=jnp.float32)
        hid = (a * jax.nn.sigmoid(a)) * u
        _rows_from_2d(ys_ref, jnp.dot(hid.astype(jnp.bfloat16), wd_bf[...],
                                      preferred_element_type=jnp.float32))

    @pl.when(i >= nu_ref[0])
    def _():
        ys_ref[...] = jnp.zeros(ys_ref.shape, ys_ref.dtype)


def _mlp(tiles, xs, w_gate, w_up, w_down):
    n_rows = xs.shape[0] // ROW_SUB
    _, d, f = w_gate.shape
    used = lambda i, te, nu, *_: (jnp.minimum(i, nu[0] - 1), 0)
    return pl.pallas_call(
        _mlp_kernel,
        grid_spec=pltpu.PrefetchScalarGridSpec(
            num_scalar_prefetch=len(tiles),
            grid=(n_rows // MOE_TM,),
            in_specs=[pl.BlockSpec((MOE_TM * ROW_SUB, LANES), used),
                      pl.BlockSpec(memory_space=pl.ANY),
                      pl.BlockSpec(memory_space=pl.ANY),
                      pl.BlockSpec(memory_space=pl.ANY)],
            out_specs=pl.BlockSpec((MOE_TM * ROW_SUB, LANES), lambda i, *_: (i, 0)),
            scratch_shapes=[pltpu.VMEM((2, d, f), jnp.float32),
                            pltpu.VMEM((2, d, f), jnp.float32),
                            pltpu.VMEM((2, f, d), jnp.float32),
                            pltpu.VMEM((d, f), jnp.bfloat16),
                            pltpu.VMEM((d, f), jnp.bfloat16),
                            pltpu.VMEM((f, d), jnp.bfloat16),
                            pltpu.SemaphoreType.DMA((2,))]),
        out_shape=jax.ShapeDtypeStruct((n_rows * ROW_SUB, LANES), jnp.float32),
        compiler_params=_cparams(("arbitrary",)),
        name="mlp",
    )(*tiles, xs, w_gate, w_up, w_down)


def _combine_kernel(e1_ref, e2_ref, r1_ref, r2_ref, off_ref, ys_ref, x1_ref, route_ref, gt_ref, g_ref,
                    sc_ref, sh_ref, o_ref, a0, b0, a1, b1, sem, *, tm, n_tiles):
    i = pl.program_id(0)
    bufs = ((a0, b0), (a1, b1))

    def issue(tile, sl):
        def issue_row(r, carry):
            tok = tile * tm + r
            for buf, e_ref, r_ref in ((bufs[sl][0], e1_ref, r1_ref), (bufs[sl][1], e2_ref, r2_ref)):
                pltpu.make_async_copy(_row(ys_ref, _slot(off_ref, e_ref, r_ref, tok)),
                                      _row(buf, r), sem.at[sl]).start()
            return carry

        lax.fori_loop(0, tm, issue_row, 0)

    @pl.when(i == 0)
    def _():
        issue(0, 0)

    for sl in range(2):
        @pl.when(i % 2 == sl)
        def _(sl=sl):
            @pl.when(i + 1 < n_tiles)
            def _():
                issue(i + 1, 1 - sl)

            for buf in bufs[sl]:
                pltpu.make_async_copy(ys_ref.at[pl.ds(0, tm * ROW_SUB)], buf, sem.at[sl]).wait()

            route = route_ref[...]
            y = (route[:, R_W1:R_W1 + 1] * _rows_to_2d(bufs[sl][0], tm)
                 + route[:, R_W2:R_W2 + 1] * _rows_to_2d(bufs[sl][1], tm))
            x = x1_ref[...] + gt_ref[0] * y
            o_ref[...] = _rms(x) * g_ref[...] * (1.0 + sc_ref[0]) + sh_ref[0]


def _combine(plan, ys, x1, route, gt2, g_final, scf, shf, seq, tm=512):
    t, d = x1.shape
    per_b = seq // tm
    n_tiles = t // tm
    bmap = lambda m, *_: (m // per_b, 0, 0)
    kern = functools.partial(_combine_kernel, tm=tm, n_tiles=n_tiles)
    return pl.pallas_call(
        kern,
        grid_spec=pltpu.PrefetchScalarGridSpec(
            num_scalar_prefetch=len(plan),
            grid=(n_tiles,),
            in_specs=[pl.BlockSpec(memory_space=pl.ANY),
                      pl.BlockSpec((tm, d), lambda m, *_: (m, 0)),
                      pl.BlockSpec((tm, ROUTE_LANES), lambda m, *_: (m, 0)),
                      pl.BlockSpec((1, 1, d), bmap),
                      pl.BlockSpec((1, d), lambda m, *_: (0, 0)),
                      pl.BlockSpec((1, 1, d), bmap),
                      pl.BlockSpec((1, 1, d), bmap)],
            out_specs=pl.BlockSpec((tm, d), lambda m, *_: (m, 0)),
            scratch_shapes=[pltpu.VMEM((tm * ROW_SUB, LANES), jnp.float32) for _ in range(4)]
            + [pltpu.SemaphoreType.DMA((2,))]),
        out_shape=jax.ShapeDtypeStruct((t, d), jnp.float32),
        compiler_params=_cparams(("arbitrary",)),
        name="combine",
    )(*plan, ys, x1, route, gt2, g_final, scf, shf)


def _moe_plan(route_t, counts, t):
    i32 = jnp.int32
    cnt = counts[0, N_GROUPS:N_GROUPS + N_EXPERTS].astype(i32)
    padded = (cnt + MOE_TM - 1) // MOE_TM * MOE_TM
    off_end = jnp.cumsum(padded)
    off = off_end - padded
    n_tiles = (2 * t + N_EXPERTS * MOE_TM) // MOE_TM
    n_used = off_end[-1] // MOE_TM
    tile_start = jnp.arange(n_tiles, dtype=i32) * MOE_TM
    te = jnp.sum((off_end[None, :] <= tile_start[:, None]).astype(i32), axis=1)
    te = jnp.minimum(te, N_EXPERTS - 1)
    tile = jnp.arange(n_tiles, dtype=i32)
    te = jnp.where(tile < n_used, te, te[n_used - 1])
    first = jnp.concatenate([jnp.ones((1,), i32), (te[1:] != te[:-1]).astype(i32)])
    nxt_tile = off_end[te] // MOE_TM
    nxt = jnp.where(nxt_tile < n_used, te[jnp.minimum(nxt_tile, n_tiles - 1)], -1)
    slot = (jnp.cumsum(first) - 1) % 2
    assign = tuple(route_t[k].astype(i32) for k in (R_E1, R_E2, R_RANK1, R_RANK2))
    zstart = (off + cnt) // ZERO_ROWS * ZERO_ROWS
    zcount = (off_end - zstart) // ZERO_ROWS
    n_used = n_used.reshape(1)
    return assign, off, (zstart, zcount), (te, n_used, first, nxt, slot), n_tiles * MOE_TM


def kernel(x, c, w_ada, b_ada, g_mix, w_in, conv_w, w_uk, kv_norm_g, w_uv, g_conv_out, g_attn_out,
           w_out, g_ffn, w_rg, b_rg, w_re, b_re, w_gate, w_up, w_down, w_ada_f, b_ada_f, g_final):
    b, s, d = x.shape
    assert w_ada.shape[0] == 1, "single layer"
    bf = jnp.bfloat16
    x2 = x.reshape(b * s, d)

    c_pad = jnp.zeros((SUBLANES, d), jnp.float32).at[:b].set(c)
    mod = _mod(c_pad, w_ada[0], b_ada[0])[:b]
    modf = _mod(c_pad, w_ada_f, b_ada_f)[:b]
    vec = lambda a, i: a[:, i * d:(i + 1) * d].reshape(b, 1, d)
    sh1, sc1, gt1, sh2, sc2, gt2 = (vec(mod, i) for i in range(N_MOD))
    shf, scf = vec(modf, 0), vec(modf, 1)
    row = lambda a: a.reshape(1, -1)

    w_in_t = jnp.swapaxes(w_in[0], 0, 1)
    h, q, kv, iq, ikw = _hproj(x2, row(g_mix[0]), sc1, sh1, w_in_t, s)
    mix_c = _projconv(h, w_in_t, conv_w[0], row(g_conv_out[0]), s)

    mix_a = _attn(iq.reshape(b, s, -1), ikw.reshape(b, s, -1), kv.reshape(b, s, -1),
                  q.reshape(b, s, -1), w_uk[0].astype(bf), w_uv[0].astype(bf),
                  row(kv_norm_g[0]), row(g_attn_out[0]))

    w_route = jnp.zeros((d, ROUTE_LANES), bf).at[:, :N_GROUPS].set(w_rg[0].astype(bf))
    w_route = w_route.at[:, N_GROUPS:N_GROUPS + N_EXPERTS].set(w_re[0].astype(bf))
    b_route = jnp.zeros((1, ROUTE_LANES), jnp.float32).at[0, :N_GROUPS].set(b_rg[0])
    b_route = b_route.at[0, N_GROUPS:N_GROUPS + N_EXPERTS].set(b_re[0])
    x1, h2, route, route_t, counts = _out(mix_c, mix_a.reshape(b * s, -1),
                                 w_out[0], x2, gt1, row(g_ffn[0]),
                                 sc2, sh2, w_route, b_route, s)

    assign, off, zero, tiles, n_rows = _moe_plan(route_t, counts, b * s)
    xs = _dispatch((*assign, off, *zero, tiles[1]), h2, n_rows)
    ys = _mlp(tiles, xs, w_gate[0], w_up[0], w_down[0])
    out = _combine((*assign, off), ys, x1, route, gt2, row(g_final), scf, shf, s)
    return out.reshape(b, s, d)
```

```python
import functools

import jax
import jax.numpy as jnp
from jax import lax
from jax.experimental import pallas as pl
from jax.experimental.pallas import tpu as pltpu

D_MODEL = 2048
CONV_DIM = 1024
CONV_WIDTH = 3
N_HEADS = 8
HEAD_DIM = 128
ATTN_DIM = N_HEADS * HEAD_DIM
KV_RANK = 512
IDX_HEADS = 16
IDX_DIM = 128
TOPK_MAX = 256
N_GROUPS = 4
EXPERTS_PER_GROUP = 8
N_EXPERTS = N_GROUPS * EXPERTS_PER_GROUP
EXPERT_FF = 512
N_MOD = 6
EPS = 1e-6

OFF_Q = 3 * CONV_DIM
OFF_KV = OFF_Q + ATTN_DIM
OFF_IQ = OFF_KV + KV_RANK
OFF_IK = OFF_IQ + IDX_HEADS * IDX_DIM
OFF_IW = OFF_IK + IDX_DIM
IN_COLS = OFF_IW + IDX_HEADS

LANES = 128
SUBLANES = 8
VMEM_LIMIT = 56 * 1024 * 1024

IKW_COLS = 256
ATTN_Q = 256
ATTN_KC = 256
ATTN_RB = 256
MASKED = -1e30
TIE_ALL = 2 ** 30
ROUTE_LANES = 128
STAGE_ROWS = 256
OUT_SUB = 2
MOE_TM = 256
ZERO_ROWS = 64
ROW_SUB = D_MODEL // LANES
INT_MIN = -2 ** 31
KEY_NEG_INF = INT_MIN + 0x7FFFFF


def _cparams(sem):
    return pltpu.CompilerParams(dimension_semantics=sem, vmem_limit_bytes=VMEM_LIMIT)


def _rms(v, axis=-1):
    return v * lax.rsqrt(jnp.mean(v * v, axis=axis, keepdims=True) + EPS)


def _tile_lanes(v, n):
    return jnp.concatenate([v] * n, axis=1)


def _rows_to_2d(ref, n):
    return jnp.concatenate([ref[pl.ds(c, n, stride=ROW_SUB), :] for c in range(ROW_SUB)], axis=1)


def _rows_from_2d(ref, val):
    n = val.shape[0]
    for c in range(ROW_SUB):
        ref[pl.ds(c, n, stride=ROW_SUB), :] = val[:, c * LANES:(c + 1) * LANES]


def _row(ref, i):
    return ref.at[pl.ds(pl.multiple_of(i * ROW_SUB, ROW_SUB), ROW_SUB)]


def _load_bf16(w_hbm, pieces, w_bf, stage, sem):
    def copy(i):
        src, n, _ = pieces[i]
        return pltpu.make_async_copy(w_hbm.at[pl.ds(src, n)], stage.at[i % 2, pl.ds(0, n)],
                                     sem.at[i % 2])

    copy(0).start()
    for i, (_, n, dst) in enumerate(pieces):
        if i + 1 < len(pieces):
            copy(i + 1).start()
        copy(i).wait()
        w_bf[pl.ds(dst, n), :] = stage[i % 2, pl.ds(0, n), :].astype(jnp.bfloat16)


def _mod_kernel(c_ref, w_ref, b_ref, o_ref):
    c = c_ref[...]
    ca = (c * jax.nn.sigmoid(c)).astype(jnp.bfloat16)
    o_ref[...] = jnp.dot(ca, w_ref[...].astype(jnp.bfloat16),
                         preferred_element_type=jnp.float32) + b_ref[...]


def _mod(c_pad, w, b, tn=1024):
    d, n = w.shape
    return pl.pallas_call(
        _mod_kernel,
        grid=(n // tn,),
        in_specs=[pl.BlockSpec((c_pad.shape[0], d), lambda j: (0, 0)),
                  pl.BlockSpec((d, tn), lambda j: (0, j)),
                  pl.BlockSpec((1, tn), lambda j: (0, j))],
        out_specs=pl.BlockSpec((c_pad.shape[0], tn), lambda j: (0, j)),
        out_shape=jax.ShapeDtypeStruct((c_pad.shape[0], n), jnp.float32),
        compiler_params=_cparams(("arbitrary",)),
        name="mod",
    )(c_pad, w, b.reshape(1, n))


ATT_COLS = IN_COLS - OFF_Q
ATT_PAD = OFF_IK - OFF_Q + IKW_COLS


def _hproj_kernel(x_ref, g_ref, sc_ref, sh_ref, wt_hbm, h_ref, q_ref, kv_ref, iq_ref, ikw_ref,
                  w_bf, stage, sem):
    @pl.when(pl.program_id(0) == 0)
    def _():
        full = ATT_COLS // STAGE_ROWS
        pieces = [(OFF_Q + k * STAGE_ROWS, STAGE_ROWS, k * STAGE_ROWS) for k in range(full)]
        pieces.append((OFF_Q + full * STAGE_ROWS, ATT_COLS - full * STAGE_ROWS, full * STAGE_ROWS))
        _load_bf16(wt_hbm, pieces, w_bf, stage, sem)
        w_bf[pl.ds(ATT_COLS, ATT_PAD - ATT_COLS), :] = jnp.zeros((ATT_PAD - ATT_COLS, w_bf.shape[1]),
                                                                 jnp.bfloat16)

    h = (_rms(x_ref[...]) * g_ref[...] * (1.0 + sc_ref[0]) + sh_ref[0]).astype(h_ref.dtype)
    h_ref[...] = h
    nt = (((1,), (1,)), ((), ()))
    row0 = 0
    for o_ref in (q_ref, kv_ref, iq_ref, ikw_ref):
        n = o_ref.shape[1]
        o_ref[...] = lax.dot_general(h, w_bf[row0:row0 + n, :], nt,
                                     preferred_element_type=jnp.float32).astype(o_ref.dtype)
        row0 += n


def _hproj(x2, g_mix, sc1, sh1, wt, seq, tm=512):
    t, d = x2.shape
    per_b = seq // tm
    bmap = lambda m: (m // per_b, 0, 0)
    widths = (ATTN_DIM, KV_RANK, IDX_HEADS * IDX_DIM, IKW_COLS)
    dtypes = (jnp.bfloat16, jnp.float32, jnp.bfloat16, jnp.float32)
    assert sum(widths) == ATT_PAD and (ATT_COLS % STAGE_ROWS) % SUBLANES == 0
    return pl.pallas_call(
        _hproj_kernel,
        grid=(t // tm,),
        in_specs=[pl.BlockSpec((tm, d), lambda m: (m, 0)),
                  pl.BlockSpec((1, d), lambda m: (0, 0)),
                  pl.BlockSpec((1, 1, d), bmap),
                  pl.BlockSpec((1, 1, d), bmap),
                  pl.BlockSpec(memory_space=pl.ANY)],
        out_specs=[pl.BlockSpec((tm, d), lambda m: (m, 0))]
        + [pl.BlockSpec((tm, w), lambda m: (m, 0)) for w in widths],
        out_shape=[jax.ShapeDtypeStruct((t, d), jnp.bfloat16)]
        + [jax.ShapeDtypeStruct((t, w), dt) for w, dt in zip(widths, dtypes)],
        scratch_shapes=[pltpu.VMEM((ATT_PAD, d), jnp.bfloat16),
                        pltpu.VMEM((2, STAGE_ROWS, d), jnp.float32),
                        pltpu.SemaphoreType.DMA((2,))],
        compiler_params=_cparams(("arbitrary",)),
        name="hproj",
    )(x2, g_mix, sc1, sh1, wt)


def _projconv_kernel(h_ref, wt_hbm, cw_ref, g_ref, o_ref, w_bf, stage, halo_scr, sem, *, per_b):
    m = pl.program_id(0)

    @pl.when(m == 0)
    def _():
        _load_bf16(wt_hbm, [(r, STAGE_ROWS, r) for r in range(0, w_bf.shape[0], STAGE_ROWS)],
                   w_bf, stage, sem)

    nt = (((1,), (1,)), ((), ()))
    h = h_ref[...]
    bg, cg, xv = (lax.dot_general(h, w_bf[k * CONV_DIM:(k + 1) * CONV_DIM, :], nt,
                                  preferred_element_type=jnp.float32) for k in range(3))
    u = cg * xv
    halo = jnp.where(m % per_b == 0, 0.0, halo_scr[...])
    halo_scr[...] = u[u.shape[0] - SUBLANES:, :]
    row = lax.broadcasted_iota(jnp.int32, u.shape, 0)
    h1 = halo[SUBLANES - 1:SUBLANES, :]
    h2 = halo[SUBLANES - 2:SUBLANES - 1, :]
    u1 = jnp.where(row == 0, h1, pltpu.roll(u, 1, 0))
    u2 = jnp.where(row == 0, h2, jnp.where(row == 1, h1, pltpu.roll(u, 2, 0)))
    w = cw_ref[...]
    y = bg * (w[0:1, :] * u2 + w[1:2, :] * u1 + w[2:3, :] * u)
    o_ref[...] = (_rms(y) * g_ref[...]).astype(o_ref.dtype)


def _projconv(h, wt, conv_w, g_conv_out, seq, tm=1024):
    t, d = h.shape
    c = CONV_DIM
    kern = functools.partial(_projconv_kernel, per_b=seq // tm)
    return pl.pallas_call(
        kern,
        grid=(t // tm,),
        in_specs=[pl.BlockSpec((tm, d), lambda m: (m, 0)),
                  pl.BlockSpec(memory_space=pl.ANY),
                  pl.BlockSpec((CONV_WIDTH, c), lambda m: (0, 0)),
                  pl.BlockSpec((1, c), lambda m: (0, 0))],
        out_specs=pl.BlockSpec((tm, c), lambda m: (m, 0)),
        out_shape=jax.ShapeDtypeStruct((t, c), jnp.bfloat16),
        scratch_shapes=[pltpu.VMEM((3 * c, d), jnp.bfloat16),
                        pltpu.VMEM((2, STAGE_ROWS, d), jnp.float32),
                        pltpu.VMEM((SUBLANES, c), jnp.float32),
                        pltpu.SemaphoreType.DMA((2,))],
        compiler_params=_cparams(("arbitrary",)),
        name="proj_conv",
    )(h, wt, conv_w, g_conv_out)


def _ordered_bits(v):
    return v ^ ((v >> 31) & jnp.int32(0x7FFFFFFF))


def _attn_search(nch, sc_scr, k_sel):
    n_keys = float(nch * ATTN_KC)
    half = ATTN_Q // 2

    def half_step(bit, thr, rows):
        cand = thr + bit
        cand_f = lax.bitcast_convert_type(_ordered_bits(cand), jnp.float32)
        hits = jnp.where(sc_scr[0, rows, :] >= cand_f, 1.0, 0.0)
        for c in range(1, nch):
            hits = hits + jnp.where(sc_scr[c, rows, :] >= cand_f, 1.0, 0.0)
        cnt = jnp.sum(hits, axis=1, keepdims=True)
        cnt = jnp.where(cand < KEY_NEG_INF, n_keys, cnt)
        return jnp.where(cnt >= k_sel, cand, thr)

    def bit_step(i, thrs):
        bit = lax.shift_left(jnp.int32(1), 31 - i)
        return tuple(half_step(bit, thr, pl.ds(k * half, half)) for k, thr in enumerate(thrs))

    init = jnp.full((half, 1), INT_MIN, jnp.int32)
    thrs = lax.fori_loop(0, 32, bit_step, (init, init), unroll=2)
    thr = jnp.concatenate(thrs, axis=0)
    return lax.bitcast_convert_type(_ordered_bits(thr), jnp.float32)


def _attn_tiebreak(nch, sc_scr, thr_f, k_sel, tie_scr):
    kloc = lax.broadcasted_iota(jnp.int32, (ATTN_Q, ATTN_KC), 1)

    def count(pred):
        hits = pred(sc_scr[0], 0)
        for c in range(1, nch):
            hits = hits + pred(sc_scr[c], c)
        return jnp.sum(hits, axis=1, keepdims=True)

    tie_scr[...] = jnp.full(tie_scr.shape, TIE_ALL, jnp.int32)
    n_ge = count(lambda sc, c: jnp.where(sc >= thr_f, 1.0, 0.0))
    excess = jnp.where(thr_f > -jnp.inf, n_ge - k_sel, 0.0)

    @pl.when(jnp.max(excess) > 0.0)
    def _():
        need = k_sel - count(lambda sc, c: jnp.where(sc > thr_f, 1.0, 0.0))
        n_bits = (nch * ATTN_KC - 1).bit_length()

        def step(i, lim):
            cand = lim + lax.shift_left(jnp.int32(1), n_bits - 1 - i)
            below = count(lambda sc, c: jnp.where(
                sc == thr_f, jnp.where(c * ATTN_KC + kloc < cand, 1.0, 0.0), 0.0))
            return jnp.where(below < need, cand, lim)

        lim = lax.fori_loop(0, n_bits, step, jnp.zeros((ATTN_Q, 1), jnp.int32))
        tie_scr[...] = jnp.where(excess > 0.0, lim, TIE_ALL)


def _attn_kernel(iq_ref, ikwq_ref, ikwa_ref, kv_ref, q_ref, wuk_ref, wuv_ref, kvg_ref, go_ref,
                 o_ref, ckv_scr, ik_scr, sc_scr, thr_scr, tie_scr, ql_scr, m_scr, l_scr, acc_scr, y_scr,
                 *, seq, k_sel):
    j = pl.program_id(1)
    nch = j + 1
    nt = (((1,), (1,)), ((), ()))

    @pl.when(j == 0)
    def _():
        ckv_scr[...] = (_rms(kv_ref[0]) * kvg_ref[...]).astype(jnp.bfloat16)
        ik_scr[...] = ikwa_ref[0][:, :IDX_DIM].astype(jnp.bfloat16)

    iw = ikwq_ref[0][:, IDX_DIM:IDX_DIM + IDX_HEADS] * (IDX_HEADS ** -0.5 * IDX_DIM ** -0.5)
    qpos = j * ATTN_Q + lax.broadcasted_iota(jnp.int32, (ATTN_Q, ATTN_KC), 0)
    kloc = lax.broadcasted_iota(jnp.int32, (ATTN_Q, ATTN_KC), 1)

    def score_chunk(c, carry):
        k0 = pl.multiple_of(c * ATTN_KC, ATTN_KC)
        ik_c = ik_scr[pl.ds(k0, ATTN_KC), :]
        score = jnp.zeros((ATTN_Q, ATTN_KC), jnp.float32)
        for h in range(IDX_HEADS):
            s = lax.dot_general(iq_ref[0][:, h * IDX_DIM:(h + 1) * IDX_DIM], ik_c, nt,
                                preferred_element_type=jnp.float32)
            score = score + jnp.maximum(s, 0.0) * iw[:, h:h + 1]
        sc_scr[c] = jnp.where(k0 + kloc <= qpos, score, -jnp.inf)
        return carry

    lax.fori_loop(0, nch, score_chunk, 0)

    for v in range(seq // ATTN_KC):
        @pl.when(j == v)
        def _(v=v):
            thr_scr[...] = _attn_search(v + 1, sc_scr, k_sel)
            _attn_tiebreak(v + 1, sc_scr, thr_scr[...], k_sel, tie_scr)

    for h in range(N_HEADS):
        ql = lax.dot_general(q_ref[0][:, h * HEAD_DIM:(h + 1) * HEAD_DIM], wuk_ref[h], nt,
                             preferred_element_type=jnp.float32)
        ql_scr[h * ATTN_Q:(h + 1) * ATTN_Q, :] = ql.astype(jnp.bfloat16)
    thr_f = thr_scr[...]
    tie = tie_scr[...]
    scale = HEAD_DIM ** -0.5

    def attn_chunk(c, first):
        k0 = pl.multiple_of(c * ATTN_KC, ATTN_KC)
        ckv_c = ckv_scr[pl.ds(k0, ATTN_KC), :]
        sc = sc_scr[c]
        kpos = k0 + kloc
        tied = jnp.where(sc == thr_f, jnp.where(kpos <= tie, 0.0, MASKED), MASKED)
        bias = jnp.where(kpos <= qpos, jnp.where(sc > thr_f, 0.0, tied), MASKED)
        for u in range(N_HEADS * ATTN_Q // ATTN_RB):
            rows = pl.ds(u * ATTN_RB, ATTN_RB)
            qrow = (u * ATTN_RB) % ATTN_Q
            lg = lax.dot_general(ql_scr[rows, :], ckv_c, nt,
                                 preferred_element_type=jnp.float32) * scale + bias[qrow:qrow + ATTN_RB]
            row_max = jnp.max(lg, axis=1, keepdims=True)
            if first:
                m_new = jnp.maximum(jnp.full((ATTN_RB, LANES), MASKED, jnp.float32), row_max)
            else:
                m_old = m_scr[rows, :]
                m_new = jnp.maximum(m_old, row_max)
                alpha = jnp.exp(m_old - m_new)
            p = jnp.exp(lg - _tile_lanes(m_new, ATTN_KC // LANES))
            p_sum = jnp.sum(p, axis=1, keepdims=True)
            pv = jnp.dot(p.astype(jnp.bfloat16), ckv_c, preferred_element_type=jnp.float32)
            if first:
                l_scr[rows, :] = jnp.broadcast_to(p_sum, (ATTN_RB, LANES))
                acc_scr[rows, :] = pv
            else:
                l_scr[rows, :] = alpha * l_scr[rows, :] + p_sum
                acc_scr[rows, :] = _tile_lanes(alpha, KV_RANK // LANES) * acc_scr[rows, :] + pv
            m_scr[rows, :] = m_new

    attn_chunk(0, True)

    def later_chunk(c, carry):
        attn_chunk(c, False)
        return carry

    lax.fori_loop(1, nch, later_chunk, 0)

    o = (acc_scr[...] * _tile_lanes(1.0 / l_scr[...], KV_RANK // LANES)).astype(jnp.bfloat16)
    for h in range(N_HEADS):
        y_scr[:, h * HEAD_DIM:(h + 1) * HEAD_DIM] = jnp.dot(
            o[h * ATTN_Q:(h + 1) * ATTN_Q], wuv_ref[h], preferred_element_type=jnp.float32)
    o_ref[0] = (_rms(y_scr[...]) * go_ref[...]).astype(o_ref.dtype)


def _attn(iq, ikw, kv, q, w_uk_bf, w_uv_bf, kv_norm_g, g_attn_out):
    b, s, _ = iq.shape
    assert ATTN_Q == ATTN_KC and s % ATTN_Q == 0
    k_sel = min(TOPK_MAX, s // 4)
    rows = N_HEADS * ATTN_Q
    kern = functools.partial(_attn_kernel, seq=s, k_sel=k_sel)
    return pl.pallas_call(
        kern,
        grid=(b, s // ATTN_Q),
        in_specs=[pl.BlockSpec((1, ATTN_Q, IDX_HEADS * IDX_DIM), lambda bi, j: (bi, j, 0)),
                  pl.BlockSpec((1, ATTN_Q, IKW_COLS), lambda bi, j: (bi, j, 0)),
                  pl.BlockSpec((1, s, IKW_COLS), lambda bi, j: (bi, 0, 0)),
                  pl.BlockSpec((1, s, KV_RANK), lambda bi, j: (bi, 0, 0)),
                  pl.BlockSpec((1, ATTN_Q, ATTN_DIM), lambda bi, j: (bi, j, 0)),
                  pl.BlockSpec((N_HEADS, KV_RANK, HEAD_DIM), lambda bi, j: (0, 0, 0)),
                  pl.BlockSpec((N_HEADS, KV_RANK, HEAD_DIM), lambda bi, j: (0, 0, 0)),
                  pl.BlockSpec((1, KV_RANK), lambda bi, j: (0, 0)),
                  pl.BlockSpec((1, ATTN_DIM), lambda bi, j: (0, 0))],
        out_specs=pl.BlockSpec((1, ATTN_Q, ATTN_DIM), lambda bi, j: (bi, j, 0)),
        out_shape=jax.ShapeDtypeStruct((b, s, ATTN_DIM), jnp.bfloat16),
        scratch_shapes=[pltpu.VMEM((s, KV_RANK), jnp.bfloat16),
                        pltpu.VMEM((s, IDX_DIM), jnp.bfloat16),
                        pltpu.VMEM((s // ATTN_KC, ATTN_Q, ATTN_KC), jnp.float32),
                        pltpu.VMEM((ATTN_Q, 1), jnp.float32),
                        pltpu.VMEM((ATTN_Q, 1), jnp.int32),
                        pltpu.VMEM((rows, KV_RANK), jnp.bfloat16),
                        pltpu.VMEM((rows, LANES), jnp.float32),
                        pltpu.VMEM((rows, LANES), jnp.float32),
                        pltpu.VMEM((rows, KV_RANK), jnp.float32),
                        pltpu.VMEM((ATTN_Q, ATTN_DIM), jnp.float32)],
        compiler_params=_cparams(("arbitrary", "arbitrary")),
        name="attn",
    )(iq, ikw, ikw, kv, q, w_uk_bf, w_uv_bf, kv_norm_g, g_attn_out)


R_E1, R_E2, R_W1, R_W2, R_RANK1, R_RANK2 = range(6)


def _out_kernel(mc_ref, ma_ref, w_hbm, x_ref, gt_ref, g_ref, sc_ref, sh_ref, wr_ref, br_ref,
                x1_ref, h2_ref, route_ref, routet_ref, cnt_ref, cnt_scr, tri_scr, w_bf, stage, sem):
    @pl.when(pl.program_id(0) == 0)
    def _():
        cnt_scr[...] = jnp.zeros(cnt_scr.shape, jnp.float32)
        earlier = (lax.broadcasted_iota(jnp.int32, tri_scr.shape, 1)
                   < lax.broadcasted_iota(jnp.int32, tri_scr.shape, 0))
        tri_scr[...] = jnp.where(earlier, 1.0, 0.0).astype(tri_scr.dtype)
        _load_bf16(w_hbm, [(r, STAGE_ROWS, r) for r in range(0, w_bf.shape[0], STAGE_ROWS)],
                   w_bf, stage, sem)

    rs = x_ref.shape[0] // OUT_SUB
    for sub in range(OUT_SUB):
        rows = pl.ds(sub * rs, rs)
        _out_rows(mc_ref[rows, :], ma_ref[rows, :], w_bf, x_ref[rows, :], gt_ref, g_ref, sc_ref, sh_ref,
                  wr_ref, br_ref, x1_ref.at[rows], h2_ref.at[pl.ds(sub * rs * ROW_SUB, rs * ROW_SUB)],
                  route_ref.at[rows], routet_ref.at[:, rows], cnt_scr, tri_scr[0:rs, 0:rs])
    cnt_ref[...] = cnt_scr[...]


def _out_rows(mc, ma, w_bf, x, gt_ref, g_ref, sc_ref, sh_ref, wr_ref, br_ref,
              x1_ref, h2_ref, route_ref, routet_ref, cnt_scr, tri):
    mix = (jnp.dot(mc, w_bf[0:CONV_DIM, :], preferred_element_type=jnp.float32)
           + jnp.dot(ma, w_bf[CONV_DIM:, :], preferred_element_type=jnp.float32))
    x1 = x + gt_ref[0] * mix
    x1_ref[...] = x1
    h2 = _rms(x1) * g_ref[...] * (1.0 + sc_ref[0]) + sh_ref[0]
    _rows_from_2d(h2_ref, h2)

    logit = jnp.dot(h2.astype(jnp.bfloat16), wr_ref[...],
                    preferred_element_type=jnp.float32) + br_ref[...]
    lane = lax.broadcasted_iota(jnp.int32, logit.shape, 1).astype(jnp.float32)
    neg = -jnp.inf
    big = float(ROUTE_LANES)
    is_g = lane < N_GROUPS
    gl = jnp.where(is_g, logit, neg)
    gmax = jnp.max(gl, axis=1, keepdims=True)
    p_group = 1.0 / jnp.sum(jnp.exp(gl - gmax), axis=1, keepdims=True)
    g_sel = jnp.min(jnp.where(is_g & (gl == gmax), lane, big), axis=1, keepdims=True)
    lo = N_GROUPS + g_sel * EXPERTS_PER_GROUP
    in_grp = (lane >= lo) & (lane < lo + EXPERTS_PER_GROUP)
    el = jnp.where(in_grp, logit, neg)
    m1 = jnp.max(el, axis=1, keepdims=True)
    i1 = jnp.min(jnp.where(in_grp & (el == m1), lane, big), axis=1, keepdims=True)
    el2 = jnp.where(lane == i1, neg, el)
    m2 = jnp.max(el2, axis=1, keepdims=True)
    i2 = jnp.min(jnp.where(in_grp & (lane != i1) & (el2 == m2), lane, big), axis=1, keepdims=True)
    r = jnp.exp(m2 - m1)
    w1 = p_group / (1.0 + r)
    w2 = p_group * r / (1.0 + r)

    member = jnp.where(lane == i1, 1.0, jnp.where(lane == i2, 1.0, 0.0))
    before = jnp.dot(tri, member.astype(jnp.bfloat16),
                     preferred_element_type=jnp.float32) + cnt_scr[...]
    rank1 = jnp.sum(jnp.where(lane == i1, before, 0.0), axis=1, keepdims=True)
    rank2 = jnp.sum(jnp.where(lane == i2, before, 0.0), axis=1, keepdims=True)
    cnt_scr[...] = cnt_scr[...] + jnp.sum(member, axis=0, keepdims=True)

    route = jnp.zeros_like(logit)
    for idx, val in ((R_E1, i1 - N_GROUPS), (R_E2, i2 - N_GROUPS), (R_W1, w1), (R_W2, w2),
                     (R_RANK1, rank1), (R_RANK2, rank2)):
        route = jnp.where(lane == idx, val, route)
    route_ref[...] = route
    routet_ref[...] = route.T[0:SUBLANES, :]


def _out(mix_c, mix_a, w_out, x2, gt1, g_ffn, sc2, sh2, w_route, b_route, seq, tm=512):
    t, d = x2.shape
    per_b = seq // tm
    bmap = lambda m: (m // per_b, 0, 0)
    return pl.pallas_call(
        _out_kernel,
        grid=(t // tm,),
        in_specs=[pl.BlockSpec((tm, CONV_DIM), lambda m: (m, 0)),
                  pl.BlockSpec((tm, ATTN_DIM), lambda m: (m, 0)),
                  pl.BlockSpec(memory_space=pl.ANY),
                  pl.BlockSpec((tm, d), lambda m: (m, 0)),
                  pl.BlockSpec((1, 1, d), bmap),
                  pl.BlockSpec((1, d), lambda m: (0, 0)),
                  pl.BlockSpec((1, 1, d), bmap),
                  pl.BlockSpec((1, 1, d), bmap),
                  pl.BlockSpec((d, ROUTE_LANES), lambda m: (0, 0)),
                  pl.BlockSpec((1, ROUTE_LANES), lambda m: (0, 0))],
        out_specs=[pl.BlockSpec((tm, d), lambda m: (m, 0)),
                   pl.BlockSpec((tm * ROW_SUB, LANES), lambda m: (m, 0)),
                   pl.BlockSpec((tm, ROUTE_LANES), lambda m: (m, 0)),
                   pl.BlockSpec((SUBLANES, tm), lambda m: (0, m)),
                   pl.BlockSpec((1, ROUTE_LANES), lambda m: (0, 0))],
        out_shape=[jax.ShapeDtypeStruct((t, d), jnp.float32),
                   jax.ShapeDtypeStruct((t * ROW_SUB, LANES), jnp.float32),
                   jax.ShapeDtypeStruct((t, ROUTE_LANES), jnp.float32),
                   jax.ShapeDtypeStruct((SUBLANES, t), jnp.float32),
                   jax.ShapeDtypeStruct((1, ROUTE_LANES), jnp.float32)],
        scratch_shapes=[pltpu.VMEM((1, ROUTE_LANES), jnp.float32),
                        pltpu.VMEM((tm, tm), jnp.bfloat16),
                        pltpu.VMEM(w_out.shape, jnp.bfloat16),
                        pltpu.VMEM((2, STAGE_ROWS, d), jnp.float32),
                        pltpu.SemaphoreType.DMA((2,))],
        compiler_params=_cparams(("arbitrary",)),
        name="out",
    )(mix_c, mix_a, w_out, x2, gt1, g_ffn, sc2, sh2, w_route, b_route)


def _slot(off_ref, e_ref, r_ref, tok):
    return off_ref[e_ref[tok]] + r_ref[tok]


def _dispatch_kernel(e1_ref, e2_ref, r1_ref, r2_ref, off_ref, zs_ref, zc_ref, nu_ref, h2_ref, xs_ref,
                     zbuf, sem_z, sem_r, *, tm, n_tiles):
    i = pl.program_id(0)

    @pl.when(i == 0)
    def _():
        zbuf[...] = jnp.zeros(zbuf.shape, zbuf.dtype)

        def zero_copy(start):
            rows = pl.ds(pl.multiple_of(start * ROW_SUB, ZERO_ROWS * ROW_SUB), ZERO_ROWS * ROW_SUB)
            return pltpu.make_async_copy(zbuf, xs_ref.at[rows], sem_z)

        def zero_fill(first_row, n_granules):
            def start(k, carry):
                zero_copy(first_row + k * ZERO_ROWS).start()
                return carry

            def wait(k, carry):
                zero_copy(first_row + k * ZERO_ROWS).wait()
                return carry

            lax.fori_loop(0, n_granules, start, 0)
            lax.fori_loop(0, n_granules, wait, 0)

        for e in range(N_EXPERTS):
            zero_fill(zs_ref[e], zc_ref[e])
        zero_fill(nu_ref[0] * MOE_TM, (n_tiles - nu_ref[0]) * (MOE_TM // ZERO_ROWS))

    def issue_row(r, carry):
        tok = i * tm + r
        for prio, (e_ref, r_ref) in enumerate(((e1_ref, r1_ref), (e2_ref, r2_ref))):
            pltpu.make_async_copy(_row(h2_ref, r), _row(xs_ref, _slot(off_ref, e_ref, r_ref, tok)),
                                  sem_r).start(priority=prio)
        return carry

    lax.fori_loop(0, tm, issue_row, 0)
    for _ in range(2):
        pltpu.make_async_copy(h2_ref, xs_ref.at[pl.ds(0, tm * ROW_SUB)], sem_r).wait()


def _dispatch(plan, h2, n_rows, tm=512):
    t = h2.shape[0] // ROW_SUB
    kern = functools.partial(_dispatch_kernel, tm=tm, n_tiles=n_rows // MOE_TM)
    return pl.pallas_call(
        kern,
        grid_spec=pltpu.PrefetchScalarGridSpec(
            num_scalar_prefetch=len(plan),
            grid=(t // tm,),
            in_specs=[pl.BlockSpec((tm * ROW_SUB, LANES), lambda i, *_: (i, 0))],
            out_specs=pl.BlockSpec(memory_space=pl.ANY),
            scratch_shapes=[pltpu.VMEM((ZERO_ROWS * ROW_SUB, LANES), h2.dtype),
                            pltpu.SemaphoreType.DMA(()),
                            pltpu.SemaphoreType.DMA(())]),
        out_shape=jax.ShapeDtypeStruct((n_rows * ROW_SUB, LANES), h2.dtype),
        compiler_params=_cparams(("arbitrary",)),
        name="dispatch",
    )(*plan, h2)


def _mlp_kernel(te_ref, nu_ref, first_ref, nxt_ref, slot_ref, xs_ref, wg_hbm, wu_hbm, wd_hbm, ys_ref,
                wg_f32, wu_f32, wd_f32, wg_bf, wu_bf, wd_bf, sem):
    i = pl.program_id(0)

    def copies(e, s):
        return [pltpu.make_async_copy(hbm.at[e], buf.at[s], sem.at[s])
                for hbm, buf in ((wg_hbm, wg_f32), (wu_hbm, wu_f32), (wd_hbm, wd_f32))]

    @pl.when(i == 0)
    def _():
        for cp in copies(te_ref[0], 0):
            cp.start()

    @pl.when((i < nu_ref[0]) & (first_ref[i] == 1))
    def _():
        s = slot_ref[i]
        for cp in copies(te_ref[i], s):
            cp.wait()
        wg_bf[...] = wg_f32[s].astype(jnp.bfloat16)
        wu_bf[...] = wu_f32[s].astype(jnp.bfloat16)
        wd_bf[...] = wd_f32[s].astype(jnp.bfloat16)

        @pl.when(nxt_ref[i] >= 0)
        def _():
            for cp in copies(nxt_ref[i], 1 - s):
                cp.start()

    @pl.when(i < nu_ref[0])
    def _():
        x = _rows_to_2d(xs_ref, MOE_TM).astype(jnp.bfloat16)
        a = jnp.dot(x, wg_bf[...], preferred_element_type=jnp.float32)
        u = jnp.dot(x, wu_bf[...], preferred_element_type=jnp.float32)
        hid = (a * jax.nn.sigmoid(a)) * u
        _rows_from_2d(ys_ref, jnp.dot(hid.astype(jnp.bfloat16), wd_bf[...],
                                      preferred_element_type=jnp.float32))

    @pl.when(i >= nu_ref[0])
    def _():
        ys_ref[...] = jnp.zeros(ys_ref.shape, ys_ref.dtype)


def _mlp(tiles, xs, w_gate, w_up, w_down):
    n_rows = xs.shape[0] // ROW_SUB
    _, d, f = w_gate.shape
    used = lambda i, te, nu, *_: (jnp.minimum(i, nu[0] - 1), 0)
    return pl.pallas_call(
        _mlp_kernel,
        grid_spec=pltpu.PrefetchScalarGridSpec(
            num_scalar_prefetch=len(tiles),
            grid=(n_rows // MOE_TM,),
            in_specs=[pl.BlockSpec((MOE_TM * ROW_SUB, LANES), used),
                      pl.BlockSpec(memory_space=pl.ANY),
                      pl.BlockSpec(memory_space=pl.ANY),
                      pl.BlockSpec(memory_space=pl.ANY)],
            out_specs=pl.BlockSpec((MOE_TM * ROW_SUB, LANES), lambda i, *_: (i, 0)),
            scratch_shapes=[pltpu.VMEM((2, d, f), jnp.float32),
                            pltpu.VMEM((2, d, f), jnp.float32),
                            pltpu.VMEM((2, f, d), jnp.float32),
                            pltpu.VMEM((d, f), jnp.bfloat16),
                            pltpu.VMEM((d, f), jnp.bfloat16),
                            pltpu.VMEM((f, d), jnp.bfloat16),
                            pltpu.SemaphoreType.DMA((2,))]),
        out_shape=jax.ShapeDtypeStruct((n_rows * ROW_SUB, LANES), jnp.float32),
        compiler_params=_cparams(("arbitrary",)),
        name="mlp",
    )(*tiles, xs, w_gate, w_up, w_down)


def _combine_kernel(e1_ref, e2_ref, r1_ref, r2_ref, off_ref, ys_ref, x1_ref, route_ref, gt_ref, g_ref,
                    sc_ref, sh_ref, o_ref, a0, b0, a1, b1, sem, *, tm, n_tiles):
    i = pl.program_id(0)
    bufs = ((a0, b0), (a1, b1))

    def issue(tile, sl):
        def issue_row(r, carry):
            tok = tile * tm + r
            for prio, (buf, e_ref, r_ref) in enumerate(((bufs[sl][0], e1_ref, r1_ref),
                                                        (bufs[sl][1], e2_ref, r2_ref))):
                pltpu.make_async_copy(_row(ys_ref, _slot(off_ref, e_ref, r_ref, tok)),
                                      _row(buf, r), sem.at[sl]).start(priority=prio)
            return carry

        lax.fori_loop(0, tm, issue_row, 0)

    @pl.when(i == 0)
    def _():
        issue(0, 0)

    for sl in range(2):
        @pl.when(i % 2 == sl)
        def _(sl=sl):
            @pl.when(i + 1 < n_tiles)
            def _():
                issue(i + 1, 1 - sl)

            for buf in bufs[sl]:
                pltpu.make_async_copy(ys_ref.at[pl.ds(0, tm * ROW_SUB)], buf, sem.at[sl]).wait()

            route = route_ref[...]
            y = (route[:, R_W1:R_W1 + 1] * _rows_to_2d(bufs[sl][0], tm)
                 + route[:, R_W2:R_W2 + 1] * _rows_to_2d(bufs[sl][1], tm))
            x = x1_ref[...] + gt_ref[0] * y
            o_ref[...] = _rms(x) * g_ref[...] * (1.0 + sc_ref[0]) + sh_ref[0]


def _combine(plan, ys, x1, route, gt2, g_final, scf, shf, seq, tm=512):
    t, d = x1.shape
    per_b = seq // tm
    n_tiles = t // tm
    bmap = lambda m, *_: (m // per_b, 0, 0)
    kern = functools.partial(_combine_kernel, tm=tm, n_tiles=n_tiles)
    return pl.pallas_call(
        kern,
        grid_spec=pltpu.PrefetchScalarGridSpec(
            num_scalar_prefetch=len(plan),
            grid=(n_tiles,),
            in_specs=[pl.BlockSpec(memory_space=pl.ANY),
                      pl.BlockSpec((tm, d), lambda m, *_: (m, 0)),
                      pl.BlockSpec((tm, ROUTE_LANES), lambda m, *_: (m, 0)),
                      pl.BlockSpec((1, 1, d), bmap),
                      pl.BlockSpec((1, d), lambda m, *_: (0, 0)),
                      pl.BlockSpec((1, 1, d), bmap),
                      pl.BlockSpec((1, 1, d), bmap)],
            out_specs=pl.BlockSpec((tm, d), lambda m, *_: (m, 0)),
            scratch_shapes=[pltpu.VMEM((tm * ROW_SUB, LANES), jnp.float32) for _ in range(4)]
            + [pltpu.SemaphoreType.DMA((2,))]),
        out_shape=jax.ShapeDtypeStruct((t, d), jnp.float32),
        compiler_params=_cparams(("arbitrary",)),
        name="combine",
    )(*plan, ys, x1, route, gt2, g_final, scf, shf)


def _moe_plan(route_t, counts, t):
    i32 = jnp.int32
    cnt = counts[0, N_GROUPS:N_GROUPS + N_EXPERTS].astype(i32)
    padded = (cnt + MOE_TM - 1) // MOE_TM * MOE_TM
    off_end = jnp.cumsum(padded)
    off = off_end - padded
    n_tiles = (2 * t + N_EXPERTS * MOE_TM) // MOE_TM
    n_used = off_end[-1] // MOE_TM
    tile_start = jnp.arange(n_tiles, dtype=i32) * MOE_TM
    te = jnp.sum((off_end[None, :] <= tile_start[:, None]).astype(i32), axis=1)
    te = jnp.minimum(te, N_EXPERTS - 1)
    tile = jnp.arange(n_tiles, dtype=i32)
    te = jnp.where(tile < n_used, te, te[n_used - 1])
    first = jnp.concatenate([jnp.ones((1,), i32), (te[1:] != te[:-1]).astype(i32)])
    nxt_tile = off_end[te] // MOE_TM
    nxt = jnp.where(nxt_tile < n_used, te[jnp.minimum(nxt_tile, n_tiles - 1)], -1)
    slot = (jnp.cumsum(first) - 1) % 2
    assign = tuple(route_t[k].astype(i32) for k in (R_E1, R_E2, R_RANK1, R_RANK2))
    zstart = (off + cnt) // ZERO_ROWS * ZERO_ROWS
    zcount = (off_end - zstart) // ZERO_ROWS
    n_used = n_used.reshape(1)
    return assign, off, (zstart, zcount), (te, n_used, first, nxt, slot), n_tiles * MOE_TM


def kernel(x, c, w_ada, b_ada, g_mix, w_in, conv_w, w_uk, kv_norm_g, w_uv, g_conv_out, g_attn_out,
           w_out, g_ffn, w_rg, b_rg, w_re, b_re, w_gate, w_up, w_down, w_ada_f, b_ada_f, g_final):
    b, s, d = x.shape
    assert w_ada.shape[0] == 1, "single layer"
    bf = jnp.bfloat16
    x2 = x.reshape(b * s, d)

    c_pad = jnp.zeros((SUBLANES, d), jnp.float32).at[:b].set(c)
    mod = _mod(c_pad, w_ada[0], b_ada[0])[:b]
    modf = _mod(c_pad, w_ada_f, b_ada_f)[:b]
    vec = lambda a, i: a[:, i * d:(i + 1) * d].reshape(b, 1, d)
    sh1, sc1, gt1, sh2, sc2, gt2 = (vec(mod, i) for i in range(N_MOD))
    shf, scf = vec(modf, 0), vec(modf, 1)
    row = lambda a: a.reshape(1, -1)

    w_in_t = jnp.swapaxes(w_in[0], 0, 1)
    h, q, kv, iq, ikw = _hproj(x2, row(g_mix[0]), sc1, sh1, w_in_t, s)
    mix_c = _projconv(h, w_in_t, conv_w[0], row(g_conv_out[0]), s)

    mix_a = _attn(iq.reshape(b, s, -1), ikw.reshape(b, s, -1), kv.reshape(b, s, -1),
                  q.reshape(b, s, -1), w_uk[0].astype(bf), w_uv[0].astype(bf),
                  row(kv_norm_g[0]), row(g_attn_out[0]))

    w_route = jnp.zeros((d, ROUTE_LANES), bf).at[:, :N_GROUPS].set(w_rg[0].astype(bf))
    w_route = w_route.at[:, N_GROUPS:N_GROUPS + N_EXPERTS].set(w_re[0].astype(bf))
    b_route = jnp.zeros((1, ROUTE_LANES), jnp.float32).at[0, :N_GROUPS].set(b_rg[0])
    b_route = b_route.at[0, N_GROUPS:N_GROUPS + N_EXPERTS].set(b_re[0])
    x1, h2, route, route_t, counts = _out(mix_c, mix_a.reshape(b * s, -1),
                                 w_out[0], x2, gt1, row(g_ffn[0]),
                                 sc2, sh2, w_route, b_route, s)

    assign, off, zero, tiles, n_rows = _moe_plan(route_t, counts, b * s)
    xs = _dispatch((*assign, off, *zero, tiles[1]), h2, n_rows)
    ys = _mlp(tiles, xs, w_gate[0], w_up[0], w_down[0])
    out = _combine((*assign, off), ys, x1, route, gt2, row(g_final), scf, shf, s)
    return out.reshape(b, s, d)
```

```python
import functools

import jax
import jax.numpy as jnp
from jax import lax
from jax.experimental import pallas as pl
from jax.experimental.pallas import tpu as pltpu

D_MODEL = 2048
CONV_DIM = 1024
CONV_WIDTH = 3
N_HEADS = 8
HEAD_DIM = 128
ATTN_DIM = N_HEADS * HEAD_DIM
KV_RANK = 512
IDX_HEADS = 16
IDX_DIM = 128
TOPK_MAX = 256
N_GROUPS = 4
EXPERTS_PER_GROUP = 8
N_EXPERTS = N_GROUPS * EXPERTS_PER_GROUP
EXPERT_FF = 512
N_MOD = 6
EPS = 1e-6

OFF_Q = 3 * CONV_DIM
OFF_KV = OFF_Q + ATTN_DIM
OFF_IQ = OFF_KV + KV_RANK
OFF_IK = OFF_IQ + IDX_HEADS * IDX_DIM
OFF_IW = OFF_IK + IDX_DIM
IN_COLS = OFF_IW + IDX_HEADS

LANES = 128
SUBLANES = 8
VMEM_LIMIT = 56 * 1024 * 1024

IKW_COLS = 256
ATTN_Q = 256
ATTN_KC = 256
ATTN_RB = 256
MASKED = -1e30
TIE_ALL = 2 ** 30
ROUTE_LANES = 128
STAGE_ROWS = 256
OUT_SUB = 2
MOE_TM = 256
ZERO_ROWS = 64
ROW_SUB = D_MODEL // LANES
INT_MIN = -2 ** 31
KEY_NEG_INF = INT_MIN + 0x7FFFFF


def _cparams(sem):
    return pltpu.CompilerParams(dimension_semantics=sem, vmem_limit_bytes=VMEM_LIMIT)


def _rms(v, axis=-1):
    return v * lax.rsqrt(jnp.mean(v * v, axis=axis, keepdims=True) + EPS)


def _tile_lanes(v, n):
    return jnp.concatenate([v] * n, axis=1)


def _rows_to_2d(ref, n):
    return jnp.concatenate([ref[pl.ds(c, n, stride=ROW_SUB), :] for c in range(ROW_SUB)], axis=1)


def _rows_from_2d(ref, val):
    n = val.shape[0]
    for c in range(ROW_SUB):
        ref[pl.ds(c, n, stride=ROW_SUB), :] = val[:, c * LANES:(c + 1) * LANES]


def _row(ref, i):
    return ref.at[pl.ds(pl.multiple_of(i * ROW_SUB, ROW_SUB), ROW_SUB)]


def _load_bf16(w_hbm, pieces, w_bf, stage, sem):
    def copy(i):
        src, n, _ = pieces[i]
        return pltpu.make_async_copy(w_hbm.at[pl.ds(src, n)], stage.at[i % 2, pl.ds(0, n)],
                                     sem.at[i % 2])

    copy(0).start()
    for i, (_, n, dst) in enumerate(pieces):
        if i + 1 < len(pieces):
            copy(i + 1).start()
        copy(i).wait()
        w_bf[pl.ds(dst, n), :] = stage[i % 2, pl.ds(0, n), :].astype(jnp.bfloat16)


def _mod_kernel(c_ref, w_ref, b_ref, o_ref):
    c = c_ref[...]
    ca = (c * jax.nn.sigmoid(c)).astype(jnp.bfloat16)
    o_ref[...] = jnp.dot(ca, w_ref[...].astype(jnp.bfloat16),
                         preferred_element_type=jnp.float32) + b_ref[...]


def _mod(c_pad, w, b, tn=1024):
    d, n = w.shape
    return pl.pallas_call(
        _mod_kernel,
        grid=(n // tn,),
        in_specs=[pl.BlockSpec((c_pad.shape[0], d), lambda j: (0, 0)),
                  pl.BlockSpec((d, tn), lambda j: (0, j)),
                  pl.BlockSpec((1, tn), lambda j: (0, j))],
        out_specs=pl.BlockSpec((c_pad.shape[0], tn), lambda j: (0, j)),
        out_shape=jax.ShapeDtypeStruct((c_pad.shape[0], n), jnp.float32),
        compiler_params=_cparams(("arbitrary",)),
        name="mod",
    )(c_pad, w, b.reshape(1, n))


ATT_COLS = IN_COLS - OFF_Q
ATT_PAD = OFF_IK - OFF_Q + IKW_COLS


def _hproj_kernel(x_ref, g_ref, sc_ref, sh_ref, wt_hbm, h_ref, q_ref, kv_ref, iq_ref, ikw_ref,
                  w_bf, stage, sem):
    @pl.when(pl.program_id(0) == 0)
    def _():
        full = ATT_COLS // STAGE_ROWS
        pieces = [(OFF_Q + k * STAGE_ROWS, STAGE_ROWS, k * STAGE_ROWS) for k in range(full)]
        pieces.append((OFF_Q + full * STAGE_ROWS, ATT_COLS - full * STAGE_ROWS, full * STAGE_ROWS))
        _load_bf16(wt_hbm, pieces, w_bf, stage, sem)
        w_bf[pl.ds(ATT_COLS, ATT_PAD - ATT_COLS), :] = jnp.zeros((ATT_PAD - ATT_COLS, w_bf.shape[1]),
                                                                 jnp.bfloat16)

    h = (_rms(x_ref[...]) * g_ref[...] * (1.0 + sc_ref[0]) + sh_ref[0]).astype(h_ref.dtype)
    h_ref[...] = h
    nt = (((1,), (1,)), ((), ()))
    row0 = 0
    for o_ref in (q_ref, kv_ref, iq_ref, ikw_ref):
        n = o_ref.shape[1]
        o_ref[...] = lax.dot_general(h, w_bf[row0:row0 + n, :], nt,
                                     preferred_element_type=jnp.float32).astype(o_ref.dtype)
        row0 += n


def _hproj(x2, g_mix, sc1, sh1, wt, seq, tm=512):
    t, d = x2.shape
    per_b = seq // tm
    bmap = lambda m: (m // per_b, 0, 0)
    widths = (ATTN_DIM, KV_RANK, IDX_HEADS * IDX_DIM, IKW_COLS)
    dtypes = (jnp.bfloat16, jnp.float32, jnp.bfloat16, jnp.float32)
    assert sum(widths) == ATT_PAD and (ATT_COLS % STAGE_ROWS) % SUBLANES == 0
    return pl.pallas_call(
        _hproj_kernel,
        grid=(t // tm,),
        in_specs=[pl.BlockSpec((tm, d), lambda m: (m, 0)),
                  pl.BlockSpec((1, d), lambda m: (0, 0)),
                  pl.BlockSpec((1, 1, d), bmap),
                  pl.BlockSpec((1, 1, d), bmap),
                  pl.BlockSpec(memory_space=pl.ANY)],
        out_specs=[pl.BlockSpec((tm, d), lambda m: (m, 0))]
        + [pl.BlockSpec((tm, w), lambda m: (m, 0)) for w in widths],
        out_shape=[jax.ShapeDtypeStruct((t, d), jnp.bfloat16)]
        + [jax.ShapeDtypeStruct((t, w), dt) for w, dt in zip(widths, dtypes)],
        scratch_shapes=[pltpu.VMEM((ATT_PAD, d), jnp.bfloat16),
                        pltpu.VMEM((2, STAGE_ROWS, d), jnp.float32),
                        pltpu.SemaphoreType.DMA((2,))],
        compiler_params=_cparams(("arbitrary",)),
        name="hproj",
    )(x2, g_mix, sc1, sh1, wt)


def _projconv_kernel(h_ref, wt_hbm, cw_ref, g_ref, o_ref, w_bf, stage, halo_scr, sem, *, per_b):
    m = pl.program_id(0)

    @pl.when(m == 0)
    def _():
        _load_bf16(wt_hbm, [(r, STAGE_ROWS, r) for r in range(0, w_bf.shape[0], STAGE_ROWS)],
                   w_bf, stage, sem)

    nt = (((1,), (1,)), ((), ()))
    h = h_ref[...]
    bg, cg, xv = (lax.dot_general(h, w_bf[k * CONV_DIM:(k + 1) * CONV_DIM, :], nt,
                                  preferred_element_type=jnp.float32) for k in range(3))
    u = cg * xv
    halo = jnp.where(m % per_b == 0, 0.0, halo_scr[...])
    halo_scr[...] = u[u.shape[0] - SUBLANES:, :]
    row = lax.broadcasted_iota(jnp.int32, u.shape, 0)
    h1 = halo[SUBLANES - 1:SUBLANES, :]
    h2 = halo[SUBLANES - 2:SUBLANES - 1, :]
    u1 = jnp.where(row == 0, h1, pltpu.roll(u, 1, 0))
    u2 = jnp.where(row == 0, h2, jnp.where(row == 1, h1, pltpu.roll(u, 2, 0)))
    w = cw_ref[...]
    y = bg * (w[0:1, :] * u2 + w[1:2, :] * u1 + w[2:3, :] * u)
    o_ref[...] = (_rms(y) * g_ref[...]).astype(o_ref.dtype)


def _projconv(h, wt, conv_w, g_conv_out, seq, tm=1024):
    t, d = h.shape
    c = CONV_DIM
    kern = functools.partial(_projconv_kernel, per_b=seq // tm)
    return pl.pallas_call(
        kern,
        grid=(t // tm,),
        in_specs=[pl.BlockSpec((tm, d), lambda m: (m, 0)),
                  pl.BlockSpec(memory_space=pl.ANY),
                  pl.BlockSpec((CONV_WIDTH, c), lambda m: (0, 0)),
                  pl.BlockSpec((1, c), lambda m: (0, 0))],
        out_specs=pl.BlockSpec((tm, c), lambda m: (m, 0)),
        out_shape=jax.ShapeDtypeStruct((t, c), jnp.bfloat16),
        scratch_shapes=[pltpu.VMEM((3 * c, d), jnp.bfloat16),
                        pltpu.VMEM((2, STAGE_ROWS, d), jnp.float32),
                        pltpu.VMEM((SUBLANES, c), jnp.float32),
                        pltpu.SemaphoreType.DMA((2,))],
        compiler_params=_cparams(("arbitrary",)),
        name="proj_conv",
    )(h, wt, conv_w, g_conv_out)


def _ordered_bits(v):
    return v ^ ((v >> 31) & jnp.int32(0x7FFFFFFF))


def _attn_search(nch, sc_scr, k_sel):
    n_keys = float(nch * ATTN_KC)
    half = ATTN_Q // 2

    def half_step(bit, thr, rows):
        cand = thr + bit
        cand_f = lax.bitcast_convert_type(_ordered_bits(cand), jnp.float32)
        hits = jnp.where(sc_scr[0, rows, :] >= cand_f, 1.0, 0.0)
        for c in range(1, nch):
            hits = hits + jnp.where(sc_scr[c, rows, :] >= cand_f, 1.0, 0.0)
        cnt = jnp.sum(hits, axis=1, keepdims=True)
        cnt = jnp.where(cand < KEY_NEG_INF, n_keys, cnt)
        return jnp.where(cnt >= k_sel, cand, thr)

    def bit_step(i, thrs):
        bit = lax.shift_left(jnp.int32(1), 31 - i)
        return tuple(half_step(bit, thr, pl.ds(k * half, half)) for k, thr in enumerate(thrs))

    init = jnp.full((half, 1), INT_MIN, jnp.int32)
    thrs = lax.fori_loop(0, 32, bit_step, (init, init), unroll=2)
    thr = jnp.concatenate(thrs, axis=0)
    return lax.bitcast_convert_type(_ordered_bits(thr), jnp.float32)


def _attn_tiebreak(nch, sc_scr, thr_f, k_sel, tie_scr):
    kloc = lax.broadcasted_iota(jnp.int32, (ATTN_Q, ATTN_KC), 1)

    def count(pred):
        hits = pred(sc_scr[0], 0)
        for c in range(1, nch):
            hits = hits + pred(sc_scr[c], c)
        return jnp.sum(hits, axis=1, keepdims=True)

    tie_scr[...] = jnp.full(tie_scr.shape, TIE_ALL, jnp.int32)
    n_ge = count(lambda sc, c: jnp.where(sc >= thr_f, 1.0, 0.0))
    excess = jnp.where(thr_f > -jnp.inf, n_ge - k_sel, 0.0)

    @pl.when(jnp.max(excess) > 0.0)
    def _():
        need = k_sel - count(lambda sc, c: jnp.where(sc > thr_f, 1.0, 0.0))
        n_bits = (nch * ATTN_KC - 1).bit_length()

        def step(i, lim):
            cand = lim + lax.shift_left(jnp.int32(1), n_bits - 1 - i)
            below = count(lambda sc, c: jnp.where(
                sc == thr_f, jnp.where(c * ATTN_KC + kloc < cand, 1.0, 0.0), 0.0))
            return jnp.where(below < need, cand, lim)

        lim = lax.fori_loop(0, n_bits, step, jnp.zeros((ATTN_Q, 1), jnp.int32))
        tie_scr[...] = jnp.where(excess > 0.0, lim, TIE_ALL)


def _attn_kernel(iq_ref, ikwq_ref, ikwa_ref, kv_ref, q_ref, wuk_ref, wuv_ref, kvg_ref, go_ref,
                 o_ref, ckv_scr, ik_scr, sc_scr, thr_scr, tie_scr, ql_scr, m_scr, l_scr, acc_scr, y_scr,
                 *, seq, k_sel):
    j = pl.program_id(1)
    nch = j + 1
    nt = (((1,), (1,)), ((), ()))

    @pl.when(j == 0)
    def _():
        ckv_scr[...] = (_rms(kv_ref[0]) * kvg_ref[...]).astype(jnp.bfloat16)
        ik_scr[...] = ikwa_ref[0][:, :IDX_DIM].astype(jnp.bfloat16)

    iw = ikwq_ref[0][:, IDX_DIM:IDX_DIM + IDX_HEADS] * (IDX_HEADS ** -0.5 * IDX_DIM ** -0.5)
    qpos = j * ATTN_Q + lax.broadcasted_iota(jnp.int32, (ATTN_Q, ATTN_KC), 0)
    kloc = lax.broadcasted_iota(jnp.int32, (ATTN_Q, ATTN_KC), 1)

    def score_chunk(c, carry):
        k0 = pl.multiple_of(c * ATTN_KC, ATTN_KC)
        ik_c = ik_scr[pl.ds(k0, ATTN_KC), :]
        score = jnp.zeros((ATTN_Q, ATTN_KC), jnp.float32)
        for h in range(IDX_HEADS):
            s = lax.dot_general(iq_ref[0][:, h * IDX_DIM:(h + 1) * IDX_DIM], ik_c, nt,
                                preferred_element_type=jnp.float32)
            score = score + jnp.maximum(s, 0.0) * iw[:, h:h + 1]
        sc_scr[c] = jnp.where(k0 + kloc <= qpos, score, -jnp.inf)
        return carry

    lax.fori_loop(0, nch, score_chunk, 0)

    for v in range(seq // ATTN_KC):
        @pl.when(j == v)
        def _(v=v):
            if (v + 1) * ATTN_KC <= k_sel:
                thr_scr[...] = jnp.full(thr_scr.shape, -jnp.inf, jnp.float32)
                tie_scr[...] = jnp.full(tie_scr.shape, TIE_ALL, jnp.int32)
            else:
                thr_scr[...] = _attn_search(v + 1, sc_scr, k_sel)
                _attn_tiebreak(v + 1, sc_scr, thr_scr[...], k_sel, tie_scr)

    for h in range(N_HEADS):
        ql = lax.dot_general(q_ref[0][:, h * HEAD_DIM:(h + 1) * HEAD_DIM], wuk_ref[h], nt,
                             preferred_element_type=jnp.float32)
        ql_scr[h * ATTN_Q:(h + 1) * ATTN_Q, :] = ql.astype(jnp.bfloat16)
    thr_f = thr_scr[...]
    tie = tie_scr[...]
    scale = HEAD_DIM ** -0.5

    def attn_chunk(c, first):
        k0 = pl.multiple_of(c * ATTN_KC, ATTN_KC)
        ckv_c = ckv_scr[pl.ds(k0, ATTN_KC), :]
        sc = sc_scr[c]
        kpos = k0 + kloc
        tied = jnp.where(sc == thr_f, jnp.where(kpos <= tie, 0.0, MASKED), MASKED)
        bias = jnp.where(kpos <= qpos, jnp.where(sc > thr_f, 0.0, tied), MASKED)
        for u in range(N_HEADS * ATTN_Q // ATTN_RB):
            rows = pl.ds(u * ATTN_RB, ATTN_RB)
            qrow = (u * ATTN_RB) % ATTN_Q
            lg = lax.dot_general(ql_scr[rows, :], ckv_c, nt,
                                 preferred_element_type=jnp.float32) * scale + bias[qrow:qrow + ATTN_RB]
            row_max = jnp.max(lg, axis=1, keepdims=True)
            if first:
                m_new = jnp.maximum(jnp.full((ATTN_RB, LANES), MASKED, jnp.float32), row_max)
            else:
                m_old = m_scr[rows, :]
                m_new = jnp.maximum(m_old, row_max)
                alpha = jnp.exp(m_old - m_new)
            p = jnp.exp(lg - _tile_lanes(m_new, ATTN_KC // LANES))
            p_sum = jnp.sum(p, axis=1, keepdims=True)
            pv = jnp.dot(p.astype(jnp.bfloat16), ckv_c, preferred_element_type=jnp.float32)
            if first:
                l_scr[rows, :] = jnp.broadcast_to(p_sum, (ATTN_RB, LANES))
                acc_scr[rows, :] = pv
            else:
                l_scr[rows, :] = alpha * l_scr[rows, :] + p_sum
                acc_scr[rows, :] = _tile_lanes(alpha, KV_RANK // LANES) * acc_scr[rows, :] + pv
            m_scr[rows, :] = m_new

    attn_chunk(0, True)

    def later_chunk(c, carry):
        attn_chunk(c, False)
        return carry

    lax.fori_loop(1, nch, later_chunk, 0)

    o = (acc_scr[...] * _tile_lanes(1.0 / l_scr[...], KV_RANK // LANES)).astype(jnp.bfloat16)
    for h in range(N_HEADS):
        y_scr[:, h * HEAD_DIM:(h + 1) * HEAD_DIM] = jnp.dot(
            o[h * ATTN_Q:(h + 1) * ATTN_Q], wuv_ref[h], preferred_element_type=jnp.float32)
    o_ref[0] = (_rms(y_scr[...]) * go_ref[...]).astype(o_ref.dtype)


def _attn(iq, ikw, kv, q, w_uk_bf, w_uv_bf, kv_norm_g, g_attn_out):
    b, s, _ = iq.shape
    assert ATTN_Q == ATTN_KC and s % ATTN_Q == 0
    k_sel = min(TOPK_MAX, s // 4)
    rows = N_HEADS * ATTN_Q
    kern = functools.partial(_attn_kernel, seq=s, k_sel=k_sel)
    return pl.pallas_call(
        kern,
        grid=(b, s // ATTN_Q),
        in_specs=[pl.BlockSpec((1, ATTN_Q, IDX_HEADS * IDX_DIM), lambda bi, j: (bi, j, 0)),
                  pl.BlockSpec((1, ATTN_Q, IKW_COLS), lambda bi, j: (bi, j, 0)),
                  pl.BlockSpec((1, s, IKW_COLS), lambda bi, j: (bi, 0, 0)),
                  pl.BlockSpec((1, s, KV_RANK), lambda bi, j: (bi, 0, 0)),
                  pl.BlockSpec((1, ATTN_Q, ATTN_DIM), lambda bi, j: (bi, j, 0)),
                  pl.BlockSpec((N_HEADS, KV_RANK, HEAD_DIM), lambda bi, j: (0, 0, 0)),
                  pl.BlockSpec((N_HEADS, KV_RANK, HEAD_DIM), lambda bi, j: (0, 0, 0)),
                  pl.BlockSpec((1, KV_RANK), lambda bi, j: (0, 0)),
                  pl.BlockSpec((1, ATTN_DIM), lambda bi, j: (0, 0))],
        out_specs=pl.BlockSpec((1, ATTN_Q, ATTN_DIM), lambda bi, j: (bi, j, 0)),
        out_shape=jax.ShapeDtypeStruct((b, s, ATTN_DIM), jnp.bfloat16),
        scratch_shapes=[pltpu.VMEM((s, KV_RANK), jnp.bfloat16),
                        pltpu.VMEM((s, IDX_DIM), jnp.bfloat16),
                        pltpu.VMEM((s // ATTN_KC, ATTN_Q, ATTN_KC), jnp.float32),
                        pltpu.VMEM((ATTN_Q, 1), jnp.float32),
                        pltpu.VMEM((ATTN_Q, 1), jnp.int32),
                        pltpu.VMEM((rows, KV_RANK), jnp.bfloat16),
                        pltpu.VMEM((rows, LANES), jnp.float32),
                        pltpu.VMEM((rows, LANES), jnp.float32),
                        pltpu.VMEM((rows, KV_RANK), jnp.float32),
                        pltpu.VMEM((ATTN_Q, ATTN_DIM), jnp.float32)],
        compiler_params=_cparams(("arbitrary", "arbitrary")),
        name="attn",
    )(iq, ikw, ikw, kv, q, w_uk_bf, w_uv_bf, kv_norm_g, g_attn_out)


R_E1, R_E2, R_W1, R_W2, R_RANK1, R_RANK2 = range(6)


def _out_kernel(mc_ref, ma_ref, w_hbm, x_ref, gt_ref, g_ref, sc_ref, sh_ref, wr_ref, br_ref,
                x1_ref, h2_ref, route_ref, routet_ref, cnt_ref, cnt_scr, tri_scr, w_bf, stage, sem):
    @pl.when(pl.program_id(0) == 0)
    def _():
        cnt_scr[...] = jnp.zeros(cnt_scr.shape, jnp.float32)
        earlier = (lax.broadcasted_iota(jnp.int32, tri_scr.shape, 1)
                   < lax.broadcasted_iota(jnp.int32, tri_scr.shape, 0))
        tri_scr[...] = jnp.where(earlier, 1.0, 0.0).astype(tri_scr.dtype)
        _load_bf16(w_hbm, [(r, STAGE_ROWS, r) for r in range(0, w_bf.shape[0], STAGE_ROWS)],
                   w_bf, stage, sem)

    rs = x_ref.shape[0] // OUT_SUB
    for sub in range(OUT_SUB):
        rows = pl.ds(sub * rs, rs)
        _out_rows(mc_ref[rows, :], ma_ref[rows, :], w_bf, x_ref[rows, :], gt_ref, g_ref, sc_ref, sh_ref,
                  wr_ref, br_ref, x1_ref.at[rows], h2_ref.at[pl.ds(sub * rs * ROW_SUB, rs * ROW_SUB)],
                  route_ref.at[rows], routet_ref.at[:, rows], cnt_scr, tri_scr[0:rs, 0:rs])
    cnt_ref[...] = cnt_scr[...]


def _out_rows(mc, ma, w_bf, x, gt_ref, g_ref, sc_ref, sh_ref, wr_ref, br_ref,
              x1_ref, h2_ref, route_ref, routet_ref, cnt_scr, tri):
    mix = (jnp.dot(mc, w_bf[0:CONV_DIM, :], preferred_element_type=jnp.float32)
           + jnp.dot(ma, w_bf[CONV_DIM:, :], preferred_element_type=jnp.float32))
    x1 = x + gt_ref[0] * mix
    x1_ref[...] = x1
    h2 = _rms(x1) * g_ref[...] * (1.0 + sc_ref[0]) + sh_ref[0]
    _rows_from_2d(h2_ref, h2)

    logit = jnp.dot(h2.astype(jnp.bfloat16), wr_ref[...],
                    preferred_element_type=jnp.float32) + br_ref[...]
    lane = lax.broadcasted_iota(jnp.int32, logit.shape, 1).astype(jnp.float32)
    neg = -jnp.inf
    big = float(ROUTE_LANES)
    is_g = lane < N_GROUPS
    gl = jnp.where(is_g, logit, neg)
    gmax = jnp.max(gl, axis=1, keepdims=True)
    p_group = 1.0 / jnp.sum(jnp.exp(gl - gmax), axis=1, keepdims=True)
    g_sel = jnp.min(jnp.where(is_g & (gl == gmax), lane, big), axis=1, keepdims=True)
    lo = N_GROUPS + g_sel * EXPERTS_PER_GROUP
    in_grp = (lane >= lo) & (lane < lo + EXPERTS_PER_GROUP)
    el = jnp.where(in_grp, logit, neg)
    m1 = jnp.max(el, axis=1, keepdims=True)
    i1 = jnp.min(jnp.where(in_grp & (el == m1), lane, big), axis=1, keepdims=True)
    el2 = jnp.where(lane == i1, neg, el)
    m2 = jnp.max(el2, axis=1, keepdims=True)
    i2 = jnp.min(jnp.where(in_grp & (lane != i1) & (el2 == m2), lane, big), axis=1, keepdims=True)
    r = jnp.exp(m2 - m1)
    w1 = p_group / (1.0 + r)
    w2 = p_group * r / (1.0 + r)

    member = jnp.where(lane == i1, 1.0, jnp.where(lane == i2, 1.0, 0.0))
    before = jnp.dot(tri, member.astype(jnp.bfloat16),
                     preferred_element_type=jnp.float32) + cnt_scr[...]
    rank1 = jnp.sum(jnp.where(lane == i1, before, 0.0), axis=1, keepdims=True)
    rank2 = jnp.sum(jnp.where(lane == i2, before, 0.0), axis=1, keepdims=True)
    cnt_scr[...] = cnt_scr[...] + jnp.sum(member, axis=0, keepdims=True)

    route = jnp.zeros_like(logit)
    for idx, val in ((R_E1, i1 - N_GROUPS), (R_E2, i2 - N_GROUPS), (R_W1, w1), (R_W2, w2),
                     (R_RANK1, rank1), (R_RANK2, rank2)):
        route = jnp.where(lane == idx, val, route)
    route_ref[...] = route
    routet_ref[...] = route.T[0:SUBLANES, :]


def _out(mix_c, mix_a, w_out, x2, gt1, g_ffn, sc2, sh2, w_route, b_route, seq, tm=512):
    t, d = x2.shape
    per_b = seq // tm
    bmap = lambda m: (m // per_b, 0, 0)
    return pl.pallas_call(
        _out_kernel,
        grid=(t // tm,),
        in_specs=[pl.BlockSpec((tm, CONV_DIM), lambda m: (m, 0)),
                  pl.BlockSpec((tm, ATTN_DIM), lambda m: (m, 0)),
                  pl.BlockSpec(memory_space=pl.ANY),
                  pl.BlockSpec((tm, d), lambda m: (m, 0)),
                  pl.BlockSpec((1, 1, d), bmap),
                  pl.BlockSpec((1, d), lambda m: (0, 0)),
                  pl.BlockSpec((1, 1, d), bmap),
                  pl.BlockSpec((1, 1, d), bmap),
                  pl.BlockSpec((d, ROUTE_LANES), lambda m: (0, 0)),
                  pl.BlockSpec((1, ROUTE_LANES), lambda m: (0, 0))],
        out_specs=[pl.BlockSpec((tm, d), lambda m: (m, 0)),
                   pl.BlockSpec((tm * ROW_SUB, LANES), lambda m: (m, 0)),
                   pl.BlockSpec((tm, ROUTE_LANES), lambda m: (m, 0)),
                   pl.BlockSpec((SUBLANES, tm), lambda m: (0, m)),
                   pl.BlockSpec((1, ROUTE_LANES), lambda m: (0, 0))],
        out_shape=[jax.ShapeDtypeStruct((t, d), jnp.float32),
                   jax.ShapeDtypeStruct((t * ROW_SUB, LANES), jnp.float32),
                   jax.ShapeDtypeStruct((t, ROUTE_LANES), jnp.float32),
                   jax.ShapeDtypeStruct((SUBLANES, t), jnp.float32),
                   jax.ShapeDtypeStruct((1, ROUTE_LANES), jnp.float32)],
        scratch_shapes=[pltpu.VMEM((1, ROUTE_LANES), jnp.float32),
                        pltpu.VMEM((tm, tm), jnp.bfloat16),
                        pltpu.VMEM(w_out.shape, jnp.bfloat16),
                        pltpu.VMEM((2, STAGE_ROWS, d), jnp.float32),
                        pltpu.SemaphoreType.DMA((2,))],
        compiler_params=_cparams(("arbitrary",)),
        name="out",
    )(mix_c, mix_a, w_out, x2, gt1, g_ffn, sc2, sh2, w_route, b_route)


def _slot(off_ref, e_ref, r_ref, tok):
    return off_ref[e_ref[tok]] + r_ref[tok]


def _dispatch_kernel(e1_ref, e2_ref, r1_ref, r2_ref, off_ref, zs_ref, zc_ref, nu_ref, h2_ref, xs_ref,
                     zbuf, sem_z, sem_r, *, tm, n_tiles):
    i = pl.program_id(0)

    @pl.when(i == 0)
    def _():
        zbuf[...] = jnp.zeros(zbuf.shape, zbuf.dtype)

        def zero_copy(start):
            rows = pl.ds(pl.multiple_of(start * ROW_SUB, ZERO_ROWS * ROW_SUB), ZERO_ROWS * ROW_SUB)
            return pltpu.make_async_copy(zbuf, xs_ref.at[rows], sem_z)

        def zero_fill(first_row, n_granules):
            def start(k, carry):
                zero_copy(first_row + k * ZERO_ROWS).start()
                return carry

            def wait(k, carry):
                zero_copy(first_row + k * ZERO_ROWS).wait()
                return carry

            lax.fori_loop(0, n_granules, start, 0)
            lax.fori_loop(0, n_granules, wait, 0)

        for e in range(N_EXPERTS):
            zero_fill(zs_ref[e], zc_ref[e])
        zero_fill(nu_ref[0] * MOE_TM, (n_tiles - nu_ref[0]) * (MOE_TM // ZERO_ROWS))

    def issue_row(r, carry):
        tok = i * tm + r
        for e_ref, r_ref in ((e1_ref, r1_ref), (e2_ref, r2_ref)):
            pltpu.make_async_copy(_row(h2_ref, r), _row(xs_ref, _slot(off_ref, e_ref, r_ref, tok)),
                                  sem_r).start()
        return carry

    lax.fori_loop(0, tm, issue_row, 0)
    for _ in range(2):
        pltpu.make_async_copy(h2_ref, xs_ref.at[pl.ds(0, tm * ROW_SUB)], sem_r).wait()


def _dispatch(plan, h2, n_rows, tm=512):
    t = h2.shape[0] // ROW_SUB
    kern = functools.partial(_dispatch_kernel, tm=tm, n_tiles=n_rows // MOE_TM)
    return pl.pallas_call(
        kern,
        grid_spec=pltpu.PrefetchScalarGridSpec(
            num_scalar_prefetch=len(plan),
            grid=(t // tm,),
            in_specs=[pl.BlockSpec((tm * ROW_SUB, LANES), lambda i, *_: (i, 0))],
            out_specs=pl.BlockSpec(memory_space=pl.ANY),
            scratch_shapes=[pltpu.VMEM((ZERO_ROWS * ROW_SUB, LANES), h2.dtype),
                            pltpu.SemaphoreType.DMA(()),
                            pltpu.SemaphoreType.DMA(())]),
        out_shape=jax.ShapeDtypeStruct((n_rows * ROW_SUB, LANES), h2.dtype),
        compiler_params=_cparams(("arbitrary",)),
        name="dispatch",
    )(*plan, h2)


def _mlp_kernel(te_ref, nu_ref, first_ref, nxt_ref, slot_ref, xs_ref, wg_hbm, wu_hbm, wd_hbm, ys_ref,
                wg_f32, wu_f32, wd_f32, wg_bf, wu_bf, wd_bf, sem):
    i = pl.program_id(0)

    def copies(e, s):
        return [pltpu.make_async_copy(hbm.at[e], buf.at[s], sem.at[s])
                for hbm, buf in ((wg_hbm, wg_f32), (wu_hbm, wu_f32), (wd_hbm, wd_f32))]

    @pl.when(i == 0)
    def _():
        for cp in copies(te_ref[0], 0):
            cp.start()

    @pl.when((i < nu_ref[0]) & (first_ref[i] == 1))
    def _():
        s = slot_ref[i]
        for cp in copies(te_ref[i], s):
            cp.wait()
        wg_bf[...] = wg_f32[s].astype(jnp.bfloat16)
        wu_bf[...] = wu_f32[s].astype(jnp.bfloat16)
        wd_bf[...] = wd_f32[s].astype(jnp.bfloat16)

        @pl.when(nxt_ref[i] >= 0)
        def _():
            for cp in copies(nxt_ref[i], 1 - s):
                cp.start()

    @pl.when(i < nu_ref[0])
    def _():
        x = _rows_to_2d(xs_ref, MOE_TM).astype(jnp.bfloat16)
        a = jnp.dot(x, wg_bf[...], preferred_element_type=jnp.float32)
        u = jnp.dot(x, wu_bf[...], preferred_element_type=jnp.float32)
        hid = (a * jax.nn.sigmoid(a)) * u
        _rows_from_2d(ys_ref, jnp.dot(hid.astype(jnp.bfloat16), wd_bf[...],
                                      preferred_element_type=jnp.float32))

    @pl.when(i >= nu_ref[0])
    def _():
        ys_ref[...] = jnp.zeros(ys_ref.shape, ys_ref.dtype)


def _mlp(tiles, xs, w_gate, w_up, w_down):
    n_rows = xs.shape[0] // ROW_SUB
    _, d, f = w_gate.shape
    used = lambda i, te, nu, *_: (jnp.minimum(i, nu[0] - 1), 0)
    return pl.pallas_call(
        _mlp_kernel,
        grid_spec=pltpu.PrefetchScalarGridSpec(
            num_scalar_prefetch=len(tiles),
            grid=(n_rows // MOE_TM,),
            in_specs=[pl.BlockSpec((MOE_TM * ROW_SUB, LANES), used),
                      pl.BlockSpec(memory_space=pl.ANY),
                      pl.BlockSpec(memory_space=pl.ANY),
                      pl.BlockSpec(memory_space=pl.ANY)],
            out_specs=pl.BlockSpec((MOE_TM * ROW_SUB, LANES), lambda i, *_: (i, 0)),
            scratch_shapes=[pltpu.VMEM((2, d, f), jnp.float32),
                            pltpu.VMEM((2, d, f), jnp.float32),
                            pltpu.VMEM((2, f, d), jnp.float32),
                            pltpu.VMEM((d, f), jnp.bfloat16),
                            pltpu.VMEM((d, f), jnp.bfloat16),
                            pltpu.VMEM((f, d), jnp.bfloat16),
                            pltpu.SemaphoreType.DMA((2,))]),
        out_shape=jax.ShapeDtypeStruct((n_rows * ROW_SUB, LANES), jnp.float32),
        compiler_params=_cparams(("arbitrary",)),
        name="mlp",
    )(*tiles, xs, w_gate, w_up, w_down)


def _combine_kernel(e1_ref, e2_ref, r1_ref, r2_ref, off_ref, ys_ref, x1_ref, route_ref, gt_ref, g_ref,
                    sc_ref, sh_ref, o_ref, a0, b0, a1, b1, sem, *, tm, n_tiles):
    i = pl.program_id(0)
    bufs = ((a0, b0), (a1, b1))

    def issue(tile, sl):
        def issue_row(r, carry):
            tok = tile * tm + r
            for buf, e_ref, r_ref in ((bufs[sl][0], e1_ref, r1_ref), (bufs[sl][1], e2_ref, r2_ref)):
                pltpu.make_async_copy(_row(ys_ref, _slot(off_ref, e_ref, r_ref, tok)),
                                      _row(buf, r), sem.at[sl]).start()
            return carry

        lax.fori_loop(0, tm, issue_row, 0)

    @pl.when(i == 0)
    def _():
        issue(0, 0)

    for sl in range(2):
        @pl.when(i % 2 == sl)
        def _(sl=sl):
            @pl.when(i + 1 < n_tiles)
            def _():
                issue(i + 1, 1 - sl)

            for buf in bufs[sl]:
                pltpu.make_async_copy(ys_ref.at[pl.ds(0, tm * ROW_SUB)], buf, sem.at[sl]).wait()

            route = route_ref[...]
            y = (route[:, R_W1:R_W1 + 1] * _rows_to_2d(bufs[sl][0], tm)
                 + route[:, R_W2:R_W2 + 1] * _rows_to_2d(bufs[sl][1], tm))
            x = x1_ref[...] + gt_ref[0] * y
            o_ref[...] = _rms(x) * g_ref[...] * (1.0 + sc_ref[0]) + sh_ref[0]


def _combine(plan, ys, x1, route, gt2, g_final, scf, shf, seq, tm=512):
    t, d = x1.shape
    per_b = seq // tm
    n_tiles = t // tm
    bmap = lambda m, *_: (m // per_b, 0, 0)
    kern = functools.partial(_combine_kernel, tm=tm, n_tiles=n_tiles)
    return pl.pallas_call(
        kern,
        grid_spec=pltpu.PrefetchScalarGridSpec(
            num_scalar_prefetch=len(plan),
            grid=(n_tiles,),
            in_specs=[pl.BlockSpec(memory_space=pl.ANY),
                      pl.BlockSpec((tm, d), lambda m, *_: (m, 0)),
                      pl.BlockSpec((tm, ROUTE_LANES), lambda m, *_: (m, 0)),
                      pl.BlockSpec((1, 1, d), bmap),
                      pl.BlockSpec((1, d), lambda m, *_: (0, 0)),
                      pl.BlockSpec((1, 1, d), bmap),
                      pl.BlockSpec((1, 1, d), bmap)],
            out_specs=pl.BlockSpec((tm, d), lambda m, *_: (m, 0)),
            scratch_shapes=[pltpu.VMEM((tm * ROW_SUB, LANES), jnp.float32) for _ in range(4)]
            + [pltpu.SemaphoreType.DMA((2,))]),
        out_shape=jax.ShapeDtypeStruct((t, d), jnp.float32),
        compiler_params=_cparams(("arbitrary",)),
        name="combine",
    )(*plan, ys, x1, route, gt2, g_final, scf, shf)


def _moe_plan(route_t, counts, t):
    i32 = jnp.int32
    cnt = counts[0, N_GROUPS:N_GROUPS + N_EXPERTS].astype(i32)
    padded = (cnt + MOE_TM - 1) // MOE_TM * MOE_TM
    off_end = jnp.cumsum(padded)
    off = off_end - padded
    n_tiles = (2 * t + N_EXPERTS * MOE_TM) // MOE_TM
    n_used = off_end[-1] // MOE_TM
    tile_start = jnp.arange(n_tiles, dtype=i32) * MOE_TM
    te = jnp.sum((off_end[None, :] <= tile_start[:, None]).astype(i32), axis=1)
    te = jnp.minimum(te, N_EXPERTS - 1)
    tile = jnp.arange(n_tiles, dtype=i32)
    te = jnp.where(tile < n_used, te, te[n_used - 1])
    first = jnp.concatenate([jnp.ones((1,), i32), (te[1:] != te[:-1]).astype(i32)])
    nxt_tile = off_end[te] // MOE_TM
    nxt = jnp.where(nxt_tile < n_used, te[jnp.minimum(nxt_tile, n_tiles - 1)], -1)
    slot = (jnp.cumsum(first) - 1) % 2
    assign = tuple(route_t[k].astype(i32) for k in (R_E1, R_E2, R_RANK1, R_RANK2))
    zstart = (off + cnt) // ZERO_ROWS * ZERO_ROWS
    zcount = (off_end - zstart) // ZERO_ROWS
    n_used = n_used.reshape(1)
    return assign, off, (zstart, zcount), (te, n_used, first, nxt, slot), n_tiles * MOE_TM


def kernel(x, c, w_ada, b_ada, g_mix, w_in, conv_w, w_uk, kv_norm_g, w_uv, g_conv_out, g_attn_out,
           w_out, g_ffn, w_rg, b_rg, w_re, b_re, w_gate, w_up, w_down, w_ada_f, b_ada_f, g_final):
    b, s, d = x.shape
    assert w_ada.shape[0] == 1, "single layer"
    bf = jnp.bfloat16
    x2 = x.reshape(b * s, d)

    c_pad = jnp.zeros((SUBLANES, d), jnp.float32).at[:b].set(c)
    mod = _mod(c_pad, w_ada[0], b_ada[0])[:b]
    modf = _mod(c_pad, w_ada_f, b_ada_f)[:b]
    vec = lambda a, i: a[:, i * d:(i + 1) * d].reshape(b, 1, d)
    sh1, sc1, gt1, sh2, sc2, gt2 = (vec(mod, i) for i in range(N_MOD))
    shf, scf = vec(modf, 0), vec(modf, 1)
    row = lambda a: a.reshape(1, -1)

    w_in_t = jnp.swapaxes(w_in[0], 0, 1)
    h, q, kv, iq, ikw = _hproj(x2, row(g_mix[0]), sc1, sh1, w_in_t, s)
    mix_c = _projconv(h, w_in_t, conv_w[0], row(g_conv_out[0]), s)

    mix_a = _attn(iq.reshape(b, s, -1), ikw.reshape(b, s, -1), kv.reshape(b, s, -1),
                  q.reshape(b, s, -1), w_uk[0].astype(bf), w_uv[0].astype(bf),
                  row(kv_norm_g[0]), row(g_attn_out[0]))

    w_route = jnp.zeros((d, ROUTE_LANES), bf).at[:, :N_GROUPS].set(w_rg[0].astype(bf))
    w_route = w_route.at[:, N_GROUPS:N_GROUPS + N_EXPERTS].set(w_re[0].astype(bf))
    b_route = jnp.zeros((1, ROUTE_LANES), jnp.float32).at[0, :N_GROUPS].set(b_rg[0])
    b_route = b_route.at[0, N_GROUPS:N_GROUPS + N_EXPERTS].set(b_re[0])
    x1, h2, route, route_t, counts = _out(mix_c, mix_a.reshape(b * s, -1),
                                 w_out[0], x2, gt1, row(g_ffn[0]),
                                 sc2, sh2, w_route, b_route, s)

    assign, off, zero, tiles, n_rows = _moe_plan(route_t, counts, b * s)
    xs = _dispatch((*assign, off, *zero, tiles[1]), h2, n_rows)
    ys = _mlp(tiles, xs, w_gate[0], w_up[0], w_down[0])
    out = _combine((*assign, off), ys, x1, route, gt2, row(g_final), scf, shf, s)
    return out.reshape(b, s, d)
```

```python
import functools

import jax
import jax.numpy as jnp
from jax import lax
from jax.experimental import pallas as pl
from jax.experimental.pallas import tpu as pltpu

D_MODEL = 2048
CONV_DIM = 1024
CONV_WIDTH = 3
N_HEADS = 8
HEAD_DIM = 128
ATTN_DIM = N_HEADS * HEAD_DIM
KV_RANK = 512
IDX_HEADS = 16
IDX_DIM = 128
TOPK_MAX = 256
N_GROUPS = 4
EXPERTS_PER_GROUP = 8
N_EXPERTS = N_GROUPS * EXPERTS_PER_GROUP
EXPERT_FF = 512
N_MOD = 6
EPS = 1e-6

OFF_Q = 3 * CONV_DIM
OFF_KV = OFF_Q + ATTN_DIM
OFF_IQ = OFF_KV + KV_RANK
OFF_IK = OFF_IQ + IDX_HEADS * IDX_DIM
OFF_IW = OFF_IK + IDX_DIM
IN_COLS = OFF_IW + IDX_HEADS

LANES = 128
SUBLANES = 8
VMEM_LIMIT = 56 * 1024 * 1024

IKW_COLS = 256
ATTN_Q = 256
ATTN_KC = 256
ATTN_RB = 256
MASKED = -1e30
TIE_ALL = 2 ** 30
ROUTE_LANES = 128
STAGE_ROWS = 256
OUT_SUB = 2
MOE_TM = 256
ZERO_ROWS = 64
ROW_SUB = D_MODEL // LANES
INT_MIN = -2 ** 31
KEY_NEG_INF = INT_MIN + 0x7FFFFF


def _cparams(sem):
    return pltpu.CompilerParams(dimension_semantics=sem, vmem_limit_bytes=VMEM_LIMIT)


def _rms(v, axis=-1):
    return v * lax.rsqrt(jnp.mean(v * v, axis=axis, keepdims=True) + EPS)


def _tile_lanes(v, n):
    return jnp.concatenate([v] * n, axis=1)


def _rows_to_2d(ref, n):
    return jnp.concatenate([ref[pl.ds(c, n, stride=ROW_SUB), :] for c in range(ROW_SUB)], axis=1)


def _rows_from_2d(ref, val):
    n = val.shape[0]
    for c in range(ROW_SUB):
        ref[pl.ds(c, n, stride=ROW_SUB), :] = val[:, c * LANES:(c + 1) * LANES]


def _row(ref, i):
    return ref.at[pl.ds(pl.multiple_of(i * ROW_SUB, ROW_SUB), ROW_SUB)]


def _load_bf16(w_hbm, pieces, w_bf, stage, sem):
    def copy(i):
        src, n, _ = pieces[i]
        return pltpu.make_async_copy(w_hbm.at[pl.ds(src, n)], stage.at[i % 2, pl.ds(0, n)],
                                     sem.at[i % 2])

    copy(0).start()
    for i, (_, n, dst) in enumerate(pieces):
        if i + 1 < len(pieces):
            copy(i + 1).start()
        copy(i).wait()
        w_bf[pl.ds(dst, n), :] = stage[i % 2, pl.ds(0, n), :].astype(jnp.bfloat16)


def _mod_kernel(c_ref, w_ref, b_ref, o_ref):
    c = c_ref[...]
    ca = (c * jax.nn.sigmoid(c)).astype(jnp.bfloat16)
    o_ref[...] = jnp.dot(ca, w_ref[...].astype(jnp.bfloat16),
                         preferred_element_type=jnp.float32) + b_ref[...]


def _mod(c_pad, w, b, tn=1024):
    d, n = w.shape
    return pl.pallas_call(
        _mod_kernel,
        grid=(n // tn,),
        in_specs=[pl.BlockSpec((c_pad.shape[0], d), lambda j: (0, 0)),
                  pl.BlockSpec((d, tn), lambda j: (0, j)),
                  pl.BlockSpec((1, tn), lambda j: (0, j))],
        out_specs=pl.BlockSpec((c_pad.shape[0], tn), lambda j: (0, j)),
        out_shape=jax.ShapeDtypeStruct((c_pad.shape[0], n), jnp.float32),
        compiler_params=_cparams(("arbitrary",)),
        name="mod",
    )(c_pad, w, b.reshape(1, n))


ATT_COLS = IN_COLS - OFF_Q
ATT_PAD = OFF_IK - OFF_Q + IKW_COLS


def _hproj_kernel(x_ref, g_ref, sc_ref, sh_ref, wt_hbm, h_ref, q_ref, kv_ref, iq_ref, ikw_ref,
                  w_bf, stage, sem):
    @pl.when(pl.program_id(0) == 0)
    def _():
        full = ATT_COLS // STAGE_ROWS
        pieces = [(OFF_Q + k * STAGE_ROWS, STAGE_ROWS, k * STAGE_ROWS) for k in range(full)]
        pieces.append((OFF_Q + full * STAGE_ROWS, ATT_COLS - full * STAGE_ROWS, full * STAGE_ROWS))
        _load_bf16(wt_hbm, pieces, w_bf, stage, sem)
        w_bf[pl.ds(ATT_COLS, ATT_PAD - ATT_COLS), :] = jnp.zeros((ATT_PAD - ATT_COLS, w_bf.shape[1]),
                                                                 jnp.bfloat16)

    h = (_rms(x_ref[...]) * g_ref[...] * (1.0 + sc_ref[0]) + sh_ref[0]).astype(h_ref.dtype)
    h_ref[...] = h
    nt = (((1,), (1,)), ((), ()))
    row0 = 0
    for o_ref in (q_ref, kv_ref, iq_ref, ikw_ref):
        n = o_ref.shape[1]
        o_ref[...] = lax.dot_general(h, w_bf[row0:row0 + n, :], nt,
                                     preferred_element_type=jnp.float32).astype(o_ref.dtype)
        row0 += n


def _hproj(x2, g_mix, sc1, sh1, wt, seq, tm=512):
    t, d = x2.shape
    per_b = seq // tm
    bmap = lambda m: (m // per_b, 0, 0)
    widths = (ATTN_DIM, KV_RANK, IDX_HEADS * IDX_DIM, IKW_COLS)
    dtypes = (jnp.bfloat16, jnp.float32, jnp.bfloat16, jnp.float32)
    assert sum(widths) == ATT_PAD and (ATT_COLS % STAGE_ROWS) % SUBLANES == 0
    return pl.pallas_call(
        _hproj_kernel,
        grid=(t // tm,),
        in_specs=[pl.BlockSpec((tm, d), lambda m: (m, 0)),
                  pl.BlockSpec((1, d), lambda m: (0, 0)),
                  pl.BlockSpec((1, 1, d), bmap),
                  pl.BlockSpec((1, 1, d), bmap),
                  pl.BlockSpec(memory_space=pl.ANY)],
        out_specs=[pl.BlockSpec((tm, d), lambda m: (m, 0))]
        + [pl.BlockSpec((tm, w), lambda m: (m, 0)) for w in widths],
        out_shape=[jax.ShapeDtypeStruct((t, d), jnp.bfloat16)]
        + [jax.ShapeDtypeStruct((t, w), dt) for w, dt in zip(widths, dtypes)],
        scratch_shapes=[pltpu.VMEM((ATT_PAD, d), jnp.bfloat16),
                        pltpu.VMEM((2, STAGE_ROWS, d), jnp.float32),
                        pltpu.SemaphoreType.DMA((2,))],
        compiler_params=_cparams(("arbitrary",)),
        name="hproj",
    )(x2, g_mix, sc1, sh1, wt)


def _projconv_kernel(h_ref, wt_hbm, cw_ref, g_ref, o_ref, w_bf, stage, halo_scr, sem, *, per_b):
    m = pl.program_id(0)

    @pl.when(m == 0)
    def _():
        _load_bf16(wt_hbm, [(r, STAGE_ROWS, r) for r in range(0, w_bf.shape[0], STAGE_ROWS)],
                   w_bf, stage, sem)

    nt = (((1,), (1,)), ((), ()))
    h = h_ref[...]
    bg, cg, xv = (lax.dot_general(h, w_bf[k * CONV_DIM:(k + 1) * CONV_DIM, :], nt,
                                  preferred_element_type=jnp.float32) for k in range(3))
    u = cg * xv
    halo = jnp.where(m % per_b == 0, 0.0, halo_scr[...])
    halo_scr[...] = u[u.shape[0] - SUBLANES:, :]
    row = lax.broadcasted_iota(jnp.int32, u.shape, 0)
    h1 = halo[SUBLANES - 1:SUBLANES, :]
    h2 = halo[SUBLANES - 2:SUBLANES - 1, :]
    u1 = jnp.where(row == 0, h1, pltpu.roll(u, 1, 0))
    u2 = jnp.where(row == 0, h2, jnp.where(row == 1, h1, pltpu.roll(u, 2, 0)))
    w = cw_ref[...]
    y = bg * (w[0:1, :] * u2 + w[1:2, :] * u1 + w[2:3, :] * u)
    o_ref[...] = (_rms(y) * g_ref[...]).astype(o_ref.dtype)


def _projconv(h, wt, conv_w, g_conv_out, seq, tm=1024):
    t, d = h.shape
    c = CONV_DIM
    kern = functools.partial(_projconv_kernel, per_b=seq // tm)
    return pl.pallas_call(
        kern,
        grid=(t // tm,),
        in_specs=[pl.BlockSpec((tm, d), lambda m: (m, 0)),
                  pl.BlockSpec(memory_space=pl.ANY),
                  pl.BlockSpec((CONV_WIDTH, c), lambda m: (0, 0)),
                  pl.BlockSpec((1, c), lambda m: (0, 0))],
        out_specs=pl.BlockSpec((tm, c), lambda m: (m, 0)),
        out_shape=jax.ShapeDtypeStruct((t, c), jnp.bfloat16),
        scratch_shapes=[pltpu.VMEM((3 * c, d), jnp.bfloat16),
                        pltpu.VMEM((2, STAGE_ROWS, d), jnp.float32),
                        pltpu.VMEM((SUBLANES, c), jnp.float32),
                        pltpu.SemaphoreType.DMA((2,))],
        compiler_params=_cparams(("arbitrary",)),
        name="proj_conv",
    )(h, wt, conv_w, g_conv_out)


def _ordered_bits(v):
    return v ^ ((v >> 31) & jnp.int32(0x7FFFFFFF))


def _attn_search(nch, sc_scr, k_sel):
    n_keys = float(nch * ATTN_KC)
    half = ATTN_Q // 2

    def half_step(bit, thr, rows):
        cand = thr + bit
        cand_f = lax.bitcast_convert_type(_ordered_bits(cand), jnp.float32)
        hits = jnp.where(sc_scr[0, rows, :] >= cand_f, 1.0, 0.0)
        for c in range(1, nch):
            hits = hits + jnp.where(sc_scr[c, rows, :] >= cand_f, 1.0, 0.0)
        cnt = jnp.sum(hits, axis=1, keepdims=True)
        cnt = jnp.where(cand < KEY_NEG_INF, n_keys, cnt)
        return jnp.where(cnt >= k_sel, cand, thr)

    def bit_step(i, thrs):
        bit = lax.shift_left(jnp.int32(1), 31 - i)
        return tuple(half_step(bit, thr, pl.ds(k * half, half)) for k, thr in enumerate(thrs))

    init = jnp.full((half, 1), INT_MIN, jnp.int32)
    thrs = lax.fori_loop(0, 32, bit_step, (init, init), unroll=2)
    thr = jnp.concatenate(thrs, axis=0)
    return lax.bitcast_convert_type(_ordered_bits(thr), jnp.float32)


def _attn_tiebreak(nch, sc_scr, thr_f, k_sel, tie_scr):
    kloc = lax.broadcasted_iota(jnp.int32, (ATTN_Q, ATTN_KC), 1)

    def count(pred):
        hits = pred(sc_scr[0], 0)
        for c in range(1, nch):
            hits = hits + pred(sc_scr[c], c)
        return jnp.sum(hits, axis=1, keepdims=True)

    tie_scr[...] = jnp.full(tie_scr.shape, TIE_ALL, jnp.int32)
    n_ge = count(lambda sc, c: jnp.where(sc >= thr_f, 1.0, 0.0))
    excess = jnp.where(thr_f > -jnp.inf, n_ge - k_sel, 0.0)

    @pl.when(jnp.max(excess) > 0.0)
    def _():
        need = k_sel - count(lambda sc, c: jnp.where(sc > thr_f, 1.0, 0.0))
        n_bits = (nch * ATTN_KC - 1).bit_length()

        def step(i, lim):
            cand = lim + lax.shift_left(jnp.int32(1), n_bits - 1 - i)
            below = count(lambda sc, c: jnp.where(
                sc == thr_f, jnp.where(c * ATTN_KC + kloc < cand, 1.0, 0.0), 0.0))
            return jnp.where(below < need, cand, lim)

        lim = lax.fori_loop(0, n_bits, step, jnp.zeros((ATTN_Q, 1), jnp.int32))
        tie_scr[...] = jnp.where(excess > 0.0, lim, TIE_ALL)


def _attn_kernel(iq_ref, ikwq_ref, ikwa_ref, kv_ref, q_ref, wuk_ref, wuv_ref, kvg_ref, go_ref,
                 o_ref, ckv_scr, ik_scr, sc_scr, thr_scr, tie_scr, ql_scr, m_scr, l_scr, acc_scr, y_scr,
                 *, seq, k_sel):
    j = pl.program_id(1)
    nch = j + 1
    nt = (((1,), (1,)), ((), ()))

    @pl.when(j == 0)
    def _():
        ckv_scr[...] = (_rms(kv_ref[0]) * kvg_ref[...]).astype(jnp.bfloat16)
        ik_scr[...] = ikwa_ref[0][:, :IDX_DIM].astype(jnp.bfloat16)

    iw = ikwq_ref[0][:, IDX_DIM:IDX_DIM + IDX_HEADS] * (IDX_HEADS ** -0.5 * IDX_DIM ** -0.5)
    qpos = j * ATTN_Q + lax.broadcasted_iota(jnp.int32, (ATTN_Q, ATTN_KC), 0)
    kloc = lax.broadcasted_iota(jnp.int32, (ATTN_Q, ATTN_KC), 1)

    def score_chunk(c, carry):
        k0 = pl.multiple_of(c * ATTN_KC, ATTN_KC)
        ik_c = ik_scr[pl.ds(k0, ATTN_KC), :]
        score = jnp.zeros((ATTN_Q, ATTN_KC), jnp.float32)
        for h in range(IDX_HEADS):
            s = lax.dot_general(iq_ref[0][:, h * IDX_DIM:(h + 1) * IDX_DIM], ik_c, nt,
                                preferred_element_type=jnp.float32)
            score = score + jnp.maximum(s, 0.0) * iw[:, h:h + 1]
        sc_scr[c] = jnp.where(k0 + kloc <= qpos, score, -jnp.inf)
        return carry

    lax.fori_loop(0, nch, score_chunk, 0)

    for v in range(seq // ATTN_KC):
        @pl.when(j == v)
        def _(v=v):
            if (v + 1) * ATTN_KC <= k_sel:
                thr_scr[...] = jnp.full(thr_scr.shape, -jnp.inf, jnp.float32)
                tie_scr[...] = jnp.full(tie_scr.shape, TIE_ALL, jnp.int32)
            else:
                thr_scr[...] = _attn_search(v + 1, sc_scr, k_sel)
                _attn_tiebreak(v + 1, sc_scr, thr_scr[...], k_sel, tie_scr)

    for h in range(N_HEADS):
        ql = lax.dot_general(q_ref[0][:, h * HEAD_DIM:(h + 1) * HEAD_DIM], wuk_ref[h], nt,
                             preferred_element_type=jnp.float32)
        ql_scr[h * ATTN_Q:(h + 1) * ATTN_Q, :] = ql.astype(jnp.bfloat16)
    thr_f = thr_scr[...]
    tie = tie_scr[...]
    scale = HEAD_DIM ** -0.5

    def attn_chunk(c, first):
        k0 = pl.multiple_of(c * ATTN_KC, ATTN_KC)
        ckv_c = ckv_scr[pl.ds(k0, ATTN_KC), :]
        sc = sc_scr[c]
        kpos = k0 + kloc
        tied = jnp.where(sc == thr_f, jnp.where(kpos <= tie, 0.0, MASKED), MASKED)
        bias = jnp.where(kpos <= qpos, jnp.where(sc > thr_f, 0.0, tied), MASKED)
        units = range(N_HEADS * ATTN_Q // ATTN_RB)
        rows_of = [pl.ds(u * ATTN_RB, ATTN_RB) for u in units]
        lgs = []
        for u in units:
            qrow = (u * ATTN_RB) % ATTN_Q
            lgs.append(lax.dot_general(ql_scr[rows_of[u], :], ckv_c, nt,
                                       preferred_element_type=jnp.float32) * scale
                       + bias[qrow:qrow + ATTN_RB])
        ps, alphas = [], []
        for u in units:
            rows = rows_of[u]
            row_max = jnp.max(lgs[u], axis=1, keepdims=True)
            if first:
                m_new = jnp.maximum(jnp.full((ATTN_RB, LANES), MASKED, jnp.float32), row_max)
                alphas.append(None)
            else:
                m_old = m_scr[rows, :]
                m_new = jnp.maximum(m_old, row_max)
                alphas.append(jnp.exp(m_old - m_new))
            p = jnp.exp(lgs[u] - _tile_lanes(m_new, ATTN_KC // LANES))
            p_sum = jnp.sum(p, axis=1, keepdims=True)
            if first:
                l_scr[rows, :] = jnp.broadcast_to(p_sum, (ATTN_RB, LANES))
            else:
                l_scr[rows, :] = alphas[u] * l_scr[rows, :] + p_sum
            m_scr[rows, :] = m_new
            ps.append(p.astype(jnp.bfloat16))
        for u in units:
            rows = rows_of[u]
            pv = jnp.dot(ps[u], ckv_c, preferred_element_type=jnp.float32)
            if first:
                acc_scr[rows, :] = pv
            else:
                acc_scr[rows, :] = _tile_lanes(alphas[u], KV_RANK // LANES) * acc_scr[rows, :] + pv

    attn_chunk(0, True)

    def later_chunk(c, carry):
        attn_chunk(c, False)
        return carry

    lax.fori_loop(1, nch, later_chunk, 0)

    o = (acc_scr[...] * _tile_lanes(1.0 / l_scr[...], KV_RANK // LANES)).astype(jnp.bfloat16)
    for h in range(N_HEADS):
        y_scr[:, h * HEAD_DIM:(h + 1) * HEAD_DIM] = jnp.dot(
            o[h * ATTN_Q:(h + 1) * ATTN_Q], wuv_ref[h], preferred_element_type=jnp.float32)
    o_ref[0] = (_rms(y_scr[...]) * go_ref[...]).astype(o_ref.dtype)


def _attn(iq, ikw, kv, q, w_uk_bf, w_uv_bf, kv_norm_g, g_attn_out):
    b, s, _ = iq.shape
    assert ATTN_Q == ATTN_KC and s % ATTN_Q == 0
    k_sel = min(TOPK_MAX, s // 4)
    rows = N_HEADS * ATTN_Q
    kern = functools.partial(_attn_kernel, seq=s, k_sel=k_sel)
    return pl.pallas_call(
        kern,
        grid=(b, s // ATTN_Q),
        in_specs=[pl.BlockSpec((1, ATTN_Q, IDX_HEADS * IDX_DIM), lambda bi, j: (bi, j, 0)),
                  pl.BlockSpec((1, ATTN_Q, IKW_COLS), lambda bi, j: (bi, j, 0)),
                  pl.BlockSpec((1, s, IKW_COLS), lambda bi, j: (bi, 0, 0)),
                  pl.BlockSpec((1, s, KV_RANK), lambda bi, j: (bi, 0, 0)),
                  pl.BlockSpec((1, ATTN_Q, ATTN_DIM), lambda bi, j: (bi, j, 0)),
                  pl.BlockSpec((N_HEADS, KV_RANK, HEAD_DIM), lambda bi, j: (0, 0, 0)),
                  pl.BlockSpec((N_HEADS, KV_RANK, HEAD_DIM), lambda bi, j: (0, 0, 0)),
                  pl.BlockSpec((1, KV_RANK), lambda bi, j: (0, 0)),
                  pl.BlockSpec((1, ATTN_DIM), lambda bi, j: (0, 0))],
        out_specs=pl.BlockSpec((1, ATTN_Q, ATTN_DIM), lambda bi, j: (bi, j, 0)),
        out_shape=jax.ShapeDtypeStruct((b, s, ATTN_DIM), jnp.bfloat16),
        scratch_shapes=[pltpu.VMEM((s, KV_RANK), jnp.bfloat16),
                        pltpu.VMEM((s, IDX_DIM), jnp.bfloat16),
                        pltpu.VMEM((s // ATTN_KC, ATTN_Q, ATTN_KC), jnp.float32),
                        pltpu.VMEM((ATTN_Q, 1), jnp.float32),
                        pltpu.VMEM((ATTN_Q, 1), jnp.int32),
                        pltpu.VMEM((rows, KV_RANK), jnp.bfloat16),
                        pltpu.VMEM((rows, LANES), jnp.float32),
                        pltpu.VMEM((rows, LANES), jnp.float32),
                        pltpu.VMEM((rows, KV_RANK), jnp.float32),
                        pltpu.VMEM((ATTN_Q, ATTN_DIM), jnp.float32)],
        compiler_params=_cparams(("arbitrary", "arbitrary")),
        name="attn",
    )(iq, ikw, ikw, kv, q, w_uk_bf, w_uv_bf, kv_norm_g, g_attn_out)


R_E1, R_E2, R_W1, R_W2, R_RANK1, R_RANK2 = range(6)


def _out_kernel(mc_ref, ma_ref, w_hbm, x_ref, gt_ref, g_ref, sc_ref, sh_ref, wr_ref, br_ref,
                x1_ref, h2_ref, route_ref, routet_ref, cnt_ref, cnt_scr, tri_scr, w_bf, stage, sem):
    @pl.when(pl.program_id(0) == 0)
    def _():
        cnt_scr[...] = jnp.zeros(cnt_scr.shape, jnp.float32)
        earlier = (lax.broadcasted_iota(jnp.int32, tri_scr.shape, 1)
                   < lax.broadcasted_iota(jnp.int32, tri_scr.shape, 0))
        tri_scr[...] = jnp.where(earlier, 1.0, 0.0).astype(tri_scr.dtype)
        _load_bf16(w_hbm, [(r, STAGE_ROWS, r) for r in range(0, w_bf.shape[0], STAGE_ROWS)],
                   w_bf, stage, sem)

    rs = x_ref.shape[0] // OUT_SUB
    for sub in range(OUT_SUB):
        rows = pl.ds(sub * rs, rs)
        _out_rows(mc_ref[rows, :], ma_ref[rows, :], w_bf, x_ref[rows, :], gt_ref, g_ref, sc_ref, sh_ref,
                  wr_ref, br_ref, x1_ref.at[rows], h2_ref.at[pl.ds(sub * rs * ROW_SUB, rs * ROW_SUB)],
                  route_ref.at[rows], routet_ref.at[:, rows], cnt_scr, tri_scr[0:rs, 0:rs])
    cnt_ref[...] = cnt_scr[...]


def _out_rows(mc, ma, w_bf, x, gt_ref, g_ref, sc_ref, sh_ref, wr_ref, br_ref,
              x1_ref, h2_ref, route_ref, routet_ref, cnt_scr, tri):
    mix = (jnp.dot(mc, w_bf[0:CONV_DIM, :], preferred_element_type=jnp.float32)
           + jnp.dot(ma, w_bf[CONV_DIM:, :], preferred_element_type=jnp.float32))
    x1 = x + gt_ref[0] * mix
    x1_ref[...] = x1
    h2 = _rms(x1) * g_ref[...] * (1.0 + sc_ref[0]) + sh_ref[0]
    _rows_from_2d(h2_ref, h2)

    logit = jnp.dot(h2.astype(jnp.bfloat16), wr_ref[...],
                    preferred_element_type=jnp.float32) + br_ref[...]
    lane = lax.broadcasted_iota(jnp.int32, logit.shape, 1).astype(jnp.float32)
    neg = -jnp.inf
    big = float(ROUTE_LANES)
    is_g = lane < N_GROUPS
    gl = jnp.where(is_g, logit, neg)
    gmax = jnp.max(gl, axis=1, keepdims=True)
    p_group = 1.0 / jnp.sum(jnp.exp(gl - gmax), axis=1, keepdims=True)
    g_sel = jnp.min(jnp.where(is_g & (gl == gmax), lane, big), axis=1, keepdims=True)
    lo = N_GROUPS + g_sel * EXPERTS_PER_GROUP
    in_grp = (lane >= lo) & (lane < lo + EXPERTS_PER_GROUP)
    el = jnp.where(in_grp, logit, neg)
    m1 = jnp.max(el, axis=1, keepdims=True)
    i1 = jnp.min(jnp.where(in_grp & (el == m1), lane, big), axis=1, keepdims=True)
    el2 = jnp.where(lane == i1, neg, el)
    m2 = jnp.max(el2, axis=1, keepdims=True)
    i2 = jnp.min(jnp.where(in_grp & (lane != i1) & (el2 == m2), lane, big), axis=1, keepdims=True)
    r = jnp.exp(m2 - m1)
    w1 = p_group / (1.0 + r)
    w2 = p_group * r / (1.0 + r)

    member = jnp.where(lane == i1, 1.0, jnp.where(lane == i2, 1.0, 0.0))
    before = jnp.dot(tri, member.astype(jnp.bfloat16),
                     preferred_element_type=jnp.float32) + cnt_scr[...]
    rank1 = jnp.sum(jnp.where(lane == i1, before, 0.0), axis=1, keepdims=True)
    rank2 = jnp.sum(jnp.where(lane == i2, before, 0.0), axis=1, keepdims=True)
    cnt_scr[...] = cnt_scr[...] + jnp.sum(member, axis=0, keepdims=True)

    route = jnp.zeros_like(logit)
    for idx, val in ((R_E1, i1 - N_GROUPS), (R_E2, i2 - N_GROUPS), (R_W1, w1), (R_W2, w2),
                     (R_RANK1, rank1), (R_RANK2, rank2)):
        route = jnp.where(lane == idx, val, route)
    route_ref[...] = route
    routet_ref[...] = route.T[0:SUBLANES, :]


def _out(mix_c, mix_a, w_out, x2, gt1, g_ffn, sc2, sh2, w_route, b_route, seq, tm=512):
    t, d = x2.shape
    per_b = seq // tm
    bmap = lambda m: (m // per_b, 0, 0)
    return pl.pallas_call(
        _out_kernel,
        grid=(t // tm,),
        in_specs=[pl.BlockSpec((tm, CONV_DIM), lambda m: (m, 0)),
                  pl.BlockSpec((tm, ATTN_DIM), lambda m: (m, 0)),
                  pl.BlockSpec(memory_space=pl.ANY),
                  pl.BlockSpec((tm, d), lambda m: (m, 0)),
                  pl.BlockSpec((1, 1, d), bmap),
                  pl.BlockSpec((1, d), lambda m: (0, 0)),
                  pl.BlockSpec((1, 1, d), bmap),
                  pl.BlockSpec((1, 1, d), bmap),
                  pl.BlockSpec((d, ROUTE_LANES), lambda m: (0, 0)),
                  pl.BlockSpec((1, ROUTE_LANES), lambda m: (0, 0))],
        out_specs=[pl.BlockSpec((tm, d), lambda m: (m, 0)),
                   pl.BlockSpec((tm * ROW_SUB, LANES), lambda m: (m, 0)),
                   pl.BlockSpec((tm, ROUTE_LANES), lambda m: (m, 0)),
                   pl.BlockSpec((SUBLANES, tm), lambda m: (0, m)),
                   pl.BlockSpec((1, ROUTE_LANES), lambda m: (0, 0))],
        out_shape=[jax.ShapeDtypeStruct((t, d), jnp.float32),
                   jax.ShapeDtypeStruct((t * ROW_SUB, LANES), jnp.float32),
                   jax.ShapeDtypeStruct((t, ROUTE_LANES), jnp.float32),
                   jax.ShapeDtypeStruct((SUBLANES, t), jnp.float32),
                   jax.ShapeDtypeStruct((1, ROUTE_LANES), jnp.float32)],
        scratch_shapes=[pltpu.VMEM((1, ROUTE_LANES), jnp.float32),
                        pltpu.VMEM((tm, tm), jnp.bfloat16),
                        pltpu.VMEM(w_out.shape, jnp.bfloat16),
                        pltpu.VMEM((2, STAGE_ROWS, d), jnp.float32),
                        pltpu.SemaphoreType.DMA((2,))],
        compiler_params=_cparams(("arbitrary",)),
        name="out",
    )(mix_c, mix_a, w_out, x2, gt1, g_ffn, sc2, sh2, w_route, b_route)


def _slot(off_ref, e_ref, r_ref, tok):
    return off_ref[e_ref[tok]] + r_ref[tok]


def _dispatch_kernel(e1_ref, e2_ref, r1_ref, r2_ref, off_ref, zs_ref, zc_ref, nu_ref, h2_ref, xs_ref,
                     zbuf, sem_z, sem_r, *, tm, n_tiles):
    i = pl.program_id(0)

    @pl.when(i == 0)
    def _():
        zbuf[...] = jnp.zeros(zbuf.shape, zbuf.dtype)

        def zero_copy(start):
            rows = pl.ds(pl.multiple_of(start * ROW_SUB, ZERO_ROWS * ROW_SUB), ZERO_ROWS * ROW_SUB)
            return pltpu.make_async_copy(zbuf, xs_ref.at[rows], sem_z)

        def zero_fill(first_row, n_granules):
            def start(k, carry):
                zero_copy(first_row + k * ZERO_ROWS).start()
                return carry

            def wait(k, carry):
                zero_copy(first_row + k * ZERO_ROWS).wait()
                return carry

            lax.fori_loop(0, n_granules, start, 0)
            lax.fori_loop(0, n_granules, wait, 0)

        for e in range(N_EXPERTS):
            zero_fill(zs_ref[e], zc_ref[e])
        zero_fill(nu_ref[0] * MOE_TM, (n_tiles - nu_ref[0]) * (MOE_TM // ZERO_ROWS))

    def issue_row(r, carry):
        tok = i * tm + r
        for e_ref, r_ref in ((e1_ref, r1_ref), (e2_ref, r2_ref)):
            pltpu.make_async_copy(_row(h2_ref, r), _row(xs_ref, _slot(off_ref, e_ref, r_ref, tok)),
                                  sem_r).start()
        return carry

    lax.fori_loop(0, tm, issue_row, 0)
    for _ in range(2):
        pltpu.make_async_copy(h2_ref, xs_ref.at[pl.ds(0, tm * ROW_SUB)], sem_r).wait()


def _dispatch(plan, h2, n_rows, tm=512):
    t = h2.shape[0] // ROW_SUB
    kern = functools.partial(_dispatch_kernel, tm=tm, n_tiles=n_rows // MOE_TM)
    return pl.pallas_call(
        kern,
        grid_spec=pltpu.PrefetchScalarGridSpec(
            num_scalar_prefetch=len(plan),
            grid=(t // tm,),
            in_specs=[pl.BlockSpec((tm * ROW_SUB, LANES), lambda i, *_: (i, 0))],
            out_specs=pl.BlockSpec(memory_space=pl.ANY),
            scratch_shapes=[pltpu.VMEM((ZERO_ROWS * ROW_SUB, LANES), h2.dtype),
                            pltpu.SemaphoreType.DMA(()),
                            pltpu.SemaphoreType.DMA(())]),
        out_shape=jax.ShapeDtypeStruct((n_rows * ROW_SUB, LANES), h2.dtype),
        compiler_params=_cparams(("arbitrary",)),
        name="dispatch",
    )(*plan, h2)


def _mlp_kernel(te_ref, nu_ref, first_ref, nxt_ref, slot_ref, xs_ref, wg_hbm, wu_hbm, wd_hbm, ys_ref,
                wg_f32, wu_f32, wd_f32, wg_bf, wu_bf, wd_bf, sem):
    i = pl.program_id(0)

    def copies(e, s):
        return [pltpu.make_async_copy(hbm.at[e], buf.at[s], sem.at[s])
                for hbm, buf in ((wg_hbm, wg_f32), (wu_hbm, wu_f32), (wd_hbm, wd_f32))]

    @pl.when(i == 0)
    def _():
        for cp in copies(te_ref[0], 0):
            cp.start()

    @pl.when((i < nu_ref[0]) & (first_ref[i] == 1))
    def _():
        s = slot_ref[i]
        for cp in copies(te_ref[i], s):
            cp.wait()
        wg_bf[...] = wg_f32[s].astype(jnp.bfloat16)
        wu_bf[...] = wu_f32[s].astype(jnp.bfloat16)
        wd_bf[...] = wd_f32[s].astype(jnp.bfloat16)

        @pl.when(nxt_ref[i] >= 0)
        def _():
            for cp in copies(nxt_ref[i], 1 - s):
                cp.start()

    @pl.when(i < nu_ref[0])
    def _():
        x = _rows_to_2d(xs_ref, MOE_TM).astype(jnp.bfloat16)
        a = jnp.dot(x, wg_bf[...], preferred_element_type=jnp.float32)
        u = jnp.dot(x, wu_bf[...], preferred_element_type=jnp.float32)
        hid = (a * jax.nn.sigmoid(a)) * u
        _rows_from_2d(ys_ref, jnp.dot(hid.astype(jnp.bfloat16), wd_bf[...],
                                      preferred_element_type=jnp.float32))

    @pl.when(i >= nu_ref[0])
    def _():
        ys_ref[...] = jnp.zeros(ys_ref.shape, ys_ref.dtype)


def _mlp(tiles, xs, w_gate, w_up, w_down):
    n_rows = xs.shape[0] // ROW_SUB
    _, d, f = w_gate.shape
    used = lambda i, te, nu, *_: (jnp.minimum(i, nu[0] - 1), 0)
    return pl.pallas_call(
        _mlp_kernel,
        grid_spec=pltpu.PrefetchScalarGridSpec(
            num_scalar_prefetch=len(tiles),
            grid=(n_rows // MOE_TM,),
            in_specs=[pl.BlockSpec((MOE_TM * ROW_SUB, LANES), used),
                      pl.BlockSpec(memory_space=pl.ANY),
                      pl.BlockSpec(memory_space=pl.ANY),
                      pl.BlockSpec(memory_space=pl.ANY)],
            out_specs=pl.BlockSpec((MOE_TM * ROW_SUB, LANES), lambda i, *_: (i, 0)),
            scratch_shapes=[pltpu.VMEM((2, d, f), jnp.float32),
                            pltpu.VMEM((2, d, f), jnp.float32),
                            pltpu.VMEM((2, f, d), jnp.float32),
                            pltpu.VMEM((d, f), jnp.bfloat16),
                            pltpu.VMEM((d, f), jnp.bfloat16),
                            pltpu.VMEM((f, d), jnp.bfloat16),
                            pltpu.SemaphoreType.DMA((2,))]),
        out_shape=jax.ShapeDtypeStruct((n_rows * ROW_SUB, LANES), jnp.float32),
        compiler_params=_cparams(("arbitrary",)),
        name="mlp",
    )(*tiles, xs, w_gate, w_up, w_down)


def _combine_kernel(e1_ref, e2_ref, r1_ref, r2_ref, off_ref, ys_ref, x1_ref, route_ref, gt_ref, g_ref,
                    sc_ref, sh_ref, o_ref, a0, b0, a1, b1, sem, *, tm, n_tiles):
    i = pl.program_id(0)
    bufs = ((a0, b0), (a1, b1))

    def issue(tile, sl):
        def issue_row(r, carry):
            tok = tile * tm + r
            for buf, e_ref, r_ref in ((bufs[sl][0], e1_ref, r1_ref), (bufs[sl][1], e2_ref, r2_ref)):
                pltpu.make_async_copy(_row(ys_ref, _slot(off_ref, e_ref, r_ref, tok)),
                                      _row(buf, r), sem.at[sl]).start()
            return carry

        lax.fori_loop(0, tm, issue_row, 0)

    @pl.when(i == 0)
    def _():
        issue(0, 0)

    for sl in range(2):
        @pl.when(i % 2 == sl)
        def _(sl=sl):
            @pl.when(i + 1 < n_tiles)
            def _():
                issue(i + 1, 1 - sl)

            for buf in bufs[sl]:
                pltpu.make_async_copy(ys_ref.at[pl.ds(0, tm * ROW_SUB)], buf, sem.at[sl]).wait()

            route = route_ref[...]
            y = (route[:, R_W1:R_W1 + 1] * _rows_to_2d(bufs[sl][0], tm)
                 + route[:, R_W2:R_W2 + 1] * _rows_to_2d(bufs[sl][1], tm))
            x = x1_ref[...] + gt_ref[0] * y
            o_ref[...] = _rms(x) * g_ref[...] * (1.0 + sc_ref[0]) + sh_ref[0]


def _combine(plan, ys, x1, route, gt2, g_final, scf, shf, seq, tm=512):
    t, d = x1.shape
    per_b = seq // tm
    n_tiles = t // tm
    bmap = lambda m, *_: (m // per_b, 0, 0)
    kern = functools.partial(_combine_kernel, tm=tm, n_tiles=n_tiles)
    return pl.pallas_call(
        kern,
        grid_spec=pltpu.PrefetchScalarGridSpec(
            num_scalar_prefetch=len(plan),
            grid=(n_tiles,),
            in_specs=[pl.BlockSpec(memory_space=pl.ANY),
                      pl.BlockSpec((tm, d), lambda m, *_: (m, 0)),
                      pl.BlockSpec((tm, ROUTE_LANES), lambda m, *_: (m, 0)),
                      pl.BlockSpec((1, 1, d), bmap),
                      pl.BlockSpec((1, d), lambda m, *_: (0, 0)),
                      pl.BlockSpec((1, 1, d), bmap),
                      pl.BlockSpec((1, 1, d), bmap)],
            out_specs=pl.BlockSpec((tm, d), lambda m, *_: (m, 0)),
            scratch_shapes=[pltpu.VMEM((tm * ROW_SUB, LANES), jnp.float32) for _ in range(4)]
            + [pltpu.SemaphoreType.DMA((2,))]),
        out_shape=jax.ShapeDtypeStruct((t, d), jnp.float32),
        compiler_params=_cparams(("arbitrary",)),
        name="combine",
    )(*plan, ys, x1, route, gt2, g_final, scf, shf)


def _moe_plan(route_t, counts, t):
    i32 = jnp.int32
    cnt = counts[0, N_GROUPS:N_GROUPS + N_EXPERTS].astype(i32)
    padded = (cnt + MOE_TM - 1) // MOE_TM * MOE_TM
    off_end = jnp.cumsum(padded)
    off = off_end - padded
    n_tiles = (2 * t + N_EXPERTS * MOE_TM) // MOE_TM
    n_used = off_end[-1] // MOE_TM
    tile_start = jnp.arange(n_tiles, dtype=i32) * MOE_TM
    te = jnp.sum((off_end[None, :] <= tile_start[:, None]).astype(i32), axis=1)
    te = jnp.minimum(te, N_EXPERTS - 1)
    tile = jnp.arange(n_tiles, dtype=i32)
    te = jnp.where(tile < n_used, te, te[n_used - 1])
    first = jnp.concatenate([jnp.ones((1,), i32), (te[1:] != te[:-1]).astype(i32)])
    nxt_tile = off_end[te] // MOE_TM
    nxt = jnp.where(nxt_tile < n_used, te[jnp.minimum(nxt_tile, n_tiles - 1)], -1)
    slot = (jnp.cumsum(first) - 1) % 2
    assign = tuple(route_t[k].astype(i32) for k in (R_E1, R_E2, R_RANK1, R_RANK2))
    zstart = (off + cnt) // ZERO_ROWS * ZERO_ROWS
    zcount = (off_end - zstart) // ZERO_ROWS
    n_used = n_used.reshape(1)
    return assign, off, (zstart, zcount), (te, n_used, first, nxt, slot), n_tiles * MOE_TM


def kernel(x, c, w_ada, b_ada, g_mix, w_in, conv_w, w_uk, kv_norm_g, w_uv, g_conv_out, g_attn_out,
           w_out, g_ffn, w_rg, b_rg, w_re, b_re, w_gate, w_up, w_down, w_ada_f, b_ada_f, g_final):
    b, s, d = x.shape
    assert w_ada.shape[0] == 1, "single layer"
    bf = jnp.bfloat16
    x2 = x.reshape(b * s, d)

    c_pad = jnp.zeros((SUBLANES, d), jnp.float32).at[:b].set(c)
    mod = _mod(c_pad, w_ada[0], b_ada[0])[:b]
    modf = _mod(c_pad, w_ada_f, b_ada_f)[:b]
    vec = lambda a, i: a[:, i * d:(i + 1) * d].reshape(b, 1, d)
    sh1, sc1, gt1, sh2, sc2, gt2 = (vec(mod, i) for i in range(N_MOD))
    shf, scf = vec(modf, 0), vec(modf, 1)
    row = lambda a: a.reshape(1, -1)

    w_in_t = jnp.swapaxes(w_in[0], 0, 1)
    h, q, kv, iq, ikw = _hproj(x2, row(g_mix[0]), sc1, sh1, w_in_t, s)
    mix_c = _projconv(h, w_in_t, conv_w[0], row(g_conv_out[0]), s)

    mix_a = _attn(iq.reshape(b, s, -1), ikw.reshape(b, s, -1), kv.reshape(b, s, -1),
                  q.reshape(b, s, -1), w_uk[0].astype(bf), w_uv[0].astype(bf),
                  row(kv_norm_g[0]), row(g_attn_out[0]))

    w_route = jnp.zeros((d, ROUTE_LANES), bf).at[:, :N_GROUPS].set(w_rg[0].astype(bf))
    w_route = w_route.at[:, N_GROUPS:N_GROUPS + N_EXPERTS].set(w_re[0].astype(bf))
    b_route = jnp.zeros((1, ROUTE_LANES), jnp.float32).at[0, :N_GROUPS].set(b_rg[0])
    b_route = b_route.at[0, N_GROUPS:N_GROUPS + N_EXPERTS].set(b_re[0])
    x1, h2, route, route_t, counts = _out(mix_c, mix_a.reshape(b * s, -1),
                                 w_out[0], x2, gt1, row(g_ffn[0]),
                                 sc2, sh2, w_route, b_route, s)

    assign, off, zero, tiles, n_rows = _moe_plan(route_t, counts, b * s)
    xs = _dispatch((*assign, off, *zero, tiles[1]), h2, n_rows)
    ys = _mlp(tiles, xs, w_gate[0], w_up[0], w_down[0])
    out = _combine((*assign, off), ys, x1, route, gt2, row(g_final), scf, shf, s)
    return out.reshape(b, s, d)
```

```python
import functools

import jax
import jax.numpy as jnp
from jax import lax
from jax.experimental import pallas as pl
from jax.experimental.pallas import tpu as pltpu

D_MODEL = 2048
CONV_DIM = 1024
CONV_WIDTH = 3
N_HEADS = 8
HEAD_DIM = 128
ATTN_DIM = N_HEADS * HEAD_DIM
KV_RANK = 512
IDX_HEADS = 16
IDX_DIM = 128
TOPK_MAX = 256
N_GROUPS = 4
EXPERTS_PER_GROUP = 8
N_EXPERTS = N_GROUPS * EXPERTS_PER_GROUP
EXPERT_FF = 512
N_MOD = 6
EPS = 1e-6

OFF_Q = 3 * CONV_DIM
OFF_KV = OFF_Q + ATTN_DIM
OFF_IQ = OFF_KV + KV_RANK
OFF_IK = OFF_IQ + IDX_HEADS * IDX_DIM
OFF_IW = OFF_IK + IDX_DIM
IN_COLS = OFF_IW + IDX_HEADS

LANES = 128
SUBLANES = 8
VMEM_LIMIT = 56 * 1024 * 1024

IKW_COLS = 256
ATTN_Q = 256
ATTN_KC = 256
ATTN_RB = 256
MASKED = -1e30
TIE_ALL = 2 ** 30
ROUTE_LANES = 128
STAGE_ROWS = 256
OUT_SUB = 2
MOE_TM = 256
ZERO_ROWS = 64
ROW_SUB = D_MODEL // LANES
INT_MIN = -2 ** 31
KEY_NEG_INF = INT_MIN + 0x7FFFFF


def _cparams(sem):
    return pltpu.CompilerParams(dimension_semantics=sem, vmem_limit_bytes=VMEM_LIMIT)


def _rms(v, axis=-1):
    return v * lax.rsqrt(jnp.mean(v * v, axis=axis, keepdims=True) + EPS)


def _tile_lanes(v, n):
    return jnp.concatenate([v] * n, axis=1)


def _rows_to_2d(ref, n):
    return jnp.concatenate([ref[pl.ds(c, n, stride=ROW_SUB), :] for c in range(ROW_SUB)], axis=1)


def _rows_from_2d(ref, val):
    n = val.shape[0]
    for c in range(ROW_SUB):
        ref[pl.ds(c, n, stride=ROW_SUB), :] = val[:, c * LANES:(c + 1) * LANES]


def _row(ref, i):
    return ref.at[pl.ds(pl.multiple_of(i * ROW_SUB, ROW_SUB), ROW_SUB)]


def _mod_spec(vec, per_b):
    idx = vec[1]
    return pl.BlockSpec((1, 1, D_MODEL), lambda m, *_: (m // per_b, 0, idx))


def _load_bf16(w_hbm, pieces, w_bf, stage, sem):
    def copy(i):
        src, n, _ = pieces[i]
        return pltpu.make_async_copy(w_hbm.at[pl.ds(src, n)], stage.at[i % 2, pl.ds(0, n)],
                                     sem.at[i % 2])

    copy(0).start()
    for i, (_, n, dst) in enumerate(pieces):
        if i + 1 < len(pieces):
            copy(i + 1).start()
        copy(i).wait()
        w_bf[pl.ds(dst, n), :] = stage[i % 2, pl.ds(0, n), :].astype(jnp.bfloat16)


def _mod_kernel(c_ref, w_ref, b_ref, o_ref):
    c = c_ref[...]
    ca = (c * jax.nn.sigmoid(c)).astype(jnp.bfloat16)
    o_ref[...] = jnp.dot(ca, w_ref[...].astype(jnp.bfloat16),
                         preferred_element_type=jnp.float32) + b_ref[...]


def _mod(c_pad, w, b, tn=1024):
    d, n = w.shape
    return pl.pallas_call(
        _mod_kernel,
        grid=(n // tn,),
        in_specs=[pl.BlockSpec((c_pad.shape[0], d), lambda j: (0, 0)),
                  pl.BlockSpec((d, tn), lambda j: (0, j)),
                  pl.BlockSpec((1, tn), lambda j: (0, j))],
        out_specs=pl.BlockSpec((c_pad.shape[0], tn), lambda j: (0, j)),
        out_shape=jax.ShapeDtypeStruct((c_pad.shape[0], n), jnp.float32),
        compiler_params=_cparams(("arbitrary",)),
        name="mod",
    )(c_pad, w, b.reshape(1, n))


ATT_COLS = IN_COLS - OFF_Q
ATT_PAD = OFF_IK - OFF_Q + IKW_COLS


def _hproj_kernel(x_ref, g_ref, sc_ref, sh_ref, wt_hbm, h_ref, q_ref, kv_ref, iq_ref, ikw_ref,
                  w_bf, stage, sem):
    @pl.when(pl.program_id(0) == 0)
    def _():
        full = ATT_COLS // STAGE_ROWS
        pieces = [(OFF_Q + k * STAGE_ROWS, STAGE_ROWS, k * STAGE_ROWS) for k in range(full)]
        pieces.append((OFF_Q + full * STAGE_ROWS, ATT_COLS - full * STAGE_ROWS, full * STAGE_ROWS))
        _load_bf16(wt_hbm, pieces, w_bf, stage, sem)
        w_bf[pl.ds(ATT_COLS, ATT_PAD - ATT_COLS), :] = jnp.zeros((ATT_PAD - ATT_COLS, w_bf.shape[1]),
                                                                 jnp.bfloat16)

    h = (_rms(x_ref[...]) * g_ref[...] * (1.0 + sc_ref[0]) + sh_ref[0]).astype(h_ref.dtype)
    h_ref[...] = h
    nt = (((1,), (1,)), ((), ()))
    row0 = 0
    for o_ref in (q_ref, kv_ref, iq_ref, ikw_ref):
        n = o_ref.shape[1]
        o_ref[...] = lax.dot_general(h, w_bf[row0:row0 + n, :], nt,
                                     preferred_element_type=jnp.float32).astype(o_ref.dtype)
        row0 += n


def _hproj(x2, g_mix, sc1, sh1, wt, seq, tm=512):
    t, d = x2.shape
    per_b = seq // tm
    widths = (ATTN_DIM, KV_RANK, IDX_HEADS * IDX_DIM, IKW_COLS)
    dtypes = (jnp.bfloat16, jnp.float32, jnp.bfloat16, jnp.float32)
    assert sum(widths) == ATT_PAD and (ATT_COLS % STAGE_ROWS) % SUBLANES == 0
    return pl.pallas_call(
        _hproj_kernel,
        grid=(t // tm,),
        in_specs=[pl.BlockSpec((tm, d), lambda m: (m, 0)),
                  pl.BlockSpec((1, d), lambda m: (0, 0)),
                  _mod_spec(sc1, per_b),
                  _mod_spec(sh1, per_b),
                  pl.BlockSpec(memory_space=pl.ANY)],
        out_specs=[pl.BlockSpec((tm, d), lambda m: (m, 0))]
        + [pl.BlockSpec((tm, w), lambda m: (m, 0)) for w in widths],
        out_shape=[jax.ShapeDtypeStruct((t, d), jnp.bfloat16)]
        + [jax.ShapeDtypeStruct((t, w), dt) for w, dt in zip(widths, dtypes)],
        scratch_shapes=[pltpu.VMEM((ATT_PAD, d), jnp.bfloat16),
                        pltpu.VMEM((2, STAGE_ROWS, d), jnp.float32),
                        pltpu.SemaphoreType.DMA((2,))],
        compiler_params=_cparams(("arbitrary",)),
        name="hproj",
    )(x2, g_mix, sc1[0], sh1[0], wt)


def _projconv_kernel(h_ref, wt_hbm, cw_ref, g_ref, o_ref, w_bf, stage, halo_scr, sem, *, per_b):
    m = pl.program_id(0)

    @pl.when(m == 0)
    def _():
        _load_bf16(wt_hbm, [(r, STAGE_ROWS, r) for r in range(0, w_bf.shape[0], STAGE_ROWS)],
                   w_bf, stage, sem)

    nt = (((1,), (1,)), ((), ()))
    h = h_ref[...]
    bg, cg, xv = (lax.dot_general(h, w_bf[k * CONV_DIM:(k + 1) * CONV_DIM, :], nt,
                                  preferred_element_type=jnp.float32) for k in range(3))
    u = cg * xv
    halo = jnp.where(m % per_b == 0, 0.0, halo_scr[...])
    halo_scr[...] = u[u.shape[0] - SUBLANES:, :]
    row = lax.broadcasted_iota(jnp.int32, u.shape, 0)
    h1 = halo[SUBLANES - 1:SUBLANES, :]
    h2 = halo[SUBLANES - 2:SUBLANES - 1, :]
    u1 = jnp.where(row == 0, h1, pltpu.roll(u, 1, 0))
    u2 = jnp.where(row == 0, h2, jnp.where(row == 1, h1, pltpu.roll(u, 2, 0)))
    w = cw_ref[...]
    y = bg * (w[0:1, :] * u2 + w[1:2, :] * u1 + w[2:3, :] * u)
    o_ref[...] = (_rms(y) * g_ref[...]).astype(o_ref.dtype)


def _projconv(h, wt, conv_w, g_conv_out, seq, tm=1024):
    t, d = h.shape
    c = CONV_DIM
    kern = functools.partial(_projconv_kernel, per_b=seq // tm)
    return pl.pallas_call(
        kern,
        grid=(t // tm,),
        in_specs=[pl.BlockSpec((tm, d), lambda m: (m, 0)),
                  pl.BlockSpec(memory_space=pl.ANY),
                  pl.BlockSpec((CONV_WIDTH, c), lambda m: (0, 0)),
                  pl.BlockSpec((1, c), lambda m: (0, 0))],
        out_specs=pl.BlockSpec((tm, c), lambda m: (m, 0)),
        out_shape=jax.ShapeDtypeStruct((t, c), jnp.bfloat16),
        scratch_shapes=[pltpu.VMEM((3 * c, d), jnp.bfloat16),
                        pltpu.VMEM((2, STAGE_ROWS, d), jnp.float32),
                        pltpu.VMEM((SUBLANES, c), jnp.float32),
                        pltpu.SemaphoreType.DMA((2,))],
        compiler_params=_cparams(("arbitrary",)),
        name="proj_conv",
    )(h, wt, conv_w, g_conv_out)


def _ordered_bits(v):
    return v ^ ((v >> 31) & jnp.int32(0x7FFFFFFF))


def _attn_search(nch, sc_scr, k_sel):
    n_keys = float(nch * ATTN_KC)
    half = ATTN_Q // 2

    def half_step(bit, thr, rows):
        cand = thr + bit
        cand_f = lax.bitcast_convert_type(_ordered_bits(cand), jnp.float32)
        hits = jnp.where(sc_scr[0, rows, :] >= cand_f, 1.0, 0.0)
        for c in range(1, nch):
            hits = hits + jnp.where(sc_scr[c, rows, :] >= cand_f, 1.0, 0.0)
        cnt = jnp.sum(hits, axis=1, keepdims=True)
        cnt = jnp.where(cand < KEY_NEG_INF, n_keys, cnt)
        return jnp.where(cnt >= k_sel, cand, thr)

    def bit_step(i, thrs):
        bit = lax.shift_left(jnp.int32(1), 31 - i)
        return tuple(half_step(bit, thr, pl.ds(k * half, half)) for k, thr in enumerate(thrs))

    init = jnp.full((half, 1), INT_MIN, jnp.int32)
    thrs = lax.fori_loop(0, 32, bit_step, (init, init), unroll=2)
    thr = jnp.concatenate(thrs, axis=0)
    return lax.bitcast_convert_type(_ordered_bits(thr), jnp.float32)


def _attn_tiebreak(nch, sc_scr, thr_f, k_sel, tie_scr):
    kloc = lax.broadcasted_iota(jnp.int32, (ATTN_Q, ATTN_KC), 1)

    def count(pred):
        hits = pred(sc_scr[0], 0)
        for c in range(1, nch):
            hits = hits + pred(sc_scr[c], c)
        return jnp.sum(hits, axis=1, keepdims=True)

    tie_scr[...] = jnp.full(tie_scr.shape, TIE_ALL, jnp.int32)
    n_ge = count(lambda sc, c: jnp.where(sc >= thr_f, 1.0, 0.0))
    excess = jnp.where(thr_f > -jnp.inf, n_ge - k_sel, 0.0)

    @pl.when(jnp.max(excess) > 0.0)
    def _():
        need = k_sel - count(lambda sc, c: jnp.where(sc > thr_f, 1.0, 0.0))
        n_bits = (nch * ATTN_KC - 1).bit_length()

        def step(i, lim):
            cand = lim + lax.shift_left(jnp.int32(1), n_bits - 1 - i)
            below = count(lambda sc, c: jnp.where(
                sc == thr_f, jnp.where(c * ATTN_KC + kloc < cand, 1.0, 0.0), 0.0))
            return jnp.where(below < need, cand, lim)

        lim = lax.fori_loop(0, n_bits, step, jnp.zeros((ATTN_Q, 1), jnp.int32))
        tie_scr[...] = jnp.where(excess > 0.0, lim, TIE_ALL)


def _attn_kernel(iq_ref, ikwq_ref, ikwa_ref, kv_ref, q_ref, wuk_ref, wuv_ref, kvg_ref, go_ref,
                 o_ref, ckv_scr, ik_scr, sc_scr, thr_scr, tie_scr, ql_scr, m_scr, l_scr, acc_scr, y_scr,
                 *, seq, k_sel):
    j = pl.program_id(1)
    nch = j + 1
    nt = (((1,), (1,)), ((), ()))

    @pl.when(j == 0)
    def _():
        ckv_scr[...] = (_rms(kv_ref[0]) * kvg_ref[...]).astype(jnp.bfloat16)
        ik_scr[...] = ikwa_ref[0][:, :IDX_DIM].astype(jnp.bfloat16)

    iw = ikwq_ref[0][:, IDX_DIM:IDX_DIM + IDX_HEADS] * (IDX_HEADS ** -0.5 * IDX_DIM ** -0.5)
    qpos = j * ATTN_Q + lax.broadcasted_iota(jnp.int32, (ATTN_Q, ATTN_KC), 0)
    kloc = lax.broadcasted_iota(jnp.int32, (ATTN_Q, ATTN_KC), 1)

    def score_chunk(c, carry):
        k0 = pl.multiple_of(c * ATTN_KC, ATTN_KC)
        ik_c = ik_scr[pl.ds(k0, ATTN_KC), :]
        score = jnp.zeros((ATTN_Q, ATTN_KC), jnp.float32)
        for h in range(IDX_HEADS):
            s = lax.dot_general(iq_ref[0][:, h * IDX_DIM:(h + 1) * IDX_DIM], ik_c, nt,
                                preferred_element_type=jnp.float32)
            score = score + jnp.maximum(s, 0.0) * iw[:, h:h + 1]
        sc_scr[c] = jnp.where(k0 + kloc <= qpos, score, -jnp.inf)
        return carry

    lax.fori_loop(0, nch, score_chunk, 0)

    for v in range(seq // ATTN_KC):
        @pl.when(j == v)
        def _(v=v):
            if (v + 1) * ATTN_KC <= k_sel:
                thr_scr[...] = jnp.full(thr_scr.shape, -jnp.inf, jnp.float32)
                tie_scr[...] = jnp.full(tie_scr.shape, TIE_ALL, jnp.int32)
            else:
                thr_scr[...] = _attn_search(v + 1, sc_scr, k_sel)
                _attn_tiebreak(v + 1, sc_scr, thr_scr[...], k_sel, tie_scr)

    for h in range(N_HEADS):
        ql = lax.dot_general(q_ref[0][:, h * HEAD_DIM:(h + 1) * HEAD_DIM], wuk_ref[h], nt,
                             preferred_element_type=jnp.float32)
        ql_scr[h * ATTN_Q:(h + 1) * ATTN_Q, :] = ql.astype(jnp.bfloat16)
    thr_f = thr_scr[...]
    tie = tie_scr[...]
    scale = HEAD_DIM ** -0.5

    def attn_chunk(c, first):
        k0 = pl.multiple_of(c * ATTN_KC, ATTN_KC)
        ckv_c = ckv_scr[pl.ds(k0, ATTN_KC), :]
        sc = sc_scr[c]
        kpos = k0 + kloc
        tied = jnp.where(sc == thr_f, jnp.where(kpos <= tie, 0.0, MASKED), MASKED)
        bias = jnp.where(kpos <= qpos, jnp.where(sc > thr_f, 0.0, tied), MASKED)
        for u in range(N_HEADS * ATTN_Q // ATTN_RB):
            rows = pl.ds(u * ATTN_RB, ATTN_RB)
            qrow = (u * ATTN_RB) % ATTN_Q
            lg = lax.dot_general(ql_scr[rows, :], ckv_c, nt,
                                 preferred_element_type=jnp.float32) * scale + bias[qrow:qrow + ATTN_RB]
            row_max = jnp.max(lg, axis=1, keepdims=True)
            if first:
                m_new = jnp.maximum(jnp.full((ATTN_RB, LANES), MASKED, jnp.float32), row_max)
            else:
                m_old = m_scr[rows, :]
                m_new = jnp.maximum(m_old, row_max)
                alpha = jnp.exp(m_old - m_new)
            p = jnp.exp(lg - _tile_lanes(m_new, ATTN_KC // LANES))
            p_sum = jnp.sum(p, axis=1, keepdims=True)
            pv = jnp.dot(p.astype(jnp.bfloat16), ckv_c, preferred_element_type=jnp.float32)
            if first:
                l_scr[rows, :] = jnp.broadcast_to(p_sum, (ATTN_RB, LANES))
                acc_scr[rows, :] = pv
            else:
                l_scr[rows, :] = alpha * l_scr[rows, :] + p_sum
                acc_scr[rows, :] = _tile_lanes(alpha, KV_RANK // LANES) * acc_scr[rows, :] + pv
            m_scr[rows, :] = m_new

    attn_chunk(0, True)

    def later_chunk(c, carry):
        attn_chunk(c, False)
        return carry

    lax.fori_loop(1, nch, later_chunk, 0)

    o = (acc_scr[...] * _tile_lanes(1.0 / l_scr[...], KV_RANK // LANES)).astype(jnp.bfloat16)
    for h in range(N_HEADS):
        y_scr[:, h * HEAD_DIM:(h + 1) * HEAD_DIM] = jnp.dot(
            o[h * ATTN_Q:(h + 1) * ATTN_Q], wuv_ref[h], preferred_element_type=jnp.float32)
    o_ref[0] = (_rms(y_scr[...]) * go_ref[...]).astype(o_ref.dtype)


def _attn(iq, ikw, kv, q, w_uk_bf, w_uv_bf, kv_norm_g, g_attn_out):
    b, s, _ = iq.shape
    assert ATTN_Q == ATTN_KC and s % ATTN_Q == 0
    k_sel = min(TOPK_MAX, s // 4)
    rows = N_HEADS * ATTN_Q
    kern = functools.partial(_attn_kernel, seq=s, k_sel=k_sel)
    return pl.pallas_call(
        kern,
        grid=(b, s // ATTN_Q),
        in_specs=[pl.BlockSpec((1, ATTN_Q, IDX_HEADS * IDX_DIM), lambda bi, j: (bi, j, 0)),
                  pl.BlockSpec((1, ATTN_Q, IKW_COLS), lambda bi, j: (bi, j, 0)),
                  pl.BlockSpec((1, s, IKW_COLS), lambda bi, j: (bi, 0, 0)),
                  pl.BlockSpec((1, s, KV_RANK), lambda bi, j: (bi, 0, 0)),
                  pl.BlockSpec((1, ATTN_Q, ATTN_DIM), lambda bi, j: (bi, j, 0)),
                  pl.BlockSpec((N_HEADS, KV_RANK, HEAD_DIM), lambda bi, j: (0, 0, 0)),
                  pl.BlockSpec((N_HEADS, KV_RANK, HEAD_DIM), lambda bi, j: (0, 0, 0)),
                  pl.BlockSpec((1, KV_RANK), lambda bi, j: (0, 0)),
                  pl.BlockSpec((1, ATTN_DIM), lambda bi, j: (0, 0))],
        out_specs=pl.BlockSpec((1, ATTN_Q, ATTN_DIM), lambda bi, j: (bi, j, 0)),
        out_shape=jax.ShapeDtypeStruct((b, s, ATTN_DIM), jnp.bfloat16),
        scratch_shapes=[pltpu.VMEM((s, KV_RANK), jnp.bfloat16),
                        pltpu.VMEM((s, IDX_DIM), jnp.bfloat16),
                        pltpu.VMEM((s // ATTN_KC, ATTN_Q, ATTN_KC), jnp.float32),
                        pltpu.VMEM((ATTN_Q, 1), jnp.float32),
                        pltpu.VMEM((ATTN_Q, 1), jnp.int32),
                        pltpu.VMEM((rows, KV_RANK), jnp.bfloat16),
                        pltpu.VMEM((rows, LANES), jnp.float32),
                        pltpu.VMEM((rows, LANES), jnp.float32),
                        pltpu.VMEM((rows, KV_RANK), jnp.float32),
                        pltpu.VMEM((ATTN_Q, ATTN_DIM), jnp.float32)],
        compiler_params=_cparams(("arbitrary", "arbitrary")),
        name="attn",
    )(iq, ikw, ikw, kv, q, w_uk_bf, w_uv_bf, kv_norm_g, g_attn_out)


R_E1, R_E2, R_W1, R_W2, R_RANK1, R_RANK2 = range(6)


def _out_kernel(mc_ref, ma_ref, w_hbm, x_ref, gt_ref, g_ref, sc_ref, sh_ref, wr_ref, br_ref,
                x1_ref, h2_ref, route_ref, routet_ref, cnt_ref, cnt_scr, tri_scr, w_bf, stage, sem):
    @pl.when(pl.program_id(0) == 0)
    def _():
        cnt_scr[...] = jnp.zeros(cnt_scr.shape, jnp.float32)
        earlier = (lax.broadcasted_iota(jnp.int32, tri_scr.shape, 1)
                   < lax.broadcasted_iota(jnp.int32, tri_scr.shape, 0))
        tri_scr[...] = jnp.where(earlier, 1.0, 0.0).astype(tri_scr.dtype)
        _load_bf16(w_hbm, [(r, STAGE_ROWS, r) for r in range(0, w_bf.shape[0], STAGE_ROWS)],
                   w_bf, stage, sem)

    rs = x_ref.shape[0] // OUT_SUB
    for sub in range(OUT_SUB):
        rows = pl.ds(sub * rs, rs)
        _out_rows(mc_ref[rows, :], ma_ref[rows, :], w_bf, x_ref[rows, :], gt_ref, g_ref, sc_ref, sh_ref,
                  wr_ref, br_ref, x1_ref.at[rows], h2_ref.at[pl.ds(sub * rs * ROW_SUB, rs * ROW_SUB)],
                  route_ref.at[rows], routet_ref.at[:, rows], cnt_scr, tri_scr[0:rs, 0:rs])
    cnt_ref[...] = cnt_scr[...]


def _out_rows(mc, ma, w_bf, x, gt_ref, g_ref, sc_ref, sh_ref, wr_ref, br_ref,
              x1_ref, h2_ref, route_ref, routet_ref, cnt_scr, tri):
    mix = (jnp.dot(mc, w_bf[0:CONV_DIM, :], preferred_element_type=jnp.float32)
           + jnp.dot(ma, w_bf[CONV_DIM:, :], preferred_element_type=jnp.float32))
    x1 = x + gt_ref[0] * mix
    x1_ref[...] = x1
    h2 = _rms(x1) * g_ref[...] * (1.0 + sc_ref[0]) + sh_ref[0]
    _rows_from_2d(h2_ref, h2)

    logit = jnp.dot(h2.astype(jnp.bfloat16), wr_ref[...],
                    preferred_element_type=jnp.float32) + br_ref[...]
    lane = lax.broadcasted_iota(jnp.int32, logit.shape, 1).astype(jnp.float32)
    neg = -jnp.inf
    big = float(ROUTE_LANES)
    is_g = lane < N_GROUPS
    gl = jnp.where(is_g, logit, neg)
    gmax = jnp.max(gl, axis=1, keepdims=True)
    p_group = 1.0 / jnp.sum(jnp.exp(gl - gmax), axis=1, keepdims=True)
    g_sel = jnp.min(jnp.where(is_g & (gl == gmax), lane, big), axis=1, keepdims=True)
    lo = N_GROUPS + g_sel * EXPERTS_PER_GROUP
    in_grp = (lane >= lo) & (lane < lo + EXPERTS_PER_GROUP)
    el = jnp.where(in_grp, logit, neg)
    m1 = jnp.max(el, axis=1, keepdims=True)
    i1 = jnp.min(jnp.where(in_grp & (el == m1), lane, big), axis=1, keepdims=True)
    el2 = jnp.where(lane == i1, neg, el)
    m2 = jnp.max(el2, axis=1, keepdims=True)
    i2 = jnp.min(jnp.where(in_grp & (lane != i1) & (el2 == m2), lane, big), axis=1, keepdims=True)
    r = jnp.exp(m2 - m1)
    w1 = p_group / (1.0 + r)
    w2 = p_group * r / (1.0 + r)

    member = jnp.where(lane == i1, 1.0, jnp.where(lane == i2, 1.0, 0.0))
    before = jnp.dot(tri, member.astype(jnp.bfloat16),
                     preferred_element_type=jnp.float32) + cnt_scr[...]
    rank1 = jnp.sum(jnp.where(lane == i1, before, 0.0), axis=1, keepdims=True)
    rank2 = jnp.sum(jnp.where(lane == i2, before, 0.0), axis=1, keepdims=True)
    cnt_scr[...] = cnt_scr[...] + jnp.sum(member, axis=0, keepdims=True)

    route = jnp.zeros_like(logit)
    for idx, val in ((R_E1, i1 - N_GROUPS), (R_E2, i2 - N_GROUPS), (R_W1, w1), (R_W2, w2),
                     (R_RANK1, rank1), (R_RANK2, rank2)):
        route = jnp.where(lane == idx, val, route)
    route_ref[...] = route
    routet_ref[...] = route.T[0:SUBLANES, :]


def _out(mix_c, mix_a, w_out, x2, gt1, g_ffn, sc2, sh2, w_route, b_route, seq, tm=512):
    t, d = x2.shape
    per_b = seq // tm
    return pl.pallas_call(
        _out_kernel,
        grid=(t // tm,),
        in_specs=[pl.BlockSpec((tm, CONV_DIM), lambda m: (m, 0)),
                  pl.BlockSpec((tm, ATTN_DIM), lambda m: (m, 0)),
                  pl.BlockSpec(memory_space=pl.ANY),
                  pl.BlockSpec((tm, d), lambda m: (m, 0)),
                  _mod_spec(gt1, per_b),
                  pl.BlockSpec((1, d), lambda m: (0, 0)),
                  _mod_spec(sc2, per_b),
                  _mod_spec(sh2, per_b),
                  pl.BlockSpec((d, ROUTE_LANES), lambda m: (0, 0)),
                  pl.BlockSpec((1, ROUTE_LANES), lambda m: (0, 0))],
        out_specs=[pl.BlockSpec((tm, d), lambda m: (m, 0)),
                   pl.BlockSpec((tm * ROW_SUB, LANES), lambda m: (m, 0)),
                   pl.BlockSpec((tm, ROUTE_LANES), lambda m: (m, 0)),
                   pl.BlockSpec((SUBLANES, tm), lambda m: (0, m)),
                   pl.BlockSpec((1, ROUTE_LANES), lambda m: (0, 0))],
        out_shape=[jax.ShapeDtypeStruct((t, d), jnp.float32),
                   jax.ShapeDtypeStruct((t * ROW_SUB, LANES), jnp.float32),
                   jax.ShapeDtypeStruct((t, ROUTE_LANES), jnp.float32),
                   jax.ShapeDtypeStruct((SUBLANES, t), jnp.float32),
                   jax.ShapeDtypeStruct((1, ROUTE_LANES), jnp.float32)],
        scratch_shapes=[pltpu.VMEM((1, ROUTE_LANES), jnp.float32),
                        pltpu.VMEM((tm, tm), jnp.bfloat16),
                        pltpu.VMEM(w_out.shape, jnp.bfloat16),
                        pltpu.VMEM((2, STAGE_ROWS, d), jnp.float32),
                        pltpu.SemaphoreType.DMA((2,))],
        compiler_params=_cparams(("arbitrary",)),
        name="out",
    )(mix_c, mix_a, w_out, x2, gt1[0], g_ffn, sc2[0], sh2[0], w_route, b_route)


def _slot(off_ref, e_ref, r_ref, tok):
    return off_ref[e_ref[tok]] + r_ref[tok]


def _dispatch_kernel(e1_ref, e2_ref, r1_ref, r2_ref, off_ref, zs_ref, zc_ref, nu_ref, h2_ref, xs_ref,
                     zbuf, sem_z, sem_r, *, tm, n_tiles):
    i = pl.program_id(0)

    @pl.when(i == 0)
    def _():
        zbuf[...] = jnp.zeros(zbuf.shape, zbuf.dtype)

        def zero_copy(start):
            rows = pl.ds(pl.multiple_of(start * ROW_SUB, ZERO_ROWS * ROW_SUB), ZERO_ROWS * ROW_SUB)
            return pltpu.make_async_copy(zbuf, xs_ref.at[rows], sem_z)

        def zero_fill(first_row, n_granules):
            def start(k, carry):
                zero_copy(first_row + k * ZERO_ROWS).start()
                return carry

            def wait(k, carry):
                zero_copy(first_row + k * ZERO_ROWS).wait()
                return carry

            lax.fori_loop(0, n_granules, start, 0)
            lax.fori_loop(0, n_granules, wait, 0)

        for e in range(N_EXPERTS):
            zero_fill(zs_ref[e], zc_ref[e])
        zero_fill(nu_ref[0] * MOE_TM, (n_tiles - nu_ref[0]) * (MOE_TM // ZERO_ROWS))

    def issue_row(r, carry):
        tok = i * tm + r
        for e_ref, r_ref in ((e1_ref, r1_ref), (e2_ref, r2_ref)):
            pltpu.make_async_copy(_row(h2_ref, r), _row(xs_ref, _slot(off_ref, e_ref, r_ref, tok)),
                                  sem_r).start()
        return carry

    lax.fori_loop(0, tm, issue_row, 0)
    for _ in range(2):
        pltpu.make_async_copy(h2_ref, xs_ref.at[pl.ds(0, tm * ROW_SUB)], sem_r).wait()


def _dispatch(plan, h2, n_rows, tm=512):
    t = h2.shape[0] // ROW_SUB
    kern = functools.partial(_dispatch_kernel, tm=tm, n_tiles=n_rows // MOE_TM)
    return pl.pallas_call(
        kern,
        grid_spec=pltpu.PrefetchScalarGridSpec(
            num_scalar_prefetch=len(plan),
            grid=(t // tm,),
            in_specs=[pl.BlockSpec((tm * ROW_SUB, LANES), lambda i, *_: (i, 0))],
            out_specs=pl.BlockSpec(memory_space=pl.ANY),
            scratch_shapes=[pltpu.VMEM((ZERO_ROWS * ROW_SUB, LANES), h2.dtype),
                            pltpu.SemaphoreType.DMA(()),
                            pltpu.SemaphoreType.DMA(())]),
        out_shape=jax.ShapeDtypeStruct((n_rows * ROW_SUB, LANES), h2.dtype),
        compiler_params=_cparams(("arbitrary",)),
        name="dispatch",
    )(*plan, h2)


def _mlp_kernel(te_ref, nu_ref, first_ref, nxt_ref, slot_ref, xs_ref, wg_hbm, wu_hbm, wd_hbm, ys_ref,
                wg_f32, wu_f32, wd_f32, wg_bf, wu_bf, wd_bf, sem):
    i = pl.program_id(0)

    def copies(e, s):
        return [pltpu.make_async_copy(hbm.at[e], buf.at[s], sem.at[s])
                for hbm, buf in ((wg_hbm, wg_f32), (wu_hbm, wu_f32), (wd_hbm, wd_f32))]

    @pl.when(i == 0)
    def _():
        for cp in copies(te_ref[0], 0):
            cp.start()

    @pl.when((i < nu_ref[0]) & (first_ref[i] == 1))
    def _():
        s = slot_ref[i]
        for cp in copies(te_ref[i], s):
            cp.wait()
        wg_bf[...] = wg_f32[s].astype(jnp.bfloat16)
        wu_bf[...] = wu_f32[s].astype(jnp.bfloat16)
        wd_bf[...] = wd_f32[s].astype(jnp.bfloat16)

        @pl.when(nxt_ref[i] >= 0)
        def _():
            for cp in copies(nxt_ref[i], 1 - s):
                cp.start()

    @pl.when(i < nu_ref[0])
    def _():
        x = _rows_to_2d(xs_ref, MOE_TM).astype(jnp.bfloat16)
        a = jnp.dot(x, wg_bf[...], preferred_element_type=jnp.float32)
        u = jnp.dot(x, wu_bf[...], preferred_element_type=jnp.float32)
        hid = (a * jax.nn.sigmoid(a)) * u
        _rows_from_2d(ys_ref, jnp.dot(hid.astype(jnp.bfloat16), wd_bf[...],
                                      preferred_element_type=jnp.float32))

    @pl.when(i >= nu_ref[0])
    def _():
        ys_ref[...] = jnp.zeros(ys_ref.shape, ys_ref.dtype)


def _mlp(tiles, xs, w_gate, w_up, w_down):
    n_rows = xs.shape[0] // ROW_SUB
    _, d, f = w_gate.shape
    used = lambda i, te, nu, *_: (jnp.minimum(i, nu[0] - 1), 0)
    return pl.pallas_call(
        _mlp_kernel,
        grid_spec=pltpu.PrefetchScalarGridSpec(
            num_scalar_prefetch=len(tiles),
            grid=(n_rows // MOE_TM,),
            in_specs=[pl.BlockSpec((MOE_TM * ROW_SUB, LANES), used),
                      pl.BlockSpec(memory_space=pl.ANY),
                      pl.BlockSpec(memory_space=pl.ANY),
                      pl.BlockSpec(memory_space=pl.ANY)],
            out_specs=pl.BlockSpec((MOE_TM * ROW_SUB, LANES), lambda i, *_: (i, 0)),
            scratch_shapes=[pltpu.VMEM((2, d, f), jnp.float32),
                            pltpu.VMEM((2, d, f), jnp.float32),
                            pltpu.VMEM((2, f, d), jnp.float32),
                            pltpu.VMEM((d, f), jnp.bfloat16),
                            pltpu.VMEM((d, f), jnp.bfloat16),
                            pltpu.VMEM((f, d), jnp.bfloat16),
                            pltpu.SemaphoreType.DMA((2,))]),
        out_shape=jax.ShapeDtypeStruct((n_rows * ROW_SUB, LANES), jnp.float32),
        compiler_params=_cparams(("arbitrary",)),
        name="mlp",
    )(*tiles, xs, w_gate, w_up, w_down)


def _combine_kernel(e1_ref, e2_ref, r1_ref, r2_ref, off_ref, ys_ref, x1_ref, route_ref, gt_ref, g_ref,
                    sc_ref, sh_ref, o_ref, a0, b0, a1, b1, sem, *, tm, n_tiles):
    i = pl.program_id(0)
    bufs = ((a0, b0), (a1, b1))

    def issue(tile, sl):
        def issue_row(r, carry):
            tok = tile * tm + r
            for buf, e_ref, r_ref in ((bufs[sl][0], e1_ref, r1_ref), (bufs[sl][1], e2_ref, r2_ref)):
                pltpu.make_async_copy(_row(ys_ref, _slot(off_ref, e_ref, r_ref, tok)),
                                      _row(buf, r), sem.at[sl]).start()
            return carry

        lax.fori_loop(0, tm, issue_row, 0)

    @pl.when(i == 0)
    def _():
        issue(0, 0)

    for sl in range(2):
        @pl.when(i % 2 == sl)
        def _(sl=sl):
            @pl.when(i + 1 < n_tiles)
            def _():
                issue(i + 1, 1 - sl)

            for buf in bufs[sl]:
                pltpu.make_async_copy(ys_ref.at[pl.ds(0, tm * ROW_SUB)], buf, sem.at[sl]).wait()

            route = route_ref[...]
            y = (route[:, R_W1:R_W1 + 1] * _rows_to_2d(bufs[sl][0], tm)
                 + route[:, R_W2:R_W2 + 1] * _rows_to_2d(bufs[sl][1], tm))
            x = x1_ref[...] + gt_ref[0] * y
            o_ref[...] = _rms(x) * g_ref[...] * (1.0 + sc_ref[0]) + sh_ref[0]


def _combine(plan, ys, x1, route, gt2, g_final, scf, shf, seq, tm=512):
    t, d = x1.shape
    per_b = seq // tm
    n_tiles = t // tm
    kern = functools.partial(_combine_kernel, tm=tm, n_tiles=n_tiles)
    return pl.pallas_call(
        kern,
        grid_spec=pltpu.PrefetchScalarGridSpec(
            num_scalar_prefetch=len(plan),
            grid=(n_tiles,),
            in_specs=[pl.BlockSpec(memory_space=pl.ANY),
                      pl.BlockSpec((tm, d), lambda m, *_: (m, 0)),
                      pl.BlockSpec((tm, ROUTE_LANES), lambda m, *_: (m, 0)),
                      _mod_spec(gt2, per_b),
                      pl.BlockSpec((1, d), lambda m, *_: (0, 0)),
                      _mod_spec(scf, per_b),
                      _mod_spec(shf, per_b)],
            out_specs=pl.BlockSpec((tm, d), lambda m, *_: (m, 0)),
            scratch_shapes=[pltpu.VMEM((tm * ROW_SUB, LANES), jnp.float32) for _ in range(4)]
            + [pltpu.SemaphoreType.DMA((2,))]),
        out_shape=jax.ShapeDtypeStruct((t, d), jnp.float32),
        compiler_params=_cparams(("arbitrary",)),
        name="combine",
    )(*plan, ys, x1, route, gt2[0], g_final, scf[0], shf[0])


def _moe_plan(route_t, counts, t):
    i32 = jnp.int32
    cnt = counts[0, N_GROUPS:N_GROUPS + N_EXPERTS].astype(i32)
    padded = (cnt + MOE_TM - 1) // MOE_TM * MOE_TM
    off_end = jnp.cumsum(padded)
    off = off_end - padded
    n_tiles = (2 * t + N_EXPERTS * MOE_TM) // MOE_TM
    n_used = off_end[-1] // MOE_TM
    tile_start = jnp.arange(n_tiles, dtype=i32) * MOE_TM
    te = jnp.sum((off_end[None, :] <= tile_start[:, None]).astype(i32), axis=1)
    te = jnp.minimum(te, N_EXPERTS - 1)
    tile = jnp.arange(n_tiles, dtype=i32)
    te = jnp.where(tile < n_used, te, te[n_used - 1])
    first = jnp.concatenate([jnp.ones((1,), i32), (te[1:] != te[:-1]).astype(i32)])
    nxt_tile = off_end[te] // MOE_TM
    nxt = jnp.where(nxt_tile < n_used, te[jnp.minimum(nxt_tile, n_tiles - 1)], -1)
    slot = (jnp.cumsum(first) - 1) % 2
    assign = tuple(route_t[k].astype(i32) for k in (R_E1, R_E2, R_RANK1, R_RANK2))
    zstart = (off + cnt) // ZERO_ROWS * ZERO_ROWS
    zcount = (off_end - zstart) // ZERO_ROWS
    n_used = n_used.reshape(1)
    return assign, off, (zstart, zcount), (te, n_used, first, nxt, slot), n_tiles * MOE_TM


def kernel(x, c, w_ada, b_ada, g_mix, w_in, conv_w, w_uk, kv_norm_g, w_uv, g_conv_out, g_attn_out,
           w_out, g_ffn, w_rg, b_rg, w_re, b_re, w_gate, w_up, w_down, w_ada_f, b_ada_f, g_final):
    b, s, d = x.shape
    assert w_ada.shape[0] == 1, "single layer"
    bf = jnp.bfloat16
    x2 = x.reshape(b * s, d)

    c_pad = jnp.zeros((SUBLANES, d), jnp.float32).at[:b].set(c)
    mod = _mod(c_pad, w_ada[0], b_ada[0]).reshape(SUBLANES, 1, N_MOD * d)
    modf = _mod(c_pad, w_ada_f, b_ada_f).reshape(SUBLANES, 1, 2 * d)
    sh1, sc1, gt1, sh2, sc2, gt2 = ((mod, i) for i in range(N_MOD))
    shf, scf = (modf, 0), (modf, 1)
    row = lambda a: a.reshape(1, -1)

    w_in_t = jnp.swapaxes(w_in[0], 0, 1)
    h, q, kv, iq, ikw = _hproj(x2, row(g_mix[0]), sc1, sh1, w_in_t, s)
    mix_c = _projconv(h, w_in_t, conv_w[0], row(g_conv_out[0]), s)

    mix_a = _attn(iq.reshape(b, s, -1), ikw.reshape(b, s, -1), kv.reshape(b, s, -1),
                  q.reshape(b, s, -1), w_uk[0].astype(bf), w_uv[0].astype(bf),
                  row(kv_norm_g[0]), row(g_attn_out[0]))

    w_route = jnp.zeros((d, ROUTE_LANES), bf).at[:, :N_GROUPS].set(w_rg[0].astype(bf))
    w_route = w_route.at[:, N_GROUPS:N_GROUPS + N_EXPERTS].set(w_re[0].astype(bf))
    b_route = jnp.zeros((1, ROUTE_LANES), jnp.float32).at[0, :N_GROUPS].set(b_rg[0])
    b_route = b_route.at[0, N_GROUPS:N_GROUPS + N_EXPERTS].set(b_re[0])
    x1, h2, route, route_t, counts = _out(mix_c, mix_a.reshape(b * s, -1),
                                 w_out[0], x2, gt1, row(g_ffn[0]),
                                 sc2, sh2, w_route, b_route, s)

    assign, off, zero, tiles, n_rows = _moe_plan(route_t, counts, b * s)
    xs = _dispatch((*assign, off, *zero, tiles[1]), h2, n_rows)
    ys = _mlp(tiles, xs, w_gate[0], w_up[0], w_down[0])
    out = _combine((*assign, off), ys, x1, route, gt2, row(g_final), scf, shf, s)
    return out.reshape(b, s, d)
```

```python
import functools

import jax
import jax.numpy as jnp
from jax import lax
from jax.experimental import pallas as pl
from jax.experimental.pallas import tpu as pltpu

D_MODEL = 2048
CONV_DIM = 1024
CONV_WIDTH = 3
N_HEADS = 8
HEAD_DIM = 128
ATTN_DIM = N_HEADS * HEAD_DIM
KV_RANK = 512
IDX_HEADS = 16
IDX_DIM = 128
TOPK_MAX = 256
N_GROUPS = 4
EXPERTS_PER_GROUP = 8
N_EXPERTS = N_GROUPS * EXPERTS_PER_GROUP
EXPERT_FF = 512
N_MOD = 6
EPS = 1e-6

OFF_Q = 3 * CONV_DIM
OFF_KV = OFF_Q + ATTN_DIM
OFF_IQ = OFF_KV + KV_RANK
OFF_IK = OFF_IQ + IDX_HEADS * IDX_DIM
OFF_IW = OFF_IK + IDX_DIM
IN_COLS = OFF_IW + IDX_HEADS

LANES = 128
SUBLANES = 8
VMEM_LIMIT = 56 * 1024 * 1024

IKW_COLS = 256
ATTN_Q = 256
ATTN_KC = 256
ATTN_RB = 256
MASKED = -1e30
TIE_ALL = 2 ** 30
ROUTE_LANES = 128
STAGE_ROWS = 256
OUT_SUB = 2
MOE_TM = 256
ZERO_ROWS = 64
ROW_SUB = D_MODEL // LANES
INT_MIN = -2 ** 31
KEY_NEG_INF = INT_MIN + 0x7FFFFF


def _cparams(sem):
    return pltpu.CompilerParams(dimension_semantics=sem, vmem_limit_bytes=VMEM_LIMIT)


def _rms(v, axis=-1):
    return v * lax.rsqrt(jnp.mean(v * v, axis=axis, keepdims=True) + EPS)


def _tile_lanes(v, n):
    return jnp.concatenate([v] * n, axis=1)


def _rows_to_2d(ref, n):
    return jnp.concatenate([ref[pl.ds(c, n, stride=ROW_SUB), :] for c in range(ROW_SUB)], axis=1)


def _rows_from_2d(ref, val):
    n = val.shape[0]
    for c in range(ROW_SUB):
        ref[pl.ds(c, n, stride=ROW_SUB), :] = val[:, c * LANES:(c + 1) * LANES]


def _row(ref, i):
    return ref.at[pl.ds(pl.multiple_of(i * ROW_SUB, ROW_SUB), ROW_SUB)]


def _mod_spec(vec, per_b):
    idx = vec[1]
    return pl.BlockSpec((1, 1, D_MODEL), lambda m, *_: (m // per_b, 0, idx))


def _load_bf16(w_hbm, pieces, w_bf, stage, sem):
    def copy(i):
        src, n, _ = pieces[i]
        return pltpu.make_async_copy(w_hbm.at[pl.ds(src, n)], stage.at[i % 2, pl.ds(0, n)],
                                     sem.at[i % 2])

    copy(0).start()
    for i, (_, n, dst) in enumerate(pieces):
        if i + 1 < len(pieces):
            copy(i + 1).start()
        copy(i).wait()
        w_bf[pl.ds(dst, n), :] = stage[i % 2, pl.ds(0, n), :].astype(jnp.bfloat16)


def _mod_kernel(c_ref, w_ref, b_ref, o_ref):
    c = c_ref[...]
    ca = (c * jax.nn.sigmoid(c)).astype(jnp.bfloat16)
    o_ref[...] = jnp.dot(ca, w_ref[...].astype(jnp.bfloat16),
                         preferred_element_type=jnp.float32) + b_ref[...]


def _mod(c_pad, w, b, tn=1024):
    d, n = w.shape
    return pl.pallas_call(
        _mod_kernel,
        grid=(n // tn,),
        in_specs=[pl.BlockSpec((c_pad.shape[0], d), lambda j: (0, 0)),
                  pl.BlockSpec((d, tn), lambda j: (0, j)),
                  pl.BlockSpec((1, tn), lambda j: (0, j))],
        out_specs=pl.BlockSpec((c_pad.shape[0], tn), lambda j: (0, j)),
        out_shape=jax.ShapeDtypeStruct((c_pad.shape[0], n), jnp.float32),
        compiler_params=_cparams(("arbitrary",)),
        name="mod",
    )(c_pad, w, b.reshape(1, n))


ATT_COLS = IN_COLS - OFF_Q
ATT_PAD = OFF_IK - OFF_Q + IKW_COLS


def _hproj_kernel(x_ref, g_ref, sc_ref, sh_ref, wt_hbm, h_ref, q_ref, kv_ref, iq_ref, ikw_ref,
                  w_bf, stage, sem):
    @pl.when(pl.program_id(0) == 0)
    def _():
        full = ATT_COLS // STAGE_ROWS
        pieces = [(OFF_Q + k * STAGE_ROWS, STAGE_ROWS, k * STAGE_ROWS) for k in range(full)]
        pieces.append((OFF_Q + full * STAGE_ROWS, ATT_COLS - full * STAGE_ROWS, full * STAGE_ROWS))
        _load_bf16(wt_hbm, pieces, w_bf, stage, sem)
        w_bf[pl.ds(ATT_COLS, ATT_PAD - ATT_COLS), :] = jnp.zeros((ATT_PAD - ATT_COLS, w_bf.shape[1]),
                                                                 jnp.bfloat16)

    h = (_rms(x_ref[...]) * g_ref[...] * (1.0 + sc_ref[0]) + sh_ref[0]).astype(h_ref.dtype)
    h_ref[...] = h
    nt = (((1,), (1,)), ((), ()))
    row0 = 0
    for o_ref in (q_ref, kv_ref, iq_ref, ikw_ref):
        n = o_ref.shape[1]
        o_ref[...] = lax.dot_general(h, w_bf[row0:row0 + n, :], nt,
                                     preferred_element_type=jnp.float32).astype(o_ref.dtype)
        row0 += n


def _hproj(x2, g_mix, sc1, sh1, wt, seq, tm=512):
    t, d = x2.shape
    per_b = seq // tm
    widths = (ATTN_DIM, KV_RANK, IDX_HEADS * IDX_DIM, IKW_COLS)
    dtypes = (jnp.bfloat16, jnp.float32, jnp.bfloat16, jnp.float32)
    assert sum(widths) == ATT_PAD and (ATT_COLS % STAGE_ROWS) % SUBLANES == 0
    return pl.pallas_call(
        _hproj_kernel,
        grid=(t // tm,),
        in_specs=[pl.BlockSpec((tm, d), lambda m: (m, 0)),
                  pl.BlockSpec((1, d), lambda m: (0, 0)),
                  _mod_spec(sc1, per_b),
                  _mod_spec(sh1, per_b),
                  pl.BlockSpec(memory_space=pl.ANY)],
        out_specs=[pl.BlockSpec((tm, d), lambda m: (m, 0))]
        + [pl.BlockSpec((tm, w), lambda m: (m, 0)) for w in widths],
        out_shape=[jax.ShapeDtypeStruct((t, d), jnp.bfloat16)]
        + [jax.ShapeDtypeStruct((t, w), dt) for w, dt in zip(widths, dtypes)],
        scratch_shapes=[pltpu.VMEM((ATT_PAD, d), jnp.bfloat16),
                        pltpu.VMEM((2, STAGE_ROWS, d), jnp.float32),
                        pltpu.SemaphoreType.DMA((2,))],
        compiler_params=_cparams(("arbitrary",)),
        name="hproj",
    )(x2, g_mix, sc1[0], sh1[0], wt)


def _projconv_kernel(h_ref, wt_hbm, cw_ref, g_ref, o_ref, w_bf, stage, halo_scr, sem, *, per_b):
    m = pl.program_id(0)

    @pl.when(m == 0)
    def _():
        _load_bf16(wt_hbm, [(r, STAGE_ROWS, r) for r in range(0, w_bf.shape[0], STAGE_ROWS)],
                   w_bf, stage, sem)

    nt = (((1,), (1,)), ((), ()))
    h = h_ref[...]
    bg, cg, xv = (lax.dot_general(h, w_bf[k * CONV_DIM:(k + 1) * CONV_DIM, :], nt,
                                  preferred_element_type=jnp.float32) for k in range(3))
    u = cg * xv
    halo = jnp.where(m % per_b == 0, 0.0, halo_scr[...])
    halo_scr[...] = u[u.shape[0] - SUBLANES:, :]
    row = lax.broadcasted_iota(jnp.int32, u.shape, 0)
    h1 = halo[SUBLANES - 1:SUBLANES, :]
    h2 = halo[SUBLANES - 2:SUBLANES - 1, :]
    u1 = jnp.where(row == 0, h1, pltpu.roll(u, 1, 0))
    u2 = jnp.where(row == 0, h2, jnp.where(row == 1, h1, pltpu.roll(u, 2, 0)))
    w = cw_ref[...]
    y = bg * (w[0:1, :] * u2 + w[1:2, :] * u1 + w[2:3, :] * u)
    o_ref[...] = (_rms(y) * g_ref[...]).astype(o_ref.dtype)


def _projconv(h, wt, conv_w, g_conv_out, seq, tm=1024):
    t, d = h.shape
    c = CONV_DIM
    kern = functools.partial(_projconv_kernel, per_b=seq // tm)
    return pl.pallas_call(
        kern,
        grid=(t // tm,),
        in_specs=[pl.BlockSpec((tm, d), lambda m: (m, 0)),
                  pl.BlockSpec(memory_space=pl.ANY),
                  pl.BlockSpec((CONV_WIDTH, c), lambda m: (0, 0)),
                  pl.BlockSpec((1, c), lambda m: (0, 0))],
        out_specs=pl.BlockSpec((tm, c), lambda m: (m, 0)),
        out_shape=jax.ShapeDtypeStruct((t, c), jnp.bfloat16),
        scratch_shapes=[pltpu.VMEM((3 * c, d), jnp.bfloat16),
                        pltpu.VMEM((2, STAGE_ROWS, d), jnp.float32),
                        pltpu.VMEM((SUBLANES, c), jnp.float32),
                        pltpu.SemaphoreType.DMA((2,))],
        compiler_params=_cparams(("arbitrary",)),
        name="proj_conv",
    )(h, wt, conv_w, g_conv_out)


def _ordered_bits(v):
    return v ^ ((v >> 31) & jnp.int32(0x7FFFFFFF))


def _attn_search(nch, sc_scr, k_sel):
    n_keys = float(nch * ATTN_KC)
    half = ATTN_Q // 2

    def half_step(bit, thr, rows):
        cand = thr + bit
        cand_f = lax.bitcast_convert_type(_ordered_bits(cand), jnp.float32)
        hits = jnp.where(sc_scr[0, rows, :] >= cand_f, 1.0, 0.0)
        for c in range(1, nch):
            hits = hits + jnp.where(sc_scr[c, rows, :] >= cand_f, 1.0, 0.0)
        cnt = jnp.sum(hits, axis=1, keepdims=True)
        cnt = jnp.where(cand < KEY_NEG_INF, n_keys, cnt)
        return jnp.where(cnt >= k_sel, cand, thr)

    def bit_step(i, thrs):
        bit = lax.shift_left(jnp.int32(1), 31 - i)
        return tuple(half_step(bit, thr, pl.ds(k * half, half)) for k, thr in enumerate(thrs))

    init = jnp.full((half, 1), INT_MIN, jnp.int32)
    thrs = lax.fori_loop(0, 32, bit_step, (init, init), unroll=2)
    thr = jnp.concatenate(thrs, axis=0)
    return lax.bitcast_convert_type(_ordered_bits(thr), jnp.float32)


def _attn_tiebreak(nch, sc_scr, thr_f, k_sel, tie_scr):
    kloc = lax.broadcasted_iota(jnp.int32, (ATTN_Q, ATTN_KC), 1)

    def count(pred):
        hits = pred(sc_scr[0], 0)
        for c in range(1, nch):
            hits = hits + pred(sc_scr[c], c)
        return jnp.sum(hits, axis=1, keepdims=True)

    tie_scr[...] = jnp.full(tie_scr.shape, TIE_ALL, jnp.int32)
    n_ge = count(lambda sc, c: jnp.where(sc >= thr_f, 1.0, 0.0))
    excess = jnp.where(thr_f > -jnp.inf, n_ge - k_sel, 0.0)

    @pl.when(jnp.max(excess) > 0.0)
    def _():
        need = k_sel - count(lambda sc, c: jnp.where(sc > thr_f, 1.0, 0.0))
        n_bits = (nch * ATTN_KC - 1).bit_length()

        def step(i, lim):
            cand = lim + lax.shift_left(jnp.int32(1), n_bits - 1 - i)
            below = count(lambda sc, c: jnp.where(
                sc == thr_f, jnp.where(c * ATTN_KC + kloc < cand, 1.0, 0.0), 0.0))
            return jnp.where(below < need, cand, lim)

        lim = lax.fori_loop(0, n_bits, step, jnp.zeros((ATTN_Q, 1), jnp.int32))
        tie_scr[...] = jnp.where(excess > 0.0, lim, TIE_ALL)


def _attn_kernel(iq_ref, ikwq_ref, ikwa_ref, kv_ref, q_ref, wuk_ref, wuv_ref, kvg_ref, go_ref,
                 o_ref, ckv_scr, ik_scr, sc_scr, thr_scr, tie_scr, ql_scr, m_scr, l_scr, acc_scr, y_scr,
                 *, seq, k_sel):
    j = pl.program_id(1)
    nch = j + 1
    nt = (((1,), (1,)), ((), ()))

    @pl.when(j == 0)
    def _():
        ckv_scr[...] = (_rms(kv_ref[0]) * kvg_ref[...]).astype(jnp.bfloat16)
        ik_scr[...] = ikwa_ref[0][:, :IDX_DIM].astype(jnp.bfloat16)

    iw = ikwq_ref[0][:, IDX_DIM:IDX_DIM + IDX_HEADS] * (IDX_HEADS ** -0.5 * IDX_DIM ** -0.5)
    qpos = j * ATTN_Q + lax.broadcasted_iota(jnp.int32, (ATTN_Q, ATTN_KC), 0)
    kloc = lax.broadcasted_iota(jnp.int32, (ATTN_Q, ATTN_KC), 1)

    def score_chunk(c, carry):
        k0 = pl.multiple_of(c * ATTN_KC, ATTN_KC)
        ik_c = ik_scr[pl.ds(k0, ATTN_KC), :]
        score = jnp.zeros((ATTN_Q, ATTN_KC), jnp.float32)
        for h in range(IDX_HEADS):
            s = lax.dot_general(iq_ref[0][:, h * IDX_DIM:(h + 1) * IDX_DIM], ik_c, nt,
                                preferred_element_type=jnp.float32)
            score = score + jnp.maximum(s, 0.0) * iw[:, h:h + 1]
        sc_scr[c] = jnp.where(k0 + kloc <= qpos, score, -jnp.inf)
        return carry

    lax.fori_loop(0, nch, score_chunk, 0)

    for v in range(seq // ATTN_KC):
        @pl.when(j == v)
        def _(v=v):
            if (v + 1) * ATTN_KC <= k_sel:
                thr_scr[...] = jnp.full(thr_scr.shape, -jnp.inf, jnp.float32)
                tie_scr[...] = jnp.full(tie_scr.shape, TIE_ALL, jnp.int32)
            else:
                thr_scr[...] = _attn_search(v + 1, sc_scr, k_sel)
                _attn_tiebreak(v + 1, sc_scr, thr_scr[...], k_sel, tie_scr)

    for h in range(N_HEADS):
        ql = lax.dot_general(q_ref[0][:, h * HEAD_DIM:(h + 1) * HEAD_DIM], wuk_ref[h], nt,
                             preferred_element_type=jnp.float32)
        ql_scr[h * ATTN_Q:(h + 1) * ATTN_Q, :] = ql.astype(jnp.bfloat16)
    thr_f = thr_scr[...]
    tie = tie_scr[...]
    scale = HEAD_DIM ** -0.5

    def attn_chunk(c, first):
        k0 = pl.multiple_of(c * ATTN_KC, ATTN_KC)
        ckv_c = ckv_scr[pl.ds(k0, ATTN_KC), :]
        sc = sc_scr[c]
        kpos = k0 + kloc
        tied = jnp.where(sc == thr_f, jnp.where(kpos <= tie, 0.0, MASKED), MASKED)
        bias = jnp.where(kpos <= qpos, jnp.where(sc > thr_f, 0.0, tied), MASKED)
        for u in range(N_HEADS * ATTN_Q // ATTN_RB):
            rows = pl.ds(u * ATTN_RB, ATTN_RB)
            qrow = (u * ATTN_RB) % ATTN_Q
            lg = lax.dot_general(ql_scr[rows, :], ckv_c, nt,
                                 preferred_element_type=jnp.float32) * scale + bias[qrow:qrow + ATTN_RB]
            row_max = jnp.max(lg, axis=1, keepdims=True)
            if first:
                m_new = jnp.maximum(jnp.full((ATTN_RB, LANES), MASKED, jnp.float32), row_max)
            else:
                m_old = m_scr[rows, :]
                m_new = jnp.maximum(m_old, row_max)
                alpha = jnp.exp(m_old - m_new)
            p = jnp.exp(lg - _tile_lanes(m_new, ATTN_KC // LANES))
            p_sum = jnp.sum(p, axis=1, keepdims=True)
            pv = jnp.dot(p.astype(jnp.bfloat16), ckv_c, preferred_element_type=jnp.float32)
            if first:
                l_scr[rows, :] = jnp.broadcast_to(p_sum, (ATTN_RB, LANES))
                acc_scr[rows, :] = pv
            else:
                l_scr[rows, :] = alpha * l_scr[rows, :] + p_sum
                acc_scr[rows, :] = _tile_lanes(alpha, KV_RANK // LANES) * acc_scr[rows, :] + pv
            m_scr[rows, :] = m_new

    attn_chunk(0, True)

    def later_chunk(c, carry):
        attn_chunk(c, False)
        return carry

    lax.fori_loop(1, nch, later_chunk, 0)

    o = (acc_scr[...] * _tile_lanes(1.0 / l_scr[...], KV_RANK // LANES)).astype(jnp.bfloat16)
    for h in range(N_HEADS):
        y_scr[:, h * HEAD_DIM:(h + 1) * HEAD_DIM] = jnp.dot(
            o[h * ATTN_Q:(h + 1) * ATTN_Q], wuv_ref[h], preferred_element_type=jnp.float32)
    o_ref[0] = (_rms(y_scr[...]) * go_ref[...]).astype(o_ref.dtype)


def _attn(iq, ikw, kv, q, w_uk_bf, w_uv_bf, kv_norm_g, g_attn_out):
    b, s, _ = iq.shape
    assert ATTN_Q == ATTN_KC and s % ATTN_Q == 0
    k_sel = min(TOPK_MAX, s // 4)
    rows = N_HEADS * ATTN_Q
    kern = functools.partial(_attn_kernel, seq=s, k_sel=k_sel)
    return pl.pallas_call(
        kern,
        grid=(b, s // ATTN_Q),
        in_specs=[pl.BlockSpec((1, ATTN_Q, IDX_HEADS * IDX_DIM), lambda bi, j: (bi, j, 0)),
                  pl.BlockSpec((1, ATTN_Q, IKW_COLS), lambda bi, j: (bi, j, 0)),
                  pl.BlockSpec((1, s, IKW_COLS), lambda bi, j: (bi, 0, 0)),
                  pl.BlockSpec((1, s, KV_RANK), lambda bi, j: (bi, 0, 0)),
                  pl.BlockSpec((1, ATTN_Q, ATTN_DIM), lambda bi, j: (bi, j, 0)),
                  pl.BlockSpec((N_HEADS, KV_RANK, HEAD_DIM), lambda bi, j: (0, 0, 0)),
                  pl.BlockSpec((N_HEADS, KV_RANK, HEAD_DIM), lambda bi, j: (0, 0, 0)),
                  pl.BlockSpec((1, KV_RANK), lambda bi, j: (0, 0)),
                  pl.BlockSpec((1, ATTN_DIM), lambda bi, j: (0, 0))],
        out_specs=pl.BlockSpec((1, ATTN_Q, ATTN_DIM), lambda bi, j: (bi, j, 0)),
        out_shape=jax.ShapeDtypeStruct((b, s, ATTN_DIM), jnp.bfloat16),
        scratch_shapes=[pltpu.VMEM((s, KV_RANK), jnp.bfloat16),
                        pltpu.VMEM((s, IDX_DIM), jnp.bfloat16),
                        pltpu.VMEM((s // ATTN_KC, ATTN_Q, ATTN_KC), jnp.float32),
                        pltpu.VMEM((ATTN_Q, 1), jnp.float32),
                        pltpu.VMEM((ATTN_Q, 1), jnp.int32),
                        pltpu.VMEM((rows, KV_RANK), jnp.bfloat16),
                        pltpu.VMEM((rows, LANES), jnp.float32),
                        pltpu.VMEM((rows, LANES), jnp.float32),
                        pltpu.VMEM((rows, KV_RANK), jnp.float32),
                        pltpu.VMEM((ATTN_Q, ATTN_DIM), jnp.float32)],
        compiler_params=_cparams(("arbitrary", "arbitrary")),
        name="attn",
    )(iq, ikw, ikw, kv, q, w_uk_bf, w_uv_bf, kv_norm_g, g_attn_out)


R_E1, R_E2, R_W1, R_W2, R_RANK1, R_RANK2 = range(6)


def _out_kernel(mc_ref, ma_ref, w_hbm, x_ref, gt_ref, g_ref, sc_ref, sh_ref, wr_ref, br_ref,
                x1_ref, h2_ref, route_ref, routet_ref, cnt_ref, cnt_scr, tri_scr, w_bf, stage, sem):
    @pl.when(pl.program_id(0) == 0)
    def _():
        cnt_scr[...] = jnp.zeros(cnt_scr.shape, jnp.float32)
        earlier = (lax.broadcasted_iota(jnp.int32, tri_scr.shape, 1)
                   < lax.broadcasted_iota(jnp.int32, tri_scr.shape, 0))
        tri_scr[...] = jnp.where(earlier, 1.0, 0.0).astype(tri_scr.dtype)
        _load_bf16(w_hbm, [(r, STAGE_ROWS, r) for r in range(0, w_bf.shape[0], STAGE_ROWS)],
                   w_bf, stage, sem)

    rs = x_ref.shape[0] // OUT_SUB
    for sub in range(OUT_SUB):
        rows = pl.ds(sub * rs, rs)
        _out_rows(mc_ref[rows, :], ma_ref[rows, :], w_bf, x_ref[rows, :], gt_ref, g_ref, sc_ref, sh_ref,
                  wr_ref, br_ref, x1_ref.at[rows], h2_ref.at[pl.ds(sub * rs * ROW_SUB, rs * ROW_SUB)],
                  route_ref.at[rows], routet_ref.at[:, rows], cnt_scr, tri_scr[0:rs, 0:rs])
    cnt_ref[...] = cnt_scr[...]


def _out_rows(mc, ma, w_bf, x, gt_ref, g_ref, sc_ref, sh_ref, wr_ref, br_ref,
              x1_ref, h2_ref, route_ref, routet_ref, cnt_scr, tri):
    mix = (jnp.dot(mc, w_bf[0:CONV_DIM, :], preferred_element_type=jnp.float32)
           + jnp.dot(ma, w_bf[CONV_DIM:, :], preferred_element_type=jnp.float32))
    x1 = x + gt_ref[0] * mix
    x1_ref[...] = x1
    h2 = _rms(x1) * g_ref[...] * (1.0 + sc_ref[0]) + sh_ref[0]
    _rows_from_2d(h2_ref, h2)

    logit = jnp.dot(h2.astype(jnp.bfloat16), wr_ref[...],
                    preferred_element_type=jnp.float32) + br_ref[...]
    lane = lax.broadcasted_iota(jnp.int32, logit.shape, 1).astype(jnp.float32)
    neg = -jnp.inf
    big = float(ROUTE_LANES)
    is_g = lane < N_GROUPS
    gl = jnp.where(is_g, logit, neg)
    gmax = jnp.max(gl, axis=1, keepdims=True)
    p_group = 1.0 / jnp.sum(jnp.exp(gl - gmax), axis=1, keepdims=True)
    g_sel = jnp.min(jnp.where(is_g & (gl == gmax), lane, big), axis=1, keepdims=True)
    lo = N_GROUPS + g_sel * EXPERTS_PER_GROUP
    in_grp = (lane >= lo) & (lane < lo + EXPERTS_PER_GROUP)
    el = jnp.where(in_grp, logit, neg)
    m1 = jnp.max(el, axis=1, keepdims=True)
    i1 = jnp.min(jnp.where(in_grp & (el == m1), lane, big), axis=1, keepdims=True)
    el2 = jnp.where(lane == i1, neg, el)
    m2 = jnp.max(el2, axis=1, keepdims=True)
    i2 = jnp.min(jnp.where(in_grp & (lane != i1) & (el2 == m2), lane, big), axis=1, keepdims=True)
    r = jnp.exp(m2 - m1)
    w1 = p_group / (1.0 + r)
    w2 = p_group * r / (1.0 + r)

    member = jnp.where(lane == i1, 1.0, jnp.where(lane == i2, 1.0, 0.0))
    before = jnp.dot(tri, member.astype(jnp.bfloat16),
                     preferred_element_type=jnp.float32) + cnt_scr[...]
    rank1 = jnp.sum(jnp.where(lane == i1, before, 0.0), axis=1, keepdims=True)
    rank2 = jnp.sum(jnp.where(lane == i2, before, 0.0), axis=1, keepdims=True)
    cnt_scr[...] = cnt_scr[...] + jnp.sum(member, axis=0, keepdims=True)

    route = jnp.zeros_like(logit)
    for idx, val in ((R_E1, i1 - N_GROUPS), (R_E2, i2 - N_GROUPS), (R_W1, w1), (R_W2, w2),
                     (R_RANK1, rank1), (R_RANK2, rank2)):
        route = jnp.where(lane == idx, val, route)
    route_ref[...] = route
    routet_ref[...] = route.T[0:SUBLANES, :]


def _out(mix_c, mix_a, w_out, x2, gt1, g_ffn, sc2, sh2, w_route, b_route, seq, tm=512):
    t, d = x2.shape
    per_b = seq // tm
    return pl.pallas_call(
        _out_kernel,
        grid=(t // tm,),
        in_specs=[pl.BlockSpec((tm, CONV_DIM), lambda m: (m, 0)),
                  pl.BlockSpec((tm, ATTN_DIM), lambda m: (m, 0)),
                  pl.BlockSpec(memory_space=pl.ANY),
                  pl.BlockSpec((tm, d), lambda m: (m, 0)),
                  _mod_spec(gt1, per_b),
                  pl.BlockSpec((1, d), lambda m: (0, 0)),
                  _mod_spec(sc2, per_b),
                  _mod_spec(sh2, per_b),
                  pl.BlockSpec((d, ROUTE_LANES), lambda m: (0, 0)),
                  pl.BlockSpec((1, ROUTE_LANES), lambda m: (0, 0))],
        out_specs=[pl.BlockSpec((tm, d), lambda m: (m, 0)),
                   pl.BlockSpec((tm * ROW_SUB, LANES), lambda m: (m, 0)),
                   pl.BlockSpec((tm, ROUTE_LANES), lambda m: (m, 0)),
                   pl.BlockSpec((SUBLANES, tm), lambda m: (0, m)),
                   pl.BlockSpec((1, ROUTE_LANES), lambda m: (0, 0))],
        out_shape=[jax.ShapeDtypeStruct((t, d), jnp.float32),
                   jax.ShapeDtypeStruct((t * ROW_SUB, LANES), jnp.float32),
                   jax.ShapeDtypeStruct((t, ROUTE_LANES), jnp.float32),
                   jax.ShapeDtypeStruct((SUBLANES, t), jnp.float32),
                   jax.ShapeDtypeStruct((1, ROUTE_LANES), jnp.float32)],
        scratch_shapes=[pltpu.VMEM((1, ROUTE_LANES), jnp.float32),
                        pltpu.VMEM((tm, tm), jnp.bfloat16),
                        pltpu.VMEM(w_out.shape, jnp.bfloat16),
                        pltpu.VMEM((2, STAGE_ROWS, d), jnp.float32),
                        pltpu.SemaphoreType.DMA((2,))],
        compiler_params=_cparams(("arbitrary",)),
        name="out",
    )(mix_c, mix_a, w_out, x2, gt1[0], g_ffn, sc2[0], sh2[0], w_route, b_route)


def _slot(off_ref, e_ref, r_ref, tok):
    return off_ref[e_ref[tok]] + r_ref[tok]


def _dispatch_kernel(e1_ref, e2_ref, r1_ref, r2_ref, off_ref, zs_ref, zc_ref, nu_ref, h2_ref, xs_ref,
                     zbuf, sem_z, sem_r, *, tm, n_tiles):
    i = pl.program_id(0)

    @pl.when(i == 0)
    def _():
        zbuf[...] = jnp.zeros(zbuf.shape, zbuf.dtype)

        def zero_copy(start):
            rows = pl.ds(pl.multiple_of(start * ROW_SUB, ZERO_ROWS * ROW_SUB), ZERO_ROWS * ROW_SUB)
            return pltpu.make_async_copy(zbuf, xs_ref.at[rows], sem_z)

        def zero_fill(first_row, n_granules):
            def start(k, carry):
                zero_copy(first_row + k * ZERO_ROWS).start()
                return carry

            def wait(k, carry):
                zero_copy(first_row + k * ZERO_ROWS).wait()
                return carry

            lax.fori_loop(0, n_granules, start, 0)
            lax.fori_loop(0, n_granules, wait, 0)

        for e in range(N_EXPERTS):
            zero_fill(zs_ref[e], zc_ref[e])
        zero_fill(nu_ref[0] * MOE_TM, (n_tiles - nu_ref[0]) * (MOE_TM // ZERO_ROWS))

    def issue_row(r, carry):
        tok = i * tm + r
        for e_ref, r_ref in ((e1_ref, r1_ref), (e2_ref, r2_ref)):
            pltpu.make_async_copy(_row(h2_ref, r), _row(xs_ref, _slot(off_ref, e_ref, r_ref, tok)),
                                  sem_r).start()
        return carry

    lax.fori_loop(0, tm, issue_row, 0)
    for _ in range(2):
        pltpu.make_async_copy(h2_ref, xs_ref.at[pl.ds(0, tm * ROW_SUB)], sem_r).wait()


def _dispatch(plan, h2, n_rows, tm=512):
    t = h2.shape[0] // ROW_SUB
    kern = functools.partial(_dispatch_kernel, tm=tm, n_tiles=n_rows // MOE_TM)
    return pl.pallas_call(
        kern,
        grid_spec=pltpu.PrefetchScalarGridSpec(
            num_scalar_prefetch=len(plan),
            grid=(t // tm,),
            in_specs=[pl.BlockSpec((tm * ROW_SUB, LANES), lambda i, *_: (i, 0))],
            out_specs=pl.BlockSpec(memory_space=pl.ANY),
            scratch_shapes=[pltpu.VMEM((ZERO_ROWS * ROW_SUB, LANES), h2.dtype),
                            pltpu.SemaphoreType.DMA(()),
                            pltpu.SemaphoreType.DMA(())]),
        out_shape=jax.ShapeDtypeStruct((n_rows * ROW_SUB, LANES), h2.dtype),
        compiler_params=_cparams(("arbitrary",)),
        name="dispatch",
    )(*plan, h2)


def _mlp_kernel(te_ref, nu_ref, first_ref, nxt_ref, slot_ref, xs_ref, wg_hbm, wu_hbm, wd_hbm, ys_ref,
                wg_f32, wu_f32, wd_f32, wg_bf, wu_bf, wd_bf, sem):
    i = pl.program_id(0)

    def copies(e, s):
        return [pltpu.make_async_copy(hbm.at[e], buf.at[s], sem.at[s])
                for hbm, buf in ((wg_hbm, wg_f32), (wu_hbm, wu_f32), (wd_hbm, wd_f32))]

    @pl.when(i == 0)
    def _():
        for cp in copies(te_ref[0], 0):
            cp.start()

    @pl.when((i < nu_ref[0]) & (first_ref[i] == 1))
    def _():
        s = slot_ref[i]
        for cp in copies(te_ref[i], s):
            cp.wait()
        wg_bf[...] = wg_f32[s].astype(jnp.bfloat16)
        wu_bf[...] = wu_f32[s].astype(jnp.bfloat16)
        wd_bf[...] = wd_f32[s].astype(jnp.bfloat16)

        @pl.when(nxt_ref[i] >= 0)
        def _():
            for cp in copies(nxt_ref[i], 1 - s):
                cp.start()

    @pl.when(i < nu_ref[0])
    def _():
        x = _rows_to_2d(xs_ref, MOE_TM).astype(jnp.bfloat16)
        a = jnp.dot(x, wg_bf[...], preferred_element_type=jnp.float32)
        u = jnp.dot(x, wu_bf[...], preferred_element_type=jnp.float32)
        hid = (a * jax.nn.sigmoid(a)) * u
        _rows_from_2d(ys_ref, jnp.dot(hid.astype(jnp.bfloat16), wd_bf[...],
                                      preferred_element_type=jnp.float32))

    @pl.when(i >= nu_ref[0])
    def _():
        ys_ref[...] = jnp.zeros(ys_ref.shape, ys_ref.dtype)


def _mlp(tiles, xs, w_gate, w_up, w_down):
    n_rows = xs.shape[0] // ROW_SUB
    _, d, f = w_gate.shape
    used = lambda i, te, nu, *_: (jnp.minimum(i, nu[0] - 1), 0)
    return pl.pallas_call(
        _mlp_kernel,
        grid_spec=pltpu.PrefetchScalarGridSpec(
            num_scalar_prefetch=len(tiles),
            grid=(n_rows // MOE_TM,),
            in_specs=[pl.BlockSpec((MOE_TM * ROW_SUB, LANES), used),
                      pl.BlockSpec(memory_space=pl.ANY),
                      pl.BlockSpec(memory_space=pl.ANY),
                      pl.BlockSpec(memory_space=pl.ANY)],
            out_specs=pl.BlockSpec((MOE_TM * ROW_SUB, LANES), lambda i, *_: (i, 0)),
            scratch_shapes=[pltpu.VMEM((2, d, f), jnp.float32),
                            pltpu.VMEM((2, d, f), jnp.float32),
                            pltpu.VMEM((2, f, d), jnp.float32),
                            pltpu.VMEM((d, f), jnp.bfloat16),
                            pltpu.VMEM((d, f), jnp.bfloat16),
                            pltpu.VMEM((f, d), jnp.bfloat16),
                            pltpu.SemaphoreType.DMA((2,))]),
        out_shape=jax.ShapeDtypeStruct((n_rows * ROW_SUB, LANES), jnp.float32),
        compiler_params=_cparams(("arbitrary",)),
        name="mlp",
    )(*tiles, xs, w_gate, w_up, w_down)


def _combine_kernel(e1_ref, e2_ref, r1_ref, r2_ref, off_ref, ys_ref, x1_ref, route_ref, gt_ref, g_ref,
                    sc_ref, sh_ref, o_ref, a0, b0, a1, b1, sem, *, tm, n_tiles):
    i = pl.program_id(0)
    bufs = ((a0, b0), (a1, b1))

    def issue(tile, sl):
        def issue_row(r, carry):
            tok = tile * tm + r
            for buf, e_ref, r_ref in ((bufs[sl][0], e1_ref, r1_ref), (bufs[sl][1], e2_ref, r2_ref)):
                pltpu.make_async_copy(_row(ys_ref, _slot(off_ref, e_ref, r_ref, tok)),
                                      _row(buf, r), sem.at[sl]).start()
            return carry

        lax.fori_loop(0, tm, issue_row, 0)

    @pl.when(i == 0)
    def _():
        issue(0, 0)

    for sl in range(2):
        @pl.when(i % 2 == sl)
        def _(sl=sl):
            @pl.when(i + 1 < n_tiles)
            def _():
                issue(i + 1, 1 - sl)

            for buf in bufs[sl]:
                pltpu.make_async_copy(ys_ref.at[pl.ds(0, tm * ROW_SUB)], buf, sem.at[sl]).wait()

            route = route_ref[...]
            y = (route[:, R_W1:R_W1 + 1] * _rows_to_2d(bufs[sl][0], tm)
                 + route[:, R_W2:R_W2 + 1] * _rows_to_2d(bufs[sl][1], tm))
            x = x1_ref[...] + gt_ref[0] * y
            o_ref[...] = _rms(x) * g_ref[...] * (1.0 + sc_ref[0]) + sh_ref[0]


def _combine(plan, ys, x1, route, gt2, g_final, scf, shf, seq, tm=512):
    t, d = x1.shape
    per_b = seq // tm
    n_tiles = t // tm
    kern = functools.partial(_combine_kernel, tm=tm, n_tiles=n_tiles)
    return pl.pallas_call(
        kern,
        grid_spec=pltpu.PrefetchScalarGridSpec(
            num_scalar_prefetch=len(plan),
            grid=(n_tiles,),
            in_specs=[pl.BlockSpec(memory_space=pl.ANY),
                      pl.BlockSpec((tm, d), lambda m, *_: (m, 0)),
                      pl.BlockSpec((tm, ROUTE_LANES), lambda m, *_: (m, 0)),
                      _mod_spec(gt2, per_b),
                      pl.BlockSpec((1, d), lambda m, *_: (0, 0)),
                      _mod_spec(scf, per_b),
                      _mod_spec(shf, per_b)],
            out_specs=pl.BlockSpec((tm, d), lambda m, *_: (m, 0)),
            scratch_shapes=[pltpu.VMEM((tm * ROW_SUB, LANES), jnp.float32) for _ in range(4)]
            + [pltpu.SemaphoreType.DMA((2,))]),
        out_shape=jax.ShapeDtypeStruct((t, d), jnp.float32),
        compiler_params=_cparams(("arbitrary",)),
        name="combine",
    )(*plan, ys, x1, route, gt2[0], g_final, scf[0], shf[0])


def _moe_plan(route_t, counts, t):
    i32 = jnp.int32
    cnt = counts[0, N_GROUPS:N_GROUPS + N_EXPERTS].astype(i32)
    padded = (cnt + MOE_TM - 1) // MOE_TM * MOE_TM
    off_end = jnp.cumsum(padded)
    off = off_end - padded
    n_tiles = (2 * t + N_EXPERTS * MOE_TM) // MOE_TM
    n_used = off_end[-1] // MOE_TM
    tile_start = jnp.arange(n_tiles, dtype=i32) * MOE_TM
    te = jnp.sum((off_end[None, :] <= tile_start[:, None]).astype(i32), axis=1)
    te = jnp.minimum(te, N_EXPERTS - 1)
    tile = jnp.arange(n_tiles, dtype=i32)
    te = jnp.where(tile < n_used, te, te[n_used - 1])
    first = jnp.concatenate([jnp.ones((1,), i32), (te[1:] != te[:-1]).astype(i32)])
    nxt_tile = off_end[te] // MOE_TM
    nxt = jnp.where(nxt_tile < n_used, te[jnp.minimum(nxt_tile, n_tiles - 1)], -1)
    slot = (jnp.cumsum(first) - 1) % 2
    assign = tuple(route_t[k].astype(i32) for k in (R_E1, R_E2, R_RANK1, R_RANK2))
    zstart = (off + cnt) // ZERO_ROWS * ZERO_ROWS
    zcount = (off_end - zstart) // ZERO_ROWS
    n_used = n_used.reshape(1)
    return assign, off, (zstart, zcount), (te, n_used, first, nxt, slot), n_tiles * MOE_TM


def kernel(x, c, w_ada, b_ada, g_mix, w_in, conv_w, w_uk, kv_norm_g, w_uv, g_conv_out, g_attn_out,
           w_out, g_ffn, w_rg, b_rg, w_re, b_re, w_gate, w_up, w_down, w_ada_f, b_ada_f, g_final):
    b, s, d = x.shape
    assert w_ada.shape[0] == 1, "single layer"
    bf = jnp.bfloat16
    x2 = x.reshape(b * s, d)

    c_pad = jnp.pad(c, ((0, SUBLANES - b), (0, 0)))
    mod = _mod(c_pad, w_ada[0], b_ada[0]).reshape(SUBLANES, 1, N_MOD * d)
    modf = _mod(c_pad, w_ada_f, b_ada_f).reshape(SUBLANES, 1, 2 * d)
    sh1, sc1, gt1, sh2, sc2, gt2 = ((mod, i) for i in range(N_MOD))
    shf, scf = (modf, 0), (modf, 1)
    row = lambda a: a.reshape(1, -1)

    w_in_t = jnp.swapaxes(w_in[0], 0, 1)
    h, q, kv, iq, ikw = _hproj(x2, row(g_mix[0]), sc1, sh1, w_in_t, s)
    mix_c = _projconv(h, w_in_t, conv_w[0], row(g_conv_out[0]), s)

    mix_a = _attn(iq.reshape(b, s, -1), ikw.reshape(b, s, -1), kv.reshape(b, s, -1),
                  q.reshape(b, s, -1), w_uk[0].astype(bf), w_uv[0].astype(bf),
                  row(kv_norm_g[0]), row(g_attn_out[0]))

    lane_pad = ROUTE_LANES - N_GROUPS - N_EXPERTS
    w_route = jnp.pad(jnp.concatenate([w_rg[0], w_re[0]], axis=1), ((0, 0), (0, lane_pad))).astype(bf)
    b_route = jnp.pad(jnp.concatenate([b_rg[0], b_re[0]]), (0, lane_pad)).reshape(1, ROUTE_LANES)
    x1, h2, route, route_t, counts = _out(mix_c, mix_a.reshape(b * s, -1),
                                 w_out[0], x2, gt1, row(g_ffn[0]),
                                 sc2, sh2, w_route, b_route, s)

    assign, off, zero, tiles, n_rows = _moe_plan(route_t, counts, b * s)
    xs = _dispatch((*assign, off, *zero, tiles[1]), h2, n_rows)
    ys = _mlp(tiles, xs, w_gate[0], w_up[0], w_down[0])
    out = _combine((*assign, off), ys, x1, route, gt2, row(g_final), scf, shf, s)
    return out.reshape(b, s, d)
```

```python
import functools

import jax
import jax.numpy as jnp
from jax import lax
from jax.experimental import pallas as pl
from jax.experimental.pallas import tpu as pltpu

D_MODEL = 2048
CONV_DIM = 1024
CONV_WIDTH = 3
N_HEADS = 8
HEAD_DIM = 128
ATTN_DIM = N_HEADS * HEAD_DIM
KV_RANK = 512
IDX_HEADS = 16
IDX_DIM = 128
TOPK_MAX = 256
N_GROUPS = 4
EXPERTS_PER_GROUP = 8
N_EXPERTS = N_GROUPS * EXPERTS_PER_GROUP
EXPERT_FF = 512
N_MOD = 6
EPS = 1e-6

OFF_Q = 3 * CONV_DIM
OFF_KV = OFF_Q + ATTN_DIM
OFF_IQ = OFF_KV + KV_RANK
OFF_IK = OFF_IQ + IDX_HEADS * IDX_DIM
OFF_IW = OFF_IK + IDX_DIM
IN_COLS = OFF_IW + IDX_HEADS

LANES = 128
SUBLANES = 8
VMEM_LIMIT = 56 * 1024 * 1024

IKW_COLS = 256
ATTN_Q = 256
ATTN_KC = 256
ATTN_RB = 256
MASKED = -1e30
TIE_ALL = 2 ** 30
ROUTE_LANES = 128
STAGE_ROWS = 256
OUT_SUB = 2
MOE_TM = 256
ZERO_ROWS = 64
ROW_SUB = D_MODEL // LANES
INT_MIN = -2 ** 31
KEY_NEG_INF = INT_MIN + 0x7FFFFF


def _cparams(sem):
    return pltpu.CompilerParams(dimension_semantics=sem, vmem_limit_bytes=VMEM_LIMIT)


def _rms(v, axis=-1):
    return v * lax.rsqrt(jnp.mean(v * v, axis=axis, keepdims=True) + EPS)


def _tile_lanes(v, n):
    return jnp.concatenate([v] * n, axis=1)


def _rows_to_2d(ref, n):
    return jnp.concatenate([ref[pl.ds(c, n, stride=ROW_SUB), :] for c in range(ROW_SUB)], axis=1)


def _rows_from_2d(ref, val):
    n = val.shape[0]
    for c in range(ROW_SUB):
        ref[pl.ds(c, n, stride=ROW_SUB), :] = val[:, c * LANES:(c + 1) * LANES]


def _row(ref, i):
    return ref.at[pl.ds(pl.multiple_of(i * ROW_SUB, ROW_SUB), ROW_SUB)]


def _mod_spec(vec, per_b):
    idx = vec[1]
    return pl.BlockSpec((1, 1, D_MODEL), lambda m, *_: (m // per_b, 0, idx))


def _load_bf16(w_hbm, pieces, w_bf, stage, sem):
    def copy(i):
        src, n, _ = pieces[i]
        return pltpu.make_async_copy(w_hbm.at[pl.ds(src, n)], stage.at[i % 2, pl.ds(0, n)],
                                     sem.at[i % 2])

    copy(0).start()
    for i, (_, n, dst) in enumerate(pieces):
        if i + 1 < len(pieces):
            copy(i + 1).start()
        copy(i).wait()
        w_bf[pl.ds(dst, n), :] = stage[i % 2, pl.ds(0, n), :].astype(jnp.bfloat16)


def _mod_kernel(c_ref, w_ref, b_ref, o_ref):
    c = c_ref[...]
    ca = (c * jax.nn.sigmoid(c)).astype(jnp.bfloat16)
    o_ref[...] = jnp.dot(ca, w_ref[...].astype(jnp.bfloat16),
                         preferred_element_type=jnp.float32) + b_ref[...]


def _mod(c_pad, w, b, tn=1024):
    d, n = w.shape
    return pl.pallas_call(
        _mod_kernel,
        grid=(n // tn,),
        in_specs=[pl.BlockSpec((c_pad.shape[0], d), lambda j: (0, 0)),
                  pl.BlockSpec((d, tn), lambda j: (0, j)),
                  pl.BlockSpec((1, tn), lambda j: (0, j))],
        out_specs=pl.BlockSpec((c_pad.shape[0], tn), lambda j: (0, j)),
        out_shape=jax.ShapeDtypeStruct((c_pad.shape[0], n), jnp.float32),
        compiler_params=_cparams(("arbitrary",)),
        name="mod",
    )(c_pad, w, b.reshape(1, n))


ATT_COLS = IN_COLS - OFF_Q
ATT_PAD = OFF_IK - OFF_Q + IKW_COLS


def _hproj_kernel(x_ref, g_ref, sc_ref, sh_ref, wt_hbm, h_ref, q_ref, kv_ref, iq_ref, ikw_ref,
                  w_bf, stage, sem):
    @pl.when(pl.program_id(0) == 0)
    def _():
        full = ATT_COLS // STAGE_ROWS
        pieces = [(OFF_Q + k * STAGE_ROWS, STAGE_ROWS, k * STAGE_ROWS) for k in range(full)]
        pieces.append((OFF_Q + full * STAGE_ROWS, ATT_COLS - full * STAGE_ROWS, full * STAGE_ROWS))
        _load_bf16(wt_hbm, pieces, w_bf, stage, sem)
        w_bf[pl.ds(ATT_COLS, ATT_PAD - ATT_COLS), :] = jnp.zeros((ATT_PAD - ATT_COLS, w_bf.shape[1]),
                                                                 jnp.bfloat16)

    h = (_rms(x_ref[...]) * g_ref[...] * (1.0 + sc_ref[0]) + sh_ref[0]).astype(h_ref.dtype)
    h_ref[...] = h
    nt = (((1,), (1,)), ((), ()))
    row0 = 0
    for o_ref in (q_ref, kv_ref, iq_ref, ikw_ref):
        n = o_ref.shape[1]
        o_ref[...] = lax.dot_general(h, w_bf[row0:row0 + n, :], nt,
                                     preferred_element_type=jnp.float32).astype(o_ref.dtype)
        row0 += n


def _hproj(x2, g_mix, sc1, sh1, wt, seq, tm=512):
    t, d = x2.shape
    per_b = seq // tm
    widths = (ATTN_DIM, KV_RANK, IDX_HEADS * IDX_DIM, IKW_COLS)
    dtypes = (jnp.bfloat16, jnp.float32, jnp.bfloat16, jnp.float32)
    assert sum(widths) == ATT_PAD and (ATT_COLS % STAGE_ROWS) % SUBLANES == 0
    return pl.pallas_call(
        _hproj_kernel,
        grid=(t // tm,),
        in_specs=[pl.BlockSpec((tm, d), lambda m: (m, 0)),
                  pl.BlockSpec((1, d), lambda m: (0, 0)),
                  _mod_spec(sc1, per_b),
                  _mod_spec(sh1, per_b),
                  pl.BlockSpec(memory_space=pl.ANY)],
        out_specs=[pl.BlockSpec((tm, d), lambda m: (m, 0))]
        + [pl.BlockSpec((tm, w), lambda m: (m, 0)) for w in widths],
        out_shape=[jax.ShapeDtypeStruct((t, d), jnp.bfloat16)]
        + [jax.ShapeDtypeStruct((t, w), dt) for w, dt in zip(widths, dtypes)],
        scratch_shapes=[pltpu.VMEM((ATT_PAD, d), jnp.bfloat16),
                        pltpu.VMEM((2, STAGE_ROWS, d), jnp.float32),
                        pltpu.SemaphoreType.DMA((2,))],
        compiler_params=_cparams(("arbitrary",)),
        name="hproj",
    )(x2, g_mix, sc1[0], sh1[0], wt)


def _projconv_kernel(h_ref, wt_hbm, cw_ref, g_ref, o_ref, w_bf, stage, halo_scr, sem, *, per_b):
    m = pl.program_id(0)

    @pl.when(m == 0)
    def _():
        _load_bf16(wt_hbm, [(r, STAGE_ROWS, r) for r in range(0, w_bf.shape[0], STAGE_ROWS)],
                   w_bf, stage, sem)

    nt = (((1,), (1,)), ((), ()))
    h = h_ref[...]
    bg, cg, xv = (lax.dot_general(h, w_bf[k * CONV_DIM:(k + 1) * CONV_DIM, :], nt,
                                  preferred_element_type=jnp.float32) for k in range(3))
    u = cg * xv
    halo = jnp.where(m % per_b == 0, 0.0, halo_scr[...])
    halo_scr[...] = u[u.shape[0] - SUBLANES:, :]
    row = lax.broadcasted_iota(jnp.int32, u.shape, 0)
    h1 = halo[SUBLANES - 1:SUBLANES, :]
    h2 = halo[SUBLANES - 2:SUBLANES - 1, :]
    u1 = jnp.where(row == 0, h1, pltpu.roll(u, 1, 0))
    u2 = jnp.where(row == 0, h2, jnp.where(row == 1, h1, pltpu.roll(u, 2, 0)))
    w = cw_ref[...]
    y = bg * (w[0:1, :] * u2 + w[1:2, :] * u1 + w[2:3, :] * u)
    o_ref[...] = (_rms(y) * g_ref[...]).astype(o_ref.dtype)


def _projconv(h, wt, conv_w, g_conv_out, seq, tm=1024):
    t, d = h.shape
    c = CONV_DIM
    kern = functools.partial(_projconv_kernel, per_b=seq // tm)
    return pl.pallas_call(
        kern,
        grid=(t // tm,),
        in_specs=[pl.BlockSpec((tm, d), lambda m: (m, 0)),
                  pl.BlockSpec(memory_space=pl.ANY),
                  pl.BlockSpec((CONV_WIDTH, c), lambda m: (0, 0)),
                  pl.BlockSpec((1, c), lambda m: (0, 0))],
        out_specs=pl.BlockSpec((tm, c), lambda m: (m, 0)),
        out_shape=jax.ShapeDtypeStruct((t, c), jnp.bfloat16),
        scratch_shapes=[pltpu.VMEM((3 * c, d), jnp.bfloat16),
                        pltpu.VMEM((2, STAGE_ROWS, d), jnp.float32),
                        pltpu.VMEM((SUBLANES, c), jnp.float32),
                        pltpu.SemaphoreType.DMA((2,))],
        compiler_params=_cparams(("arbitrary",)),
        name="proj_conv",
    )(h, wt, conv_w, g_conv_out)


def _ordered_bits(v):
    return v ^ ((v >> 31) & jnp.int32(0x7FFFFFFF))


def _attn_search(nch, sc_scr, k_sel):
    n_keys = float(nch * ATTN_KC)
    half = ATTN_Q // 2

    def half_step(bit, thr, rows):
        cand = thr + bit
        cand_f = lax.bitcast_convert_type(_ordered_bits(cand), jnp.float32)
        hits = jnp.where(sc_scr[0, rows, :] >= cand_f, 1.0, 0.0)
        for c in range(1, nch):
            hits = hits + jnp.where(sc_scr[c, rows, :] >= cand_f, 1.0, 0.0)
        cnt = jnp.sum(hits, axis=1, keepdims=True)
        cnt = jnp.where(cand < KEY_NEG_INF, n_keys, cnt)
        return jnp.where(cnt >= k_sel, cand, thr)

    def bit_step(i, thrs):
        bit = lax.shift_left(jnp.int32(1), 31 - i)
        return tuple(half_step(bit, thr, pl.ds(k * half, half)) for k, thr in enumerate(thrs))

    init = jnp.full((half, 1), INT_MIN, jnp.int32)
    thrs = lax.fori_loop(0, 32, bit_step, (init, init), unroll=2)
    thr = jnp.concatenate(thrs, axis=0)
    return lax.bitcast_convert_type(_ordered_bits(thr), jnp.float32)


def _attn_tiebreak(nch, sc_scr, thr_f, k_sel, tie_scr):
    kloc = lax.broadcasted_iota(jnp.int32, (ATTN_Q, ATTN_KC), 1)

    def count(pred):
        hits = pred(sc_scr[0], 0)
        for c in range(1, nch):
            hits = hits + pred(sc_scr[c], c)
        return jnp.sum(hits, axis=1, keepdims=True)

    tie_scr[...] = jnp.full(tie_scr.shape, TIE_ALL, jnp.int32)
    n_ge = count(lambda sc, c: jnp.where(sc >= thr_f, 1.0, 0.0))
    excess = jnp.where(thr_f > -jnp.inf, n_ge - k_sel, 0.0)

    @pl.when(jnp.max(excess) > 0.0)
    def _():
        need = k_sel - count(lambda sc, c: jnp.where(sc > thr_f, 1.0, 0.0))
        n_bits = (nch * ATTN_KC - 1).bit_length()

        def step(i, lim):
            cand = lim + lax.shift_left(jnp.int32(1), n_bits - 1 - i)
            below = count(lambda sc, c: jnp.where(
                sc == thr_f, jnp.where(c * ATTN_KC + kloc < cand, 1.0, 0.0), 0.0))
            return jnp.where(below < need, cand, lim)

        lim = lax.fori_loop(0, n_bits, step, jnp.zeros((ATTN_Q, 1), jnp.int32))
        tie_scr[...] = jnp.where(excess > 0.0, lim, TIE_ALL)


def _attn_kernel(iq_ref, ikwq_ref, ikwa_ref, kv_ref, q_ref, wuk_ref, wuv_ref, kvg_ref, go_ref,
                 o_ref, ckv_scr, ik_scr, sc_scr, thr_scr, tie_scr, bias_scr, ql_scr, m_scr, l_scr, acc_scr,
                 y_scr,
                 *, seq, k_sel):
    j = pl.program_id(1)
    nch = j + 1
    nt = (((1,), (1,)), ((), ()))

    @pl.when(j == 0)
    def _():
        ckv_scr[...] = (_rms(kv_ref[0]) * kvg_ref[...]).astype(jnp.bfloat16)
        ik_scr[...] = ikwa_ref[0][:, :IDX_DIM].astype(jnp.bfloat16)

    iw = ikwq_ref[0][:, IDX_DIM:IDX_DIM + IDX_HEADS] * (IDX_HEADS ** -0.5 * IDX_DIM ** -0.5)
    qpos = j * ATTN_Q + lax.broadcasted_iota(jnp.int32, (ATTN_Q, ATTN_KC), 0)
    kloc = lax.broadcasted_iota(jnp.int32, (ATTN_Q, ATTN_KC), 1)

    def score_chunk(c, carry):
        k0 = pl.multiple_of(c * ATTN_KC, ATTN_KC)
        ik_c = ik_scr[pl.ds(k0, ATTN_KC), :]
        score = jnp.zeros((ATTN_Q, ATTN_KC), jnp.float32)
        for h in range(IDX_HEADS):
            s = lax.dot_general(iq_ref[0][:, h * IDX_DIM:(h + 1) * IDX_DIM], ik_c, nt,
                                preferred_element_type=jnp.float32)
            score = score + jnp.maximum(s, 0.0) * iw[:, h:h + 1]
        sc_scr[c] = jnp.where(k0 + kloc <= qpos, score, -jnp.inf)
        return carry

    lax.fori_loop(0, nch, score_chunk, 0)

    for v in range(seq // ATTN_KC):
        @pl.when(j == v)
        def _(v=v):
            if (v + 1) * ATTN_KC <= k_sel:
                thr_scr[...] = jnp.full(thr_scr.shape, -jnp.inf, jnp.float32)
                tie_scr[...] = jnp.full(tie_scr.shape, TIE_ALL, jnp.int32)
            else:
                thr_scr[...] = _attn_search(v + 1, sc_scr, k_sel)
                _attn_tiebreak(v + 1, sc_scr, thr_scr[...], k_sel, tie_scr)

    for h in range(N_HEADS):
        ql = lax.dot_general(q_ref[0][:, h * HEAD_DIM:(h + 1) * HEAD_DIM], wuk_ref[h], nt,
                             preferred_element_type=jnp.float32)
        ql_scr[h * ATTN_Q:(h + 1) * ATTN_Q, :] = ql.astype(jnp.bfloat16)
    thr_f = thr_scr[...]
    tie = tie_scr[...]
    scale = HEAD_DIM ** -0.5

    def attn_chunk(c, first):
        k0 = pl.multiple_of(c * ATTN_KC, ATTN_KC)
        keys = pl.ds(k0, ATTN_KC)
        sc = sc_scr[c]
        kpos = k0 + kloc
        tied = jnp.where(sc == thr_f, jnp.where(kpos <= tie, 0.0, MASKED), MASKED)
        bias_scr[...] = jnp.where(kpos <= qpos, jnp.where(sc > thr_f, 0.0, tied), MASKED)
        for u in range(N_HEADS * ATTN_Q // ATTN_RB):
            rows = pl.ds(u * ATTN_RB, ATTN_RB)
            qrow = (u * ATTN_RB) % ATTN_Q
            lg = lax.dot_general(ql_scr[rows, :], ckv_scr[keys, :], nt,
                                 preferred_element_type=jnp.float32) * scale + bias_scr[qrow:qrow + ATTN_RB, :]
            row_max = jnp.max(lg, axis=1, keepdims=True)
            if first:
                m_new = jnp.maximum(jnp.full((ATTN_RB, LANES), MASKED, jnp.float32), row_max)
            else:
                m_old = m_scr[rows, :]
                m_new = jnp.maximum(m_old, row_max)
                alpha = jnp.exp(m_old - m_new)
            p = jnp.exp(lg - _tile_lanes(m_new, ATTN_KC // LANES))
            p_sum = jnp.sum(p, axis=1, keepdims=True)
            pv = jnp.dot(p.astype(jnp.bfloat16), ckv_scr[keys, :], preferred_element_type=jnp.float32)
            if first:
                l_scr[rows, :] = jnp.broadcast_to(p_sum, (ATTN_RB, LANES))
                acc_scr[rows, :] = pv
            else:
                l_scr[rows, :] = alpha * l_scr[rows, :] + p_sum
                acc_scr[rows, :] = _tile_lanes(alpha, KV_RANK // LANES) * acc_scr[rows, :] + pv
            m_scr[rows, :] = m_new

    attn_chunk(0, True)

    def later_chunk(c, carry):
        attn_chunk(c, False)
        return carry

    lax.fori_loop(1, nch, later_chunk, 0)

    o = (acc_scr[...] * _tile_lanes(1.0 / l_scr[...], KV_RANK // LANES)).astype(jnp.bfloat16)
    for h in range(N_HEADS):
        y_scr[:, h * HEAD_DIM:(h + 1) * HEAD_DIM] = jnp.dot(
            o[h * ATTN_Q:(h + 1) * ATTN_Q], wuv_ref[h], preferred_element_type=jnp.float32)
    o_ref[0] = (_rms(y_scr[...]) * go_ref[...]).astype(o_ref.dtype)


def _attn(iq, ikw, kv, q, w_uk_bf, w_uv_bf, kv_norm_g, g_attn_out):
    b, s, _ = iq.shape
    assert ATTN_Q == ATTN_KC and s % ATTN_Q == 0
    k_sel = min(TOPK_MAX, s // 4)
    rows = N_HEADS * ATTN_Q
    kern = functools.partial(_attn_kernel, seq=s, k_sel=k_sel)
    return pl.pallas_call(
        kern,
        grid=(b, s // ATTN_Q),
        in_specs=[pl.BlockSpec((1, ATTN_Q, IDX_HEADS * IDX_DIM), lambda bi, j: (bi, j, 0)),
                  pl.BlockSpec((1, ATTN_Q, IKW_COLS), lambda bi, j: (bi, j, 0)),
                  pl.BlockSpec((1, s, IKW_COLS), lambda bi, j: (bi, 0, 0)),
                  pl.BlockSpec((1, s, KV_RANK), lambda bi, j: (bi, 0, 0)),
                  pl.BlockSpec((1, ATTN_Q, ATTN_DIM), lambda bi, j: (bi, j, 0)),
                  pl.BlockSpec((N_HEADS, KV_RANK, HEAD_DIM), lambda bi, j: (0, 0, 0)),
                  pl.BlockSpec((N_HEADS, KV_RANK, HEAD_DIM), lambda bi, j: (0, 0, 0)),
                  pl.BlockSpec((1, KV_RANK), lambda bi, j: (0, 0)),
                  pl.BlockSpec((1, ATTN_DIM), lambda bi, j: (0, 0))],
        out_specs=pl.BlockSpec((1, ATTN_Q, ATTN_DIM), lambda bi, j: (bi, j, 0)),
        out_shape=jax.ShapeDtypeStruct((b, s, ATTN_DIM), jnp.bfloat16),
        scratch_shapes=[pltpu.VMEM((s, KV_RANK), jnp.bfloat16),
                        pltpu.VMEM((s, IDX_DIM), jnp.bfloat16),
                        pltpu.VMEM((s // ATTN_KC, ATTN_Q, ATTN_KC), jnp.float32),
                        pltpu.VMEM((ATTN_Q, 1), jnp.float32),
                        pltpu.VMEM((ATTN_Q, 1), jnp.int32),
                        pltpu.VMEM((ATTN_Q, ATTN_KC), jnp.float32),
                        pltpu.VMEM((rows, KV_RANK), jnp.bfloat16),
                        pltpu.VMEM((rows, LANES), jnp.float32),
                        pltpu.VMEM((rows, LANES), jnp.float32),
                        pltpu.VMEM((rows, KV_RANK), jnp.float32),
                        pltpu.VMEM((ATTN_Q, ATTN_DIM), jnp.float32)],
        compiler_params=_cparams(("arbitrary", "arbitrary")),
        name="attn",
    )(iq, ikw, ikw, kv, q, w_uk_bf, w_uv_bf, kv_norm_g, g_attn_out)


R_E1, R_E2, R_W1, R_W2, R_RANK1, R_RANK2 = range(6)


def _out_kernel(mc_ref, ma_ref, w_hbm, x_ref, gt_ref, g_ref, sc_ref, sh_ref, wr_ref, br_ref,
                x1_ref, h2_ref, route_ref, routet_ref, cnt_ref, cnt_scr, tri_scr, w_bf, stage, sem):
    @pl.when(pl.program_id(0) == 0)
    def _():
        cnt_scr[...] = jnp.zeros(cnt_scr.shape, jnp.float32)
        earlier = (lax.broadcasted_iota(jnp.int32, tri_scr.shape, 1)
                   < lax.broadcasted_iota(jnp.int32, tri_scr.shape, 0))
        tri_scr[...] = jnp.where(earlier, 1.0, 0.0).astype(tri_scr.dtype)
        _load_bf16(w_hbm, [(r, STAGE_ROWS, r) for r in range(0, w_bf.shape[0], STAGE_ROWS)],
                   w_bf, stage, sem)

    rs = x_ref.shape[0] // OUT_SUB
    for sub in range(OUT_SUB):
        rows = pl.ds(sub * rs, rs)
        _out_rows(mc_ref[rows, :], ma_ref[rows, :], w_bf, x_ref[rows, :], gt_ref, g_ref, sc_ref, sh_ref,
                  wr_ref, br_ref, x1_ref.at[rows], h2_ref.at[pl.ds(sub * rs * ROW_SUB, rs * ROW_SUB)],
                  route_ref.at[rows], routet_ref.at[:, rows], cnt_scr, tri_scr[0:rs, 0:rs])
    cnt_ref[...] = cnt_scr[...]


def _out_rows(mc, ma, w_bf, x, gt_ref, g_ref, sc_ref, sh_ref, wr_ref, br_ref,
              x1_ref, h2_ref, route_ref, routet_ref, cnt_scr, tri):
    mix = (jnp.dot(mc, w_bf[0:CONV_DIM, :], preferred_element_type=jnp.float32)
           + jnp.dot(ma, w_bf[CONV_DIM:, :], preferred_element_type=jnp.float32))
    x1 = x + gt_ref[0] * mix
    x1_ref[...] = x1
    h2 = _rms(x1) * g_ref[...] * (1.0 + sc_ref[0]) + sh_ref[0]
    _rows_from_2d(h2_ref, h2)

    logit = jnp.dot(h2.astype(jnp.bfloat16), wr_ref[...],
                    preferred_element_type=jnp.float32) + br_ref[...]
    lane = lax.broadcasted_iota(jnp.int32, logit.shape, 1).astype(jnp.float32)
    neg = -jnp.inf
    big = float(ROUTE_LANES)
    is_g = lane < N_GROUPS
    gl = jnp.where(is_g, logit, neg)
    gmax = jnp.max(gl, axis=1, keepdims=True)
    p_group = 1.0 / jnp.sum(jnp.exp(gl - gmax), axis=1, keepdims=True)
    g_sel = jnp.min(jnp.where(is_g & (gl == gmax), lane, big), axis=1, keepdims=True)
    lo = N_GROUPS + g_sel * EXPERTS_PER_GROUP
    in_grp = (lane >= lo) & (lane < lo + EXPERTS_PER_GROUP)
    el = jnp.where(in_grp, logit, neg)
    m1 = jnp.max(el, axis=1, keepdims=True)
    i1 = jnp.min(jnp.where(in_grp & (el == m1), lane, big), axis=1, keepdims=True)
    el2 = jnp.where(lane == i1, neg, el)
    m2 = jnp.max(el2, axis=1, keepdims=True)
    i2 = jnp.min(jnp.where(in_grp & (lane != i1) & (el2 == m2), lane, big), axis=1, keepdims=True)
    r = jnp.exp(m2 - m1)
    w1 = p_group / (1.0 + r)
    w2 = p_group * r / (1.0 + r)

    member = jnp.where(lane == i1, 1.0, jnp.where(lane == i2, 1.0, 0.0))
    before = jnp.dot(tri, member.astype(jnp.bfloat16),
                     preferred_element_type=jnp.float32) + cnt_scr[...]
    rank1 = jnp.sum(jnp.where(lane == i1, before, 0.0), axis=1, keepdims=True)
    rank2 = jnp.sum(jnp.where(lane == i2, before, 0.0), axis=1, keepdims=True)
    cnt_scr[...] = cnt_scr[...] + jnp.sum(member, axis=0, keepdims=True)

    route = jnp.zeros_like(logit)
    for idx, val in ((R_E1, i1 - N_GROUPS), (R_E2, i2 - N_GROUPS), (R_W1, w1), (R_W2, w2),
                     (R_RANK1, rank1), (R_RANK2, rank2)):
        route = jnp.where(lane == idx, val, route)
    route_ref[...] = route
    routet_ref[...] = route.T[0:SUBLANES, :]


def _out(mix_c, mix_a, w_out, x2, gt1, g_ffn, sc2, sh2, w_route, b_route, seq, tm=512):
    t, d = x2.shape
    per_b = seq // tm
    return pl.pallas_call(
        _out_kernel,
        grid=(t // tm,),
        in_specs=[pl.BlockSpec((tm, CONV_DIM), lambda m: (m, 0)),
                  pl.BlockSpec((tm, ATTN_DIM), lambda m: (m, 0)),
                  pl.BlockSpec(memory_space=pl.ANY),
                  pl.BlockSpec((tm, d), lambda m: (m, 0)),
                  _mod_spec(gt1, per_b),
                  pl.BlockSpec((1, d), lambda m: (0, 0)),
                  _mod_spec(sc2, per_b),
                  _mod_spec(sh2, per_b),
                  pl.BlockSpec((d, ROUTE_LANES), lambda m: (0, 0)),
                  pl.BlockSpec((1, ROUTE_LANES), lambda m: (0, 0))],
        out_specs=[pl.BlockSpec((tm, d), lambda m: (m, 0)),
                   pl.BlockSpec((tm * ROW_SUB, LANES), lambda m: (m, 0)),
                   pl.BlockSpec((tm, ROUTE_LANES), lambda m: (m, 0)),
                   pl.BlockSpec((SUBLANES, tm), lambda m: (0, m)),
                   pl.BlockSpec((1, ROUTE_LANES), lambda m: (0, 0))],
        out_shape=[jax.ShapeDtypeStruct((t, d), jnp.float32),
                   jax.ShapeDtypeStruct((t * ROW_SUB, LANES), jnp.float32),
                   jax.ShapeDtypeStruct((t, ROUTE_LANES), jnp.float32),
                   jax.ShapeDtypeStruct((SUBLANES, t), jnp.float32),
                   jax.ShapeDtypeStruct((1, ROUTE_LANES), jnp.float32)],
        scratch_shapes=[pltpu.VMEM((1, ROUTE_LANES), jnp.float32),
                        pltpu.VMEM((tm, tm), jnp.bfloat16),
                        pltpu.VMEM(w_out.shape, jnp.bfloat16),
                        pltpu.VMEM((2, STAGE_ROWS, d), jnp.float32),
                        pltpu.SemaphoreType.DMA((2,))],
        compiler_params=_cparams(("arbitrary",)),
        name="out",
    )(mix_c, mix_a, w_out, x2, gt1[0], g_ffn, sc2[0], sh2[0], w_route, b_route)


def _slot(off_ref, e_ref, r_ref, tok):
    return off_ref[e_ref[tok]] + r_ref[tok]


def _dispatch_kernel(e1_ref, e2_ref, r1_ref, r2_ref, off_ref, zs_ref, zc_ref, nu_ref, h2_ref, xs_ref,
                     zbuf, sem_z, sem_r, *, tm, n_tiles):
    i = pl.program_id(0)

    @pl.when(i == 0)
    def _():
        zbuf[...] = jnp.zeros(zbuf.shape, zbuf.dtype)

        def zero_copy(start):
            rows = pl.ds(pl.multiple_of(start * ROW_SUB, ZERO_ROWS * ROW_SUB), ZERO_ROWS * ROW_SUB)
            return pltpu.make_async_copy(zbuf, xs_ref.at[rows], sem_z)

        def zero_fill(first_row, n_granules):
            def start(k, carry):
                zero_copy(first_row + k * ZERO_ROWS).start()
                return carry

            def wait(k, carry):
                zero_copy(first_row + k * ZERO_ROWS).wait()
                return carry

            lax.fori_loop(0, n_granules, start, 0)
            lax.fori_loop(0, n_granules, wait, 0)

        for e in range(N_EXPERTS):
            zero_fill(zs_ref[e], zc_ref[e])
        zero_fill(nu_ref[0] * MOE_TM, (n_tiles - nu_ref[0]) * (MOE_TM // ZERO_ROWS))

    def issue_row(r, carry):
        tok = i * tm + r
        for e_ref, r_ref in ((e1_ref, r1_ref), (e2_ref, r2_ref)):
            pltpu.make_async_copy(_row(h2_ref, r), _row(xs_ref, _slot(off_ref, e_ref, r_ref, tok)),
                                  sem_r).start()
        return carry

    lax.fori_loop(0, tm, issue_row, 0)
    for _ in range(2):
        pltpu.make_async_copy(h2_ref, xs_ref.at[pl.ds(0, tm * ROW_SUB)], sem_r).wait()


def _dispatch(plan, h2, n_rows, tm=512):
    t = h2.shape[0] // ROW_SUB
    kern = functools.partial(_dispatch_kernel, tm=tm, n_tiles=n_rows // MOE_TM)
    return pl.pallas_call(
        kern,
        grid_spec=pltpu.PrefetchScalarGridSpec(
            num_scalar_prefetch=len(plan),
            grid=(t // tm,),
            in_specs=[pl.BlockSpec((tm * ROW_SUB, LANES), lambda i, *_: (i, 0))],
            out_specs=pl.BlockSpec(memory_space=pl.ANY),
            scratch_shapes=[pltpu.VMEM((ZERO_ROWS * ROW_SUB, LANES), h2.dtype),
                            pltpu.SemaphoreType.DMA(()),
                            pltpu.SemaphoreType.DMA(())]),
        out_shape=jax.ShapeDtypeStruct((n_rows * ROW_SUB, LANES), h2.dtype),
        compiler_params=_cparams(("arbitrary",)),
        name="dispatch",
    )(*plan, h2)


def _mlp_kernel(te_ref, nu_ref, first_ref, nxt_ref, slot_ref, xs_ref, wg_hbm, wu_hbm, wd_hbm, ys_ref,
                wg_f32, wu_f32, wd_f32, wg_bf, wu_bf, wd_bf, sem):
    i = pl.program_id(0)

    def copies(e, s):
        return [pltpu.make_async_copy(hbm.at[e], buf.at[s], sem.at[s])
                for hbm, buf in ((wg_hbm, wg_f32), (wu_hbm, wu_f32), (wd_hbm, wd_f32))]

    @pl.when(i == 0)
    def _():
        for cp in copies(te_ref[0], 0):
            cp.start()

    @pl.when((i < nu_ref[0]) & (first_ref[i] == 1))
    def _():
        s = slot_ref[i]
        for cp in copies(te_ref[i], s):
            cp.wait()
        wg_bf[...] = wg_f32[s].astype(jnp.bfloat16)
        wu_bf[...] = wu_f32[s].astype(jnp.bfloat16)
        wd_bf[...] = wd_f32[s].astype(jnp.bfloat16)

        @pl.when(nxt_ref[i] >= 0)
        def _():
            for cp in copies(nxt_ref[i], 1 - s):
                cp.start()

    @pl.when(i < nu_ref[0])
    def _():
        x = _rows_to_2d(xs_ref, MOE_TM).astype(jnp.bfloat16)
        a = jnp.dot(x, wg_bf[...], preferred_element_type=jnp.float32)
        u = jnp.dot(x, wu_bf[...], preferred_element_type=jnp.float32)
        hid = (a * jax.nn.sigmoid(a)) * u
        _rows_from_2d(ys_ref, jnp.dot(hid.astype(jnp.bfloat16), wd_bf[...],
                                      preferred_element_type=jnp.float32))

    @pl.when(i >= nu_ref[0])
    def _():
        ys_ref[...] = jnp.zeros(ys_ref.shape, ys_ref.dtype)


def _mlp(tiles, xs, w_gate, w_up, w_down):
    n_rows = xs.shape[0] // ROW_SUB
    _, d, f = w_gate.shape
    used = lambda i, te, nu, *_: (jnp.minimum(i, nu[0] - 1), 0)
    return pl.pallas_call(
        _mlp_kernel,
        grid_spec=pltpu.PrefetchScalarGridSpec(
            num_scalar_prefetch=len(tiles),
            grid=(n_rows // MOE_TM,),
            in_specs=[pl.BlockSpec((MOE_TM * ROW_SUB, LANES), used),
                      pl.BlockSpec(memory_space=pl.ANY),
                      pl.BlockSpec(memory_space=pl.ANY),
                      pl.BlockSpec(memory_space=pl.ANY)],
            out_specs=pl.BlockSpec((MOE_TM * ROW_SUB, LANES), lambda i, *_: (i, 0)),
            scratch_shapes=[pltpu.VMEM((2, d, f), jnp.float32),
                            pltpu.VMEM((2, d, f), jnp.float32),
                            pltpu.VMEM((2, f, d), jnp.float32),
                            pltpu.VMEM((d, f), jnp.bfloat16),
                            pltpu.VMEM((d, f), jnp.bfloat16),
                            pltpu.VMEM((f, d), jnp.bfloat16),
                            pltpu.SemaphoreType.DMA((2,))]),
        out_shape=jax.ShapeDtypeStruct((n_rows * ROW_SUB, LANES), jnp.float32),
        compiler_params=_cparams(("arbitrary",)),
        name="mlp",
    )(*tiles, xs, w_gate, w_up, w_down)


def _combine_kernel(e1_ref, e2_ref, r1_ref, r2_ref, off_ref, ys_ref, x1_ref, route_ref, gt_ref, g_ref,
                    sc_ref, sh_ref, o_ref, a0, b0, a1, b1, sem, *, tm, n_tiles):
    i = pl.program_id(0)
    bufs = ((a0, b0), (a1, b1))

    def issue(tile, sl):
        def issue_row(r, carry):
            tok = tile * tm + r
            for buf, e_ref, r_ref in ((bufs[sl][0], e1_ref, r1_ref), (bufs[sl][1], e2_ref, r2_ref)):
                pltpu.make_async_copy(_row(ys_ref, _slot(off_ref, e_ref, r_ref, tok)),
                                      _row(buf, r), sem.at[sl]).start()
            return carry

        lax.fori_loop(0, tm, issue_row, 0)

    @pl.when(i == 0)
    def _():
        issue(0, 0)

    for sl in range(2):
        @pl.when(i % 2 == sl)
        def _(sl=sl):
            @pl.when(i + 1 < n_tiles)
            def _():
                issue(i + 1, 1 - sl)

            for buf in bufs[sl]:
                pltpu.make_async_copy(ys_ref.at[pl.ds(0, tm * ROW_SUB)], buf, sem.at[sl]).wait()

            route = route_ref[...]
            y = (route[:, R_W1:R_W1 + 1] * _rows_to_2d(bufs[sl][0], tm)
                 + route[:, R_W2:R_W2 + 1] * _rows_to_2d(bufs[sl][1], tm))
            x = x1_ref[...] + gt_ref[0] * y
            o_ref[...] = _rms(x) * g_ref[...] * (1.0 + sc_ref[0]) + sh_ref[0]


def _combine(plan, ys, x1, route, gt2, g_final, scf, shf, seq, tm=512):
    t, d = x1.shape
    per_b = seq // tm
    n_tiles = t // tm
    kern = functools.partial(_combine_kernel, tm=tm, n_tiles=n_tiles)
    return pl.pallas_call(
        kern,
        grid_spec=pltpu.PrefetchScalarGridSpec(
            num_scalar_prefetch=len(plan),
            grid=(n_tiles,),
            in_specs=[pl.BlockSpec(memory_space=pl.ANY),
                      pl.BlockSpec((tm, d), lambda m, *_: (m, 0)),
                      pl.BlockSpec((tm, ROUTE_LANES), lambda m, *_: (m, 0)),
                      _mod_spec(gt2, per_b),
                      pl.BlockSpec((1, d), lambda m, *_: (0, 0)),
                      _mod_spec(scf, per_b),
                      _mod_spec(shf, per_b)],
            out_specs=pl.BlockSpec((tm, d), lambda m, *_: (m, 0)),
            scratch_shapes=[pltpu.VMEM((tm * ROW_SUB, LANES), jnp.float32) for _ in range(4)]
            + [pltpu.SemaphoreType.DMA((2,))]),
        out_shape=jax.ShapeDtypeStruct((t, d), jnp.float32),
        compiler_params=_cparams(("arbitrary",)),
        name="combine",
    )(*plan, ys, x1, route, gt2[0], g_final, scf[0], shf[0])


def _moe_plan(route_t, counts, t):
    i32 = jnp.int32
    cnt = counts[0, N_GROUPS:N_GROUPS + N_EXPERTS].astype(i32)
    padded = (cnt + MOE_TM - 1) // MOE_TM * MOE_TM
    off_end = jnp.cumsum(padded)
    off = off_end - padded
    n_tiles = (2 * t + N_EXPERTS * MOE_TM) // MOE_TM
    n_used = off_end[-1] // MOE_TM
    tile_start = jnp.arange(n_tiles, dtype=i32) * MOE_TM
    te = jnp.sum((off_end[None, :] <= tile_start[:, None]).astype(i32), axis=1)
    te = jnp.minimum(te, N_EXPERTS - 1)
    tile = jnp.arange(n_tiles, dtype=i32)
    te = jnp.where(tile < n_used, te, te[n_used - 1])
    first = jnp.concatenate([jnp.ones((1,), i32), (te[1:] != te[:-1]).astype(i32)])
    nxt_tile = off_end[te] // MOE_TM
    nxt = jnp.where(nxt_tile < n_used, te[jnp.minimum(nxt_tile, n_tiles - 1)], -1)
    slot = (jnp.cumsum(first) - 1) % 2
    assign = tuple(route_t[k].astype(i32) for k in (R_E1, R_E2, R_RANK1, R_RANK2))
    zstart = (off + cnt) // ZERO_ROWS * ZERO_ROWS
    zcount = (off_end - zstart) // ZERO_ROWS
    n_used = n_used.reshape(1)
    return assign, off, (zstart, zcount), (te, n_used, first, nxt, slot), n_tiles * MOE_TM


def kernel(x, c, w_ada, b_ada, g_mix, w_in, conv_w, w_uk, kv_norm_g, w_uv, g_conv_out, g_attn_out,
           w_out, g_ffn, w_rg, b_rg, w_re, b_re, w_gate, w_up, w_down, w_ada_f, b_ada_f, g_final):
    b, s, d = x.shape
    assert w_ada.shape[0] == 1, "single layer"
    bf = jnp.bfloat16
    x2 = x.reshape(b * s, d)

    c_pad = jnp.zeros((SUBLANES, d), jnp.float32).at[:b].set(c)
    mod = _mod(c_pad, w_ada[0], b_ada[0]).reshape(SUBLANES, 1, N_MOD * d)
    modf = _mod(c_pad, w_ada_f, b_ada_f).reshape(SUBLANES, 1, 2 * d)
    sh1, sc1, gt1, sh2, sc2, gt2 = ((mod, i) for i in range(N_MOD))
    shf, scf = (modf, 0), (modf, 1)
    row = lambda a: a.reshape(1, -1)

    w_in_t = jnp.swapaxes(w_in[0], 0, 1)
    h, q, kv, iq, ikw = _hproj(x2, row(g_mix[0]), sc1, sh1, w_in_t, s)
    mix_c = _projconv(h, w_in_t, conv_w[0], row(g_conv_out[0]), s)

    mix_a = _attn(iq.reshape(b, s, -1), ikw.reshape(b, s, -1), kv.reshape(b, s, -1),
                  q.reshape(b, s, -1), w_uk[0].astype(bf), w_uv[0].astype(bf),
                  row(kv_norm_g[0]), row(g_attn_out[0]))

    w_route = jnp.zeros((d, ROUTE_LANES), bf).at[:, :N_GROUPS].set(w_rg[0].astype(bf))
    w_route = w_route.at[:, N_GROUPS:N_GROUPS + N_EXPERTS].set(w_re[0].astype(bf))
    b_route = jnp.zeros((1, ROUTE_LANES), jnp.float32).at[0, :N_GROUPS].set(b_rg[0])
    b_route = b_route.at[0, N_GROUPS:N_GROUPS + N_EXPERTS].set(b_re[0])
    x1, h2, route, route_t, counts = _out(mix_c, mix_a.reshape(b * s, -1),
                                 w_out[0], x2, gt1, row(g_ffn[0]),
                                 sc2, sh2, w_route, b_route, s)

    assign, off, zero, tiles, n_rows = _moe_plan(route_t, counts, b * s)
    xs = _dispatch((*assign, off, *zero, tiles[1]), h2, n_rows)
    ys = _mlp(tiles, xs, w_gate[0], w_up[0], w_down[0])
    out = _combine((*assign, off), ys, x1, route, gt2, row(g_final), scf, shf, s)
    return out.reshape(b, s, d)
```
